```python
import math
import jax, jax.numpy as jnp
from jax import lax
import numpy as np

D_MODEL = 2048
BATCH = 4
SEQ = 2048
DEPTH = 1
DEC_BATCH = 8
DEC_SEQ = 4
PAST_LEN = 16384
PAGE_SIZE = 128

N_HEADS = 16
HEAD_DIM = 64
ATTN_WIDTH = N_HEADS * HEAD_DIM
POOL_WIDTH = D_MODEL - ATTN_WIDTH
IN_WIDTH = 3 * ATTN_WIDTH + POOL_WIDTH
POOL_WINDOWS = (2, 4, 8, 16)
N_POOL_GROUPS = len(POOL_WINDOWS)
POOL_GROUP_WIDTH = POOL_WIDTH // N_POOL_GROUPS
POOL_BUF = max(POOL_WINDOWS) - 1
DILATED = ((128, 1), (512, 4), (2048, 16))
MAX_WINDOW = max(w for w, _ in DILATED)
QBLOCK = 128
NUM_BUCKETS = 32
MAX_DISTANCE = MAX_WINDOW
N_GROUPS = 4
EXPERTS_PER_GROUP = 4
N_EXPERTS = N_GROUPS * EXPERTS_PER_GROUP
TOP_K_INNER = 2
D_EXPERT = 512
EPS = 1e-6
SCALE = HEAD_DIM ** -0.5
NEG_INF = -1e30

kernel_name = 'hymba_pool_dilated_hmoe_step'


def _rms_norm(x, g):
    xf = x.astype(jnp.float32)
    y = xf * lax.rsqrt(jnp.mean(xf * xf, axis=-1, keepdims=True) + EPS)
    return (y * g.astype(jnp.float32)).astype(x.dtype)


def _rel_bucket(dist):
    max_exact = NUM_BUCKETS // 2
    df = jnp.maximum(dist, 1).astype(jnp.float32)
    large = max_exact + (jnp.log(df / max_exact) / math.log(MAX_DISTANCE / max_exact)
                         * (NUM_BUCKETS - max_exact)).astype(jnp.int32)
    large = jnp.minimum(large, NUM_BUCKETS - 1)
    return jnp.where(dist < max_exact, dist, large)


def _branch_bias(rel_bias, dilation, n_steps):
    j = jnp.arange(n_steps + 1, dtype=jnp.int32)
    return rel_bias[_rel_bucket(j * dilation)].astype(jnp.float32)


def _mixer_inputs(x, ln1_w, w_in, q_norm_w, k_norm_w):
    n, L, _ = x.shape
    z = _rms_norm(x, ln1_w) @ w_in
    q = z[..., :ATTN_WIDTH].reshape(n, L, N_HEADS, HEAD_DIM)
    k = z[..., ATTN_WIDTH:2 * ATTN_WIDTH].reshape(n, L, N_HEADS, HEAD_DIM)
    v = z[..., 2 * ATTN_WIDTH:3 * ATTN_WIDTH].reshape(n, L, N_HEADS, HEAD_DIM)
    u = z[..., 3 * ATTN_WIDTH:]
    return _rms_norm(q, q_norm_w), _rms_norm(k, k_norm_w), v, u


def _softmax_stats(scores, valid):
    s = jnp.where(valid, scores, NEG_INF)
    m = jnp.max(s, axis=-1, keepdims=True)
    p = jnp.exp(s - m)
    den = jnp.sum(p, axis=-1, keepdims=True)
    return p / den, m[..., 0] + jnp.log(den[..., 0])


def _dilated_prompt(q, k, v, bias_j, dilation, n_steps):
    b, s, h, hd = q.shape
    L = s // dilation
    n = b * dilation
    nb = -(-L // QBLOCK)
    lp = nb * QBLOCK

    def to_res(t):
        return t.reshape(b, L, dilation, h, hd).transpose(0, 2, 1, 3, 4).reshape(n, L, h, hd)

    qb = jnp.pad(to_res(q), ((0, 0), (0, lp - L), (0, 0), (0, 0))).reshape(n, nb, QBLOCK, h, hd)

    def key_windows(t):
        tp = jnp.pad(to_res(t), ((0, 0), (QBLOCK, lp - L), (0, 0), (0, 0)))
        tp = tp.reshape(n, nb + 1, QBLOCK, h, hd)
        return jnp.concatenate([tp[:, :-1], tp[:, 1:]], axis=2)

    kw, vw = key_windows(k), key_windows(v)
    qi = jnp.arange(QBLOCK)[:, None]
    kk = jnp.arange(2 * QBLOCK)[None, :]
    j = qi + QBLOCK - kk
    key_pos = jnp.arange(nb)[:, None, None] * QBLOCK - QBLOCK + kk[None]
    valid = ((j >= 0) & (j <= n_steps))[None] & (key_pos >= 0)
    bias = bias_j[jnp.clip(j, 0, n_steps)].transpose(2, 0, 1)
    scores = jnp.einsum('nbqhd,nbkhd->nbhqk', qb, kw,
                        preferred_element_type=jnp.float32) * SCALE + bias[None, None]
    p, lse = _softmax_stats(scores, valid[None, :, None])
    o = jnp.einsum('nbhqk,nbkhd->nbqhd', p, vw.astype(jnp.float32))
    lse = lse.transpose(0, 1, 3, 2)

    def from_res(t):
        t = t.reshape((n, lp) + t.shape[3:])[:, :L]
        t = t.reshape((b, dilation, L) + t.shape[2:]).swapaxes(1, 2)
        return t.reshape((b, s) + t.shape[3:])

    return from_res(o), from_res(lse)


def _dilated_sample(q, k_all, v_all, bias_j, dilation, n_steps, q_offset):
    T = q.shape[1]
    steps = jnp.arange(n_steps + 1)
    idx = (q_offset + jnp.arange(T))[:, None] - steps[None, :] * dilation
    valid = idx >= 0
    idx = jnp.maximum(idx, 0)
    kg = k_all[:, idx]
    vg = v_all[:, idx]
    scores = jnp.einsum('bthd,btjhd->bhtj', q, kg,
                        preferred_element_type=jnp.float32) * SCALE + bias_j.T[None, :, None, :]
    p, lse = _softmax_stats(scores, valid[None, None])
    o = jnp.einsum('bhtj,btjhd->bthd', p, vg.astype(jnp.float32))
    return o, lse.transpose(0, 2, 1)


def _merge_branches(results):
    o = jnp.stack([r[0] for r in results])
    lse = jnp.stack([r[1] for r in results])
    w = jax.nn.softmax(lse, axis=0)
    return jnp.sum(w[..., None] * o, axis=0)


def _pool_mix(u, prefix, start, w_pool, pool_scale):
    n, L, C = u.shape
    comb = jnp.concatenate([prefix.astype(u.dtype), u], axis=1)
    cs = jnp.cumsum(comb.astype(jnp.float32), axis=1)
    cs = jnp.concatenate([jnp.zeros((n, 1, C), jnp.float32), cs], axis=1)
    uf = u.astype(jnp.float32)
    pos = start + jnp.arange(L)
    parts = []
    for gi, w in enumerate(POOL_WINDOWS):
        sl = slice(gi * POOL_GROUP_WIDTH, (gi + 1) * POOL_GROUP_WIDTH)
        win_sum = cs[:, POOL_BUF + 1:POOL_BUF + 1 + L, sl] - cs[:, POOL_BUF + 1 - w:POOL_BUF + 1 - w + L, sl]
        cnt = jnp.minimum(pos + 1, w).astype(jnp.float32)[None, :, None]
        parts.append(win_sum / cnt - uf[..., sl])
    d = jnp.stack(parts, axis=2)
    y = jnp.einsum('nlgc,gce->nlge', d, w_pool.astype(jnp.float32)).reshape(n, L, C)
    y = y * pool_scale.astype(jnp.float32)
    return y.astype(u.dtype), comb[:, -POOL_BUF:]


def _hier_moe(h, w_rg, b_rg, w_re, b_re, w_gate, w_up, w_down):
    n, L, D = h.shape
    hf = h.reshape(n * L, D)
    lg = (hf @ w_rg).astype(jnp.float32) + b_rg.astype(jnp.float32)
    pg = jax.nn.softmax(lg, axis=-1)
    g_top = jnp.argmax(lg, axis=-1)
    p_top = jnp.take_along_axis(pg, g_top[:, None], axis=-1)[:, 0]
    le = ((hf @ w_re).astype(jnp.float32) + b_re.astype(jnp.float32)).reshape(-1, N_GROUPS, EXPERTS_PER_GROUP)
    le_g = jnp.take_along_axis(le, g_top[:, None, None], axis=1)[:, 0]
    top_v, top_i = lax.top_k(le_g, TOP_K_INNER)
    pe = jax.nn.softmax(top_v, axis=-1)
    eidx = g_top[:, None] * EXPERTS_PER_GROUP + top_i
    gates = jnp.sum(jax.nn.one_hot(eidx, N_EXPERTS, dtype=jnp.float32)
                    * (p_top[:, None] * pe)[..., None], axis=1)
    a = jnp.einsum('nd,edf->nef', hf, w_gate)
    b = jnp.einsum('nd,edf->nef', hf, w_up)
    hid = jax.nn.silu(a) * b * gates[..., None].astype(a.dtype)
    y = jnp.einsum('nef,efd->nd', hid, w_down)
    return y.reshape(n, L, D)


def _block_output(x, attn, pool, w_o, ln2_w, w_rg, b_rg, w_re, b_re, w_gate, w_up, w_down):
    n, L, _ = x.shape
    mix = jnp.concatenate([attn.reshape(n, L, ATTN_WIDTH).astype(x.dtype), pool.astype(x.dtype)], axis=-1)
    x = x + mix @ w_o
    return x + _hier_moe(_rms_norm(x, ln2_w), w_rg, b_rg, w_re, b_re, w_gate, w_up, w_down)


def setup_inputs(seed: int = 0) -> dict:
    key = jax.random.key(seed)
    ks = jax.random.split(key, 24)
    win_buf = min(MAX_WINDOW, PAST_LEN)
    nrm = jax.random.normal

    def w(k, shape, fan_in):
        return nrm(k, shape, jnp.float32) * fan_in ** -0.5

    return {
        'x_prompt': nrm(ks[0], (BATCH, SEQ, D_MODEL), jnp.float32),
        'x_sample': nrm(ks[1], (DEC_BATCH, DEC_SEQ, D_MODEL), jnp.float32),
        'cache_k': nrm(ks[2], (DEPTH, DEC_BATCH, win_buf, N_HEADS, HEAD_DIM), jnp.float32),
        'cache_v': nrm(ks[3], (DEPTH, DEC_BATCH, win_buf, N_HEADS, HEAD_DIM), jnp.float32),
        'state_pool': nrm(ks[4], (DEPTH, DEC_BATCH, POOL_BUF, POOL_WIDTH), jnp.float32),
        'rel_bias': 0.5 * nrm(ks[5], (NUM_BUCKETS, N_HEADS), jnp.float32),
        'ln1_w': 1.0 + 0.1 * nrm(ks[6], (DEPTH, D_MODEL), jnp.float32),
        'w_in': w(ks[7], (DEPTH, D_MODEL, IN_WIDTH), D_MODEL),
        'q_norm_w': 1.0 + 0.1 * nrm(ks[8], (DEPTH, HEAD_DIM), jnp.float32),
        'k_norm_w': 1.0 + 0.1 * nrm(ks[9], (DEPTH, HEAD_DIM), jnp.float32),
        'w_pool': w(ks[10], (DEPTH, N_POOL_GROUPS, POOL_GROUP_WIDTH, POOL_GROUP_WIDTH), POOL_GROUP_WIDTH),
        'pool_scale': 1.0 + 0.1 * nrm(ks[11], (DEPTH, POOL_WIDTH), jnp.float32),
        'w_o': w(ks[12], (DEPTH, D_MODEL, D_MODEL), D_MODEL),
        'ln2_w': 1.0 + 0.1 * nrm(ks[13], (DEPTH, D_MODEL), jnp.float32),
        'w_router_group': w(ks[14], (DEPTH, D_MODEL, N_GROUPS), D_MODEL),
        'b_router_group': 0.01 * nrm(ks[15], (DEPTH, N_GROUPS), jnp.float32),
        'w_router_expert': w(ks[16], (DEPTH, D_MODEL, N_EXPERTS), D_MODEL),
        'b_router_expert': 0.01 * nrm(ks[17], (DEPTH, N_EXPERTS), jnp.float32),
        'w_gate': w(ks[18], (DEPTH, N_EXPERTS, D_MODEL, D_EXPERT), D_MODEL),
        'w_up': w(ks[19], (DEPTH, N_EXPERTS, D_MODEL, D_EXPERT), D_MODEL),
        'w_down': w(ks[20], (DEPTH, N_EXPERTS, D_EXPERT, D_MODEL), D_EXPERT),
    }


def reference(x_prompt, x_sample, cache_k, cache_v, state_pool, rel_bias, ln1_w, w_in,
              q_norm_w, k_norm_w, w_pool, pool_scale, w_o, ln2_w, w_router_group,
              b_router_group, w_router_expert, b_router_expert, w_gate, w_up, w_down):
    biases = [_branch_bias(rel_bias, d, w // d) for (w, d) in DILATED]
    xp, xs = x_prompt, x_sample
    kp_l, vp_l, pp_l, ks_l, vs_l, ps_l = [], [], [], [], [], []
    for l in range(DEPTH):
        moe_w = (w_router_group[l], b_router_group[l], w_router_expert[l], b_router_expert[l],
                 w_gate[l], w_up[l], w_down[l])
        q, k, v, u = _mixer_inputs(xp, ln1_w[l], w_in[l], q_norm_w[l], k_norm_w[l])
        attn = _merge_branches([_dilated_prompt(q, k, v, bj, d, w // d)
                                for (w, d), bj in zip(DILATED, biases)])
        zeros_prefix = jnp.zeros((u.shape[0], POOL_BUF, u.shape[2]), u.dtype)
        pool, pbuf = _pool_mix(u, zeros_prefix, 0, w_pool[l], pool_scale[l])
        keep = min(MAX_WINDOW, xp.shape[1])
        kp_l.append(k[:, -keep:])
        vp_l.append(v[:, -keep:])
        pp_l.append(pbuf)
        xp = _block_output(xp, attn, pool, w_o[l], ln2_w[l], *moe_w)
        q, k, v, u = _mixer_inputs(xs, ln1_w[l], w_in[l], q_norm_w[l], k_norm_w[l])
        k_all = jnp.concatenate([cache_k[l].astype(k.dtype), k], axis=1)
        v_all = jnp.concatenate([cache_v[l].astype(v.dtype), v], axis=1)
        q_offset = cache_k.shape[2]
        attn = _merge_branches([_dilated_sample(q, k_all, v_all, bj, d, w // d, q_offset)
                                for (w, d), bj in zip(DILATED, biases)])
        pool, sbuf = _pool_mix(u, state_pool[l], PAST_LEN, w_pool[l], pool_scale[l])
        ks_l.append(k)
        vs_l.append(v)
        ps_l.append(sbuf)
        xs = _block_output(xs, attn, pool, w_o[l], ln2_w[l], *moe_w)
    return (xp, xs, jnp.stack(kp_l), jnp.stack(vp_l), jnp.stack(pp_l),
            jnp.stack(ks_l), jnp.stack(vs_l), jnp.stack(ps_l))
```

```python
import functools
import math

import numpy as np
import jax
import jax.numpy as jnp
from jax import lax
from jax.experimental import pallas as pl
from jax.experimental.pallas import tpu as pltpu

F32 = jnp.float32
BF16 = jnp.bfloat16

N_HEADS = 16
HEAD_DIM = 64
ATTN_WIDTH = N_HEADS * HEAD_DIM
POOL_WINDOWS = (2, 4, 8, 16)
POOL_GROUP_WIDTH = 256
POOL_BUF = max(POOL_WINDOWS) - 1
POOL_HALO = POOL_BUF + 1
DILATED = ((128, 1), (512, 4), (2048, 16))
N_STEPS = 128
QBLOCK = 128
NUM_BUCKETS = 32
MAX_DISTANCE = 2048
PAST_LEN = 16384
N_GROUPS = 4
EXPERTS_PER_GROUP = 4
N_EXPERTS = N_GROUPS * EXPERTS_PER_GROUP
TOP_K = 2
EPS = 1e-6
SCALE = HEAD_DIM ** -0.5
NEG_INF = -1e30
LANES = 128
MXU_DIM = 256
VMEM_LIMIT = 56 * 1024 * 1024


def _params(*sem):
    return pltpu.CompilerParams(dimension_semantics=sem, vmem_limit_bytes=VMEM_LIMIT)


def _const_spec(shape):
    zeros = (0,) * len(shape)
    return pl.BlockSpec(shape, lambda *_: zeros)


def _cast_kernel(x_ref, o_ref):
    o_ref[...] = x_ref[...].astype(o_ref.dtype)


def _to_bf16(w2d, block_rows):
    rows, cols = w2d.shape
    return pl.pallas_call(
        _cast_kernel,
        grid=(rows // block_rows,),
        in_specs=[pl.BlockSpec((block_rows, cols), lambda i: (i, 0))],
        out_specs=pl.BlockSpec((block_rows, cols), lambda i: (i, 0)),
        out_shape=jax.ShapeDtypeStruct((rows, cols), BF16),
        compiler_params=_params("parallel"),
        name="cast_bf16",
    )(w2d)


def _inproj_kernel(x_ref, ln_ref, w_ref, qg_ref, kg_ref, avg_ref,
                   q_ref, k_ref, kb_ref, v_ref, vb_ref, u_ref):
    x = x_ref[...]
    ms = jnp.mean(x * x, axis=-1, keepdims=True)
    h = (x * lax.rsqrt(ms + EPS) * ln_ref[...]).astype(BF16)

    def section(s):
        return jnp.dot(h, w_ref[:, s * ATTN_WIDTH:(s + 1) * ATTN_WIDTH],
                       preferred_element_type=F32)

    def head_norm(z, g_ref, c):
        zc = z[:, c * MXU_DIM:(c + 1) * MXU_DIM]
        msh = jnp.dot((zc * zc).astype(BF16), avg_ref[...], preferred_element_type=F32)
        return zc * lax.rsqrt(msh + EPS) * g_ref[:, c * MXU_DIM:(c + 1) * MXU_DIM]

    zq = section(0)
    for c in range(ATTN_WIDTH // MXU_DIM):
        q_ref[:, c * MXU_DIM:(c + 1) * MXU_DIM] = head_norm(zq, qg_ref, c).astype(BF16)
    zk = section(1)
    for c in range(ATTN_WIDTH // MXU_DIM):
        kn = head_norm(zk, kg_ref, c)
        k_ref[:, c * MXU_DIM:(c + 1) * MXU_DIM] = kn
        kb_ref[:, c * MXU_DIM:(c + 1) * MXU_DIM] = kn.astype(BF16)
    zv = section(2)
    v_ref[...] = zv
    vb_ref[...] = zv.astype(BF16)
    u_ref[...] = section(3)


def _inproj(x2d, ln1, w_in_b, q_gain, k_gain, avg, tm):
    n, d = x2d.shape
    row = lambda i: (i, 0)
    wide = pl.BlockSpec((tm, ATTN_WIDTH), row)
    f32o = jax.ShapeDtypeStruct((n, ATTN_WIDTH), F32)
    bf16o = jax.ShapeDtypeStruct((n, ATTN_WIDTH), BF16)
    return pl.pallas_call(
        _inproj_kernel,
        grid=(n // tm,),
        in_specs=[pl.BlockSpec((tm, d), row),
                  _const_spec((1, d)),
                  pl.BlockSpec(w_in_b.shape, lambda i: (0, 0), pipeline_mode=pl.Buffered(1)),
                  _const_spec((1, ATTN_WIDTH)), _const_spec((1, ATTN_WIDTH)),
                  _const_spec((MXU_DIM, MXU_DIM))],
        out_specs=[wide] * 6,
        out_shape=[bf16o, f32o, bf16o, f32o, bf16o, f32o],
        compiler_params=_params("parallel"),
        name="inproj",
    )(x2d, ln1, w_in_b, q_gain, k_gain, avg)


def _attn_kernel(q_ref, kp_ref, kc_ref, vp_ref, vc_ref, bias_ref, o_ref, lse_ref):
    i = pl.program_id(2)
    lane = lax.broadcasted_iota(jnp.int32, (QBLOCK, LANES), 1)
    nt = (((1,), (1,)), ((), ()))

    def run(with_prev):
        lse_tile = jnp.zeros((QBLOCK, LANES), F32)
        for h in range(N_HEADS):
            hs = slice(h * HEAD_DIM, (h + 1) * HEAD_DIM)
            qh = q_ref[:, hs]
            s_c = lax.dot_general(qh, kc_ref[:, hs], nt, preferred_element_type=F32)
            s_c = s_c + bias_ref[h, :, QBLOCK:]
            m = jnp.max(s_c, axis=-1, keepdims=True)
            if with_prev:
                s_p = lax.dot_general(qh, kp_ref[:, hs], nt, preferred_element_type=F32)
                s_p = s_p + bias_ref[h, :, :QBLOCK]
                m = jnp.maximum(m, jnp.max(s_p, axis=-1, keepdims=True))
            p_c = jnp.exp(s_c - m)
            den = jnp.sum(p_c, axis=-1, keepdims=True)
            acc = jnp.dot(p_c.astype(BF16), vc_ref[:, hs], preferred_element_type=F32)
            if with_prev:
                p_p = jnp.exp(s_p - m)
                den = den + jnp.sum(p_p, axis=-1, keepdims=True)
                acc = acc + jnp.dot(p_p.astype(BF16), vp_ref[:, hs], preferred_element_type=F32)
            o_ref[:, hs] = acc / den
            lse_tile = jnp.where(lane == h, m + jnp.log(den), lse_tile)
        lse_ref[...] = lse_tile

    @pl.when(i == 0)
    def _():
        run(False)

    @pl.when(i > 0)
    def _():
        run(True)


def _attn_branch(q, kb, vb, bias_m, batch, seq, dil):
    sub = seq // dil
    nblk = sub // QBLOCK
    view = lambda a: a.reshape(batch, sub, dil * a.shape[-1])
    cur = lambda b, r, i: (b, i, r)
    prev = lambda b, r, i: (b, jnp.maximum(i - 1, 0), r)
    wide_c = pl.BlockSpec((None, QBLOCK, ATTN_WIDTH), cur)
    wide_p = pl.BlockSpec((None, QBLOCK, ATTN_WIDTH), prev)
    o, lse = pl.pallas_call(
        _attn_kernel,
        grid=(batch, dil, nblk),
        in_specs=[wide_c, wide_p, wide_c, wide_p, wide_c, _const_spec(bias_m.shape)],
        out_specs=[wide_c, pl.BlockSpec((None, QBLOCK, LANES), cur)],
        out_shape=[jax.ShapeDtypeStruct((batch, sub, dil * ATTN_WIDTH), F32),
                   jax.ShapeDtypeStruct((batch, sub, dil * LANES), F32)],
        compiler_params=_params("parallel", "parallel", "parallel"),
        name=f"attn_d{dil}",
    )(view(q), view(kb), view(kb), view(vb), view(vb), bias_m)
    return o.reshape(batch * seq, ATTN_WIDTH), lse.reshape(batch * seq, LANES)


def _split_dot(a, b_bf16):
    hi = a.astype(BF16)
    lo = (a - hi.astype(F32)).astype(BF16)
    return (jnp.dot(hi, b_bf16, preferred_element_type=F32)
            + jnp.dot(lo, b_bf16, preferred_element_type=F32))


def _pool_groups(comb, u, cnt_fn, wp_ref, ps_ref):
    t = u.shape[0]
    outs = []
    run = comb
    width = 1
    for g, w in enumerate(POOL_WINDOWS):
        while width < w:
            run = run + pltpu.roll(run, width, 0)
            width *= 2
        cs = slice(g * POOL_GROUP_WIDTH, (g + 1) * POOL_GROUP_WIDTH)
        d = run[POOL_HALO:POOL_HALO + t, cs] / cnt_fn(w) - u[:, cs]
        y = jnp.dot(d.astype(BF16), wp_ref[g], preferred_element_type=F32)
        outs.append(y * ps_ref[:, cs])
    return outs


def _mix_kernel(o1_ref, o2_ref, o3_ref, l1_ref, l2_ref, l3_ref, u_ref, halo_ref,
                ex_ref, wp_ref, ps_ref, mix_ref, *, seq):
    tm = u_ref.shape[0]
    l1, l2, l3 = l1_ref[...], l2_ref[...], l3_ref[...]
    m = jnp.maximum(jnp.maximum(l1, l2), l3)
    e1, e2, e3 = jnp.exp(l1 - m), jnp.exp(l2 - m), jnp.exp(l3 - m)
    inv = 1.0 / (e1 + e2 + e3)
    ex = ex_ref[...]
    attn = (_split_dot(e1 * inv, ex) * o1_ref[...]
            + _split_dot(e2 * inv, ex) * o2_ref[...]
            + _split_dot(e3 * inv, ex) * o3_ref[...])
    mix_ref[:, :ATTN_WIDTH] = attn.astype(BF16)

    pos0 = (pl.program_id(0) * tm) % seq
    u = u_ref[...]
    halo = jnp.where(pos0 == 0, 0.0, halo_ref[...])
    comb = jnp.concatenate([halo, u], axis=0)
    pos = pos0 + lax.broadcasted_iota(jnp.int32, (tm, 1), 0)
    cnt_fn = lambda w: jnp.minimum(pos + 1, w).astype(F32)
    for g, y in enumerate(_pool_groups(comb, u, cnt_fn, wp_ref, ps_ref)):
        lo = ATTN_WIDTH + g * POOL_GROUP_WIDTH
        mix_ref[:, lo:lo + POOL_GROUP_WIDTH] = y.astype(BF16)


def _mix(o_list, l_list, u, expand, wp_b, pool_scale, seq, tm):
    n = u.shape[0]
    row = lambda i: (i, 0)
    wide = pl.BlockSpec((tm, ATTN_WIDTH), row)
    stat = pl.BlockSpec((tm, LANES), row)
    halo = pl.BlockSpec((POOL_HALO, ATTN_WIDTH),
                        lambda i: (jnp.maximum(i * (tm // POOL_HALO) - 1, 0), 0))
    return pl.pallas_call(
        functools.partial(_mix_kernel, seq=seq),
        grid=(n // tm,),
        in_specs=[wide, wide, wide, stat, stat, stat, wide, halo,
                  _const_spec(expand.shape), _const_spec(wp_b.shape),
                  _const_spec((1, ATTN_WIDTH))],
        out_specs=pl.BlockSpec((tm, 2 * ATTN_WIDTH), row),
        out_shape=jax.ShapeDtypeStruct((n, 2 * ATTN_WIDTH), BF16),
        compiler_params=_params("parallel"),
        name="mix",
    )(*o_list, *l_list, u, u, expand, wp_b, pool_scale)


def _sample_kernel(q_ref, kn_ref, vn_ref, u_ref, st_ref,
                   k1_ref, k4_ref, k16_ref, v1_ref, v4_ref, v16_ref,
                   bias_ref, ext_ref, ex_ref, wp_ref, ps_ref,
                   mix_ref, sbuf_ref, knew_ref, vnew_ref, comb_ref, *, start):
    t_new = q_ref.shape[0]
    knew_ref[...] = jnp.zeros_like(knew_ref)
    vnew_ref[...] = jnp.zeros_like(vnew_ref)
    knew_ref[:t_new, :] = kn_ref[...]
    vnew_ref[:t_new, :] = vn_ref[...]
    k_new = knew_ref[...]
    v_new = vnew_ref[...]
    hp = lax.Precision.HIGHEST
    k_refs = (k1_ref, k4_ref, k16_ref)
    v_refs = (v1_ref, v4_ref, v16_ref)
    for t in range(t_new):
        qt = q_ref[t:t + 1, :]
        probs, lses, vals = [], [], []
        for br in range(len(DILATED)):
            cs = slice(0, ATTN_WIDTH) if br == 0 else slice(t * ATTN_WIDTH, (t + 1) * ATTN_WIDTH)
            keys = jnp.concatenate([k_refs[br][:, cs], k_new], axis=0)
            vals.append(jnp.concatenate([v_refs[br][:, cs], v_new], axis=0))
            s = jnp.dot(keys * qt, ext_ref[...], precision=hp, preferred_element_type=F32)
            s = s + bias_ref[br, t]
            m = jnp.max(s, axis=0, keepdims=True)
            p = jnp.exp(s - m)
            den = jnp.sum(p, axis=0, keepdims=True)
            probs.append(p / den)
            lses.append(m + jnp.log(den))
        mm = jnp.maximum(jnp.maximum(lses[0], lses[1]), lses[2])
        es = [jnp.exp(l - mm) for l in lses]
        inv = 1.0 / (es[0] + es[1] + es[2])
        acc = jnp.zeros((1, ATTN_WIDTH), F32)
        for br in range(len(DILATED)):
            pw = jnp.dot(probs[br] * (es[br] * inv), ex_ref[...], precision=hp,
                         preferred_element_type=F32)
            acc = acc + jnp.sum(pw * vals[br], axis=0, keepdims=True)
        mix_ref[t:t + 1, :ATTN_WIDTH] = acc

    u = u_ref[...]
    comb_ref[...] = jnp.zeros_like(comb_ref)
    comb_ref[1:POOL_HALO, :] = st_ref[...]
    comb_ref[POOL_HALO:POOL_HALO + t_new, :] = u
    pos = start + lax.broadcasted_iota(jnp.int32, (t_new, 1), 0)
    cnt_fn = lambda w: jnp.minimum(pos + 1, w).astype(F32)
    for g, y in enumerate(_pool_groups(comb_ref[...], u, cnt_fn, wp_ref, ps_ref)):
        lo = ATTN_WIDTH + g * POOL_GROUP_WIDTH
        mix_ref[:, lo:lo + POOL_GROUP_WIDTH] = y
    sbuf_ref[:POOL_BUF - t_new, :] = st_ref[t_new:, :]
    sbuf_ref[POOL_BUF - t_new:, :] = u


def _sample_mixer(q, k, v, u, state, cache_k, cache_v, bias_s, expand_t, expand, wp_b,
                  pool_scale):
    nb, t_new, w = q.shape
    win = cache_k.shape[1]
    per = lambda shape, imap: pl.BlockSpec((None,) + shape, imap)
    b0 = lambda b: (b, 0, 0)
    tail1 = per((QBLOCK, w), lambda b: (b, win // QBLOCK - 1, 0))
    view4 = lambda c: c.reshape(nb, win // 4, 4 * w)
    tail4 = per((QBLOCK, 4 * w), lambda b: (b, win // 4 // QBLOCK - 1, 0))
    view16 = lambda c: c.reshape(nb, win // 16, 16 * w)
    head16 = per((QBLOCK, t_new * w), b0)
    new = per((t_new, w), b0)
    return pl.pallas_call(
        functools.partial(_sample_kernel, start=PAST_LEN),
        grid=(nb,),
        in_specs=[new, new, new, new, per((POOL_BUF, w), b0),
                  tail1, tail4, head16, tail1, tail4, head16,
                  _const_spec(bias_s.shape), _const_spec(expand_t.shape),
                  _const_spec(expand.shape), _const_spec(wp_b.shape), _const_spec((1, w))],
        out_specs=[per((t_new, 2 * w), b0), per((POOL_BUF, w), b0)],
        out_shape=[jax.ShapeDtypeStruct((nb, t_new, 2 * w), F32),
                   jax.ShapeDtypeStruct((nb, POOL_BUF, w), F32)],
        scratch_shapes=[pltpu.VMEM((8, w), F32), pltpu.VMEM((8, w), F32),
                        pltpu.VMEM((POOL_HALO + 8, w), F32)],
        compiler_params=_params("parallel"),
        name="sample_mixer",
    )(q, k, v, u, state, cache_k, view4(cache_k), view16(cache_k),
      cache_v, view4(cache_v), view16(cache_v), bias_s, expand_t, expand, wp_b, pool_scale)


def _outproj_kernel(x_ref, mix_ref, wo_ref, ln_ref, wrh_ref, wrl_ref, br_ref,
                    x1_ref, h_ref, gate_ref, eid_ref):
    x1 = x_ref[...] + jnp.dot(mix_ref[...].astype(BF16), wo_ref[...],
                              preferred_element_type=F32)
    x1_ref[...] = x1
    ms = jnp.mean(x1 * x1, axis=-1, keepdims=True)
    h = x1 * lax.rsqrt(ms + EPS) * ln_ref[...]
    h_ref[...] = h
    hi = h.astype(BF16)
    lo = (h - hi.astype(F32)).astype(BF16)
    lg = (jnp.dot(hi, wrh_ref[...], preferred_element_type=F32)
          + jnp.dot(lo, wrh_ref[...], preferred_element_type=F32)
          + jnp.dot(hi, wrl_ref[...], preferred_element_type=F32)) + br_ref[...]

    lane = lax.broadcasted_iota(jnp.int32, lg.shape, 1).astype(F32)
    big = float(LANES)
    row_max = lambda mask: jnp.max(jnp.where(mask, lg, -jnp.inf), axis=-1, keepdims=True)
    first = lambda mask: jnp.min(jnp.where(mask, lane, big), axis=-1, keepdims=True)
    is_g = lane < N_GROUPS
    mg = row_max(is_g)
    g_top = first(jnp.logical_and(is_g, lg == mg))
    den = jnp.sum(jnp.where(is_g, jnp.exp(lg - mg), 0.0), axis=-1, keepdims=True)
    p_top = 1.0 / den
    base = N_GROUPS + EXPERTS_PER_GROUP * g_top
    in_grp = jnp.logical_and(lane >= base, lane < base + EXPERTS_PER_GROUP)
    v1 = row_max(in_grp)
    i1 = first(jnp.logical_and(in_grp, lg == v1))
    rest = jnp.logical_and(in_grp, lane != i1)
    v2 = row_max(rest)
    i2 = first(jnp.logical_and(rest, lg == v2))
    e21 = jnp.exp(v2 - v1)
    s21 = 1.0 + e21
    gate1 = p_top * (1.0 / s21)
    gate2 = p_top * (e21 / s21)
    gate_ref[...] = jnp.where(lane == 0.0, gate1, jnp.where(lane == 1.0, gate2, 0.0))
    eid = jnp.where(lane == 0.0, i1 - N_GROUPS, jnp.where(lane == 1.0, i2 - N_GROUPS, 0.0))
    eid_ref[...] = eid.astype(jnp.int32)


def _outproj(x2d, mix, w_o_b, ln2, wr_hi, wr_lo, b_r, tm):
    n, d = x2d.shape
    row = lambda i: (i, 0)
    full = pl.BlockSpec((tm, d), row)
    stat = pl.BlockSpec((tm, LANES), row)
    return pl.pallas_call(
        _outproj_kernel,
        grid=(n // tm,),
        in_specs=[full, full,
                  pl.BlockSpec(w_o_b.shape, lambda i: (0, 0), pipeline_mode=pl.Buffered(1)),
                  _const_spec((1, d)), _const_spec(wr_hi.shape), _const_spec(wr_lo.shape),
                  _const_spec((1, LANES))],
        out_specs=[full, full, stat, stat],
        out_shape=[jax.ShapeDtypeStruct((n, d), F32), jax.ShapeDtypeStruct((n, d), F32),
                   jax.ShapeDtypeStruct((n, LANES), F32),
                   jax.ShapeDtypeStruct((n, LANES), jnp.int32)],
        compiler_params=_params("parallel"),
        name="outproj",
    )(x2d, mix, w_o_b, ln2, wr_hi, wr_lo, b_r)


def _row_copy(src_hbm, dst_vmem, sem, src_row, dst_row):
    return pltpu.make_async_copy(src_hbm.at[pl.ds(src_row, 1)],
                                 dst_vmem.at[pl.ds(dst_row, 1)], sem)


def _expert_kernel(tile_e_ref, tile_on_ref, src_ref, h_hbm, gate_ref, wg_ref, wu_ref, wd_ref,
                   y_ref, buf_ref, sem_ref):
    i = pl.program_id(0)
    n_tiles = pl.num_programs(0)
    tm = y_ref.shape[0]

    def issue(tile, slot):
        def body(r, c):
            _row_copy(h_hbm, buf_ref.at[slot], sem_ref.at[slot], src_ref[tile * tm + r], r).start()
            return c
        lax.fori_loop(0, tm, body, 0)

    def drain(slot):
        def body(r, c):
            _row_copy(h_hbm, buf_ref.at[slot], sem_ref.at[slot], 0, r).wait()
            return c
        lax.fori_loop(0, tm, body, 0)

    slot = i % 2

    @pl.when(i == 0)
    def _():
        issue(0, 0)

    drain(slot)

    @pl.when(i + 1 < n_tiles)
    def _():
        issue(i + 1, 1 - slot)

    @pl.when(tile_on_ref[i] == 1)
    def _():
        hb = buf_ref[slot].astype(BF16)
        a = jnp.dot(hb, wg_ref[...], preferred_element_type=F32)
        b = jnp.dot(hb, wu_ref[...], preferred_element_type=F32)
        hid = a * jax.nn.sigmoid(a) * b * gate_ref[...]
        y_ref[...] = jnp.dot(hid.astype(BF16), wd_ref[...], preferred_element_type=F32)

    @pl.when(tile_on_ref[i] == 0)
    def _():
        y_ref[...] = jnp.zeros_like(y_ref)


def _experts(tile_e, tile_on, src_tok, h_all, gate_sorted, wg_b, wu_b, wd_b, tm):
    rows = src_tok.shape[0]
    d = h_all.shape[1]
    f = wg_b.shape[2]
    grid_spec = pltpu.PrefetchScalarGridSpec(
        num_scalar_prefetch=3,
        grid=(rows // tm,),
        in_specs=[pl.BlockSpec(memory_space=pl.ANY),
                  pl.BlockSpec((tm, 1), lambda i, te, on, src: (i, 0)),
                  pl.BlockSpec((None, d, f), lambda i, te, on, src: (te[i], 0, 0)),
                  pl.BlockSpec((None, d, f), lambda i, te, on, src: (te[i], 0, 0)),
                  pl.BlockSpec((None, f, d), lambda i, te, on, src: (te[i], 0, 0))],
        out_specs=pl.BlockSpec((tm, d), lambda i, te, on, src: (i, 0)),
        scratch_shapes=[pltpu.VMEM((2, tm, d), F32), pltpu.SemaphoreType.DMA((2,))],
    )
    return pl.pallas_call(
        _expert_kernel,
        grid_spec=grid_spec,
        out_shape=jax.ShapeDtypeStruct((rows, d), F32),
        compiler_params=_params("arbitrary"),
        name="experts",
    )(tile_e, tile_on, src_tok, h_all, gate_sorted, wg_b, wu_b, wd_b)


def _combine_kernel(pos_ref, x1_ref, ys_hbm, y_ref, buf_ref, sem_ref):
    i = pl.program_id(0)
    n_tiles = pl.num_programs(0)
    tm = y_ref.shape[0]

    def issue(tile, slot):
        def body(r, c):
            for k in range(TOP_K):
                _row_copy(ys_hbm, buf_ref.at[slot, k], sem_ref.at[slot],
                          pos_ref[(tile * tm + r) * TOP_K + k], r).start()
            return c
        lax.fori_loop(0, tm, body, 0)

    def drain(slot):
        def body(r, c):
            for k in range(TOP_K):
                _row_copy(ys_hbm, buf_ref.at[slot, k], sem_ref.at[slot], 0, r).wait()
            return c
        lax.fori_loop(0, tm, body, 0)

    slot = i % 2

    @pl.when(i == 0)
    def _():
        issue(0, 0)

    drain(slot)

    @pl.when(i + 1 < n_tiles)
    def _():
        issue(i + 1, 1 - slot)

    y_ref[...] = x1_ref[...] + buf_ref[slot, 0] + buf_ref[slot, 1]


def _combine(pos_flat, x1, ys, tm):
    n, d = x1.shape
    grid_spec = pltpu.PrefetchScalarGridSpec(
        num_scalar_prefetch=1,
        grid=(n // tm,),
        in_specs=[pl.BlockSpec((tm, d), lambda i, pos: (i, 0)),
                  pl.BlockSpec(memory_space=pl.ANY)],
        out_specs=pl.BlockSpec((tm, d), lambda i, pos: (i, 0)),
        scratch_shapes=[pltpu.VMEM((2, TOP_K, tm, d), F32), pltpu.SemaphoreType.DMA((2,))],
    )
    return pl.pallas_call(
        _combine_kernel,
        grid_spec=grid_spec,
        out_shape=jax.ShapeDtypeStruct((n, d), F32),
        compiler_params=_params("arbitrary"),
        name="combine",
    )(pos_flat, x1, ys)


def _bucket_table(dilation):
    dist = np.arange(N_STEPS + 1, dtype=np.int64) * dilation
    max_exact = NUM_BUCKETS // 2
    df = np.maximum(dist, 1).astype(np.float32)
    large = max_exact + (np.log(df / np.float32(max_exact))
                         / np.float32(math.log(MAX_DISTANCE / max_exact))
                         * np.float32(NUM_BUCKETS - max_exact)).astype(np.int32)
    large = np.minimum(large, NUM_BUCKETS - 1)
    return np.where(dist < max_exact, dist, large).astype(np.int32)


def _prompt_bias(rel_bias, dilation):
    qi = np.arange(QBLOCK)[:, None]
    kk = np.arange(2 * QBLOCK)[None, :]
    j = qi + QBLOCK - kk
    valid = (j >= 0) & (j <= N_STEPS)
    bucket = _bucket_table(dilation)[np.clip(j, 0, N_STEPS)]
    vals = rel_bias.astype(F32)[bucket]
    vals = jnp.where(valid[:, :, None], vals, NEG_INF)
    return vals.transpose(2, 0, 1)


def _sample_bias(rel_bias, t_new):
    n_rows = QBLOCK + 8
    step = np.zeros((len(DILATED), t_new, n_rows), np.int64)
    valid = np.zeros((len(DILATED), t_new, n_rows), bool)
    n = np.arange(QBLOCK)
    tp = np.arange(8)
    for br, (_, dil) in enumerate(DILATED):
        for t in range(t_new):
            if dil == 1:
                j_c = QBLOCK + t - n
                j_n = t - tp
                v_n = (tp <= t) & (tp < t_new)
            else:
                j_c = QBLOCK - n
                j_n = np.zeros(8, np.int64)
                v_n = tp == t
            step[br, t] = np.concatenate([j_c, j_n])
            valid[br, t] = np.concatenate([(j_c >= 0) & (j_c <= N_STEPS), v_n])
    buckets = np.stack([_bucket_table(dil)[np.clip(step[br], 0, N_STEPS)]
                        for br, (_, dil) in enumerate(DILATED)])
    vals = rel_bias.astype(F32)[buckets]
    vals = jnp.where(valid[..., None], vals, NEG_INF)
    return jnp.pad(vals, ((0, 0), (0, 0), (0, 0), (0, LANES - N_HEADS)), constant_values=NEG_INF)


def _routing_plan(eid, gate, tm):
    n = eid.shape[0]
    flat_e = eid.reshape(-1)
    onehot = (flat_e[:, None] == jnp.arange(N_EXPERTS)[None, :]).astype(jnp.int32)
    csum = jnp.cumsum(onehot, axis=0)
    rank = jnp.take_along_axis(csum, flat_e[:, None], axis=1)[:, 0] - 1
    counts = csum[-1]
    padded = ((counts + tm - 1) // tm) * tm
    ends = jnp.cumsum(padded)
    offsets = ends - padded
    pos = offsets[flat_e] + rank
    rows = ((n * TOP_K + tm - 1) // tm + N_EXPERTS) * tm
    tok = jnp.arange(n * TOP_K, dtype=jnp.int32) // TOP_K
    src_tok = jnp.zeros((rows,), jnp.int32).at[pos].set(tok)
    gate_sorted = jnp.zeros((rows,), F32).at[pos].set(gate.reshape(-1))
    tile_start = jnp.arange(rows // tm, dtype=jnp.int32) * tm
    tile_e = jnp.searchsorted(ends, tile_start, side="right").astype(jnp.int32)
    tile_on = (tile_e < N_EXPERTS).astype(jnp.int32)
    tile_e = jnp.minimum(tile_e, N_EXPERTS - 1)
    return pos.astype(jnp.int32), src_tok, gate_sorted[:, None], tile_e, tile_on


TM_PROJ = 256
TM_EXPERT = 256
TM_COMBINE = 256


def kernel(x_prompt, x_sample, cache_k, cache_v, state_pool, rel_bias, ln1_w, w_in,
           q_norm_w, k_norm_w, w_pool, pool_scale, w_o, ln2_w, w_router_group,
           b_router_group, w_router_expert, b_router_expert, w_gate, w_up, w_down):
    depth = w_in.shape[0]
    assert depth == 1
    batch, seq, d_model = x_prompt.shape
    nb, t_new, _ = x_sample.shape
    win = cache_k.shape[2]
    f_exp = w_gate.shape[-1]

    w_in_b = _to_bf16(w_in[0], 256)
    w_o_b = _to_bf16(w_o[0], 256)
    wp_b = _to_bf16(w_pool[0].reshape(-1, POOL_GROUP_WIDTH), 256).reshape(w_pool.shape[1:])
    wg_b = _to_bf16(w_gate[0].reshape(-1, f_exp), 2048).reshape(w_gate.shape[1:])
    wu_b = _to_bf16(w_up[0].reshape(-1, f_exp), 2048).reshape(w_up.shape[1:])
    wd_b = _to_bf16(w_down[0].reshape(-1, d_model), 512).reshape(w_down.shape[1:])
    ln1 = ln1_w[0][None, :]
    ln2 = ln2_w[0][None, :]
    q_gain = jnp.tile(q_norm_w[0], N_HEADS)[None, :] * SCALE
    k_gain = jnp.tile(k_norm_w[0], N_HEADS)[None, :]
    ps = pool_scale[0][None, :]
    blk = np.arange(MXU_DIM) // HEAD_DIM
    avg = jnp.asarray((blk[:, None] == blk[None, :]) / HEAD_DIM, BF16)
    head_of_col = np.arange(ATTN_WIDTH) // HEAD_DIM
    expand_np = (np.arange(LANES)[:, None] == head_of_col[None, :])
    expand = jnp.asarray(expand_np, BF16)
    expand_f = jnp.asarray(expand_np, F32)
    expand_t = jnp.asarray(expand_np.T, F32)
    w_r = jnp.concatenate([w_router_group[0], w_router_expert[0]], axis=1)
    w_r = jnp.pad(w_r, ((0, 0), (0, LANES - w_r.shape[1])))
    wr_hi = w_r.astype(BF16)
    wr_lo = (w_r - wr_hi.astype(F32)).astype(BF16)
    b_r = jnp.pad(jnp.concatenate([b_router_group[0], b_router_expert[0]]),
                  (0, LANES - N_GROUPS - N_EXPERTS))[None, :]

    xp = x_prompt.reshape(batch * seq, d_model)
    q, k, kb, v, vb, u = _inproj(xp, ln1, w_in_b, q_gain, k_gain, avg, TM_PROJ)
    o_list, l_list = [], []
    for _, dil in DILATED:
        o, lse = _attn_branch(q, kb, vb, _prompt_bias(rel_bias, dil), batch, seq, dil)
        o_list.append(o)
        l_list.append(lse)
    mix_p = _mix(o_list, l_list, u, expand, wp_b, ps, seq, TM_PROJ)
    x1_p, h_p, gate_p, eid_p = _outproj(xp, mix_p, w_o_b, ln2, wr_hi, wr_lo, b_r, TM_PROJ)

    n_s = nb * t_new
    xs = x_sample.reshape(n_s, d_model)
    q_s, k_s, _, v_s, _, u_s = _inproj(xs, ln1, w_in_b, q_gain, k_gain, avg, n_s)
    shape_s = (nb, t_new, ATTN_WIDTH)
    mix_s, sbuf = _sample_mixer(
        q_s.astype(F32).reshape(shape_s), k_s.reshape(shape_s), v_s.reshape(shape_s),
        u_s.reshape(shape_s), state_pool[0], cache_k[0].reshape(nb, win, ATTN_WIDTH),
        cache_v[0].reshape(nb, win, ATTN_WIDTH), _sample_bias(rel_bias, t_new),
        expand_t, expand_f, wp_b, ps)
    x1_s, h_s, gate_s, eid_s = _outproj(xs, mix_s.reshape(n_s, d_model), w_o_b, ln2,
                                        wr_hi, wr_lo, b_r, n_s)

    h_all = jnp.concatenate([h_p, h_s], axis=0)
    eid = jnp.concatenate([eid_p[:, :TOP_K], eid_s[:, :TOP_K]], axis=0)
    gate = jnp.concatenate([gate_p[:, :TOP_K], gate_s[:, :TOP_K]], axis=0)
    pos, src_tok, gate_sorted, tile_e, tile_on = _routing_plan(eid, gate, TM_EXPERT)
    ys = _experts(tile_e, tile_on, src_tok, h_all, gate_sorted, wg_b, wu_b, wd_b, TM_EXPERT)
    n_p = batch * seq
    y_p = _combine(pos[:n_p * TOP_K], x1_p, ys, TM_COMBINE)
    y_s = _combine(pos[n_p * TOP_K:], x1_s, ys, n_s)

    keep = min(MAX_DISTANCE, seq)
    kv_shape = (batch, seq, N_HEADS, HEAD_DIM)
    return (y_p.reshape(batch, seq, d_model),
            y_s.reshape(nb, t_new, d_model),
            k.reshape(kv_shape)[None, :, -keep:],
            v.reshape(kv_shape)[None, :, -keep:],
            u.reshape(batch, seq, ATTN_WIDTH)[None, :, -POOL_BUF:],
            k_s.reshape(nb, t_new, N_HEADS, HEAD_DIM)[None],
            v_s.reshape(nb, t_new, N_HEADS, HEAD_DIM)[None],
            sbuf[None])
```

```python
import functools
import math

import numpy as np
import jax
import jax.numpy as jnp
from jax import lax
from jax.experimental import pallas as pl
from jax.experimental.pallas import tpu as pltpu

F32 = jnp.float32
BF16 = jnp.bfloat16

N_HEADS = 16
HEAD_DIM = 64
ATTN_WIDTH = N_HEADS * HEAD_DIM
POOL_WINDOWS = (2, 4, 8, 16)
POOL_GROUP_WIDTH = 256
POOL_BUF = max(POOL_WINDOWS) - 1
POOL_HALO = POOL_BUF + 1
DILATED = ((128, 1), (512, 4), (2048, 16))
N_STEPS = 128
QBLOCK = 128
NUM_BUCKETS = 32
MAX_DISTANCE = 2048
PAST_LEN = 16384
N_GROUPS = 4
EXPERTS_PER_GROUP = 4
N_EXPERTS = N_GROUPS * EXPERTS_PER_GROUP
TOP_K = 2
EPS = 1e-6
SCALE = HEAD_DIM ** -0.5
NEG_INF = -1e30
LANES = 128
MXU_DIM = 256
VMEM_LIMIT = 56 * 1024 * 1024


def _params(*sem):
    return pltpu.CompilerParams(dimension_semantics=sem, vmem_limit_bytes=VMEM_LIMIT)


def _const_spec(shape):
    zeros = (0,) * len(shape)
    return pl.BlockSpec(shape, lambda *_: zeros)


def _cast_kernel(x_ref, o_ref):
    o_ref[...] = x_ref[...].astype(o_ref.dtype)


def _to_bf16(w2d, block_rows):
    rows, cols = w2d.shape
    return pl.pallas_call(
        _cast_kernel,
        grid=(rows // block_rows,),
        in_specs=[pl.BlockSpec((block_rows, cols), lambda i: (i, 0))],
        out_specs=pl.BlockSpec((block_rows, cols), lambda i: (i, 0)),
        out_shape=jax.ShapeDtypeStruct((rows, cols), BF16),
        compiler_params=_params("parallel"),
        name="cast_bf16",
    )(w2d)


def _inproj_kernel(x_ref, ln_ref, w_ref, qg_ref, kg_ref, avg_ref, *refs, dilated):
    k_ref, v_ref, u_ref, q_ref, kb_ref, vb_ref = refs[:6]
    tm = x_ref.shape[0]
    x = x_ref[...]
    ms = jnp.mean(x * x, axis=-1, keepdims=True)
    h = (x * lax.rsqrt(ms + EPS) * ln_ref[...]).astype(BF16)
    n_chunks = ATTN_WIDTH // MXU_DIM

    def section(s):
        return jnp.dot(h, w_ref[:, s * ATTN_WIDTH:(s + 1) * ATTN_WIDTH],
                       preferred_element_type=F32)

    def head_norm(z, g_ref, c):
        zc = z[:, c * MXU_DIM:(c + 1) * MXU_DIM]
        msh = jnp.dot((zc * zc).astype(BF16), avg_ref[...], preferred_element_type=F32)
        return zc * lax.rsqrt(msh + EPS) * g_ref[:, c * MXU_DIM:(c + 1) * MXU_DIM]

    def emit(chunks, which, f32_ref, bf_ref):
        for c, zc in enumerate(chunks):
            cs = slice(c * MXU_DIM, (c + 1) * MXU_DIM)
            if f32_ref is not None:
                f32_ref[:, cs] = zc
            bf_ref[:, cs] = zc.astype(BF16)
        if not dilated:
            return
        stage_ref = refs[-1]
        for c, zc in enumerate(chunks):
            for half in range(MXU_DIM // LANES):
                stage_ref[2 * c + half] = zc[:, half * LANES:(half + 1) * LANES]
        for bi, (_, dil) in enumerate(DILATED[1:]):
            out_ref = refs[6 + 3 * bi + which]
            n = tm // dil
            for r in range(dil):
                for s in range(ATTN_WIDTH // LANES):
                    rows = stage_ref[s, pl.ds(r, n, stride=dil), :]
                    out_ref[r, :, s * LANES:(s + 1) * LANES] = rows.astype(BF16)

    zq = section(0)
    emit([head_norm(zq, qg_ref, c) for c in range(n_chunks)], 0, None, q_ref)
    zk = section(1)
    emit([head_norm(zk, kg_ref, c) for c in range(n_chunks)], 1, k_ref, kb_ref)
    zv = section(2)
    emit([zv[:, c * MXU_DIM:(c + 1) * MXU_DIM] for c in range(n_chunks)], 2, v_ref, vb_ref)
    u_ref[...] = section(3)


def _inproj(x2d, ln1, w_in_b, q_gain, k_gain, avg, tm, batch=None, seq=None):
    n, d = x2d.shape
    dilated = seq is not None
    row = lambda i: (i, 0)
    wide = pl.BlockSpec((tm, ATTN_WIDTH), row)
    f32o = jax.ShapeDtypeStruct((n, ATTN_WIDTH), F32)
    bf16o = jax.ShapeDtypeStruct((n, ATTN_WIDTH), BF16)
    out_specs = [wide] * 6
    out_shape = [f32o, f32o, f32o, bf16o, bf16o, bf16o]
    scratch = []
    if dilated:
        tiles = seq // tm
        for _, dil in DILATED[1:]:
            spec = pl.BlockSpec((None, dil, tm // dil, ATTN_WIDTH),
                                lambda i: (i // tiles, 0, i % tiles, 0))
            out_specs += [spec] * 3
            out_shape += [jax.ShapeDtypeStruct((batch, dil, seq // dil, ATTN_WIDTH), BF16)] * 3
        scratch = [pltpu.VMEM((ATTN_WIDTH // LANES, tm, LANES), F32)]
    return pl.pallas_call(
        functools.partial(_inproj_kernel, dilated=dilated),
        grid=(n // tm,),
        in_specs=[pl.BlockSpec((tm, d), row),
                  _const_spec((1, d)),
                  pl.BlockSpec(w_in_b.shape, lambda i: (0, 0), pipeline_mode=pl.Buffered(1)),
                  _const_spec((1, ATTN_WIDTH)), _const_spec((1, ATTN_WIDTH)),
                  _const_spec((MXU_DIM, MXU_DIM))],
        out_specs=out_specs,
        out_shape=out_shape,
        scratch_shapes=scratch,
        compiler_params=_params("parallel"),
        name="inproj",
    )(x2d, ln1, w_in_b, q_gain, k_gain, avg)


def _attn_kernel(q_ref, kp_ref, kc_ref, vp_ref, vc_ref, bias_ref, o_ref, lse_ref,
                 s_ref, p_ref):
    i = pl.program_id(2)
    lane = lax.broadcasted_iota(jnp.int32, (QBLOCK, LANES), 1)
    low = lane < HEAD_DIM
    keep_low = low.astype(F32).astype(BF16)
    keep_high = (1.0 - low.astype(F32)).astype(BF16)
    nt = (((1,), (1,)), ((), ()))
    n_pairs = N_HEADS // 2

    def run(with_prev):
        k0 = 0 if with_prev else QBLOCK
        nk = 2 * QBLOCK - k0
        ones = jnp.ones((nk, LANES), BF16)
        for hp in range(n_pairs):
            cs = slice(hp * LANES, (hp + 1) * LANES)
            qp = q_ref[:, cs]
            q2 = jnp.concatenate([qp * keep_low, qp * keep_high], axis=0)
            keys = kc_ref[:, cs]
            if with_prev:
                keys = jnp.concatenate([kp_ref[:, cs], keys], axis=0)
            s2 = lax.dot_general(q2, keys, nt, preferred_element_type=F32)
            s_ref[hp, :, k0:] = s2 + bias_ref[hp, :, k0:]

        m_all = jnp.zeros((QBLOCK, LANES), F32)
        for hp in range(n_pairs):
            for sub in range(2):
                rows = slice(sub * QBLOCK, (sub + 1) * QBLOCK)
                m = jnp.max(s_ref[hp, rows, k0:], axis=-1, keepdims=True)
                p_ref[hp, rows, k0:] = jnp.exp(s_ref[hp, rows, k0:] - m).astype(BF16)
                m_all = jnp.where(lane == 2 * hp + sub, m, m_all)

        den_all = jnp.ones((QBLOCK, LANES), F32)
        for hp in range(n_pairs):
            cs = slice(hp * LANES, (hp + 1) * LANES)
            vals = vc_ref[:, cs]
            if with_prev:
                vals = jnp.concatenate([vp_ref[:, cs], vals], axis=0)
            r = jnp.dot(p_ref[hp, :, k0:], jnp.concatenate([vals, ones], axis=1),
                        preferred_element_type=F32)
            den0, den1 = r[:QBLOCK, LANES:], r[QBLOCK:, LANES:]
            o_ref[:, cs] = jnp.where(low, r[:QBLOCK, :LANES] / den0, r[QBLOCK:, :LANES] / den1)
            den_all = jnp.where(lane == 2 * hp, den0, den_all)
            den_all = jnp.where(lane == 2 * hp + 1, den1, den_all)
        lse_ref[...] = m_all + jnp.log(den_all)

    @pl.when(i == 0)
    def _():
        run(False)

    @pl.when(i > 0)
    def _():
        run(True)


def _attn_branch(q, kb, vb, bias_all, branch):
    batch, dil, sub, _ = q.shape
    cur = lambda b, r, i: (b, r, i, 0)
    prev = lambda b, r, i: (b, r, jnp.maximum(i - 1, 0), 0)
    wide_c = pl.BlockSpec((None, None, QBLOCK, ATTN_WIDTH), cur)
    wide_p = pl.BlockSpec((None, None, QBLOCK, ATTN_WIDTH), prev)
    pairs = N_HEADS // 2
    bias_spec = pl.BlockSpec((None, pairs, 2 * QBLOCK, 2 * QBLOCK), lambda b, r, i: (branch, 0, 0, 0))
    return pl.pallas_call(
        _attn_kernel,
        grid=(batch, dil, sub // QBLOCK),
        in_specs=[wide_c, wide_p, wide_c, wide_p, wide_c, bias_spec],
        out_specs=[wide_c, pl.BlockSpec((None, None, QBLOCK, LANES), cur)],
        out_shape=[jax.ShapeDtypeStruct((batch, dil, sub, ATTN_WIDTH), F32),
                   jax.ShapeDtypeStruct((batch, dil, sub, LANES), F32)],
        scratch_shapes=[pltpu.VMEM((pairs, 2 * QBLOCK, 2 * QBLOCK), F32),
                        pltpu.VMEM((pairs, 2 * QBLOCK, 2 * QBLOCK), BF16)],
        compiler_params=_params("parallel", "parallel", "parallel"),
        name=f"attn_d{dil}",
    )(q, kb, kb, vb, vb, bias_all)


def _bias_kernel(rb_ref, bucket_ref, out_ref):
    bucket = bucket_ref[...]
    for h in range(N_HEADS):
        acc = jnp.full(bucket.shape, NEG_INF, F32)
        for b in range(NUM_BUCKETS):
            acc = jnp.where(bucket == b, rb_ref[b, h], acc)
        out_ref[h // 2, (h % 2) * QBLOCK:(h % 2 + 1) * QBLOCK, :] = acc


def _prompt_bias(rel_bias):
    qi = np.arange(QBLOCK)[:, None]
    kk = np.arange(2 * QBLOCK)[None, :]
    j = qi + QBLOCK - kk
    valid = (j >= 0) & (j <= N_STEPS)
    buckets = np.stack([np.where(valid, _bucket_table(dil)[np.clip(j, 0, N_STEPS)], -1)
                        for _, dil in DILATED]).astype(np.int32)
    pairs = N_HEADS // 2
    return pl.pallas_call(
        _bias_kernel,
        grid=(len(DILATED),),
        in_specs=[pl.BlockSpec(memory_space=pltpu.SMEM),
                  pl.BlockSpec((None, QBLOCK, 2 * QBLOCK), lambda g: (g, 0, 0))],
        out_specs=pl.BlockSpec((None, pairs, 2 * QBLOCK, 2 * QBLOCK), lambda g: (g, 0, 0, 0)),
        out_shape=jax.ShapeDtypeStruct((len(DILATED), pairs, 2 * QBLOCK, 2 * QBLOCK), F32),
        compiler_params=_params("parallel"),
        name="bias_table",
    )(rel_bias.astype(F32), jnp.asarray(buckets))


def _split_dot(a, b_bf16):
    hi = a.astype(BF16)
    lo = (a - hi.astype(F32)).astype(BF16)
    return (jnp.dot(hi, b_bf16, preferred_element_type=F32)
            + jnp.dot(lo, b_bf16, preferred_element_type=F32))


def _pool_groups(comb, u, cnt_fn, wp_ref, ps_ref):
    t = u.shape[0]
    outs = []
    run = comb
    width = 1
    for g, w in enumerate(POOL_WINDOWS):
        while width < w:
            run = run + pltpu.roll(run, width, 0)
            width *= 2
        cs = slice(g * POOL_GROUP_WIDTH, (g + 1) * POOL_GROUP_WIDTH)
        d = run[POOL_HALO:POOL_HALO + t, cs] / cnt_fn(w) - u[:, cs]
        y = jnp.dot(d.astype(BF16), wp_ref[g], preferred_element_type=F32)
        outs.append(y * ps_ref[:, cs])
    return outs


def _mix_kernel(o1_ref, o2_ref, o3_ref, l1_ref, l2_ref, l3_ref, u_ref, halo_ref,
                ex_ref, wp_ref, ps_ref, mix_ref, il_ref, ls_ref, *, seq):
    tm = u_ref.shape[0]
    n_slabs = ATTN_WIDTH // LANES
    lses = [l1_ref[0]]
    for bi, (o_ref, l_ref) in enumerate(((o2_ref, l2_ref), (o3_ref, l3_ref))):
        dil = o_ref.shape[0]
        n = tm // dil
        for r in range(dil):
            ls_ref[bi, pl.ds(r, n, stride=dil), :] = l_ref[r]
            for s in range(n_slabs):
                il_ref[bi, s, pl.ds(r, n, stride=dil), :] = o_ref[r, :, s * LANES:(s + 1) * LANES]
        lses.append(ls_ref[bi])
    l1, l2, l3 = lses
    m = jnp.maximum(jnp.maximum(l1, l2), l3)
    e1, e2, e3 = jnp.exp(l1 - m), jnp.exp(l2 - m), jnp.exp(l3 - m)
    inv = 1.0 / (e1 + e2 + e3)
    ex = ex_ref[...]
    w1, w2, w3 = (_split_dot(e * inv, ex) for e in (e1, e2, e3))
    for s in range(n_slabs):
        cs = slice(s * LANES, (s + 1) * LANES)
        attn = w1[:, cs] * o1_ref[0, :, cs] + w2[:, cs] * il_ref[0, s] + w3[:, cs] * il_ref[1, s]
        mix_ref[:, cs] = attn.astype(BF16)

    pos0 = (pl.program_id(0) * tm) % seq
    u = u_ref[...]
    halo = jnp.where(pos0 == 0, 0.0, halo_ref[...])
    comb = jnp.concatenate([halo, u], axis=0)
    pos = pos0 + lax.broadcasted_iota(jnp.int32, (tm, 1), 0)
    cnt_fn = lambda w: jnp.minimum(pos + 1, w).astype(F32)
    for g, y in enumerate(_pool_groups(comb, u, cnt_fn, wp_ref, ps_ref)):
        lo = ATTN_WIDTH + g * POOL_GROUP_WIDTH
        mix_ref[:, lo:lo + POOL_GROUP_WIDTH] = y.astype(BF16)


def _mix(o_list, l_list, u, expand, wp_b, pool_scale, seq, tm):
    n = u.shape[0]
    tiles = seq // tm
    row = lambda i: (i, 0)
    res = lambda i: (i // tiles, 0, i % tiles, 0)
    o_specs = [pl.BlockSpec((None, o.shape[1], tm // o.shape[1], ATTN_WIDTH), res) for o in o_list]
    l_specs = [pl.BlockSpec((None, l.shape[1], tm // l.shape[1], LANES), res) for l in l_list]
    halo = pl.BlockSpec((POOL_HALO, ATTN_WIDTH),
                        lambda i: (jnp.maximum(i * (tm // POOL_HALO) - 1, 0), 0))
    n_dilated = len(o_list) - 1
    return pl.pallas_call(
        functools.partial(_mix_kernel, seq=seq),
        grid=(n // tm,),
        in_specs=o_specs + l_specs + [pl.BlockSpec((tm, ATTN_WIDTH), row), halo,
                                      _const_spec(expand.shape), _const_spec(wp_b.shape),
                                      _const_spec((1, ATTN_WIDTH))],
        out_specs=pl.BlockSpec((tm, 2 * ATTN_WIDTH), row),
        out_shape=jax.ShapeDtypeStruct((n, 2 * ATTN_WIDTH), BF16),
        scratch_shapes=[pltpu.VMEM((n_dilated, ATTN_WIDTH // LANES, tm, LANES), F32),
                        pltpu.VMEM((n_dilated, tm, LANES), F32)],
        compiler_params=_params("parallel"),
        name="mix",
    )(*o_list, *l_list, u, u, expand, wp_b, pool_scale)


def _sample_kernel(q_ref, kn_ref, vn_ref, u_ref, st_ref,
                   k1_ref, k4_ref, k16_ref, v1_ref, v4_ref, v16_ref,
                   bias_ref, ext_ref, ex_ref, wp_ref, ps_ref,
                   mix_ref, sbuf_ref, knew_ref, vnew_ref, comb_ref, *, start):
    t_new = q_ref.shape[0]
    knew_ref[...] = jnp.zeros_like(knew_ref)
    vnew_ref[...] = jnp.zeros_like(vnew_ref)
    knew_ref[:t_new, :] = kn_ref[...]
    vnew_ref[:t_new, :] = vn_ref[...]
    k_new = knew_ref[...]
    v_new = vnew_ref[...]
    hp = lax.Precision.HIGHEST
    k_refs = (k1_ref, k4_ref, k16_ref)
    v_refs = (v1_ref, v4_ref, v16_ref)
    for t in range(t_new):
        qt = q_ref[t:t + 1, :]
        probs, lses, vals = [], [], []
        for br in range(len(DILATED)):
            cs = slice(0, ATTN_WIDTH) if br == 0 else slice(t * ATTN_WIDTH, (t + 1) * ATTN_WIDTH)
            keys = jnp.concatenate([k_refs[br][:, cs], k_new], axis=0)
            vals.append(jnp.concatenate([v_refs[br][:, cs], v_new], axis=0))
            s = jnp.dot(keys * qt, ext_ref[...], precision=hp, preferred_element_type=F32)
            s = s + bias_ref[br, t]
            m = jnp.max(s, axis=0, keepdims=True)
            p = jnp.exp(s - m)
            den = jnp.sum(p, axis=0, keepdims=True)
            probs.append(p / den)
            lses.append(m + jnp.log(den))
        mm = jnp.maximum(jnp.maximum(lses[0], lses[1]), lses[2])
        es = [jnp.exp(l - mm) for l in lses]
        inv = 1.0 / (es[0] + es[1] + es[2])
        acc = jnp.zeros((1, ATTN_WIDTH), F32)
        for br in range(len(DILATED)):
            pw = jnp.dot(probs[br] * (es[br] * inv), ex_ref[...], precision=hp,
                         preferred_element_type=F32)
            acc = acc + jnp.sum(pw * vals[br], axis=0, keepdims=True)
        mix_ref[t:t + 1, :ATTN_WIDTH] = acc

    u = u_ref[...]
    comb_ref[...] = jnp.zeros_like(comb_ref)
    comb_ref[1:POOL_HALO, :] = st_ref[...]
    comb_ref[POOL_HALO:POOL_HALO + t_new, :] = u
    pos = start + lax.broadcasted_iota(jnp.int32, (t_new, 1), 0)
    cnt_fn = lambda w: jnp.minimum(pos + 1, w).astype(F32)
    for g, y in enumerate(_pool_groups(comb_ref[...], u, cnt_fn, wp_ref, ps_ref)):
        lo = ATTN_WIDTH + g * POOL_GROUP_WIDTH
        mix_ref[:, lo:lo + POOL_GROUP_WIDTH] = y
    sbuf_ref[:POOL_BUF - t_new, :] = st_ref[t_new:, :]
    sbuf_ref[POOL_BUF - t_new:, :] = u


def _sample_mixer(q, k, v, u, state, cache_k, cache_v, bias_s, expand_t, expand, wp_b,
                  pool_scale):
    nb, t_new, w = q.shape
    win = cache_k.shape[1]
    per = lambda shape, imap: pl.BlockSpec((None,) + shape, imap)
    b0 = lambda b: (b, 0, 0)
    tail1 = per((QBLOCK, w), lambda b: (b, win // QBLOCK - 1, 0))
    view4 = lambda c: c.reshape(nb, win // 4, 4 * w)
    tail4 = per((QBLOCK, 4 * w), lambda b: (b, win // 4 // QBLOCK - 1, 0))
    view16 = lambda c: c.reshape(nb, win // 16, 16 * w)
    head16 = per((QBLOCK, t_new * w), b0)
    new = per((t_new, w), b0)
    return pl.pallas_call(
        functools.partial(_sample_kernel, start=PAST_LEN),
        grid=(nb,),
        in_specs=[new, new, new, new, per((POOL_BUF, w), b0),
                  tail1, tail4, head16, tail1, tail4, head16,
                  _const_spec(bias_s.shape), _const_spec(expand_t.shape),
                  _const_spec(expand.shape), _const_spec(wp_b.shape), _const_spec((1, w))],
        out_specs=[per((t_new, 2 * w), b0), per((POOL_BUF, w), b0)],
        out_shape=[jax.ShapeDtypeStruct((nb, t_new, 2 * w), F32),
                   jax.ShapeDtypeStruct((nb, POOL_BUF, w), F32)],
        scratch_shapes=[pltpu.VMEM((8, w), F32), pltpu.VMEM((8, w), F32),
                        pltpu.VMEM((POOL_HALO + 8, w), F32)],
        compiler_params=_params("parallel"),
        name="sample_mixer",
    )(q, k, v, u, state, cache_k, view4(cache_k), view16(cache_k),
      cache_v, view4(cache_v), view16(cache_v), bias_s, expand_t, expand, wp_b, pool_scale)


def _outproj_kernel(x_ref, mix_ref, wo_ref, ln_ref, wrh_ref, wrl_ref, br_ref,
                    x1_ref, h_ref, gate_ref, eid_ref):
    x1 = x_ref[...] + jnp.dot(mix_ref[...].astype(BF16), wo_ref[...],
                              preferred_element_type=F32)
    x1_ref[...] = x1
    ms = jnp.mean(x1 * x1, axis=-1, keepdims=True)
    h = x1 * lax.rsqrt(ms + EPS) * ln_ref[...]
    h_ref[...] = h
    hi = h.astype(BF16)
    lo = (h - hi.astype(F32)).astype(BF16)
    lg = (jnp.dot(hi, wrh_ref[...], preferred_element_type=F32)
          + jnp.dot(lo, wrh_ref[...], preferred_element_type=F32)
          + jnp.dot(hi, wrl_ref[...], preferred_element_type=F32)) + br_ref[...]

    lane = lax.broadcasted_iota(jnp.int32, lg.shape, 1).astype(F32)
    big = float(LANES)
    row_max = lambda mask: jnp.max(jnp.where(mask, lg, -jnp.inf), axis=-1, keepdims=True)
    first = lambda mask: jnp.min(jnp.where(mask, lane, big), axis=-1, keepdims=True)
    is_g = lane < N_GROUPS
    mg = row_max(is_g)
    g_top = first(jnp.logical_and(is_g, lg == mg))
    den = jnp.sum(jnp.where(is_g, jnp.exp(lg - mg), 0.0), axis=-1, keepdims=True)
    p_top = 1.0 / den
    base = N_GROUPS + EXPERTS_PER_GROUP * g_top
    in_grp = jnp.logical_and(lane >= base, lane < base + EXPERTS_PER_GROUP)
    v1 = row_max(in_grp)
    i1 = first(jnp.logical_and(in_grp, lg == v1))
    rest = jnp.logical_and(in_grp, lane != i1)
    v2 = row_max(rest)
    i2 = first(jnp.logical_and(rest, lg == v2))
    e21 = jnp.exp(v2 - v1)
    s21 = 1.0 + e21
    gate1 = p_top * (1.0 / s21)
    gate2 = p_top * (e21 / s21)
    gate_ref[...] = jnp.where(lane == 0.0, gate1, jnp.where(lane == 1.0, gate2, 0.0))
    eid = jnp.where(lane == 0.0, i1 - N_GROUPS, jnp.where(lane == 1.0, i2 - N_GROUPS, 0.0))
    eid_ref[...] = eid.astype(jnp.int32)


def _outproj(x2d, mix, w_o_b, ln2, wr_hi, wr_lo, b_r, tm):
    n, d = x2d.shape
    row = lambda i: (i, 0)
    full = pl.BlockSpec((tm, d), row)
    stat = pl.BlockSpec((tm, LANES), row)
    return pl.pallas_call(
        _outproj_kernel,
        grid=(n // tm,),
        in_specs=[full, full,
                  pl.BlockSpec(w_o_b.shape, lambda i: (0, 0), pipeline_mode=pl.Buffered(1)),
                  _const_spec((1, d)), _const_spec(wr_hi.shape), _const_spec(wr_lo.shape),
                  _const_spec((1, LANES))],
        out_specs=[full, full, stat, stat],
        out_shape=[jax.ShapeDtypeStruct((n, d), F32), jax.ShapeDtypeStruct((n, d), F32),
                   jax.ShapeDtypeStruct((n, LANES), F32),
                   jax.ShapeDtypeStruct((n, LANES), jnp.int32)],
        compiler_params=_params("parallel"),
        name="outproj",
    )(x2d, mix, w_o_b, ln2, wr_hi, wr_lo, b_r)


def _row_copy(src_hbm, dst_vmem, sem, src_row, dst_row):
    return pltpu.make_async_copy(src_hbm.at[pl.ds(src_row, 1)],
                                 dst_vmem.at[pl.ds(dst_row, 1)], sem)


def _expert_kernel(tile_e_ref, tile_on_ref, src_ref, h_hbm, gate_ref, wg_ref, wu_ref, wd_ref,
                   y_ref, buf_ref, sem_ref):
    i = pl.program_id(0)
    n_tiles = pl.num_programs(0)
    tm = y_ref.shape[0]

    def issue(tile, slot):
        def body(r, c):
            _row_copy(h_hbm, buf_ref.at[slot], sem_ref.at[slot], src_ref[tile * tm + r], r).start()
            return c
        lax.fori_loop(0, tm, body, 0)

    def drain(slot):
        def body(r, c):
            _row_copy(h_hbm, buf_ref.at[slot], sem_ref.at[slot], 0, r).wait()
            return c
        lax.fori_loop(0, tm, body, 0)

    slot = i % 2

    @pl.when(i == 0)
    def _():
        issue(0, 0)

    drain(slot)

    @pl.when(i + 1 < n_tiles)
    def _():
        issue(i + 1, 1 - slot)

    @pl.when(tile_on_ref[i] == 1)
    def _():
        hb = buf_ref[slot].astype(BF16)
        a = jnp.dot(hb, wg_ref[...], preferred_element_type=F32)
        b = jnp.dot(hb, wu_ref[...], preferred_element_type=F32)
        hid = a * jax.nn.sigmoid(a) * b * gate_ref[...]
        y_ref[...] = jnp.dot(hid.astype(BF16), wd_ref[...], preferred_element_type=F32)

    @pl.when(tile_on_ref[i] == 0)
    def _():
        y_ref[...] = jnp.zeros_like(y_ref)


def _experts(tile_e, tile_on, src_tok, h_all, gate_sorted, wg_b, wu_b, wd_b, tm):
    rows = src_tok.shape[0]
    d = h_all.shape[1]
    f = wg_b.shape[2]
    grid_spec = pltpu.PrefetchScalarGridSpec(
        num_scalar_prefetch=3,
        grid=(rows // tm,),
        in_specs=[pl.BlockSpec(memory_space=pl.ANY),
                  pl.BlockSpec((tm, 1), lambda i, te, on, src: (i, 0)),
                  pl.BlockSpec((None, d, f), lambda i, te, on, src: (te[i], 0, 0)),
                  pl.BlockSpec((None, d, f), lambda i, te, on, src: (te[i], 0, 0)),
                  pl.BlockSpec((None, f, d), lambda i, te, on, src: (te[i], 0, 0))],
        out_specs=pl.BlockSpec((tm, d), lambda i, te, on, src: (i, 0)),
        scratch_shapes=[pltpu.VMEM((2, tm, d), F32), pltpu.SemaphoreType.DMA((2,))],
    )
    return pl.pallas_call(
        _expert_kernel,
        grid_spec=grid_spec,
        out_shape=jax.ShapeDtypeStruct((rows, d), F32),
        compiler_params=_params("arbitrary"),
        name="experts",
    )(tile_e, tile_on, src_tok, h_all, gate_sorted, wg_b, wu_b, wd_b)


def _combine_kernel(pos_ref, x1_ref, ys_hbm, y_ref, buf_ref, sem_ref):
    i = pl.program_id(0)
    n_tiles = pl.num_programs(0)
    tm = y_ref.shape[0]

    def issue(tile, slot):
        def body(r, c):
            for k in range(TOP_K):
                _row_copy(ys_hbm, buf_ref.at[slot, k], sem_ref.at[slot],
                          pos_ref[(tile * tm + r) * TOP_K + k], r).start()
            return c
        lax.fori_loop(0, tm, body, 0)

    def drain(slot):
        def body(r, c):
            for k in range(TOP_K):
                _row_copy(ys_hbm, buf_ref.at[slot, k], sem_ref.at[slot], 0, r).wait()
            return c
        lax.fori_loop(0, tm, body, 0)

    slot = i % 2

    @pl.when(i == 0)
    def _():
        issue(0, 0)

    drain(slot)

    @pl.when(i + 1 < n_tiles)
    def _():
        issue(i + 1, 1 - slot)

    y_ref[...] = x1_ref[...] + buf_ref[slot, 0] + buf_ref[slot, 1]


def _combine(pos_flat, x1, ys, tm):
    n, d = x1.shape
    grid_spec = pltpu.PrefetchScalarGridSpec(
        num_scalar_prefetch=1,
        grid=(n // tm,),
        in_specs=[pl.BlockSpec((tm, d), lambda i, pos: (i, 0)),
                  pl.BlockSpec(memory_space=pl.ANY)],
        out_specs=pl.BlockSpec((tm, d), lambda i, pos: (i, 0)),
        scratch_shapes=[pltpu.VMEM((2, TOP_K, tm, d), F32), pltpu.SemaphoreType.DMA((2,))],
    )
    return pl.pallas_call(
        _combine_kernel,
        grid_spec=grid_spec,
        out_shape=jax.ShapeDtypeStruct((n, d), F32),
        compiler_params=_params("arbitrary"),
        name="combine",
    )(pos_flat, x1, ys)


def _bucket_table(dilation):
    dist = np.arange(N_STEPS + 1, dtype=np.int64) * dilation
    max_exact = NUM_BUCKETS // 2
    df = np.maximum(dist, 1).astype(np.float32)
    large = max_exact + (np.log(df / np.float32(max_exact))
                         / np.float32(math.log(MAX_DISTANCE / max_exact))
                         * np.float32(NUM_BUCKETS - max_exact)).astype(np.int32)
    large = np.minimum(large, NUM_BUCKETS - 1)
    return np.where(dist < max_exact, dist, large).astype(np.int32)


def _sample_bias(rel_bias, t_new):
    n_rows = QBLOCK + 8
    step = np.zeros((len(DILATED), t_new, n_rows), np.int64)
    valid = np.zeros((len(DILATED), t_new, n_rows), bool)
    n = np.arange(QBLOCK)
    tp = np.arange(8)
    for br, (_, dil) in enumerate(DILATED):
        for t in range(t_new):
            if dil == 1:
                j_c = QBLOCK + t - n
                j_n = t - tp
                v_n = (tp <= t) & (tp < t_new)
            else:
                j_c = QBLOCK - n
                j_n = np.zeros(8, np.int64)
                v_n = tp == t
            step[br, t] = np.concatenate([j_c, j_n])
            valid[br, t] = np.concatenate([(j_c >= 0) & (j_c <= N_STEPS), v_n])
    buckets = np.stack([_bucket_table(dil)[np.clip(step[br], 0, N_STEPS)]
                        for br, (_, dil) in enumerate(DILATED)])
    vals = rel_bias.astype(F32)[buckets]
    vals = jnp.where(valid[..., None], vals, NEG_INF)
    return jnp.pad(vals, ((0, 0), (0, 0), (0, 0), (0, LANES - N_HEADS)), constant_values=NEG_INF)


def _routing_plan(eid, gate, tm):
    n = eid.shape[0]
    flat_e = eid.reshape(-1)
    onehot = (flat_e[:, None] == jnp.arange(N_EXPERTS)[None, :]).astype(jnp.int32)
    csum = jnp.cumsum(onehot, axis=0)
    rank = jnp.take_along_axis(csum, flat_e[:, None], axis=1)[:, 0] - 1
    counts = csum[-1]
    padded = ((counts + tm - 1) // tm) * tm
    ends = jnp.cumsum(padded)
    offsets = ends - padded
    pos = offsets[flat_e] + rank
    rows = ((n * TOP_K + tm - 1) // tm + N_EXPERTS) * tm
    tok = jnp.arange(n * TOP_K, dtype=jnp.int32) // TOP_K
    src_tok = jnp.zeros((rows,), jnp.int32).at[pos].set(tok)
    gate_sorted = jnp.zeros((rows,), F32).at[pos].set(gate.reshape(-1))
    tile_start = jnp.arange(rows // tm, dtype=jnp.int32) * tm
    tile_e = jnp.searchsorted(ends, tile_start, side="right").astype(jnp.int32)
    tile_on = (tile_e < N_EXPERTS).astype(jnp.int32)
    tile_e = jnp.minimum(tile_e, N_EXPERTS - 1)
    return pos.astype(jnp.int32), src_tok, gate_sorted[:, None], tile_e, tile_on


TM_PROJ = 256
TM_EXPERT = 256
TM_COMBINE = 256


def kernel(x_prompt, x_sample, cache_k, cache_v, state_pool, rel_bias, ln1_w, w_in,
           q_norm_w, k_norm_w, w_pool, pool_scale, w_o, ln2_w, w_router_group,
           b_router_group, w_router_expert, b_router_expert, w_gate, w_up, w_down):
    depth = w_in.shape[0]
    assert depth == 1
    batch, seq, d_model = x_prompt.shape
    nb, t_new, _ = x_sample.shape
    win = cache_k.shape[2]
    f_exp = w_gate.shape[-1]

    w_in_b = _to_bf16(w_in[0], 256)
    w_o_b = _to_bf16(w_o[0], 256)
    wp_b = _to_bf16(w_pool[0].reshape(-1, POOL_GROUP_WIDTH), 256).reshape(w_pool.shape[1:])
    wg_b = _to_bf16(w_gate[0].reshape(-1, f_exp), 2048).reshape(w_gate.shape[1:])
    wu_b = _to_bf16(w_up[0].reshape(-1, f_exp), 2048).reshape(w_up.shape[1:])
    wd_b = _to_bf16(w_down[0].reshape(-1, d_model), 512).reshape(w_down.shape[1:])
    ln1 = ln1_w[0][None, :]
    ln2 = ln2_w[0][None, :]
    q_gain = jnp.tile(q_norm_w[0], N_HEADS)[None, :] * SCALE
    k_gain = jnp.tile(k_norm_w[0], N_HEADS)[None, :]
    ps = pool_scale[0][None, :]
    blk = np.arange(MXU_DIM) // HEAD_DIM
    avg = jnp.asarray((blk[:, None] == blk[None, :]) / HEAD_DIM, BF16)
    head_of_col = np.arange(ATTN_WIDTH) // HEAD_DIM
    expand_np = (np.arange(LANES)[:, None] == head_of_col[None, :])
    expand = jnp.asarray(expand_np, BF16)
    expand_f = jnp.asarray(expand_np, F32)
    expand_t = jnp.asarray(expand_np.T, F32)
    w_r = jnp.concatenate([w_router_group[0], w_router_expert[0]], axis=1)
    w_r = jnp.pad(w_r, ((0, 0), (0, LANES - w_r.shape[1])))
    wr_hi = w_r.astype(BF16)
    wr_lo = (w_r - wr_hi.astype(F32)).astype(BF16)
    b_r = jnp.pad(jnp.concatenate([b_router_group[0], b_router_expert[0]]),
                  (0, LANES - N_GROUPS - N_EXPERTS))[None, :]

    xp = x_prompt.reshape(batch * seq, d_model)
    proj = _inproj(xp, ln1, w_in_b, q_gain, k_gain, avg, TM_PROJ, batch, seq)
    k, v, u = proj[:3]
    natural = tuple(a.reshape(batch, 1, seq, ATTN_WIDTH) for a in proj[3:6])
    qkv = [natural] + [tuple(proj[6 + 3 * bi:9 + 3 * bi]) for bi in range(len(DILATED) - 1)]
    bias_all = _prompt_bias(rel_bias)
    o_list, l_list = [], []
    for branch, (qd, kd, vd) in enumerate(qkv):
        o, lse = _attn_branch(qd, kd, vd, bias_all, branch)
        o_list.append(o)
        l_list.append(lse)
    mix_p = _mix(o_list, l_list, u, expand, wp_b, ps, seq, TM_PROJ)
    x1_p, h_p, gate_p, eid_p = _outproj(xp, mix_p, w_o_b, ln2, wr_hi, wr_lo, b_r, TM_PROJ)

    n_s = nb * t_new
    xs = x_sample.reshape(n_s, d_model)
    k_s, v_s, u_s, q_s, _, _ = _inproj(xs, ln1, w_in_b, q_gain, k_gain, avg, n_s)
    shape_s = (nb, t_new, ATTN_WIDTH)
    mix_s, sbuf = _sample_mixer(
        q_s.astype(F32).reshape(shape_s), k_s.reshape(shape_s), v_s.reshape(shape_s),
        u_s.reshape(shape_s), state_pool[0], cache_k[0].reshape(nb, win, ATTN_WIDTH),
        cache_v[0].reshape(nb, win, ATTN_WIDTH), _sample_bias(rel_bias, t_new),
        expand_t, expand_f, wp_b, ps)
    x1_s, h_s, gate_s, eid_s = _outproj(xs, mix_s.reshape(n_s, d_model), w_o_b, ln2,
                                        wr_hi, wr_lo, b_r, n_s)

    h_all = jnp.concatenate([h_p, h_s], axis=0)
    eid = jnp.concatenate([eid_p[:, :TOP_K], eid_s[:, :TOP_K]], axis=0)
    gate = jnp.concatenate([gate_p[:, :TOP_K], gate_s[:, :TOP_K]], axis=0)
    pos, src_tok, gate_sorted, tile_e, tile_on = _routing_plan(eid, gate, TM_EXPERT)
    ys = _experts(tile_e, tile_on, src_tok, h_all, gate_sorted, wg_b, wu_b, wd_b, TM_EXPERT)
    n_p = batch * seq
    y_p = _combine(pos[:n_p * TOP_K], x1_p, ys, TM_COMBINE)
    y_s = _combine(pos[n_p * TOP_K:], x1_s, ys, n_s)

    keep = min(MAX_DISTANCE, seq)
    kv_shape = (batch, seq, N_HEADS, HEAD_DIM)
    return (y_p.reshape(batch, seq, d_model),
            y_s.reshape(nb, t_new, d_model),
            k.reshape(kv_shape)[None, :, -keep:],
            v.reshape(kv_shape)[None, :, -keep:],
            u.reshape(batch, seq, ATTN_WIDTH)[None, :, -POOL_BUF:],
            k_s.reshape(nb, t_new, N_HEADS, HEAD_DIM)[None],
            v_s.reshape(nb, t_new, N_HEADS, HEAD_DIM)[None],
            sbuf[None])
```

```python
import functools
import math

import numpy as np
import jax
import jax.numpy as jnp
from jax import lax
from jax.experimental import pallas as pl
from jax.experimental.pallas import tpu as pltpu

F32 = jnp.float32
BF16 = jnp.bfloat16

N_HEADS = 16
HEAD_DIM = 64
ATTN_WIDTH = N_HEADS * HEAD_DIM
POOL_WINDOWS = (2, 4, 8, 16)
POOL_GROUP_WIDTH = 256
POOL_BUF = max(POOL_WINDOWS) - 1
POOL_HALO = POOL_BUF + 1
DILATED = ((128, 1), (512, 4), (2048, 16))
N_STEPS = 128
QBLOCK = 128
NUM_BUCKETS = 32
MAX_DISTANCE = 2048
PAST_LEN = 16384
N_GROUPS = 4
EXPERTS_PER_GROUP = 4
N_EXPERTS = N_GROUPS * EXPERTS_PER_GROUP
TOP_K = 2
EPS = 1e-6
SCALE = HEAD_DIM ** -0.5
NEG_INF = -1e30
LANES = 128
MXU_DIM = 256
VMEM_LIMIT = 56 * 1024 * 1024


def _params(*sem):
    return pltpu.CompilerParams(dimension_semantics=sem, vmem_limit_bytes=VMEM_LIMIT)


def _const_spec(shape):
    zeros = (0,) * len(shape)
    return pl.BlockSpec(shape, lambda *_: zeros)


def _cast_kernel(x_ref, o_ref):
    o_ref[...] = x_ref[...].astype(o_ref.dtype)


def _to_bf16(w2d, block_rows):
    rows, cols = w2d.shape
    return pl.pallas_call(
        _cast_kernel,
        grid=(rows // block_rows,),
        in_specs=[pl.BlockSpec((block_rows, cols), lambda i: (i, 0))],
        out_specs=pl.BlockSpec((block_rows, cols), lambda i: (i, 0)),
        out_shape=jax.ShapeDtypeStruct((rows, cols), BF16),
        compiler_params=_params("parallel"),
        name="cast_bf16",
    )(w2d)


def _inproj_kernel(x_ref, ln_ref, w_ref, qg_ref, kg_ref, avg_ref, *refs, dilated):
    k_ref, v_ref, u_ref, q_ref, kb_ref, vb_ref = refs[:6]
    tm = x_ref.shape[0]
    x = x_ref[...]
    ms = jnp.mean(x * x, axis=-1, keepdims=True)
    h = (x * lax.rsqrt(ms + EPS) * ln_ref[...]).astype(BF16)
    n_chunks = ATTN_WIDTH // MXU_DIM

    def section(s):
        return jnp.dot(h, w_ref[:, s * ATTN_WIDTH:(s + 1) * ATTN_WIDTH],
                       preferred_element_type=F32)

    def head_norm(z, g_ref, c):
        zc = z[:, c * MXU_DIM:(c + 1) * MXU_DIM]
        msh = jnp.dot((zc * zc).astype(BF16), avg_ref[...], preferred_element_type=F32)
        return zc * lax.rsqrt(msh + EPS) * g_ref[:, c * MXU_DIM:(c + 1) * MXU_DIM]

    def emit(chunks, which, f32_ref, bf_ref):
        for c, zc in enumerate(chunks):
            cs = slice(c * MXU_DIM, (c + 1) * MXU_DIM)
            if f32_ref is not None:
                f32_ref[:, cs] = zc
            bf_ref[:, cs] = zc.astype(BF16)
        if not dilated:
            return
        stage_ref = refs[-1]
        for c, zc in enumerate(chunks):
            for half in range(MXU_DIM // LANES):
                stage_ref[2 * c + half] = zc[:, half * LANES:(half + 1) * LANES]
        for bi, (_, dil) in enumerate(DILATED[1:]):
            out_ref = refs[6 + 3 * bi + which]
            n = tm // dil
            for r in range(dil):
                for s in range(ATTN_WIDTH // LANES):
                    rows = stage_ref[s, pl.ds(r, n, stride=dil), :]
                    out_ref[r, :, s * LANES:(s + 1) * LANES] = rows.astype(BF16)

    zq = section(0)
    emit([head_norm(zq, qg_ref, c) for c in range(n_chunks)], 0, None, q_ref)
    zk = section(1)
    emit([head_norm(zk, kg_ref, c) for c in range(n_chunks)], 1, k_ref, kb_ref)
    zv = section(2)
    emit([zv[:, c * MXU_DIM:(c + 1) * MXU_DIM] for c in range(n_chunks)], 2, v_ref, vb_ref)
    u_ref[...] = section(3)


def _inproj(x2d, ln1, w_in_b, q_gain, k_gain, avg, tm, batch=None, seq=None):
    n, d = x2d.shape
    dilated = seq is not None
    row = lambda i: (i, 0)
    wide = pl.BlockSpec((tm, ATTN_WIDTH), row)
    f32o = jax.ShapeDtypeStruct((n, ATTN_WIDTH), F32)
    bf16o = jax.ShapeDtypeStruct((n, ATTN_WIDTH), BF16)
    out_specs = [wide] * 6
    out_shape = [f32o, f32o, f32o, bf16o, bf16o, bf16o]
    scratch = []
    if dilated:
        tiles = seq // tm
        for _, dil in DILATED[1:]:
            spec = pl.BlockSpec((None, dil, tm // dil, ATTN_WIDTH),
                                lambda i: (i // tiles, 0, i % tiles, 0))
            out_specs += [spec] * 3
            out_shape += [jax.ShapeDtypeStruct((batch, dil, seq // dil, ATTN_WIDTH), BF16)] * 3
        scratch = [pltpu.VMEM((ATTN_WIDTH // LANES, tm, LANES), F32)]
    return pl.pallas_call(
        functools.partial(_inproj_kernel, dilated=dilated),
        grid=(n // tm,),
        in_specs=[pl.BlockSpec((tm, d), row),
                  _const_spec((1, d)),
                  pl.BlockSpec(w_in_b.shape, lambda i: (0, 0), pipeline_mode=pl.Buffered(1)),
                  _const_spec((1, ATTN_WIDTH)), _const_spec((1, ATTN_WIDTH)),
                  _const_spec((MXU_DIM, MXU_DIM))],
        out_specs=out_specs,
        out_shape=out_shape,
        scratch_shapes=scratch,
        compiler_params=_params("parallel"),
        name="inproj",
    )(x2d, ln1, w_in_b, q_gain, k_gain, avg)


def _attn_kernel(q_ref, kp_ref, kc_ref, vp_ref, vc_ref, bias_ref, o_ref, lse_ref,
                 s_ref, p_ref):
    i = pl.program_id(2)
    lane = lax.broadcasted_iota(jnp.int32, (QBLOCK, LANES), 1)
    low = lane < HEAD_DIM
    keep_low = low.astype(F32).astype(BF16)
    keep_high = (1.0 - low.astype(F32)).astype(BF16)
    nt = (((1,), (1,)), ((), ()))
    n_pairs = N_HEADS // 2

    def run(with_prev):
        k0 = 0 if with_prev else QBLOCK
        nk = 2 * QBLOCK - k0
        ones = jnp.ones((nk, LANES), BF16)
        for hp in range(n_pairs):
            cs = slice(hp * LANES, (hp + 1) * LANES)
            qp = q_ref[:, cs]
            q2 = jnp.concatenate([qp * keep_low, qp * keep_high], axis=0)
            keys = kc_ref[:, cs]
            if with_prev:
                keys = jnp.concatenate([kp_ref[:, cs], keys], axis=0)
            s2 = lax.dot_general(q2, keys, nt, preferred_element_type=F32)
            s_ref[hp, :, k0:] = s2 + bias_ref[hp, :, k0:]

        m_all = jnp.zeros((QBLOCK, LANES), F32)
        for hp in range(n_pairs):
            for sub in range(2):
                rows = slice(sub * QBLOCK, (sub + 1) * QBLOCK)
                m = jnp.max(s_ref[hp, rows, k0:], axis=-1, keepdims=True)
                p_ref[hp, rows, k0:] = jnp.exp(s_ref[hp, rows, k0:] - m).astype(BF16)
                m_all = jnp.where(lane == 2 * hp + sub, m, m_all)

        den_all = jnp.ones((QBLOCK, LANES), F32)
        for hp in range(n_pairs):
            cs = slice(hp * LANES, (hp + 1) * LANES)
            vals = vc_ref[:, cs]
            if with_prev:
                vals = jnp.concatenate([vp_ref[:, cs], vals], axis=0)
            r = jnp.dot(p_ref[hp, :, k0:], jnp.concatenate([vals, ones], axis=1),
                        preferred_element_type=F32)
            den0, den1 = r[:QBLOCK, LANES:], r[QBLOCK:, LANES:]
            o_ref[:, cs] = jnp.where(low, r[:QBLOCK, :LANES] / den0, r[QBLOCK:, :LANES] / den1)
            den_all = jnp.where(lane == 2 * hp, den0, den_all)
            den_all = jnp.where(lane == 2 * hp + 1, den1, den_all)
        lse_ref[...] = m_all + jnp.log(den_all)

    @pl.when(i == 0)
    def _():
        run(False)

    @pl.when(i > 0)
    def _():
        run(True)


def _attn_branch(q, kb, vb, bias_all, branch):
    batch, dil, sub, _ = q.shape
    cur = lambda b, r, i: (b, r, i, 0)
    prev = lambda b, r, i: (b, r, jnp.maximum(i - 1, 0), 0)
    wide_c = pl.BlockSpec((None, None, QBLOCK, ATTN_WIDTH), cur)
    wide_p = pl.BlockSpec((None, None, QBLOCK, ATTN_WIDTH), prev)
    pairs = N_HEADS // 2
    bias_spec = pl.BlockSpec((None, pairs, 2 * QBLOCK, 2 * QBLOCK), lambda b, r, i: (branch, 0, 0, 0))
    return pl.pallas_call(
        _attn_kernel,
        grid=(batch, dil, sub // QBLOCK),
        in_specs=[wide_c, wide_p, wide_c, wide_p, wide_c, bias_spec],
        out_specs=[wide_c, pl.BlockSpec((None, None, QBLOCK, LANES), cur)],
        out_shape=[jax.ShapeDtypeStruct((batch, dil, sub, ATTN_WIDTH), F32),
                   jax.ShapeDtypeStruct((batch, dil, sub, LANES), F32)],
        scratch_shapes=[pltpu.VMEM((pairs, 2 * QBLOCK, 2 * QBLOCK), F32),
                        pltpu.VMEM((pairs, 2 * QBLOCK, 2 * QBLOCK), BF16)],
        compiler_params=_params("parallel", "parallel", "parallel"),
        name=f"attn_d{dil}",
    )(q, kb, kb, vb, vb, bias_all)


def _bias_kernel(rb_ref, bucket_ref, out_ref):
    bucket = bucket_ref[...]
    for h in range(N_HEADS):
        acc = jnp.full(bucket.shape, NEG_INF, F32)
        for b in range(NUM_BUCKETS):
            acc = jnp.where(bucket == b, rb_ref[b, h], acc)
        out_ref[h // 2, (h % 2) * QBLOCK:(h % 2 + 1) * QBLOCK, :] = acc


def _prompt_bias(rel_bias):
    qi = np.arange(QBLOCK)[:, None]
    kk = np.arange(2 * QBLOCK)[None, :]
    j = qi + QBLOCK - kk
    valid = (j >= 0) & (j <= N_STEPS)
    buckets = np.stack([np.where(valid, _bucket_table(dil)[np.clip(j, 0, N_STEPS)], -1)
                        for _, dil in DILATED]).astype(np.int32)
    pairs = N_HEADS // 2
    return pl.pallas_call(
        _bias_kernel,
        grid=(len(DILATED),),
        in_specs=[pl.BlockSpec(memory_space=pltpu.SMEM),
                  pl.BlockSpec((None, QBLOCK, 2 * QBLOCK), lambda g: (g, 0, 0))],
        out_specs=pl.BlockSpec((None, pairs, 2 * QBLOCK, 2 * QBLOCK), lambda g: (g, 0, 0, 0)),
        out_shape=jax.ShapeDtypeStruct((len(DILATED), pairs, 2 * QBLOCK, 2 * QBLOCK), F32),
        compiler_params=_params("parallel"),
        name="bias_table",
    )(rel_bias.astype(F32), jnp.asarray(buckets))


def _split_dot(a, b_bf16):
    hi = a.astype(BF16)
    lo = (a - hi.astype(F32)).astype(BF16)
    return (jnp.dot(hi, b_bf16, preferred_element_type=F32)
            + jnp.dot(lo, b_bf16, preferred_element_type=F32))


def _pool_groups(comb, u, cnt_fn, wp_ref, ps_ref):
    t = u.shape[0]
    outs = []
    run = comb
    width = 1
    for g, w in enumerate(POOL_WINDOWS):
        while width < w:
            run = run + pltpu.roll(run, width, 0)
            width *= 2
        cs = slice(g * POOL_GROUP_WIDTH, (g + 1) * POOL_GROUP_WIDTH)
        d = run[POOL_HALO:POOL_HALO + t, cs] / cnt_fn(w) - u[:, cs]
        y = jnp.dot(d.astype(BF16), wp_ref[g], preferred_element_type=F32)
        outs.append(y * ps_ref[:, cs])
    return outs


def _mix_kernel(o1_ref, o2_ref, o3_ref, l1_ref, l2_ref, l3_ref, u_ref, halo_ref,
                ex_ref, wp_ref, ps_ref, mix_ref, il_ref, ls_ref, *, seq):
    tm = u_ref.shape[0]
    n_slabs = ATTN_WIDTH // LANES
    lses = [l1_ref[0]]
    for bi, (o_ref, l_ref) in enumerate(((o2_ref, l2_ref), (o3_ref, l3_ref))):
        dil = o_ref.shape[0]
        n = tm // dil
        for r in range(dil):
            ls_ref[bi, pl.ds(r, n, stride=dil), :] = l_ref[r]
            for s in range(n_slabs):
                il_ref[bi, s, pl.ds(r, n, stride=dil), :] = o_ref[r, :, s * LANES:(s + 1) * LANES]
        lses.append(ls_ref[bi])
    l1, l2, l3 = lses
    m = jnp.maximum(jnp.maximum(l1, l2), l3)
    e1, e2, e3 = jnp.exp(l1 - m), jnp.exp(l2 - m), jnp.exp(l3 - m)
    inv = 1.0 / (e1 + e2 + e3)
    ex = ex_ref[...]
    w1, w2, w3 = (_split_dot(e * inv, ex) for e in (e1, e2, e3))
    for s in range(n_slabs):
        cs = slice(s * LANES, (s + 1) * LANES)
        attn = w1[:, cs] * o1_ref[0, :, cs] + w2[:, cs] * il_ref[0, s] + w3[:, cs] * il_ref[1, s]
        mix_ref[:, cs] = attn.astype(BF16)

    pos0 = (pl.program_id(0) * tm) % seq
    u = u_ref[...]
    halo = jnp.where(pos0 == 0, 0.0, halo_ref[...])
    comb = jnp.concatenate([halo, u], axis=0)
    pos = pos0 + lax.broadcasted_iota(jnp.int32, (tm, 1), 0)
    cnt_fn = lambda w: jnp.minimum(pos + 1, w).astype(F32)
    for g, y in enumerate(_pool_groups(comb, u, cnt_fn, wp_ref, ps_ref)):
        lo = ATTN_WIDTH + g * POOL_GROUP_WIDTH
        mix_ref[:, lo:lo + POOL_GROUP_WIDTH] = y.astype(BF16)


def _mix(o_list, l_list, u, expand, wp_b, pool_scale, seq, tm):
    n = u.shape[0]
    tiles = seq // tm
    row = lambda i: (i, 0)
    res = lambda i: (i // tiles, 0, i % tiles, 0)
    o_specs = [pl.BlockSpec((None, o.shape[1], tm // o.shape[1], ATTN_WIDTH), res) for o in o_list]
    l_specs = [pl.BlockSpec((None, l.shape[1], tm // l.shape[1], LANES), res) for l in l_list]
    halo = pl.BlockSpec((POOL_HALO, ATTN_WIDTH),
                        lambda i: (jnp.maximum(i * (tm // POOL_HALO) - 1, 0), 0))
    n_dilated = len(o_list) - 1
    return pl.pallas_call(
        functools.partial(_mix_kernel, seq=seq),
        grid=(n // tm,),
        in_specs=o_specs + l_specs + [pl.BlockSpec((tm, ATTN_WIDTH), row), halo,
                                      _const_spec(expand.shape), _const_spec(wp_b.shape),
                                      _const_spec((1, ATTN_WIDTH))],
        out_specs=pl.BlockSpec((tm, 2 * ATTN_WIDTH), row),
        out_shape=jax.ShapeDtypeStruct((n, 2 * ATTN_WIDTH), BF16),
        scratch_shapes=[pltpu.VMEM((n_dilated, ATTN_WIDTH // LANES, tm, LANES), F32),
                        pltpu.VMEM((n_dilated, tm, LANES), F32)],
        compiler_params=_params("parallel"),
        name="mix",
    )(*o_list, *l_list, u, u, expand, wp_b, pool_scale)


def _sample_kernel(q_ref, kn_ref, vn_ref, u_ref, st_ref,
                   k4_ref, k16_ref, v4_ref, v16_ref, bias_ref, wp_ref, ps_ref,
                   attn_ref, pool_ref, sbuf_ref, comb_ref, *, start):
    t_new = q_ref.shape[0]
    n_cache = k4_ref.shape[0]
    tail = QBLOCK // k4_ref.shape[1]
    last_k = k4_ref[n_cache - tail:].reshape(QBLOCK, N_HEADS, HEAD_DIM)
    last_v = v4_ref[n_cache - tail:].reshape(QBLOCK, N_HEADS, HEAD_DIM)
    for t in range(t_new):
        q = q_ref[t]
        new = slice(t, t + 1)
        parts = [(last_k[t:], last_v[t:], 0),
                 (kn_ref[:t + 1], vn_ref[:t + 1], N_STEPS - t),
                 (k4_ref[:, t], v4_ref[:, t], 0), (kn_ref[new], vn_ref[new], N_STEPS),
                 (k16_ref[:, t], v16_ref[:, t], 0), (kn_ref[new], vn_ref[new], N_STEPS)]
        branch_of = (0, 0, 1, 1, 2, 2)
        scores = []
        for (keys, _, b0), br in zip(parts, branch_of):
            s = jnp.sum(keys * q, axis=-1, keepdims=True)
            scores.append(s + bias_ref[br, b0:b0 + keys.shape[0]])
        m = functools.reduce(jnp.maximum, [jnp.max(s, axis=0) for s in scores])
        den = jnp.zeros((N_HEADS, 1), F32)
        acc = jnp.zeros((N_HEADS, HEAD_DIM), F32)
        for s, (_, vals, _) in zip(scores, parts):
            p = jnp.exp(s - m)
            den = den + jnp.sum(p, axis=0)
            acc = acc + jnp.sum(p * vals, axis=0)
        attn_ref[t] = acc / den

    u = u_ref[...]
    comb_ref[...] = jnp.zeros_like(comb_ref)
    comb_ref[1:POOL_HALO, :] = st_ref[...]
    comb_ref[POOL_HALO:POOL_HALO + t_new, :] = u
    pos = start + lax.broadcasted_iota(jnp.int32, (t_new, 1), 0)
    cnt_fn = lambda w: jnp.minimum(pos + 1, w).astype(F32)
    for g, y in enumerate(_pool_groups(comb_ref[...], u, cnt_fn, wp_ref, ps_ref)):
        pool_ref[:, g * POOL_GROUP_WIDTH:(g + 1) * POOL_GROUP_WIDTH] = y
    sbuf_ref[:POOL_BUF - t_new, :] = st_ref[t_new:, :]
    sbuf_ref[POOL_BUF - t_new:, :] = u


def _sample_mixer(q, k, v, u, state, cache_k, cache_v, bias_s, wp_b, pool_scale):
    nb, t_new, nh, hd = q.shape
    w = u.shape[-1]
    win = cache_k.shape[1]
    per = lambda shape, imap: pl.BlockSpec((None,) + shape, imap)
    new = per((t_new, nh, hd), lambda b: (b, 0, 0, 0))
    flat = lambda rows: per((rows, w), lambda b: (b, 0, 0))
    view4 = lambda c: c.reshape(nb, win // 4, 4, nh, hd)
    view16 = lambda c: c.reshape(nb, win // 16, 16, nh, hd)
    tail4 = per((QBLOCK, 4, nh, hd), lambda b: (b, win // 4 // QBLOCK - 1, 0, 0, 0))
    head16 = per((QBLOCK, t_new, nh, hd), lambda b: (b, 0, 0, 0, 0))
    return pl.pallas_call(
        functools.partial(_sample_kernel, start=PAST_LEN),
        grid=(nb,),
        in_specs=[new, new, new, flat(t_new), flat(POOL_BUF),
                  tail4, head16, tail4, head16,
                  _const_spec(bias_s.shape), _const_spec(wp_b.shape), _const_spec((1, w))],
        out_specs=[new, flat(t_new), flat(POOL_BUF)],
        out_shape=[jax.ShapeDtypeStruct((nb, t_new, nh, hd), F32),
                   jax.ShapeDtypeStruct((nb, t_new, w), F32),
                   jax.ShapeDtypeStruct((nb, POOL_BUF, w), F32)],
        scratch_shapes=[pltpu.VMEM((POOL_HALO + 8, w), F32)],
        compiler_params=_params("parallel"),
        name="sample_mixer",
    )(q, k, v, u, state, view4(cache_k), view16(cache_k), view4(cache_v), view16(cache_v),
      bias_s, wp_b, pool_scale)


def _outproj_kernel(x_ref, mix_ref, wo_ref, ln_ref, wrh_ref, wrl_ref, br_ref,
                    x1_ref, h_ref, gate_ref, eid_ref):
    x1 = x_ref[...] + jnp.dot(mix_ref[...].astype(BF16), wo_ref[...],
                              preferred_element_type=F32)
    x1_ref[...] = x1
    ms = jnp.mean(x1 * x1, axis=-1, keepdims=True)
    h = x1 * lax.rsqrt(ms + EPS) * ln_ref[...]
    h_ref[...] = h
    hi = h.astype(BF16)
    lo = (h - hi.astype(F32)).astype(BF16)
    lg = (jnp.dot(hi, wrh_ref[...], preferred_element_type=F32)
          + jnp.dot(lo, wrh_ref[...], preferred_element_type=F32)
          + jnp.dot(hi, wrl_ref[...], preferred_element_type=F32)) + br_ref[...]

    lane = lax.broadcasted_iota(jnp.int32, lg.shape, 1).astype(F32)
    big = float(LANES)
    row_max = lambda mask: jnp.max(jnp.where(mask, lg, -jnp.inf), axis=-1, keepdims=True)
    first = lambda mask: jnp.min(jnp.where(mask, lane, big), axis=-1, keepdims=True)
    is_g = lane < N_GROUPS
    mg = row_max(is_g)
    g_top = first(jnp.logical_and(is_g, lg == mg))
    den = jnp.sum(jnp.where(is_g, jnp.exp(lg - mg), 0.0), axis=-1, keepdims=True)
    p_top = 1.0 / den
    base = N_GROUPS + EXPERTS_PER_GROUP * g_top
    in_grp = jnp.logical_and(lane >= base, lane < base + EXPERTS_PER_GROUP)
    v1 = row_max(in_grp)
    i1 = first(jnp.logical_and(in_grp, lg == v1))
    rest = jnp.logical_and(in_grp, lane != i1)
    v2 = row_max(rest)
    i2 = first(jnp.logical_and(rest, lg == v2))
    e21 = jnp.exp(v2 - v1)
    s21 = 1.0 + e21
    gate1 = p_top * (1.0 / s21)
    gate2 = p_top * (e21 / s21)
    gate_ref[...] = jnp.where(lane == 0.0, gate1, jnp.where(lane == 1.0, gate2, 0.0))
    eid = jnp.where(lane == 0.0, i1 - N_GROUPS, jnp.where(lane == 1.0, i2 - N_GROUPS, 0.0))
    eid_ref[...] = eid.astype(jnp.int32)


def _outproj(x2d, mix, w_o_b, ln2, wr_hi, wr_lo, b_r, tm):
    n, d = x2d.shape
    row = lambda i: (i, 0)
    full = pl.BlockSpec((tm, d), row)
    stat = pl.BlockSpec((tm, LANES), row)
    return pl.pallas_call(
        _outproj_kernel,
        grid=(n // tm,),
        in_specs=[full, full,
                  pl.BlockSpec(w_o_b.shape, lambda i: (0, 0), pipeline_mode=pl.Buffered(1)),
                  _const_spec((1, d)), _const_spec(wr_hi.shape), _const_spec(wr_lo.shape),
                  _const_spec((1, LANES))],
        out_specs=[full, full, stat, stat],
        out_shape=[jax.ShapeDtypeStruct((n, d), F32), jax.ShapeDtypeStruct((n, d), F32),
                   jax.ShapeDtypeStruct((n, LANES), F32),
                   jax.ShapeDtypeStruct((n, LANES), jnp.int32)],
        compiler_params=_params("parallel"),
        name="outproj",
    )(x2d, mix, w_o_b, ln2, wr_hi, wr_lo, b_r)


def _row_copy(src_hbm, dst_vmem, sem, src_row, dst_row):
    return pltpu.make_async_copy(src_hbm.at[pl.ds(src_row, 1)],
                                 dst_vmem.at[pl.ds(dst_row, 1)], sem)


def _expert_kernel(tile_e_ref, tile_on_ref, src_ref, h_hbm, gate_ref, wg_ref, wu_ref, wd_ref,
                   y_ref, buf_ref, sem_ref):
    i = pl.program_id(0)
    n_tiles = pl.num_programs(0)
    tm = y_ref.shape[0]

    def issue(tile, slot):
        def body(r, c):
            _row_copy(h_hbm, buf_ref.at[slot], sem_ref.at[slot], src_ref[tile * tm + r], r).start()
            return c
        lax.fori_loop(0, tm, body, 0)

    def drain(slot):
        def body(r, c):
            _row_copy(h_hbm, buf_ref.at[slot], sem_ref.at[slot], 0, r).wait()
            return c
        lax.fori_loop(0, tm, body, 0)

    slot = i % 2

    @pl.when(i == 0)
    def _():
        issue(0, 0)

    drain(slot)

    @pl.when(i + 1 < n_tiles)
    def _():
        issue(i + 1, 1 - slot)

    @pl.when(tile_on_ref[i] == 1)
    def _():
        hb = buf_ref[slot].astype(BF16)
        a = jnp.dot(hb, wg_ref[...], preferred_element_type=F32)
        b = jnp.dot(hb, wu_ref[...], preferred_element_type=F32)
        hid = a * jax.nn.sigmoid(a) * b * gate_ref[...]
        y_ref[...] = jnp.dot(hid.astype(BF16), wd_ref[...], preferred_element_type=F32)

    @pl.when(tile_on_ref[i] == 0)
    def _():
        y_ref[...] = jnp.zeros_like(y_ref)


def _experts(tile_e, tile_on, src_tok, h_all, gate_sorted, wg_b, wu_b, wd_b, tm):
    rows = src_tok.shape[0]
    d = h_all.shape[1]
    f = wg_b.shape[2]
    grid_spec = pltpu.PrefetchScalarGridSpec(
        num_scalar_prefetch=3,
        grid=(rows // tm,),
        in_specs=[pl.BlockSpec(memory_space=pl.ANY),
                  pl.BlockSpec((tm, 1), lambda i, te, on, src: (i, 0)),
                  pl.BlockSpec((None, d, f), lambda i, te, on, src: (te[i], 0, 0)),
                  pl.BlockSpec((None, d, f), lambda i, te, on, src: (te[i], 0, 0)),
                  pl.BlockSpec((None, f, d), lambda i, te, on, src: (te[i], 0, 0))],
        out_specs=pl.BlockSpec((tm, d), lambda i, te, on, src: (i, 0)),
        scratch_shapes=[pltpu.VMEM((2, tm, d), F32), pltpu.SemaphoreType.DMA((2,))],
    )
    return pl.pallas_call(
        _expert_kernel,
        grid_spec=grid_spec,
        out_shape=jax.ShapeDtypeStruct((rows, d), F32),
        compiler_params=_params("arbitrary"),
        name="experts",
    )(tile_e, tile_on, src_tok, h_all, gate_sorted, wg_b, wu_b, wd_b)


def _combine_kernel(pos_ref, x1_ref, ys_hbm, y_ref, buf_ref, sem_ref):
    i = pl.program_id(0)
    n_tiles = pl.num_programs(0)
    tm = y_ref.shape[0]

    def issue(tile, slot):
        def body(r, c):
            for k in range(TOP_K):
                _row_copy(ys_hbm, buf_ref.at[slot, k], sem_ref.at[slot],
                          pos_ref[(tile * tm + r) * TOP_K + k], r).start()
            return c
        lax.fori_loop(0, tm, body, 0)

    def drain(slot):
        def body(r, c):
            for k in range(TOP_K):
                _row_copy(ys_hbm, buf_ref.at[slot, k], sem_ref.at[slot], 0, r).wait()
            return c
        lax.fori_loop(0, tm, body, 0)

    slot = i % 2

    @pl.when(i == 0)
    def _():
        issue(0, 0)

    drain(slot)

    @pl.when(i + 1 < n_tiles)
    def _():
        issue(i + 1, 1 - slot)

    y_ref[...] = x1_ref[...] + buf_ref[slot, 0] + buf_ref[slot, 1]


def _combine(pos_flat, x1, ys, tm):
    n, d = x1.shape
    grid_spec = pltpu.PrefetchScalarGridSpec(
        num_scalar_prefetch=1,
        grid=(n // tm,),
        in_specs=[pl.BlockSpec((tm, d), lambda i, pos: (i, 0)),
                  pl.BlockSpec(memory_space=pl.ANY)],
        out_specs=pl.BlockSpec((tm, d), lambda i, pos: (i, 0)),
        scratch_shapes=[pltpu.VMEM((2, TOP_K, tm, d), F32), pltpu.SemaphoreType.DMA((2,))],
    )
    return pl.pallas_call(
        _combine_kernel,
        grid_spec=grid_spec,
        out_shape=jax.ShapeDtypeStruct((n, d), F32),
        compiler_params=_params("arbitrary"),
        name="combine",
    )(pos_flat, x1, ys)


def _bucket_table(dilation):
    dist = np.arange(N_STEPS + 1, dtype=np.int64) * dilation
    max_exact = NUM_BUCKETS // 2
    df = np.maximum(dist, 1).astype(np.float32)
    large = max_exact + (np.log(df / np.float32(max_exact))
                         / np.float32(math.log(MAX_DISTANCE / max_exact))
                         * np.float32(NUM_BUCKETS - max_exact)).astype(np.int32)
    large = np.minimum(large, NUM_BUCKETS - 1)
    return np.where(dist < max_exact, dist, large).astype(np.int32)


def _sample_bias(rel_bias):
    buckets = np.stack([_bucket_table(dil)[::-1] for _, dil in DILATED])
    return rel_bias.astype(F32)[buckets][..., None]


def _routing_plan(eid, gate, tm):
    n = eid.shape[0]
    flat_e = eid.reshape(-1)
    onehot = (flat_e[:, None] == jnp.arange(N_EXPERTS)[None, :]).astype(jnp.int32)
    csum = jnp.cumsum(onehot, axis=0)
    rank = jnp.take_along_axis(csum, flat_e[:, None], axis=1)[:, 0] - 1
    counts = csum[-1]
    padded = ((counts + tm - 1) // tm) * tm
    ends = jnp.cumsum(padded)
    offsets = ends - padded
    pos = offsets[flat_e] + rank
    rows = ((n * TOP_K + tm - 1) // tm + N_EXPERTS) * tm
    tok = jnp.arange(n * TOP_K, dtype=jnp.int32) // TOP_K
    src_tok = jnp.zeros((rows,), jnp.int32).at[pos].set(tok)
    gate_sorted = jnp.zeros((rows,), F32).at[pos].set(gate.reshape(-1))
    tile_start = jnp.arange(rows // tm, dtype=jnp.int32) * tm
    tile_e = jnp.searchsorted(ends, tile_start, side="right").astype(jnp.int32)
    tile_on = (tile_e < N_EXPERTS).astype(jnp.int32)
    tile_e = jnp.minimum(tile_e, N_EXPERTS - 1)
    return pos.astype(jnp.int32), src_tok, gate_sorted[:, None], tile_e, tile_on


TM_PROJ = 256
TM_EXPERT = 256
TM_COMBINE = 256


def kernel(x_prompt, x_sample, cache_k, cache_v, state_pool, rel_bias, ln1_w, w_in,
           q_norm_w, k_norm_w, w_pool, pool_scale, w_o, ln2_w, w_router_group,
           b_router_group, w_router_expert, b_router_expert, w_gate, w_up, w_down):
    depth = w_in.shape[0]
    assert depth == 1
    batch, seq, d_model = x_prompt.shape
    nb, t_new, _ = x_sample.shape
    win = cache_k.shape[2]
    f_exp = w_gate.shape[-1]

    w_in_b = _to_bf16(w_in[0], 256)
    w_o_b = _to_bf16(w_o[0], 256)
    wp_b = _to_bf16(w_pool[0].reshape(-1, POOL_GROUP_WIDTH), 256).reshape(w_pool.shape[1:])
    wg_b = _to_bf16(w_gate[0].reshape(-1, f_exp), 2048).reshape(w_gate.shape[1:])
    wu_b = _to_bf16(w_up[0].reshape(-1, f_exp), 2048).reshape(w_up.shape[1:])
    wd_b = _to_bf16(w_down[0].reshape(-1, d_model), 512).reshape(w_down.shape[1:])
    ln1 = ln1_w[0][None, :]
    ln2 = ln2_w[0][None, :]
    q_gain = jnp.tile(q_norm_w[0], N_HEADS)[None, :] * SCALE
    k_gain = jnp.tile(k_norm_w[0], N_HEADS)[None, :]
    ps = pool_scale[0][None, :]
    blk = np.arange(MXU_DIM) // HEAD_DIM
    avg = jnp.asarray((blk[:, None] == blk[None, :]) / HEAD_DIM, BF16)
    head_of_col = np.arange(ATTN_WIDTH) // HEAD_DIM
    expand_np = (np.arange(LANES)[:, None] == head_of_col[None, :])
    expand = jnp.asarray(expand_np, BF16)
    w_r = jnp.concatenate([w_router_group[0], w_router_expert[0]], axis=1)
    w_r = jnp.pad(w_r, ((0, 0), (0, LANES - w_r.shape[1])))
    wr_hi = w_r.astype(BF16)
    wr_lo = (w_r - wr_hi.astype(F32)).astype(BF16)
    b_r = jnp.pad(jnp.concatenate([b_router_group[0], b_router_expert[0]]),
                  (0, LANES - N_GROUPS - N_EXPERTS))[None, :]

    xp = x_prompt.reshape(batch * seq, d_model)
    proj = _inproj(xp, ln1, w_in_b, q_gain, k_gain, avg, TM_PROJ, batch, seq)
    k, v, u = proj[:3]
    natural = tuple(a.reshape(batch, 1, seq, ATTN_WIDTH) for a in proj[3:6])
    qkv = [natural] + [tuple(proj[6 + 3 * bi:9 + 3 * bi]) for bi in range(len(DILATED) - 1)]
    bias_all = _prompt_bias(rel_bias)
    o_list, l_list = [], []
    for branch, (qd, kd, vd) in enumerate(qkv):
        o, lse = _attn_branch(qd, kd, vd, bias_all, branch)
        o_list.append(o)
        l_list.append(lse)
    mix_p = _mix(o_list, l_list, u, expand, wp_b, ps, seq, TM_PROJ)
    x1_p, h_p, gate_p, eid_p = _outproj(xp, mix_p, w_o_b, ln2, wr_hi, wr_lo, b_r, TM_PROJ)

    n_s = nb * t_new
    xs = x_sample.reshape(n_s, d_model)
    k_s, v_s, u_s, q_s, _, _ = _inproj(xs, ln1, w_in_b, q_gain, k_gain, avg, n_s)
    heads_s = (nb, t_new, N_HEADS, HEAD_DIM)
    k_s5, v_s5 = k_s.reshape(heads_s), v_s.reshape(heads_s)
    attn_s, pool_s, sbuf = _sample_mixer(
        q_s.astype(F32).reshape(heads_s), k_s5, v_s5, u_s.reshape(nb, t_new, ATTN_WIDTH),
        state_pool[0], cache_k[0], cache_v[0], _sample_bias(rel_bias), wp_b, ps)
    mix_s = jnp.concatenate([attn_s.reshape(n_s, ATTN_WIDTH), pool_s.reshape(n_s, ATTN_WIDTH)],
                            axis=1)
    x1_s, h_s, gate_s, eid_s = _outproj(xs, mix_s, w_o_b, ln2, wr_hi, wr_lo, b_r, n_s)

    h_all = jnp.concatenate([h_p, h_s], axis=0)
    eid = jnp.concatenate([eid_p[:, :TOP_K], eid_s[:, :TOP_K]], axis=0)
    gate = jnp.concatenate([gate_p[:, :TOP_K], gate_s[:, :TOP_K]], axis=0)
    pos, src_tok, gate_sorted, tile_e, tile_on = _routing_plan(eid, gate, TM_EXPERT)
    ys = _experts(tile_e, tile_on, src_tok, h_all, gate_sorted, wg_b, wu_b, wd_b, TM_EXPERT)
    n_p = batch * seq
    y_p = _combine(pos[:n_p * TOP_K], x1_p, ys, TM_COMBINE)
    y_s = _combine(pos[n_p * TOP_K:], x1_s, ys, n_s)

    keep = min(MAX_DISTANCE, seq)
    kv_shape = (batch, seq, N_HEADS, HEAD_DIM)
    return (y_p.reshape(batch, seq, d_model),
            y_s.reshape(nb, t_new, d_model),
            k.reshape(kv_shape)[None, :, -keep:],
            v.reshape(kv_shape)[None, :, -keep:],
            u.reshape(batch, seq, ATTN_WIDTH)[None, :, -POOL_BUF:],
            k_s5[None], v_s5[None], sbuf[None])
```

```python
import functools
import math

import numpy as np
import jax
import jax.numpy as jnp
from jax import lax
from jax.experimental import pallas as pl
from jax.experimental.pallas import tpu as pltpu

F32 = jnp.float32
BF16 = jnp.bfloat16

N_HEADS = 16
HEAD_DIM = 64
ATTN_WIDTH = N_HEADS * HEAD_DIM
POOL_WINDOWS = (2, 4, 8, 16)
POOL_GROUP_WIDTH = 256
POOL_BUF = max(POOL_WINDOWS) - 1
POOL_HALO = POOL_BUF + 1
DILATED = ((128, 1), (512, 4), (2048, 16))
N_STEPS = 128
QBLOCK = 128
NUM_BUCKETS = 32
MAX_DISTANCE = 2048
PAST_LEN = 16384
N_GROUPS = 4
EXPERTS_PER_GROUP = 4
N_EXPERTS = N_GROUPS * EXPERTS_PER_GROUP
TOP_K = 2
EPS = 1e-6
SCALE = HEAD_DIM ** -0.5
NEG_INF = -1e30
LANES = 128
MXU_DIM = 256
VMEM_LIMIT = 56 * 1024 * 1024


def _params(*sem):
    return pltpu.CompilerParams(dimension_semantics=sem, vmem_limit_bytes=VMEM_LIMIT)


def _const_spec(shape):
    zeros = (0,) * len(shape)
    return pl.BlockSpec(shape, lambda *_: zeros)


def _cast_kernel(x_ref, o_ref):
    o_ref[...] = x_ref[...].astype(o_ref.dtype)


def _to_bf16(w2d, block_rows):
    rows, cols = w2d.shape
    return pl.pallas_call(
        _cast_kernel,
        grid=(rows // block_rows,),
        in_specs=[pl.BlockSpec((block_rows, cols), lambda i: (i, 0))],
        out_specs=pl.BlockSpec((block_rows, cols), lambda i: (i, 0)),
        out_shape=jax.ShapeDtypeStruct((rows, cols), BF16),
        compiler_params=_params("parallel"),
        name="cast_bf16",
    )(w2d)


def _inproj_kernel(x_ref, ln_ref, w_ref, qg_ref, kg_ref, avg_ref, *refs, dilated):
    k_ref, v_ref, u_ref, q_ref, kb_ref, vb_ref = refs[:6]
    tm = x_ref.shape[0]
    x = x_ref[...]
    ms = jnp.mean(x * x, axis=-1, keepdims=True)
    h = (x * lax.rsqrt(ms + EPS) * ln_ref[...]).astype(BF16)
    n_chunks = ATTN_WIDTH // MXU_DIM

    def section(s):
        return jnp.dot(h, w_ref[:, s * ATTN_WIDTH:(s + 1) * ATTN_WIDTH],
                       preferred_element_type=F32)

    def head_norm(z, g_ref, c):
        zc = z[:, c * MXU_DIM:(c + 1) * MXU_DIM]
        msh = jnp.dot((zc * zc).astype(BF16), avg_ref[...], preferred_element_type=F32)
        return zc * lax.rsqrt(msh + EPS) * g_ref[:, c * MXU_DIM:(c + 1) * MXU_DIM]

    def emit(chunks, which, f32_ref, bf_ref):
        for c, zc in enumerate(chunks):
            cs = slice(c * MXU_DIM, (c + 1) * MXU_DIM)
            if f32_ref is not None:
                f32_ref[:, cs] = zc
            bf_ref[:, cs] = zc.astype(BF16)
        if not dilated:
            return
        stage_ref = refs[-1]
        for c, zc in enumerate(chunks):
            for half in range(MXU_DIM // LANES):
                stage_ref[2 * c + half] = zc[:, half * LANES:(half + 1) * LANES]
        for bi, (_, dil) in enumerate(DILATED[1:]):
            out_ref = refs[6 + 3 * bi + which]
            n = tm // dil
            for r in range(dil):
                for s in range(ATTN_WIDTH // LANES):
                    rows = stage_ref[s, pl.ds(r, n, stride=dil), :]
                    out_ref[r, :, s * LANES:(s + 1) * LANES] = rows.astype(BF16)

    zq = section(0)
    emit([head_norm(zq, qg_ref, c) for c in range(n_chunks)], 0, None, q_ref)
    zk = section(1)
    emit([head_norm(zk, kg_ref, c) for c in range(n_chunks)], 1, k_ref, kb_ref)
    zv = section(2)
    emit([zv[:, c * MXU_DIM:(c + 1) * MXU_DIM] for c in range(n_chunks)], 2, v_ref, vb_ref)
    u_ref[...] = section(3)


def _inproj(x2d, ln1, w_in_b, q_gain, k_gain, avg, tm, batch=None, seq=None):
    n, d = x2d.shape
    dilated = seq is not None
    row = lambda i: (i, 0)
    wide = pl.BlockSpec((tm, ATTN_WIDTH), row)
    f32o = jax.ShapeDtypeStruct((n, ATTN_WIDTH), F32)
    bf16o = jax.ShapeDtypeStruct((n, ATTN_WIDTH), BF16)
    out_specs = [wide] * 6
    out_shape = [f32o, f32o, f32o, bf16o, bf16o, bf16o]
    scratch = []
    if dilated:
        tiles = seq // tm
        for _, dil in DILATED[1:]:
            spec = pl.BlockSpec((None, dil, tm // dil, ATTN_WIDTH),
                                lambda i: (i // tiles, 0, i % tiles, 0))
            out_specs += [spec] * 3
            out_shape += [jax.ShapeDtypeStruct((batch, dil, seq // dil, ATTN_WIDTH), BF16)] * 3
        scratch = [pltpu.VMEM((ATTN_WIDTH // LANES, tm, LANES), F32)]
    return pl.pallas_call(
        functools.partial(_inproj_kernel, dilated=dilated),
        grid=(n // tm,),
        in_specs=[pl.BlockSpec((tm, d), row),
                  _const_spec((1, d)),
                  pl.BlockSpec(w_in_b.shape, lambda i: (0, 0), pipeline_mode=pl.Buffered(1)),
                  _const_spec((1, ATTN_WIDTH)), _const_spec((1, ATTN_WIDTH)),
                  _const_spec((MXU_DIM, MXU_DIM))],
        out_specs=out_specs,
        out_shape=out_shape,
        scratch_shapes=scratch,
        compiler_params=_params("parallel"),
        name="inproj",
    )(x2d, ln1, w_in_b, q_gain, k_gain, avg)


def _attn_kernel(q_ref, kp_ref, kc_ref, vp_ref, vc_ref, bias_ref, o_ref, lse_ref,
                 s_ref, p_ref):
    i = pl.program_id(2)
    lane = lax.broadcasted_iota(jnp.int32, (QBLOCK, LANES), 1)
    low = lane < HEAD_DIM
    keep_low = low.astype(F32).astype(BF16)
    keep_high = (1.0 - low.astype(F32)).astype(BF16)
    nt = (((1,), (1,)), ((), ()))
    n_pairs = N_HEADS // 2

    def run(with_prev):
        k0 = 0 if with_prev else QBLOCK
        nk = 2 * QBLOCK - k0
        ones = jnp.ones((nk, LANES), BF16)
        for hp in range(n_pairs):
            cs = slice(hp * LANES, (hp + 1) * LANES)
            qp = q_ref[:, cs]
            q2 = jnp.concatenate([qp * keep_low, qp * keep_high], axis=0)
            keys = kc_ref[:, cs]
            if with_prev:
                keys = jnp.concatenate([kp_ref[:, cs], keys], axis=0)
            s2 = lax.dot_general(q2, keys, nt, preferred_element_type=F32)
            s_ref[hp, :, k0:] = s2 + bias_ref[hp, :, k0:]

        m_all = jnp.zeros((QBLOCK, LANES), F32)
        for hp in range(n_pairs):
            for sub in range(2):
                rows = slice(sub * QBLOCK, (sub + 1) * QBLOCK)
                m = jnp.max(s_ref[hp, rows, k0:], axis=-1, keepdims=True)
                p_ref[hp, rows, k0:] = jnp.exp(s_ref[hp, rows, k0:] - m).astype(BF16)
                m_all = jnp.where(lane == 2 * hp + sub, m, m_all)

        den_all = jnp.ones((QBLOCK, LANES), F32)
        for hp in range(n_pairs):
            cs = slice(hp * LANES, (hp + 1) * LANES)
            vals = vc_ref[:, cs]
            if with_prev:
                vals = jnp.concatenate([vp_ref[:, cs], vals], axis=0)
            r = jnp.dot(p_ref[hp, :, k0:], jnp.concatenate([vals, ones], axis=1),
                        preferred_element_type=F32)
            den0, den1 = r[:QBLOCK, LANES:], r[QBLOCK:, LANES:]
            o_ref[:, cs] = jnp.where(low, r[:QBLOCK, :LANES] / den0, r[QBLOCK:, :LANES] / den1)
            den_all = jnp.where(lane == 2 * hp, den0, den_all)
            den_all = jnp.where(lane == 2 * hp + 1, den1, den_all)
        lse_ref[...] = m_all + jnp.log(den_all)

    @pl.when(i == 0)
    def _():
        run(False)

    @pl.when(i > 0)
    def _():
        run(True)


def _attn_branch(q, kb, vb, bias_all, branch):
    batch, dil, sub, _ = q.shape
    cur = lambda b, r, i: (b, r, i, 0)
    prev = lambda b, r, i: (b, r, jnp.maximum(i - 1, 0), 0)
    wide_c = pl.BlockSpec((None, None, QBLOCK, ATTN_WIDTH), cur)
    wide_p = pl.BlockSpec((None, None, QBLOCK, ATTN_WIDTH), prev)
    pairs = N_HEADS // 2
    bias_spec = pl.BlockSpec((None, pairs, 2 * QBLOCK, 2 * QBLOCK), lambda b, r, i: (branch, 0, 0, 0))
    return pl.pallas_call(
        _attn_kernel,
        grid=(batch, dil, sub // QBLOCK),
        in_specs=[wide_c, wide_p, wide_c, wide_p, wide_c, bias_spec],
        out_specs=[wide_c, pl.BlockSpec((None, None, QBLOCK, LANES), cur)],
        out_shape=[jax.ShapeDtypeStruct((batch, dil, sub, ATTN_WIDTH), F32),
                   jax.ShapeDtypeStruct((batch, dil, sub, LANES), F32)],
        scratch_shapes=[pltpu.VMEM((pairs, 2 * QBLOCK, 2 * QBLOCK), F32),
                        pltpu.VMEM((pairs, 2 * QBLOCK, 2 * QBLOCK), BF16)],
        compiler_params=_params("parallel", "parallel", "parallel"),
        name=f"attn_d{dil}",
    )(q, kb, kb, vb, vb, bias_all)


def _bias_kernel(rb_ref, bucket_ref, out_ref):
    bucket = bucket_ref[...]
    for h in range(N_HEADS):
        acc = jnp.full(bucket.shape, NEG_INF, F32)
        for b in range(NUM_BUCKETS):
            acc = jnp.where(bucket == b, rb_ref[b, h], acc)
        out_ref[h // 2, (h % 2) * QBLOCK:(h % 2 + 1) * QBLOCK, :] = acc


def _prompt_bias(rel_bias):
    qi = np.arange(QBLOCK)[:, None]
    kk = np.arange(2 * QBLOCK)[None, :]
    j = qi + QBLOCK - kk
    valid = (j >= 0) & (j <= N_STEPS)
    buckets = np.stack([np.where(valid, _bucket_table(dil)[np.clip(j, 0, N_STEPS)], -1)
                        for _, dil in DILATED]).astype(np.int32)
    pairs = N_HEADS // 2
    return pl.pallas_call(
        _bias_kernel,
        grid=(len(DILATED),),
        in_specs=[pl.BlockSpec(memory_space=pltpu.SMEM),
                  pl.BlockSpec((None, QBLOCK, 2 * QBLOCK), lambda g: (g, 0, 0))],
        out_specs=pl.BlockSpec((None, pairs, 2 * QBLOCK, 2 * QBLOCK), lambda g: (g, 0, 0, 0)),
        out_shape=jax.ShapeDtypeStruct((len(DILATED), pairs, 2 * QBLOCK, 2 * QBLOCK), F32),
        compiler_params=_params("parallel"),
        name="bias_table",
    )(rel_bias.astype(F32), jnp.asarray(buckets))


def _split_dot(a, b_bf16):
    hi = a.astype(BF16)
    lo = (a - hi.astype(F32)).astype(BF16)
    return (jnp.dot(hi, b_bf16, preferred_element_type=F32)
            + jnp.dot(lo, b_bf16, preferred_element_type=F32))


def _pool_groups(comb, u, cnt_fn, wp_ref, ps_ref):
    t = u.shape[0]
    outs = []
    run = comb
    width = 1
    for g, w in enumerate(POOL_WINDOWS):
        while width < w:
            run = run + pltpu.roll(run, width, 0)
            width *= 2
        cs = slice(g * POOL_GROUP_WIDTH, (g + 1) * POOL_GROUP_WIDTH)
        d = run[POOL_HALO:POOL_HALO + t, cs] / cnt_fn(w) - u[:, cs]
        y = jnp.dot(d.astype(BF16), wp_ref[g], preferred_element_type=F32)
        outs.append(y * ps_ref[:, cs])
    return outs


def _mix_kernel(o1_ref, o2_ref, o3_ref, l1_ref, l2_ref, l3_ref, u_ref, halo_ref,
                ex_ref, wp_ref, ps_ref, mix_ref, il_ref, ls_ref, *, seq):
    tm = u_ref.shape[0]
    n_slabs = ATTN_WIDTH // LANES
    lses = [l1_ref[0]]
    for bi, (o_ref, l_ref) in enumerate(((o2_ref, l2_ref), (o3_ref, l3_ref))):
        dil = o_ref.shape[0]
        n = tm // dil
        for r in range(dil):
            ls_ref[bi, pl.ds(r, n, stride=dil), :] = l_ref[r]
            for s in range(n_slabs):
                il_ref[bi, s, pl.ds(r, n, stride=dil), :] = o_ref[r, :, s * LANES:(s + 1) * LANES]
        lses.append(ls_ref[bi])
    l1, l2, l3 = lses
    m = jnp.maximum(jnp.maximum(l1, l2), l3)
    e1, e2, e3 = jnp.exp(l1 - m), jnp.exp(l2 - m), jnp.exp(l3 - m)
    inv = 1.0 / (e1 + e2 + e3)
    ex = ex_ref[...]
    w1, w2, w3 = (_split_dot(e * inv, ex) for e in (e1, e2, e3))
    for s in range(n_slabs):
        cs = slice(s * LANES, (s + 1) * LANES)
        attn = w1[:, cs] * o1_ref[0, :, cs] + w2[:, cs] * il_ref[0, s] + w3[:, cs] * il_ref[1, s]
        mix_ref[:, cs] = attn.astype(BF16)

    pos0 = (pl.program_id(0) * tm) % seq
    u = u_ref[...]
    halo = jnp.where(pos0 == 0, 0.0, halo_ref[...])
    comb = jnp.concatenate([halo, u], axis=0)
    pos = pos0 + lax.broadcasted_iota(jnp.int32, (tm, 1), 0)
    cnt_fn = lambda w: jnp.minimum(pos + 1, w).astype(F32)
    for g, y in enumerate(_pool_groups(comb, u, cnt_fn, wp_ref, ps_ref)):
        lo = ATTN_WIDTH + g * POOL_GROUP_WIDTH
        mix_ref[:, lo:lo + POOL_GROUP_WIDTH] = y.astype(BF16)


def _mix(o_list, l_list, u, expand, wp_b, pool_scale, seq, tm):
    n = u.shape[0]
    tiles = seq // tm
    row = lambda i: (i, 0)
    res = lambda i: (i // tiles, 0, i % tiles, 0)
    o_specs = [pl.BlockSpec((None, o.shape[1], tm // o.shape[1], ATTN_WIDTH), res) for o in o_list]
    l_specs = [pl.BlockSpec((None, l.shape[1], tm // l.shape[1], LANES), res) for l in l_list]
    halo = pl.BlockSpec((POOL_HALO, ATTN_WIDTH),
                        lambda i: (jnp.maximum(i * (tm // POOL_HALO) - 1, 0), 0))
    n_dilated = len(o_list) - 1
    return pl.pallas_call(
        functools.partial(_mix_kernel, seq=seq),
        grid=(n // tm,),
        in_specs=o_specs + l_specs + [pl.BlockSpec((tm, ATTN_WIDTH), row), halo,
                                      _const_spec(expand.shape), _const_spec(wp_b.shape),
                                      _const_spec((1, ATTN_WIDTH))],
        out_specs=pl.BlockSpec((tm, 2 * ATTN_WIDTH), row),
        out_shape=jax.ShapeDtypeStruct((n, 2 * ATTN_WIDTH), BF16),
        scratch_shapes=[pltpu.VMEM((n_dilated, ATTN_WIDTH // LANES, tm, LANES), F32),
                        pltpu.VMEM((n_dilated, tm, LANES), F32)],
        compiler_params=_params("parallel"),
        name="mix",
    )(*o_list, *l_list, u, u, expand, wp_b, pool_scale)


def _sample_kernel(q_ref, kn_ref, vn_ref, u_ref, st_ref,
                   k4_ref, k16_ref, v4_ref, v16_ref, bias_ref, wp_ref, ps_ref,
                   attn_ref, pool_ref, sbuf_ref, comb_ref, *, start):
    t_new = q_ref.shape[0]
    n_cache = k4_ref.shape[0]
    tail = QBLOCK // k4_ref.shape[1]
    last_k = k4_ref[n_cache - tail:].reshape(QBLOCK, N_HEADS, HEAD_DIM)
    last_v = v4_ref[n_cache - tail:].reshape(QBLOCK, N_HEADS, HEAD_DIM)
    for t in range(t_new):
        q = q_ref[t]
        new = slice(t, t + 1)
        parts = [(last_k[t:], last_v[t:], 0),
                 (kn_ref[:t + 1], vn_ref[:t + 1], N_STEPS - t),
                 (k4_ref[:, t], v4_ref[:, t], 0), (kn_ref[new], vn_ref[new], N_STEPS),
                 (k16_ref[:, t], v16_ref[:, t], 0), (kn_ref[new], vn_ref[new], N_STEPS)]
        branch_of = (0, 0, 1, 1, 2, 2)
        scores = []
        for (keys, _, b0), br in zip(parts, branch_of):
            s = jnp.sum(keys * q, axis=-1, keepdims=True)
            scores.append(s + bias_ref[br, b0:b0 + keys.shape[0]])
        m = functools.reduce(jnp.maximum, [jnp.max(s, axis=0) for s in scores])
        den = jnp.zeros((N_HEADS, 1), F32)
        acc = jnp.zeros((N_HEADS, HEAD_DIM), F32)
        for s, (_, vals, _) in zip(scores, parts):
            p = jnp.exp(s - m)
            den = den + jnp.sum(p, axis=0)
            acc = acc + jnp.sum(p * vals, axis=0)
        attn_ref[t] = acc / den

    u = u_ref[...]
    comb_ref[...] = jnp.zeros_like(comb_ref)
    comb_ref[1:POOL_HALO, :] = st_ref[...]
    comb_ref[POOL_HALO:POOL_HALO + t_new, :] = u
    pos = start + lax.broadcasted_iota(jnp.int32, (t_new, 1), 0)
    cnt_fn = lambda w: jnp.minimum(pos + 1, w).astype(F32)
    for g, y in enumerate(_pool_groups(comb_ref[...], u, cnt_fn, wp_ref, ps_ref)):
        pool_ref[:, g * POOL_GROUP_WIDTH:(g + 1) * POOL_GROUP_WIDTH] = y
    sbuf_ref[:POOL_BUF - t_new, :] = st_ref[t_new:, :]
    sbuf_ref[POOL_BUF - t_new:, :] = u


def _sample_mixer(q, k, v, u, state, cache_k, cache_v, bias_s, wp_b, pool_scale):
    nb, t_new, nh, hd = q.shape
    w = u.shape[-1]
    win = cache_k.shape[1]
    per = lambda shape, imap: pl.BlockSpec((None,) + shape, imap)
    new = per((t_new, nh, hd), lambda b: (b, 0, 0, 0))
    flat = lambda rows: per((rows, w), lambda b: (b, 0, 0))
    view4 = lambda c: c.reshape(nb, win // 4, 4, nh, hd)
    view16 = lambda c: c.reshape(nb, win // 16, 16, nh, hd)
    tail4 = per((QBLOCK, 4, nh, hd), lambda b: (b, win // 4 // QBLOCK - 1, 0, 0, 0))
    head16 = per((QBLOCK, t_new, nh, hd), lambda b: (b, 0, 0, 0, 0))
    return pl.pallas_call(
        functools.partial(_sample_kernel, start=PAST_LEN),
        grid=(nb,),
        in_specs=[new, new, new, flat(t_new), flat(POOL_BUF),
                  tail4, head16, tail4, head16,
                  _const_spec(bias_s.shape), _const_spec(wp_b.shape), _const_spec((1, w))],
        out_specs=[new, flat(t_new), flat(POOL_BUF)],
        out_shape=[jax.ShapeDtypeStruct((nb, t_new, nh, hd), F32),
                   jax.ShapeDtypeStruct((nb, t_new, w), F32),
                   jax.ShapeDtypeStruct((nb, POOL_BUF, w), F32)],
        scratch_shapes=[pltpu.VMEM((POOL_HALO + 8, w), F32)],
        compiler_params=_params("parallel"),
        name="sample_mixer",
    )(q, k, v, u, state, view4(cache_k), view16(cache_k), view4(cache_v), view16(cache_v),
      bias_s, wp_b, pool_scale)


def _pack_bf16_pairs(h):
    half = h.shape[1] // 2
    bits = lambda a: lax.bitcast_convert_type(a.astype(BF16).astype(F32), jnp.uint32)
    return (bits(h[:, :half]) >> 16) | (bits(h[:, half:]) & jnp.uint32(0xFFFF0000))


def _unpack_bf16_pairs(u):
    lo = lax.bitcast_convert_type(u << 16, F32).astype(BF16)
    hi = lax.bitcast_convert_type(u & jnp.uint32(0xFFFF0000), F32).astype(BF16)
    return lo, hi


def _outproj_kernel(x_ref, mix_ref, wo_ref, ln_ref, wrh_ref, wrl_ref, br_ref, tri_ref, cin_ref,
                    x1_ref, hp_ref, gate_ref, eid_ref, rank_ref, cout_ref, count_ref):
    @pl.when(pl.program_id(0) == 0)
    def _():
        count_ref[...] = cin_ref[...]

    x1 = x_ref[...] + jnp.dot(mix_ref[...].astype(BF16), wo_ref[...],
                              preferred_element_type=F32)
    x1_ref[...] = x1
    ms = jnp.mean(x1 * x1, axis=-1, keepdims=True)
    h = x1 * lax.rsqrt(ms + EPS) * ln_ref[...]
    hp_ref[...] = _pack_bf16_pairs(h)
    hi = h.astype(BF16)
    lo = (h - hi.astype(F32)).astype(BF16)
    lg = (jnp.dot(hi, wrh_ref[...], preferred_element_type=F32)
          + jnp.dot(lo, wrh_ref[...], preferred_element_type=F32)
          + jnp.dot(hi, wrl_ref[...], preferred_element_type=F32)) + br_ref[...]

    lane = lax.broadcasted_iota(jnp.int32, lg.shape, 1).astype(F32)
    big = float(LANES)
    row_max = lambda mask: jnp.max(jnp.where(mask, lg, -jnp.inf), axis=-1, keepdims=True)
    first = lambda mask: jnp.min(jnp.where(mask, lane, big), axis=-1, keepdims=True)
    is_g = lane < N_GROUPS
    mg = row_max(is_g)
    g_top = first(jnp.logical_and(is_g, lg == mg))
    den = jnp.sum(jnp.where(is_g, jnp.exp(lg - mg), 0.0), axis=-1, keepdims=True)
    p_top = 1.0 / den
    base = N_GROUPS + EXPERTS_PER_GROUP * g_top
    in_grp = jnp.logical_and(lane >= base, lane < base + EXPERTS_PER_GROUP)
    v1 = row_max(in_grp)
    i1 = first(jnp.logical_and(in_grp, lg == v1))
    rest = jnp.logical_and(in_grp, lane != i1)
    v2 = row_max(rest)
    i2 = first(jnp.logical_and(rest, lg == v2))
    e21 = jnp.exp(v2 - v1)
    s21 = 1.0 + e21
    gate1 = p_top * (1.0 / s21)
    gate2 = p_top * (e21 / s21)
    pick = lambda a, b: jnp.where(lane == 0.0, a, jnp.where(lane == 1.0, b, 0.0))
    gate_ref[...] = pick(gate1, gate2)
    e1, e2 = i1 - N_GROUPS, i2 - N_GROUPS
    eid_ref[...] = pick(e1, e2).astype(jnp.int32)

    hot1 = (lane == e1).astype(F32)
    hot2 = (lane == e2).astype(F32)
    hot = hot1 + hot2
    before = count_ref[...] + jnp.dot(tri_ref[...], hot.astype(BF16), preferred_element_type=F32)
    rank1 = jnp.sum(hot1 * before, axis=-1, keepdims=True)
    rank2 = jnp.sum(hot2 * before, axis=-1, keepdims=True)
    rank_ref[...] = pick(rank1, rank2).astype(jnp.int32)
    count_ref[...] = count_ref[...] + jnp.sum(hot, axis=0, keepdims=True)
    cout_ref[...] = count_ref[...]


def _outproj(x2d, mix, w_o_b, ln2, wr_hi, wr_lo, b_r, tri, counts_in, tm):
    n, d = x2d.shape
    row = lambda i: (i, 0)
    full = pl.BlockSpec((tm, d), row)
    stat = pl.BlockSpec((tm, LANES), row)
    stat_i = jax.ShapeDtypeStruct((n, LANES), jnp.int32)
    return pl.pallas_call(
        _outproj_kernel,
        grid=(n // tm,),
        in_specs=[full, full,
                  pl.BlockSpec(w_o_b.shape, lambda i: (0, 0), pipeline_mode=pl.Buffered(1)),
                  _const_spec((1, d)), _const_spec(wr_hi.shape), _const_spec(wr_lo.shape),
                  _const_spec((1, LANES)), _const_spec((tm, tm)), _const_spec((1, LANES))],
        out_specs=[full, pl.BlockSpec((tm, d // 2), row), stat, stat, stat,
                   _const_spec((1, LANES))],
        out_shape=[jax.ShapeDtypeStruct((n, d), F32),
                   jax.ShapeDtypeStruct((n, d // 2), jnp.uint32),
                   jax.ShapeDtypeStruct((n, LANES), F32), stat_i, stat_i,
                   jax.ShapeDtypeStruct((1, LANES), F32)],
        scratch_shapes=[pltpu.VMEM((1, LANES), F32)],
        compiler_params=_params("arbitrary"),
        name="outproj",
    )(x2d, mix, w_o_b, ln2, wr_hi, wr_lo, b_r, tri[:tm, :tm], counts_in)


def _row_copy(src_hbm, dst_vmem, sem, src_row, dst_row):
    return pltpu.make_async_copy(src_hbm.at[pl.ds(src_row, 1)],
                                 dst_vmem.at[pl.ds(dst_row, 1)], sem)


def _dispatch_kernel(pos_ref, pad_start_ref, pad_len_ref, tail_ref, hp_ref, hps_ref,
                     hs_ref, sem_ref, zero_ref, *, tile_rows):
    i = pl.program_id(0)
    tm = hp_ref.shape[0]
    sem = sem_ref.at[0]

    def scatter(src_ref, n_rows, first_token):
        def put(r, k):
            return _row_copy(src_ref, hs_ref, sem, r, pos_ref[(first_token + r) * TOP_K + k])

        def issue(r, c):
            for k in range(TOP_K):
                put(r, k).start()
            return c

        def drain(r, c):
            for k in range(TOP_K):
                put(r, k).wait()
            return c

        lax.fori_loop(0, n_rows, issue, 0)
        lax.fori_loop(0, n_rows, drain, 0)

    scatter(hp_ref, tm, i * tm)

    @pl.when(i == pl.num_programs(0) - 1)
    def _():
        scatter(hps_ref, hps_ref.shape[0], pl.num_programs(0) * tm)
        zero_ref[...] = jnp.zeros_like(zero_ref)
        for e in range(N_EXPERTS):
            def fill(r):
                return _row_copy(zero_ref, hs_ref, sem, 0, pad_start_ref[e] + r)

            def start(r, c):
                fill(r).start()
                return c

            def wait(r, c):
                fill(r).wait()
                return c

            lax.fori_loop(0, pad_len_ref[e], start, 0)
            lax.fori_loop(0, pad_len_ref[e], wait, 0)

        n_tiles = hs_ref.shape[0] // tile_rows

        def fill_tile(t):
            rows = pl.ds(pl.multiple_of(t * tile_rows, tile_rows), tile_rows)
            return pltpu.make_async_copy(zero_ref, hs_ref.at[rows], sem)

        def start_tile(t, c):
            fill_tile(t).start()
            return c

        def wait_tile(t, c):
            fill_tile(t).wait()
            return c

        lax.fori_loop(tail_ref[0], n_tiles, start_tile, 0)
        lax.fori_loop(tail_ref[0], n_tiles, wait_tile, 0)


def _dispatch(pos_flat, pad_start, pad_len, tail, hp, hp_s, n_tiles, tile_rows, tm):
    n, half = hp.shape
    grid_spec = pltpu.PrefetchScalarGridSpec(
        num_scalar_prefetch=4,
        grid=(n // tm,),
        in_specs=[pl.BlockSpec((tm, half), lambda i, *_: (i, 0)),
                  pl.BlockSpec(hp_s.shape, lambda i, *_: (0, 0))],
        out_specs=pl.BlockSpec(memory_space=pl.ANY),
        scratch_shapes=[pltpu.SemaphoreType.DMA((1,)), pltpu.VMEM((tile_rows, half), jnp.uint32)])
    return pl.pallas_call(
        functools.partial(_dispatch_kernel, tile_rows=tile_rows),
        grid_spec=grid_spec,
        out_shape=jax.ShapeDtypeStruct((n_tiles * tile_rows, half), jnp.uint32),
        compiler_params=_params("arbitrary"),
        name="dispatch",
    )(pos_flat, pad_start, pad_len, tail, hp, hp_s)


def _expert_kernel(tile_e_ref, tile_on_ref, hs_ref, wg_ref, wu_ref, wd_ref, y_ref):
    i = pl.program_id(0)
    half = hs_ref.shape[1]

    @pl.when(tile_on_ref[i] == 1)
    def _():
        lo, hi = _unpack_bf16_pairs(hs_ref[...])
        proj = lambda w_ref: (jnp.dot(lo, w_ref[:half], preferred_element_type=F32)
                              + jnp.dot(hi, w_ref[half:], preferred_element_type=F32))
        a = proj(wg_ref)
        b = proj(wu_ref)
        hid = a * jax.nn.sigmoid(a) * b
        y_ref[...] = jnp.dot(hid.astype(BF16), wd_ref[...], preferred_element_type=F32)

    @pl.when(tile_on_ref[i] == 0)
    def _():
        y_ref[...] = jnp.zeros_like(y_ref)


def _experts(tile_e, tile_on, hs, wg_b, wu_b, wd_b, tm):
    n_tiles = tile_e.shape[0]
    half = hs.shape[1]
    d, f = wg_b.shape[1:]
    by_expert = lambda i, te, on: (te[i], 0, 0)
    grid_spec = pltpu.PrefetchScalarGridSpec(
        num_scalar_prefetch=2,
        grid=(n_tiles,),
        in_specs=[pl.BlockSpec((tm, half), lambda i, te, on: (i, 0)),
                  pl.BlockSpec((None, d, f), by_expert),
                  pl.BlockSpec((None, d, f), by_expert),
                  pl.BlockSpec((None, f, d), by_expert)],
        out_specs=pl.BlockSpec((tm, d), lambda i, te, on: (i, 0)),
    )
    return pl.pallas_call(
        _expert_kernel,
        grid_spec=grid_spec,
        out_shape=jax.ShapeDtypeStruct((n_tiles * tm, d), F32),
        compiler_params=_params("arbitrary"),
        name="experts",
    )(tile_e, tile_on, hs, wg_b, wu_b, wd_b)


def _combine_kernel(pos_ref, x1_ref, gate_ref, ys_hbm, y_ref, buf_ref, sem_ref):
    i = pl.program_id(0)
    n_tiles = pl.num_programs(0)
    tm = y_ref.shape[0]

    def issue(tile, slot):
        def body(r, c):
            for k in range(TOP_K):
                _row_copy(ys_hbm, buf_ref.at[slot, k], sem_ref.at[slot],
                          pos_ref[(tile * tm + r) * TOP_K + k], r).start()
            return c
        lax.fori_loop(0, tm, body, 0)

    def drain(slot):
        def body(r, c):
            for k in range(TOP_K):
                _row_copy(ys_hbm, buf_ref.at[slot, k], sem_ref.at[slot], 0, r).wait()
            return c
        lax.fori_loop(0, tm, body, 0)

    slot = i % 2

    @pl.when(i == 0)
    def _():
        issue(0, 0)

    drain(slot)

    @pl.when(i + 1 < n_tiles)
    def _():
        issue(i + 1, 1 - slot)

    gate = gate_ref[...]
    y_ref[...] = (x1_ref[...] + gate[:, 0:1] * buf_ref[slot, 0]
                  + gate[:, 1:2] * buf_ref[slot, 1])


def _combine(pos_flat, x1, gate, ys, tm):
    n, d = x1.shape
    grid_spec = pltpu.PrefetchScalarGridSpec(
        num_scalar_prefetch=1,
        grid=(n // tm,),
        in_specs=[pl.BlockSpec((tm, d), lambda i, pos: (i, 0)),
                  pl.BlockSpec((tm, LANES), lambda i, pos: (i, 0)),
                  pl.BlockSpec(memory_space=pl.ANY)],
        out_specs=pl.BlockSpec((tm, d), lambda i, pos: (i, 0)),
        scratch_shapes=[pltpu.VMEM((2, TOP_K, tm, d), F32), pltpu.SemaphoreType.DMA((2,))],
    )
    return pl.pallas_call(
        _combine_kernel,
        grid_spec=grid_spec,
        out_shape=jax.ShapeDtypeStruct((n, d), F32),
        compiler_params=_params("arbitrary"),
        name="combine",
    )(pos_flat, x1, gate, ys)


def _bucket_table(dilation):
    dist = np.arange(N_STEPS + 1, dtype=np.int64) * dilation
    max_exact = NUM_BUCKETS // 2
    df = np.maximum(dist, 1).astype(np.float32)
    large = max_exact + (np.log(df / np.float32(max_exact))
                         / np.float32(math.log(MAX_DISTANCE / max_exact))
                         * np.float32(NUM_BUCKETS - max_exact)).astype(np.int32)
    large = np.minimum(large, NUM_BUCKETS - 1)
    return np.where(dist < max_exact, dist, large).astype(np.int32)


def _sample_bias(rel_bias):
    buckets = np.stack([_bucket_table(dil)[::-1] for _, dil in DILATED])
    return rel_bias.astype(F32)[buckets][..., None]


def _tile_plan(counts, n_tiles, tm):
    tiles_e = (counts + tm - 1) // tm
    ends = jnp.cumsum(tiles_e)
    first_row = (ends - tiles_e) * tm
    tile = jnp.arange(n_tiles, dtype=jnp.int32)
    tile_e = jnp.sum((ends[None, :] <= tile[:, None]).astype(jnp.int32), axis=1)
    tile_on = (tile_e < N_EXPERTS).astype(jnp.int32)
    tile_e = jnp.minimum(tile_e, N_EXPERTS - 1)
    i32 = lambda a: a.astype(jnp.int32)
    return (i32(tile_e), tile_on, i32(first_row), i32(first_row + counts),
            i32(tiles_e * tm - counts), i32(ends[-1:]))


TM_PROJ = 256
TM_EXPERT = 256
TM_COMBINE = 256


def kernel(x_prompt, x_sample, cache_k, cache_v, state_pool, rel_bias, ln1_w, w_in,
           q_norm_w, k_norm_w, w_pool, pool_scale, w_o, ln2_w, w_router_group,
           b_router_group, w_router_expert, b_router_expert, w_gate, w_up, w_down):
    depth = w_in.shape[0]
    assert depth == 1
    batch, seq, d_model = x_prompt.shape
    nb, t_new, _ = x_sample.shape
    win = cache_k.shape[2]
    f_exp = w_gate.shape[-1]

    w_in_b = _to_bf16(w_in[0], 256)
    w_o_b = _to_bf16(w_o[0], 256)
    wp_b = _to_bf16(w_pool[0].reshape(-1, POOL_GROUP_WIDTH), 256).reshape(w_pool.shape[1:])
    wg_b = _to_bf16(w_gate[0].reshape(-1, f_exp), 2048).reshape(w_gate.shape[1:])
    wu_b = _to_bf16(w_up[0].reshape(-1, f_exp), 2048).reshape(w_up.shape[1:])
    wd_b = _to_bf16(w_down[0].reshape(-1, d_model), 512).reshape(w_down.shape[1:])
    ln1 = ln1_w[0][None, :]
    ln2 = ln2_w[0][None, :]
    q_gain = jnp.tile(q_norm_w[0], N_HEADS)[None, :] * SCALE
    k_gain = jnp.tile(k_norm_w[0], N_HEADS)[None, :]
    ps = pool_scale[0][None, :]
    blk = np.arange(MXU_DIM) // HEAD_DIM
    avg = jnp.asarray((blk[:, None] == blk[None, :]) / HEAD_DIM, BF16)
    head_of_col = np.arange(ATTN_WIDTH) // HEAD_DIM
    expand_np = (np.arange(LANES)[:, None] == head_of_col[None, :])
    expand = jnp.asarray(expand_np, BF16)
    w_r = jnp.concatenate([w_router_group[0], w_router_expert[0]], axis=1)
    w_r = jnp.pad(w_r, ((0, 0), (0, LANES - w_r.shape[1])))
    wr_hi = w_r.astype(BF16)
    wr_lo = (w_r - wr_hi.astype(F32)).astype(BF16)
    b_r = jnp.pad(jnp.concatenate([b_router_group[0], b_router_expert[0]]),
                  (0, LANES - N_GROUPS - N_EXPERTS))[None, :]

    xp = x_prompt.reshape(batch * seq, d_model)
    proj = _inproj(xp, ln1, w_in_b, q_gain, k_gain, avg, TM_PROJ, batch, seq)
    k, v, u = proj[:3]
    natural = tuple(a.reshape(batch, 1, seq, ATTN_WIDTH) for a in proj[3:6])
    qkv = [natural] + [tuple(proj[6 + 3 * bi:9 + 3 * bi]) for bi in range(len(DILATED) - 1)]
    bias_all = _prompt_bias(rel_bias)
    o_list, l_list = [], []
    for branch, (qd, kd, vd) in enumerate(qkv):
        o, lse = _attn_branch(qd, kd, vd, bias_all, branch)
        o_list.append(o)
        l_list.append(lse)
    mix_p = _mix(o_list, l_list, u, expand, wp_b, ps, seq, TM_PROJ)
    tri = jnp.asarray(np.tril(np.ones((TM_PROJ, TM_PROJ)), -1), BF16)
    x1_p, hp_p, gate_p, eid_p, rank_p, counts_p = _outproj(
        xp, mix_p, w_o_b, ln2, wr_hi, wr_lo, b_r, tri, jnp.zeros((1, LANES), F32), TM_PROJ)

    n_s = nb * t_new
    xs = x_sample.reshape(n_s, d_model)
    k_s, v_s, u_s, q_s, _, _ = _inproj(xs, ln1, w_in_b, q_gain, k_gain, avg, n_s)
    heads_s = (nb, t_new, N_HEADS, HEAD_DIM)
    k_s5, v_s5 = k_s.reshape(heads_s), v_s.reshape(heads_s)
    attn_s, pool_s, sbuf = _sample_mixer(
        q_s.astype(F32).reshape(heads_s), k_s5, v_s5, u_s.reshape(nb, t_new, ATTN_WIDTH),
        state_pool[0], cache_k[0], cache_v[0], _sample_bias(rel_bias), wp_b, ps)
    mix_s = jnp.concatenate([attn_s.reshape(n_s, ATTN_WIDTH), pool_s.reshape(n_s, ATTN_WIDTH)],
                            axis=1)
    x1_s, hp_s, gate_s, eid_s, rank_s, counts = _outproj(
        xs, mix_s, w_o_b, ln2, wr_hi, wr_lo, b_r, tri, counts_p, n_s)

    n_p = batch * seq
    n_pairs = (n_p + n_s) * TOP_K
    tm_e = TM_EXPERT
    n_tiles = -(-n_pairs // tm_e) + N_EXPERTS
    counts_i = counts[0, :N_EXPERTS].astype(jnp.int32)
    tile_e, tile_on, first_row, pad_start, pad_len, tail = _tile_plan(counts_i, n_tiles, tm_e)
    row = lambda eid, rank: (first_row[eid[:, :TOP_K]] + rank[:, :TOP_K]).reshape(-1)
    row_p, row_s = row(eid_p, rank_p), row(eid_s, rank_s)
    hs = _dispatch(jnp.concatenate([row_p, row_s]), pad_start, pad_len, tail, hp_p, hp_s,
                   n_tiles, tm_e, TM_PROJ)
    ys = _experts(tile_e, tile_on, hs, wg_b, wu_b, wd_b, tm_e)
    y_p = _combine(row_p, x1_p, gate_p, ys, TM_COMBINE)
    y_s = _combine(row_s, x1_s, gate_s, ys, n_s)

    keep = min(MAX_DISTANCE, seq)
    kv_shape = (batch, seq, N_HEADS, HEAD_DIM)
    return (y_p.reshape(batch, seq, d_model),
            y_s.reshape(nb, t_new, d_model),
            k.reshape(kv_shape)[None, :, -keep:],
            v.reshape(kv_shape)[None, :, -keep:],
            u.reshape(batch, seq, ATTN_WIDTH)[None, :, -POOL_BUF:],
            k_s5[None], v_s5[None], sbuf[None])
```

```python
import functools
import math

import numpy as np
import jax
import jax.numpy as jnp
from jax import lax
from jax.experimental import pallas as pl
from jax.experimental.pallas import tpu as pltpu

F32 = jnp.float32
BF16 = jnp.bfloat16

N_HEADS = 16
HEAD_DIM = 64
ATTN_WIDTH = N_HEADS * HEAD_DIM
POOL_WINDOWS = (2, 4, 8, 16)
POOL_GROUP_WIDTH = 256
POOL_BUF = max(POOL_WINDOWS) - 1
POOL_HALO = POOL_BUF + 1
DILATED = ((128, 1), (512, 4), (2048, 16))
N_STEPS = 128
QBLOCK = 128
NUM_BUCKETS = 32
MAX_DISTANCE = 2048
PAST_LEN = 16384
N_GROUPS = 4
EXPERTS_PER_GROUP = 4
N_EXPERTS = N_GROUPS * EXPERTS_PER_GROUP
TOP_K = 2
EPS = 1e-6
SCALE = HEAD_DIM ** -0.5
NEG_INF = -1e30
LANES = 128
MXU_DIM = 256
VMEM_LIMIT = 56 * 1024 * 1024


DMA_UNROLL = 8


def _params(*sem, unchecked=False):
    return pltpu.CompilerParams(dimension_semantics=sem, vmem_limit_bytes=VMEM_LIMIT,
                                disable_bounds_checks=unchecked)


def _const_spec(shape):
    zeros = (0,) * len(shape)
    return pl.BlockSpec(shape, lambda *_: zeros)


def _cast_kernel(x_ref, o_ref):
    o_ref[...] = x_ref[...].astype(o_ref.dtype)


def _to_bf16(w2d, block_rows):
    rows, cols = w2d.shape
    return pl.pallas_call(
        _cast_kernel,
        grid=(rows // block_rows,),
        in_specs=[pl.BlockSpec((block_rows, cols), lambda i: (i, 0))],
        out_specs=pl.BlockSpec((block_rows, cols), lambda i: (i, 0)),
        out_shape=jax.ShapeDtypeStruct((rows, cols), BF16),
        compiler_params=_params("parallel"),
        name="cast_bf16",
    )(w2d)


def _inproj_kernel(x_ref, ln_ref, w_ref, qg_ref, kg_ref, avg_ref, *refs, dilated):
    k_ref, v_ref, u_ref, q_ref, kb_ref, vb_ref = refs[:6]
    tm = x_ref.shape[0]
    x = x_ref[...]
    ms = jnp.mean(x * x, axis=-1, keepdims=True)
    h = (x * lax.rsqrt(ms + EPS) * ln_ref[...]).astype(BF16)
    n_chunks = ATTN_WIDTH // MXU_DIM

    def section(s):
        return jnp.dot(h, w_ref[:, s * ATTN_WIDTH:(s + 1) * ATTN_WIDTH],
                       preferred_element_type=F32)

    def head_norm(z, g_ref, c):
        zc = z[:, c * MXU_DIM:(c + 1) * MXU_DIM]
        msh = jnp.dot((zc * zc).astype(BF16), avg_ref[...], preferred_element_type=F32)
        return zc * lax.rsqrt(msh + EPS) * g_ref[:, c * MXU_DIM:(c + 1) * MXU_DIM]

    def emit(chunks, which, f32_ref, bf_ref):
        for c, zc in enumerate(chunks):
            cs = slice(c * MXU_DIM, (c + 1) * MXU_DIM)
            if f32_ref is not None:
                f32_ref[:, cs] = zc
            bf_ref[:, cs] = zc.astype(BF16)
        if not dilated:
            return
        stage_ref = refs[-1]
        for c, zc in enumerate(chunks):
            for half in range(MXU_DIM // LANES):
                stage_ref[2 * c + half] = zc[:, half * LANES:(half + 1) * LANES]
        for bi, (_, dil) in enumerate(DILATED[1:]):
            out_ref = refs[6 + 3 * bi + which]
            n = tm // dil
            for r in range(dil):
                for s in range(ATTN_WIDTH // LANES):
                    rows = stage_ref[s, pl.ds(r, n, stride=dil), :]
                    out_ref[r, :, s * LANES:(s + 1) * LANES] = rows.astype(BF16)

    zq = section(0)
    emit([head_norm(zq, qg_ref, c) for c in range(n_chunks)], 0, None, q_ref)
    zk = section(1)
    emit([head_norm(zk, kg_ref, c) for c in range(n_chunks)], 1, k_ref, kb_ref)
    zv = section(2)
    emit([zv[:, c * MXU_DIM:(c + 1) * MXU_DIM] for c in range(n_chunks)], 2, v_ref, vb_ref)
    u_ref[...] = section(3)


def _inproj(x2d, ln1, w_in_b, q_gain, k_gain, avg, tm, batch=None, seq=None):
    n, d = x2d.shape
    dilated = seq is not None
    row = lambda i: (i, 0)
    wide = pl.BlockSpec((tm, ATTN_WIDTH), row)
    f32o = jax.ShapeDtypeStruct((n, ATTN_WIDTH), F32)
    bf16o = jax.ShapeDtypeStruct((n, ATTN_WIDTH), BF16)
    out_specs = [wide] * 6
    out_shape = [f32o, f32o, f32o, bf16o, bf16o, bf16o]
    scratch = []
    if dilated:
        tiles = seq // tm
        for _, dil in DILATED[1:]:
            spec = pl.BlockSpec((None, dil, tm // dil, ATTN_WIDTH),
                                lambda i: (i // tiles, 0, i % tiles, 0))
            out_specs += [spec] * 3
            out_shape += [jax.ShapeDtypeStruct((batch, dil, seq // dil, ATTN_WIDTH), BF16)] * 3
        scratch = [pltpu.VMEM((ATTN_WIDTH // LANES, tm, LANES), F32)]
    return pl.pallas_call(
        functools.partial(_inproj_kernel, dilated=dilated),
        grid=(n // tm,),
        in_specs=[pl.BlockSpec((tm, d), row),
                  _const_spec((1, d)),
                  pl.BlockSpec(w_in_b.shape, lambda i: (0, 0), pipeline_mode=pl.Buffered(1)),
                  _const_spec((1, ATTN_WIDTH)), _const_spec((1, ATTN_WIDTH)),
                  _const_spec((MXU_DIM, MXU_DIM))],
        out_specs=out_specs,
        out_shape=out_shape,
        scratch_shapes=scratch,
        compiler_params=_params("parallel"),
        name="inproj",
    )(x2d, ln1, w_in_b, q_gain, k_gain, avg)


def _attn_kernel(q_ref, kp_ref, kc_ref, vp_ref, vc_ref, bias_ref, o_ref, lse_ref,
                 s_ref, p_ref):
    i = pl.program_id(2)
    lane = lax.broadcasted_iota(jnp.int32, (QBLOCK, LANES), 1)
    low = lane < HEAD_DIM
    keep_low = low.astype(F32).astype(BF16)
    keep_high = (1.0 - low.astype(F32)).astype(BF16)
    nt = (((1,), (1,)), ((), ()))
    n_pairs = N_HEADS // 2

    def run(with_prev):
        k0 = 0 if with_prev else QBLOCK
        nk = 2 * QBLOCK - k0
        ones = jnp.ones((nk, LANES), BF16)
        for hp in range(n_pairs):
            cs = slice(hp * LANES, (hp + 1) * LANES)
            qp = q_ref[:, cs]
            q2 = jnp.concatenate([qp * keep_low, qp * keep_high], axis=0)
            keys = kc_ref[:, cs]
            if with_prev:
                keys = jnp.concatenate([kp_ref[:, cs], keys], axis=0)
            s2 = lax.dot_general(q2, keys, nt, preferred_element_type=F32)
            s_ref[hp, :, k0:] = s2 + bias_ref[hp, :, k0:]

        m_all = jnp.zeros((QBLOCK, LANES), F32)
        for hp in range(n_pairs):
            for sub in range(2):
                rows = slice(sub * QBLOCK, (sub + 1) * QBLOCK)
                m = jnp.max(s_ref[hp, rows, k0:], axis=-1, keepdims=True)
                p_ref[hp, rows, k0:] = jnp.exp(s_ref[hp, rows, k0:] - m).astype(BF16)
                m_all = jnp.where(lane == 2 * hp + sub, m, m_all)

        den_all = jnp.ones((QBLOCK, LANES), F32)
        for hp in range(n_pairs):
            cs = slice(hp * LANES, (hp + 1) * LANES)
            vals = vc_ref[:, cs]
            if with_prev:
                vals = jnp.concatenate([vp_ref[:, cs], vals], axis=0)
            r = jnp.dot(p_ref[hp, :, k0:], jnp.concatenate([vals, ones], axis=1),
                        preferred_element_type=F32)
            den0, den1 = r[:QBLOCK, LANES:], r[QBLOCK:, LANES:]
            o_ref[:, cs] = jnp.where(low, r[:QBLOCK, :LANES] / den0, r[QBLOCK:, :LANES] / den1)
            den_all = jnp.where(lane == 2 * hp, den0, den_all)
            den_all = jnp.where(lane == 2 * hp + 1, den1, den_all)
        lse_ref[...] = m_all + jnp.log(den_all)

    @pl.when(i == 0)
    def _():
        run(False)

    @pl.when(i > 0)
    def _():
        run(True)


def _attn_branch(q, kb, vb, bias_all, branch):
    batch, dil, sub, _ = q.shape
    cur = lambda b, r, i: (b, r, i, 0)
    prev = lambda b, r, i: (b, r, jnp.maximum(i - 1, 0), 0)
    wide_c = pl.BlockSpec((None, None, QBLOCK, ATTN_WIDTH), cur)
    wide_p = pl.BlockSpec((None, None, QBLOCK, ATTN_WIDTH), prev)
    pairs = N_HEADS // 2
    bias_spec = pl.BlockSpec((None, pairs, 2 * QBLOCK, 2 * QBLOCK), lambda b, r, i: (branch, 0, 0, 0))
    return pl.pallas_call(
        _attn_kernel,
        grid=(batch, dil, sub // QBLOCK),
        in_specs=[wide_c, wide_p, wide_c, wide_p, wide_c, bias_spec],
        out_specs=[wide_c, pl.BlockSpec((None, None, QBLOCK, LANES), cur)],
        out_shape=[jax.ShapeDtypeStruct((batch, dil, sub, ATTN_WIDTH), F32),
                   jax.ShapeDtypeStruct((batch, dil, sub, LANES), F32)],
        scratch_shapes=[pltpu.VMEM((pairs, 2 * QBLOCK, 2 * QBLOCK), F32),
                        pltpu.VMEM((pairs, 2 * QBLOCK, 2 * QBLOCK), BF16)],
        compiler_params=_params("parallel", "parallel", "parallel"),
        name=f"attn_d{dil}",
    )(q, kb, kb, vb, vb, bias_all)


def _bias_kernel(rb_ref, bucket_ref, out_ref):
    bucket = bucket_ref[...]
    for h in range(N_HEADS):
        acc = jnp.full(bucket.shape, NEG_INF, F32)
        for b in range(NUM_BUCKETS):
            acc = jnp.where(bucket == b, rb_ref[b, h], acc)
        out_ref[h // 2, (h % 2) * QBLOCK:(h % 2 + 1) * QBLOCK, :] = acc


def _prompt_bias(rel_bias):
    qi = np.arange(QBLOCK)[:, None]
    kk = np.arange(2 * QBLOCK)[None, :]
    j = qi + QBLOCK - kk
    valid = (j >= 0) & (j <= N_STEPS)
    buckets = np.stack([np.where(valid, _bucket_table(dil)[np.clip(j, 0, N_STEPS)], -1)
                        for _, dil in DILATED]).astype(np.int32)
    pairs = N_HEADS // 2
    return pl.pallas_call(
        _bias_kernel,
        grid=(len(DILATED),),
        in_specs=[pl.BlockSpec(memory_space=pltpu.SMEM),
                  pl.BlockSpec((None, QBLOCK, 2 * QBLOCK), lambda g: (g, 0, 0))],
        out_specs=pl.BlockSpec((None, pairs, 2 * QBLOCK, 2 * QBLOCK), lambda g: (g, 0, 0, 0)),
        out_shape=jax.ShapeDtypeStruct((len(DILATED), pairs, 2 * QBLOCK, 2 * QBLOCK), F32),
        compiler_params=_params("parallel"),
        name="bias_table",
    )(rel_bias.astype(F32), jnp.asarray(buckets))


def _split_dot(a, b_bf16):
    hi = a.astype(BF16)
    lo = (a - hi.astype(F32)).astype(BF16)
    return (jnp.dot(hi, b_bf16, preferred_element_type=F32)
            + jnp.dot(lo, b_bf16, preferred_element_type=F32))


def _pool_groups(comb, u, cnt_fn, wp_ref, ps_ref):
    t = u.shape[0]
    outs = []
    run = comb
    width = 1
    for g, w in enumerate(POOL_WINDOWS):
        while width < w:
            run = run + pltpu.roll(run, width, 0)
            width *= 2
        cs = slice(g * POOL_GROUP_WIDTH, (g + 1) * POOL_GROUP_WIDTH)
        d = run[POOL_HALO:POOL_HALO + t, cs] / cnt_fn(w) - u[:, cs]
        y = jnp.dot(d.astype(BF16), wp_ref[g], preferred_element_type=F32)
        outs.append(y * ps_ref[:, cs])
    return outs


def _mix_kernel(o1_ref, o2_ref, o3_ref, l1_ref, l2_ref, l3_ref, u_ref, halo_ref,
                ex_ref, wp_ref, ps_ref, mix_ref, il_ref, ls_ref, *, seq):
    tm = u_ref.shape[0]
    n_slabs = ATTN_WIDTH // LANES
    lses = [l1_ref[0]]
    for bi, (o_ref, l_ref) in enumerate(((o2_ref, l2_ref), (o3_ref, l3_ref))):
        dil = o_ref.shape[0]
        n = tm // dil
        for r in range(dil):
            ls_ref[bi, pl.ds(r, n, stride=dil), :] = l_ref[r]
            for s in range(n_slabs):
                il_ref[bi, s, pl.ds(r, n, stride=dil), :] = o_ref[r, :, s * LANES:(s + 1) * LANES]
        lses.append(ls_ref[bi])
    l1, l2, l3 = lses
    m = jnp.maximum(jnp.maximum(l1, l2), l3)
    e1, e2, e3 = jnp.exp(l1 - m), jnp.exp(l2 - m), jnp.exp(l3 - m)
    inv = 1.0 / (e1 + e2 + e3)
    ex = ex_ref[...]
    w1, w2, w3 = (_split_dot(e * inv, ex) for e in (e1, e2, e3))
    for s in range(n_slabs):
        cs = slice(s * LANES, (s + 1) * LANES)
        attn = w1[:, cs] * o1_ref[0, :, cs] + w2[:, cs] * il_ref[0, s] + w3[:, cs] * il_ref[1, s]
        mix_ref[:, cs] = attn.astype(BF16)

    pos0 = (pl.program_id(0) * tm) % seq
    u = u_ref[...]
    halo = jnp.where(pos0 == 0, 0.0, halo_ref[...])
    comb = jnp.concatenate([halo, u], axis=0)
    pos = pos0 + lax.broadcasted_iota(jnp.int32, (tm, 1), 0)
    cnt_fn = lambda w: jnp.minimum(pos + 1, w).astype(F32)
    for g, y in enumerate(_pool_groups(comb, u, cnt_fn, wp_ref, ps_ref)):
        lo = ATTN_WIDTH + g * POOL_GROUP_WIDTH
        mix_ref[:, lo:lo + POOL_GROUP_WIDTH] = y.astype(BF16)


def _mix(o_list, l_list, u, expand, wp_b, pool_scale, seq, tm):
    n = u.shape[0]
    tiles = seq // tm
    row = lambda i: (i, 0)
    res = lambda i: (i // tiles, 0, i % tiles, 0)
    o_specs = [pl.BlockSpec((None, o.shape[1], tm // o.shape[1], ATTN_WIDTH), res) for o in o_list]
    l_specs = [pl.BlockSpec((None, l.shape[1], tm // l.shape[1], LANES), res) for l in l_list]
    halo = pl.BlockSpec((POOL_HALO, ATTN_WIDTH),
                        lambda i: (jnp.maximum(i * (tm // POOL_HALO) - 1, 0), 0))
    n_dilated = len(o_list) - 1
    return pl.pallas_call(
        functools.partial(_mix_kernel, seq=seq),
        grid=(n // tm,),
        in_specs=o_specs + l_specs + [pl.BlockSpec((tm, ATTN_WIDTH), row), halo,
                                      _const_spec(expand.shape), _const_spec(wp_b.shape),
                                      _const_spec((1, ATTN_WIDTH))],
        out_specs=pl.BlockSpec((tm, 2 * ATTN_WIDTH), row),
        out_shape=jax.ShapeDtypeStruct((n, 2 * ATTN_WIDTH), BF16),
        scratch_shapes=[pltpu.VMEM((n_dilated, ATTN_WIDTH // LANES, tm, LANES), F32),
                        pltpu.VMEM((n_dilated, tm, LANES), F32)],
        compiler_params=_params("parallel"),
        name="mix",
    )(*o_list, *l_list, u, u, expand, wp_b, pool_scale)


def _sample_attn_kernel(q_ref, kn_ref, vn_ref, k_ref, v_ref, bc_ref, bn_ref, o_ref):
    n_heads, _, t_new = q_ref.shape
    n_br = bc_ref.shape[0]
    lane_max = lambda a: jnp.max(a, axis=1, keepdims=True)
    lane_sum = lambda a: jnp.sum(a, axis=1, keepdims=True)
    for h in range(n_heads):
        kt, vt, knt, vnt = k_ref[h], v_ref[h], kn_ref[h], vn_ref[h]
        for t in range(t_new):
            qc = q_ref[h, :, t:t + 1]
            s_c = jnp.sum(kt * qc, axis=0, keepdims=True)
            s_n = jnp.sum(knt * qc, axis=0, keepdims=True)
            sc = [s_c + bc_ref[br, t, h:h + 1, :] for br in range(n_br)]
            sn = [s_n + bn_ref[br, t, h:h + 1, :] for br in range(n_br)]
            m = functools.reduce(jnp.maximum, [lane_max(a) for a in sc + sn])
            p_c = functools.reduce(jnp.add, [jnp.exp(a - m) for a in sc])
            p_n = functools.reduce(jnp.add, [jnp.exp(a - m) for a in sn])
            den = lane_sum(p_c) + lane_sum(p_n)
            o_ref[h, :, t:t + 1] = (lane_sum(vt * p_c) + lane_sum(vnt * p_n)) / den


def _sample_attn(q_t, kn_t, vn_t, k_t, v_t, bias_c, bias_n, heads_per_step):
    nb, nh, hd, t_new = q_t.shape
    win = k_t.shape[-1]
    hg = heads_per_step
    small = pl.BlockSpec((None, hg, hd, t_new), lambda b, g: (b, g, 0, 0))
    wide = pl.BlockSpec((None, hg, hd, win), lambda b, g: (b, g, 0, 0))
    n_br = bias_c.shape[0]
    return pl.pallas_call(
        _sample_attn_kernel,
        grid=(nb, nh // hg),
        in_specs=[small, small, small, wide, wide,
                  pl.BlockSpec((n_br, t_new, hg, win), lambda b, g: (0, 0, g, 0)),
                  pl.BlockSpec((n_br, t_new, hg, t_new), lambda b, g: (0, 0, g, 0))],
        out_specs=small,
        out_shape=jax.ShapeDtypeStruct((nb, nh, hd, t_new), F32),
        compiler_params=_params("parallel", "parallel"),
        name="sample_attn",
    )(q_t, kn_t, vn_t, k_t, v_t, bias_c, bias_n)


def _sample_pool_kernel(u_ref, st_ref, wp_ref, ps_ref, pool_ref, sbuf_ref, comb_ref, *, start):
    t_new = u_ref.shape[0]
    u = u_ref[...]
    comb_ref[...] = jnp.zeros_like(comb_ref)
    comb_ref[1:POOL_HALO, :] = st_ref[...]
    comb_ref[POOL_HALO:POOL_HALO + t_new, :] = u
    pos = start + lax.broadcasted_iota(jnp.int32, (t_new, 1), 0)
    cnt_fn = lambda w: jnp.minimum(pos + 1, w).astype(F32)
    for g, y in enumerate(_pool_groups(comb_ref[...], u, cnt_fn, wp_ref, ps_ref)):
        pool_ref[:, g * POOL_GROUP_WIDTH:(g + 1) * POOL_GROUP_WIDTH] = y
    sbuf_ref[:POOL_BUF - t_new, :] = st_ref[t_new:, :]
    sbuf_ref[POOL_BUF - t_new:, :] = u


def _sample_pool(u, state, wp_b, pool_scale):
    nb, t_new, w = u.shape
    flat = lambda rows: pl.BlockSpec((None, rows, w), lambda b: (b, 0, 0))
    return pl.pallas_call(
        functools.partial(_sample_pool_kernel, start=PAST_LEN),
        grid=(nb,),
        in_specs=[flat(t_new), flat(POOL_BUF), _const_spec(wp_b.shape), _const_spec((1, w))],
        out_specs=[flat(t_new), flat(POOL_BUF)],
        out_shape=[jax.ShapeDtypeStruct((nb, t_new, w), F32),
                   jax.ShapeDtypeStruct((nb, POOL_BUF, w), F32)],
        scratch_shapes=[pltpu.VMEM((POOL_HALO + 8, w), F32)],
        compiler_params=_params("parallel"),
        name="sample_pool",
    )(u, state, wp_b, pool_scale)


def _pack_bf16_pairs(h):
    half = h.shape[1] // 2
    bits = lambda a: lax.bitcast_convert_type(a.astype(BF16).astype(F32), jnp.uint32)
    return (bits(h[:, :half]) >> 16) | (bits(h[:, half:]) & jnp.uint32(0xFFFF0000))


def _unpack_bf16_pairs(u):
    lo = lax.bitcast_convert_type(u << 16, F32).astype(BF16)
    hi = lax.bitcast_convert_type(u & jnp.uint32(0xFFFF0000), F32).astype(BF16)
    return lo, hi


def _outproj_kernel(x_ref, mix_ref, wo_ref, ln_ref, wrh_ref, wrl_ref, br_ref, tri_ref, cin_ref,
                    x1_ref, hp_ref, gate_ref, eid_ref, rank_ref, cout_ref, count_ref):
    @pl.when(pl.program_id(0) == 0)
    def _():
        count_ref[...] = cin_ref[...]

    x1 = x_ref[...] + jnp.dot(mix_ref[...].astype(BF16), wo_ref[...],
                              preferred_element_type=F32)
    x1_ref[...] = x1
    ms = jnp.mean(x1 * x1, axis=-1, keepdims=True)
    h = x1 * lax.rsqrt(ms + EPS) * ln_ref[...]
    hp_ref[...] = _pack_bf16_pairs(h)
    hi = h.astype(BF16)
    lo = (h - hi.astype(F32)).astype(BF16)
    lg = (jnp.dot(hi, wrh_ref[...], preferred_element_type=F32)
          + jnp.dot(lo, wrh_ref[...], preferred_element_type=F32)
          + jnp.dot(hi, wrl_ref[...], preferred_element_type=F32)) + br_ref[...]

    lane = lax.broadcasted_iota(jnp.int32, lg.shape, 1).astype(F32)
    big = float(LANES)
    row_max = lambda mask: jnp.max(jnp.where(mask, lg, -jnp.inf), axis=-1, keepdims=True)
    first = lambda mask: jnp.min(jnp.where(mask, lane, big), axis=-1, keepdims=True)
    is_g = lane < N_GROUPS
    mg = row_max(is_g)
    g_top = first(jnp.logical_and(is_g, lg == mg))
    den = jnp.sum(jnp.where(is_g, jnp.exp(lg - mg), 0.0), axis=-1, keepdims=True)
    p_top = 1.0 / den
    base = N_GROUPS + EXPERTS_PER_GROUP * g_top
    in_grp = jnp.logical_and(lane >= base, lane < base + EXPERTS_PER_GROUP)
    v1 = row_max(in_grp)
    i1 = first(jnp.logical_and(in_grp, lg == v1))
    rest = jnp.logical_and(in_grp, lane != i1)
    v2 = row_max(rest)
    i2 = first(jnp.logical_and(rest, lg == v2))
    e21 = jnp.exp(v2 - v1)
    s21 = 1.0 + e21
    gate1 = p_top * (1.0 / s21)
    gate2 = p_top * (e21 / s21)
    pick = lambda a, b: jnp.where(lane == 0.0, a, jnp.where(lane == 1.0, b, 0.0))
    gate_ref[...] = pick(gate1, gate2)
    e1, e2 = i1 - N_GROUPS, i2 - N_GROUPS
    eid_ref[...] = pick(e1, e2).astype(jnp.int32)

    hot1 = (lane == e1).astype(F32)
    hot2 = (lane == e2).astype(F32)
    hot = hot1 + hot2
    before = count_ref[...] + jnp.dot(tri_ref[...], hot.astype(BF16), preferred_element_type=F32)
    rank1 = jnp.sum(hot1 * before, axis=-1, keepdims=True)
    rank2 = jnp.sum(hot2 * before, axis=-1, keepdims=True)
    rank_ref[...] = pick(rank1, rank2).astype(jnp.int32)
    count_ref[...] = count_ref[...] + jnp.sum(hot, axis=0, keepdims=True)
    cout_ref[...] = count_ref[...]


def _outproj(x2d, mix, w_o_b, ln2, wr_hi, wr_lo, b_r, tri, counts_in, tm):
    n, d = x2d.shape
    row = lambda i: (i, 0)
    full = pl.BlockSpec((tm, d), row)
    stat = pl.BlockSpec((tm, LANES), row)
    stat_i = jax.ShapeDtypeStruct((n, LANES), jnp.int32)
    return pl.pallas_call(
        _outproj_kernel,
        grid=(n // tm,),
        in_specs=[full, full,
                  pl.BlockSpec(w_o_b.shape, lambda i: (0, 0), pipeline_mode=pl.Buffered(1)),
                  _const_spec((1, d)), _const_spec(wr_hi.shape), _const_spec(wr_lo.shape),
                  _const_spec((1, LANES)), _const_spec((tm, tm)), _const_spec((1, LANES))],
        out_specs=[full, pl.BlockSpec((tm, d // 2), row), stat, stat, stat,
                   _const_spec((1, LANES))],
        out_shape=[jax.ShapeDtypeStruct((n, d), F32),
                   jax.ShapeDtypeStruct((n, d // 2), jnp.uint32),
                   jax.ShapeDtypeStruct((n, LANES), F32), stat_i, stat_i,
                   jax.ShapeDtypeStruct((1, LANES), F32)],
        scratch_shapes=[pltpu.VMEM((1, LANES), F32)],
        compiler_params=_params("arbitrary"),
        name="outproj",
    )(x2d, mix, w_o_b, ln2, wr_hi, wr_lo, b_r, tri[:tm, :tm], counts_in)


def _row_copy(src_hbm, dst_vmem, sem, src_row, dst_row):
    return pltpu.make_async_copy(src_hbm.at[pl.ds(src_row, 1)],
                                 dst_vmem.at[pl.ds(dst_row, 1)], sem)


def _dispatch_kernel(pos_ref, pos_s_ref, pad_start_ref, pad_len_ref, tail_ref, hp_ref, hps_ref,
                     hs_ref, sem_ref, zero_ref, *, tile_rows):
    i = pl.program_id(0)
    sem = sem_ref.at[0]

    def scatter(src_ref, dst_rows_ref):
        n_rows = src_ref.shape[0]

        def put(r, k):
            return _row_copy(src_ref, hs_ref, sem, r, dst_rows_ref[0, r * TOP_K + k])

        def issue(r, c):
            for k in range(TOP_K):
                put(r, k).start()
            return c

        def drain(r, c):
            for k in range(TOP_K):
                put(r, k).wait()
            return c

        lax.fori_loop(0, n_rows, issue, 0, unroll=DMA_UNROLL)
        lax.fori_loop(0, n_rows, drain, 0, unroll=DMA_UNROLL)

    scatter(hp_ref, pos_ref)

    @pl.when(i == pl.num_programs(0) - 1)
    def _():
        scatter(hps_ref, pos_s_ref)
        zero_ref[...] = jnp.zeros_like(zero_ref)
        for e in range(N_EXPERTS):
            def fill(r):
                return _row_copy(zero_ref, hs_ref, sem, 0, pad_start_ref[e] + r)

            def start(r, c):
                fill(r).start()
                return c

            def wait(r, c):
                fill(r).wait()
                return c

            lax.fori_loop(0, pad_len_ref[e], start, 0)
            lax.fori_loop(0, pad_len_ref[e], wait, 0)

        n_tiles = hs_ref.shape[0] // tile_rows

        def fill_tile(t):
            rows = pl.ds(pl.multiple_of(t * tile_rows, tile_rows), tile_rows)
            return pltpu.make_async_copy(zero_ref, hs_ref.at[rows], sem)

        def start_tile(t, c):
            fill_tile(t).start()
            return c

        def wait_tile(t, c):
            fill_tile(t).wait()
            return c

        lax.fori_loop(tail_ref[0], n_tiles, start_tile, 0)
        lax.fori_loop(tail_ref[0], n_tiles, wait_tile, 0)


def _tile_rows_spec(tm):
    return pl.BlockSpec((None, 1, TOP_K * tm), lambda i: (i, 0, 0), memory_space=pltpu.SMEM)


def _dispatch(rows_p, rows_s, pad_start, pad_len, tail, hp, hp_s, n_tiles, tile_rows, tm):
    n, half = hp.shape
    smem = pl.BlockSpec(memory_space=pltpu.SMEM)
    return pl.pallas_call(
        functools.partial(_dispatch_kernel, tile_rows=tile_rows),
        grid=(n // tm,),
        in_specs=[_tile_rows_spec(tm), smem, smem, smem, smem,
                  pl.BlockSpec((tm, half), lambda i: (i, 0)),
                  pl.BlockSpec(hp_s.shape, lambda i: (0, 0))],
        out_specs=pl.BlockSpec(memory_space=pl.ANY),
        out_shape=jax.ShapeDtypeStruct((n_tiles * tile_rows, half), jnp.uint32),
        scratch_shapes=[pltpu.SemaphoreType.DMA((1,)), pltpu.VMEM((tile_rows, half), jnp.uint32)],
        compiler_params=_params("arbitrary", unchecked=True),
        name="dispatch",
    )(rows_p, rows_s, pad_start, pad_len, tail, hp, hp_s)


def _expert_kernel(tile_e_ref, tile_on_ref, hs_ref, wg_ref, wu_ref, wd_ref, y_ref):
    i = pl.program_id(0)
    half = hs_ref.shape[1]

    @pl.when(tile_on_ref[i] == 1)
    def _():
        lo, hi = _unpack_bf16_pairs(hs_ref[...])
        proj = lambda w_ref: (jnp.dot(lo, w_ref[:half], preferred_element_type=F32)
                              + jnp.dot(hi, w_ref[half:], preferred_element_type=F32))
        a = proj(wg_ref)
        b = proj(wu_ref)
        hid = a * jax.nn.sigmoid(a) * b
        y = jnp.dot(hid.astype(BF16), wd_ref[...], preferred_element_type=F32)
        y_ref[...] = _pack_bf16_pairs(y)

    @pl.when(tile_on_ref[i] == 0)
    def _():
        y_ref[...] = jnp.zeros_like(y_ref)


def _experts(tile_e, tile_on, hs, wg_b, wu_b, wd_b, tm):
    n_tiles = tile_e.shape[0]
    half = hs.shape[1]
    d, f = wg_b.shape[1:]
    by_expert = lambda i, te, on: (te[i], 0, 0)
    grid_spec = pltpu.PrefetchScalarGridSpec(
        num_scalar_prefetch=2,
        grid=(n_tiles,),
        in_specs=[pl.BlockSpec((tm, half), lambda i, te, on: (i, 0)),
                  pl.BlockSpec((None, d, f), by_expert),
                  pl.BlockSpec((None, d, f), by_expert),
                  pl.BlockSpec((None, f, d), by_expert)],
        out_specs=pl.BlockSpec((tm, half), lambda i, te, on: (i, 0)),
    )
    return pl.pallas_call(
        _expert_kernel,
        grid_spec=grid_spec,
        out_shape=jax.ShapeDtypeStruct((n_tiles * tm, half), jnp.uint32),
        compiler_params=_params("arbitrary"),
        name="experts",
    )(tile_e, tile_on, hs, wg_b, wu_b, wd_b)


def _combine_kernel(rows_ref, next_rows_ref, x1_ref, gate_ref, ys_hbm, y_ref, buf_ref, sem_ref):
    i = pl.program_id(0)
    n_tiles = pl.num_programs(0)
    tm = y_ref.shape[0]
    half = buf_ref.shape[-1]

    def issue(src_rows_ref, slot):
        def body(r, c):
            for k in range(TOP_K):
                _row_copy(ys_hbm, buf_ref.at[slot, k], sem_ref.at[slot],
                          src_rows_ref[0, r * TOP_K + k], r).start()
            return c
        lax.fori_loop(0, tm, body, 0, unroll=DMA_UNROLL)

    def drain(slot):
        def body(r, c):
            for k in range(TOP_K):
                _row_copy(ys_hbm, buf_ref.at[slot, k], sem_ref.at[slot], 0, r).wait()
            return c
        lax.fori_loop(0, tm, body, 0, unroll=DMA_UNROLL)

    slot = i % 2

    @pl.when(i == 0)
    def _():
        issue(rows_ref, 0)

    drain(slot)

    @pl.when(i + 1 < n_tiles)
    def _():
        issue(next_rows_ref, 1 - slot)

    gate = gate_ref[...]
    g0, g1 = gate[:, 0:1], gate[:, 1:2]
    lo0, hi0 = _unpack_bf16_pairs(buf_ref[slot, 0])
    lo1, hi1 = _unpack_bf16_pairs(buf_ref[slot, 1])
    y_ref[:, :half] = x1_ref[:, :half] + g0 * lo0.astype(F32) + g1 * lo1.astype(F32)
    y_ref[:, half:] = x1_ref[:, half:] + g0 * hi0.astype(F32) + g1 * hi1.astype(F32)


def _combine(rows, x1, gate, ys, tm):
    n, d = x1.shape
    half = ys.shape[1]
    last = n // tm - 1
    next_spec = pl.BlockSpec((None, 1, TOP_K * tm), lambda i: (jnp.minimum(i + 1, last), 0, 0),
                             memory_space=pltpu.SMEM)
    return pl.pallas_call(
        _combine_kernel,
        grid=(n // tm,),
        in_specs=[_tile_rows_spec(tm), next_spec,
                  pl.BlockSpec((tm, d), lambda i: (i, 0)),
                  pl.BlockSpec((tm, LANES), lambda i: (i, 0)),
                  pl.BlockSpec(memory_space=pl.ANY)],
        out_specs=pl.BlockSpec((tm, d), lambda i: (i, 0)),
        out_shape=jax.ShapeDtypeStruct((n, d), F32),
        scratch_shapes=[pltpu.VMEM((2, TOP_K, tm, half), jnp.uint32),
                        pltpu.SemaphoreType.DMA((2,))],
        compiler_params=_params("arbitrary", unchecked=True),
        name="combine",
    )(rows, rows, x1, gate, ys)


def _bucket_table(dilation):
    dist = np.arange(N_STEPS + 1, dtype=np.int64) * dilation
    max_exact = NUM_BUCKETS // 2
    df = np.maximum(dist, 1).astype(np.float32)
    large = max_exact + (np.log(df / np.float32(max_exact))
                         / np.float32(math.log(MAX_DISTANCE / max_exact))
                         * np.float32(NUM_BUCKETS - max_exact)).astype(np.int32)
    large = np.minimum(large, NUM_BUCKETS - 1)
    return np.where(dist < max_exact, dist, large).astype(np.int32)


def _sample_bias(rel_bias, t_new, win):
    buckets = np.stack([_bucket_table(dil) for _, dil in DILATED])
    by_step = rel_bias.astype(F32)[buckets]
    dil = jnp.asarray([d for _, d in DILATED], jnp.int32)[:, None, None, None]
    t = jnp.arange(t_new, dtype=jnp.int32)[None, :, None, None]
    j = jnp.arange(N_STEPS + 1, dtype=jnp.int32)[None, None, :, None]

    def spread(first, count):
        pos = first + jnp.arange(count, dtype=jnp.int32)[None, None, None, :]
        hit = pos == win + t - dil * j
        vals = jnp.einsum("bjh,btjp->bthp", by_step, hit.astype(F32),
                          precision=lax.Precision.HIGHEST)
        return jnp.where(jnp.any(hit, axis=2)[:, :, None, :], vals, NEG_INF)

    return spread(0, win), spread(win, t_new)


def _tile_plan(counts, n_tiles, tm):
    tiles_e = (counts + tm - 1) // tm
    ends = jnp.cumsum(tiles_e)
    first_row = (ends - tiles_e) * tm
    tile = jnp.arange(n_tiles, dtype=jnp.int32)
    tile_e = jnp.sum((ends[None, :] <= tile[:, None]).astype(jnp.int32), axis=1)
    tile_on = (tile_e < N_EXPERTS).astype(jnp.int32)
    tile_e = jnp.minimum(tile_e, N_EXPERTS - 1)
    i32 = lambda a: a.astype(jnp.int32)
    return (i32(tile_e), tile_on, i32(first_row), i32(first_row + counts),
            i32(tiles_e * tm - counts), i32(ends[-1:]))


TM_PROJ = 256
TM_EXPERT = 256
SAMPLE_HEADS_PER_STEP = 8


def kernel(x_prompt, x_sample, cache_k, cache_v, state_pool, rel_bias, ln1_w, w_in,
           q_norm_w, k_norm_w, w_pool, pool_scale, w_o, ln2_w, w_router_group,
           b_router_group, w_router_expert, b_router_expert, w_gate, w_up, w_down):
    depth = w_in.shape[0]
    assert depth == 1
    batch, seq, d_model = x_prompt.shape
    nb, t_new, _ = x_sample.shape
    win = cache_k.shape[2]
    f_exp = w_gate.shape[-1]

    w_in_b = _to_bf16(w_in[0], 256)
    w_o_b = _to_bf16(w_o[0], 256)
    wp_b = _to_bf16(w_pool[0].reshape(-1, POOL_GROUP_WIDTH), 256).reshape(w_pool.shape[1:])
    wg_b = _to_bf16(w_gate[0].reshape(-1, f_exp), 2048).reshape(w_gate.shape[1:])
    wu_b = _to_bf16(w_up[0].reshape(-1, f_exp), 2048).reshape(w_up.shape[1:])
    wd_b = _to_bf16(w_down[0].reshape(-1, d_model), 512).reshape(w_down.shape[1:])
    ln1 = ln1_w[0][None, :]
    ln2 = ln2_w[0][None, :]
    q_gain = jnp.tile(q_norm_w[0], N_HEADS)[None, :] * SCALE
    k_gain = jnp.tile(k_norm_w[0], N_HEADS)[None, :]
    ps = pool_scale[0][None, :]
    blk = np.arange(MXU_DIM) // HEAD_DIM
    avg = jnp.asarray((blk[:, None] == blk[None, :]) / HEAD_DIM, BF16)
    head_of_col = np.arange(ATTN_WIDTH) // HEAD_DIM
    expand_np = (np.arange(LANES)[:, None] == head_of_col[None, :])
    expand = jnp.asarray(expand_np, BF16)
    w_r = jnp.concatenate([w_router_group[0], w_router_expert[0]], axis=1)
    w_r = jnp.pad(w_r, ((0, 0), (0, LANES - w_r.shape[1])))
    wr_hi = w_r.astype(BF16)
    wr_lo = (w_r - wr_hi.astype(F32)).astype(BF16)
    b_r = jnp.pad(jnp.concatenate([b_router_group[0], b_router_expert[0]]),
                  (0, LANES - N_GROUPS - N_EXPERTS))[None, :]

    xp = x_prompt.reshape(batch * seq, d_model)
    proj = _inproj(xp, ln1, w_in_b, q_gain, k_gain, avg, TM_PROJ, batch, seq)
    k, v, u = proj[:3]
    natural = tuple(a.reshape(batch, 1, seq, ATTN_WIDTH) for a in proj[3:6])
    qkv = [natural] + [tuple(proj[6 + 3 * bi:9 + 3 * bi]) for bi in range(len(DILATED) - 1)]
    bias_all = _prompt_bias(rel_bias)
    o_list, l_list = [], []
    for branch, (qd, kd, vd) in enumerate(qkv):
        o, lse = _attn_branch(qd, kd, vd, bias_all, branch)
        o_list.append(o)
        l_list.append(lse)
    mix_p = _mix(o_list, l_list, u, expand, wp_b, ps, seq, TM_PROJ)
    tri = jnp.asarray(np.tril(np.ones((TM_PROJ, TM_PROJ)), -1), BF16)
    x1_p, hp_p, gate_p, eid_p, rank_p, counts_p = _outproj(
        xp, mix_p, w_o_b, ln2, wr_hi, wr_lo, b_r, tri, jnp.zeros((1, LANES), F32), TM_PROJ)

    n_s = nb * t_new
    xs = x_sample.reshape(n_s, d_model)
    k_s, v_s, u_s, q_s, _, _ = _inproj(xs, ln1, w_in_b, q_gain, k_gain, avg, n_s)
    heads_s = (nb, t_new, N_HEADS, HEAD_DIM)
    k_s5, v_s5 = k_s.reshape(heads_s), v_s.reshape(heads_s)
    to_lanes = lambda a: jnp.transpose(a, (0, 2, 3, 1))
    bias_c, bias_n = _sample_bias(rel_bias, t_new, win)
    attn_t = _sample_attn(to_lanes(q_s.astype(F32).reshape(heads_s)), to_lanes(k_s5),
                          to_lanes(v_s5), to_lanes(cache_k[0]), to_lanes(cache_v[0]),
                          bias_c, bias_n, SAMPLE_HEADS_PER_STEP)
    attn_s = jnp.transpose(attn_t, (0, 3, 1, 2)).reshape(n_s, ATTN_WIDTH)
    pool_s, sbuf = _sample_pool(u_s.reshape(nb, t_new, ATTN_WIDTH), state_pool[0], wp_b, ps)
    mix_s = jnp.concatenate([attn_s, pool_s.reshape(n_s, ATTN_WIDTH)], axis=1)
    x1_s, hp_s, gate_s, eid_s, rank_s, counts = _outproj(
        xs, mix_s, w_o_b, ln2, wr_hi, wr_lo, b_r, tri, counts_p, n_s)

    n_p = batch * seq
    n_pairs = (n_p + n_s) * TOP_K
    tm_e = TM_EXPERT
    n_tiles = -(-n_pairs // tm_e) + N_EXPERTS
    counts_i = counts[0, :N_EXPERTS].astype(jnp.int32)
    tile_e, tile_on, first_row, pad_start, pad_len, tail = _tile_plan(counts_i, n_tiles, tm_e)
    rows = lambda eid, rank, tm: (first_row[eid[:, :TOP_K]] + rank[:, :TOP_K]).reshape(-1, 1, TOP_K * tm)
    rows_p, rows_s = rows(eid_p, rank_p, TM_PROJ), rows(eid_s, rank_s, n_s)
    hs = _dispatch(rows_p, rows_s[0], pad_start, pad_len, tail, hp_p, hp_s, n_tiles, tm_e, TM_PROJ)
    ys = _experts(tile_e, tile_on, hs, wg_b, wu_b, wd_b, tm_e)
    y_p = _combine(rows_p, x1_p, gate_p, ys, TM_PROJ)
    y_s = _combine(rows_s, x1_s, gate_s, ys, n_s)

    keep = min(MAX_DISTANCE, seq)
    kv_shape = (batch, seq, N_HEADS, HEAD_DIM)
    return (y_p.reshape(batch, seq, d_model),
            y_s.reshape(nb, t_new, d_model),
            k.reshape(kv_shape)[None, :, -keep:],
            v.reshape(kv_shape)[None, :, -keep:],
            u.reshape(batch, seq, ATTN_WIDTH)[None, :, -POOL_BUF:],
            k_s5[None], v_s5[None], sbuf[None])
```

```python
import functools
import math

import numpy as np
import jax
import jax.numpy as jnp
from jax import lax
from jax.experimental import pallas as pl
from jax.experimental.pallas import tpu as pltpu

F32 = jnp.float32
BF16 = jnp.bfloat16

N_HEADS = 16
HEAD_DIM = 64
ATTN_WIDTH = N_HEADS * HEAD_DIM
POOL_WINDOWS = (2, 4, 8, 16)
POOL_GROUP_WIDTH = 256
POOL_BUF = max(POOL_WINDOWS) - 1
POOL_HALO = POOL_BUF + 1
DILATED = ((128, 1), (512, 4), (2048, 16))
N_STEPS = 128
QBLOCK = 128
NUM_BUCKETS = 32
MAX_DISTANCE = 2048
PAST_LEN = 16384
N_GROUPS = 4
EXPERTS_PER_GROUP = 4
N_EXPERTS = N_GROUPS * EXPERTS_PER_GROUP
TOP_K = 2
EPS = 1e-6
SCALE = HEAD_DIM ** -0.5
NEG_INF = -1e30
LANES = 128
MXU_DIM = 256
VMEM_LIMIT = 56 * 1024 * 1024


DMA_UNROLL = 8


def _params(*sem, unchecked=False):
    return pltpu.CompilerParams(dimension_semantics=sem, vmem_limit_bytes=VMEM_LIMIT,
                                disable_bounds_checks=unchecked)


def _const_spec(shape):
    zeros = (0,) * len(shape)
    return pl.BlockSpec(shape, lambda *_: zeros)


def _cast_kernel(x_ref, o_ref):
    o_ref[...] = x_ref[...].astype(o_ref.dtype)


def _to_bf16(w2d, block_rows):
    rows, cols = w2d.shape
    return pl.pallas_call(
        _cast_kernel,
        grid=(rows // block_rows,),
        in_specs=[pl.BlockSpec((block_rows, cols), lambda i: (i, 0))],
        out_specs=pl.BlockSpec((block_rows, cols), lambda i: (i, 0)),
        out_shape=jax.ShapeDtypeStruct((rows, cols), BF16),
        compiler_params=_params("parallel"),
        name="cast_bf16",
    )(w2d)


def _inproj_kernel(x_ref, ln_ref, w_ref, qg_ref, kg_ref, avg_ref, *refs, dilated):
    k_ref, v_ref, u_ref, q_ref, kb_ref, vb_ref = refs[:6]
    tm = x_ref.shape[0]
    x = x_ref[...]
    ms = jnp.mean(x * x, axis=-1, keepdims=True)
    h = (x * lax.rsqrt(ms + EPS) * ln_ref[...]).astype(BF16)
    n_chunks = ATTN_WIDTH // MXU_DIM

    def section(s):
        return jnp.dot(h, w_ref[:, s * ATTN_WIDTH:(s + 1) * ATTN_WIDTH],
                       preferred_element_type=F32)

    def head_norm(z, g_ref, c):
        zc = z[:, c * MXU_DIM:(c + 1) * MXU_DIM]
        msh = jnp.dot((zc * zc).astype(BF16), avg_ref[...], preferred_element_type=F32)
        return zc * lax.rsqrt(msh + EPS) * g_ref[:, c * MXU_DIM:(c + 1) * MXU_DIM]

    def emit(chunks, which, f32_ref, bf_ref):
        heads_per_chunk = MXU_DIM // HEAD_DIM
        for c, zc in enumerate(chunks):
            cs = slice(c * MXU_DIM, (c + 1) * MXU_DIM)
            bf_ref[:, cs] = zc.astype(BF16)
            if f32_ref is None:
                continue
            if dilated:
                zt = zc.T
                for j in range(heads_per_chunk):
                    f32_ref[c * heads_per_chunk + j] = zt[j * HEAD_DIM:(j + 1) * HEAD_DIM, :]
            else:
                f32_ref[:, cs] = zc
        if not dilated:
            return
        stage_ref = refs[-1]
        for c, zc in enumerate(chunks):
            for half in range(MXU_DIM // LANES):
                stage_ref[2 * c + half] = zc[:, half * LANES:(half + 1) * LANES]
        for bi, (_, dil) in enumerate(DILATED[1:]):
            out_ref = refs[6 + 3 * bi + which]
            n = tm // dil
            for r in range(dil):
                for s in range(ATTN_WIDTH // LANES):
                    rows = stage_ref[s, pl.ds(r, n, stride=dil), :]
                    out_ref[r, :, s * LANES:(s + 1) * LANES] = rows.astype(BF16)

    zq = section(0)
    emit([head_norm(zq, qg_ref, c) for c in range(n_chunks)], 0, None, q_ref)
    zk = section(1)
    emit([head_norm(zk, kg_ref, c) for c in range(n_chunks)], 1, k_ref, kb_ref)
    zv = section(2)
    emit([zv[:, c * MXU_DIM:(c + 1) * MXU_DIM] for c in range(n_chunks)], 2, v_ref, vb_ref)
    u_ref[...] = section(3)


def _inproj(x2d, ln1, w_in_b, q_gain, k_gain, avg, tm, batch=None, seq=None):
    n, d = x2d.shape
    dilated = seq is not None
    row = lambda i: (i, 0)
    wide = pl.BlockSpec((tm, ATTN_WIDTH), row)
    f32o = jax.ShapeDtypeStruct((n, ATTN_WIDTH), F32)
    bf16o = jax.ShapeDtypeStruct((n, ATTN_WIDTH), BF16)
    out_specs = [wide] * 6
    out_shape = [f32o, f32o, f32o, bf16o, bf16o, bf16o]
    scratch = []
    if dilated:
        tiles = seq // tm
        kv_t = pl.BlockSpec((None, N_HEADS, HEAD_DIM, tm), lambda i: (i // tiles, 0, 0, i % tiles))
        out_specs[:2] = [kv_t, kv_t]
        out_shape[:2] = [jax.ShapeDtypeStruct((batch, N_HEADS, HEAD_DIM, seq), F32)] * 2
        for _, dil in DILATED[1:]:
            spec = pl.BlockSpec((None, dil, tm // dil, ATTN_WIDTH),
                                lambda i: (i // tiles, 0, i % tiles, 0))
            out_specs += [spec] * 3
            out_shape += [jax.ShapeDtypeStruct((batch, dil, seq // dil, ATTN_WIDTH), BF16)] * 3
        scratch = [pltpu.VMEM((ATTN_WIDTH // LANES, tm, LANES), F32)]
    return pl.pallas_call(
        functools.partial(_inproj_kernel, dilated=dilated),
        grid=(n // tm,),
        in_specs=[pl.BlockSpec((tm, d), row),
                  _const_spec((1, d)),
                  pl.BlockSpec(w_in_b.shape, lambda i: (0, 0), pipeline_mode=pl.Buffered(1)),
                  _const_spec((1, ATTN_WIDTH)), _const_spec((1, ATTN_WIDTH)),
                  _const_spec((MXU_DIM, MXU_DIM))],
        out_specs=out_specs,
        out_shape=out_shape,
        scratch_shapes=scratch,
        compiler_params=_params("parallel"),
        name="inproj",
    )(x2d, ln1, w_in_b, q_gain, k_gain, avg)


def _attn_kernel(q_ref, kp_ref, kc_ref, vp_ref, vc_ref, bias_ref, o_ref, lse_ref,
                 s_ref, p_ref):
    i = pl.program_id(2)
    lane = lax.broadcasted_iota(jnp.int32, (QBLOCK, LANES), 1)
    low = lane < HEAD_DIM
    keep_low = low.astype(F32).astype(BF16)
    keep_high = (1.0 - low.astype(F32)).astype(BF16)
    nt = (((1,), (1,)), ((), ()))
    n_pairs = N_HEADS // 2

    def run(with_prev):
        k0 = 0 if with_prev else QBLOCK
        nk = 2 * QBLOCK - k0
        ones = jnp.ones((nk, LANES), BF16)
        for hp in range(n_pairs):
            cs = slice(hp * LANES, (hp + 1) * LANES)
            qp = q_ref[:, cs]
            q2 = jnp.concatenate([qp * keep_low, qp * keep_high], axis=0)
            keys = kc_ref[:, cs]
            if with_prev:
                keys = jnp.concatenate([kp_ref[:, cs], keys], axis=0)
            s2 = lax.dot_general(q2, keys, nt, preferred_element_type=F32)
            s_ref[hp, :, k0:] = s2 + bias_ref[hp, :, k0:]

        m_all = jnp.zeros((QBLOCK, LANES), F32)
        for hp in range(n_pairs):
            for sub in range(2):
                rows = slice(sub * QBLOCK, (sub + 1) * QBLOCK)
                m = jnp.max(s_ref[hp, rows, k0:], axis=-1, keepdims=True)
                p_ref[hp, rows, k0:] = jnp.exp(s_ref[hp, rows, k0:] - m).astype(BF16)
                m_all = jnp.where(lane == 2 * hp + sub, m, m_all)

        den_all = jnp.ones((QBLOCK, LANES), F32)
        for hp in range(n_pairs):
            cs = slice(hp * LANES, (hp + 1) * LANES)
            vals = vc_ref[:, cs]
            if with_prev:
                vals = jnp.concatenate([vp_ref[:, cs], vals], axis=0)
            r = jnp.dot(p_ref[hp, :, k0:], jnp.concatenate([vals, ones], axis=1),
                        preferred_element_type=F32)
            den0, den1 = r[:QBLOCK, LANES:], r[QBLOCK:, LANES:]
            o_ref[:, cs] = jnp.where(low, r[:QBLOCK, :LANES] / den0, r[QBLOCK:, :LANES] / den1)
            den_all = jnp.where(lane == 2 * hp, den0, den_all)
            den_all = jnp.where(lane == 2 * hp + 1, den1, den_all)
        lse_ref[...] = m_all + jnp.log(den_all)

    @pl.when(i == 0)
    def _():
        run(False)

    @pl.when(i > 0)
    def _():
        run(True)


def _attn_branch(q, kb, vb, bias_all, branch):
    batch, dil, sub, _ = q.shape
    cur = lambda b, r, i: (b, r, i, 0)
    prev = lambda b, r, i: (b, r, jnp.maximum(i - 1, 0), 0)
    wide_c = pl.BlockSpec((None, None, QBLOCK, ATTN_WIDTH), cur)
    wide_p = pl.BlockSpec((None, None, QBLOCK, ATTN_WIDTH), prev)
    pairs = N_HEADS // 2
    bias_spec = pl.BlockSpec((None, pairs, 2 * QBLOCK, 2 * QBLOCK), lambda b, r, i: (branch, 0, 0, 0))
    return pl.pallas_call(
        _attn_kernel,
        grid=(batch, dil, sub // QBLOCK),
        in_specs=[wide_c, wide_p, wide_c, wide_p, wide_c, bias_spec],
        out_specs=[wide_c, pl.BlockSpec((None, None, QBLOCK, LANES), cur)],
        out_shape=[jax.ShapeDtypeStruct((batch, dil, sub, ATTN_WIDTH), F32),
                   jax.ShapeDtypeStruct((batch, dil, sub, LANES), F32)],
        scratch_shapes=[pltpu.VMEM((pairs, 2 * QBLOCK, 2 * QBLOCK), F32),
                        pltpu.VMEM((pairs, 2 * QBLOCK, 2 * QBLOCK), BF16)],
        compiler_params=_params("parallel", "parallel", "parallel"),
        name=f"attn_d{dil}",
    )(q, kb, kb, vb, vb, bias_all)


def _bias_kernel(rb_ref, bucket_ref, out_ref):
    bucket = bucket_ref[...]
    for h in range(N_HEADS):
        acc = jnp.full(bucket.shape, NEG_INF, F32)
        for b in range(NUM_BUCKETS):
            acc = jnp.where(bucket == b, rb_ref[b, h], acc)
        out_ref[h // 2, (h % 2) * QBLOCK:(h % 2 + 1) * QBLOCK, :] = acc


def _prompt_bias(rel_bias):
    qi = np.arange(QBLOCK)[:, None]
    kk = np.arange(2 * QBLOCK)[None, :]
    j = qi + QBLOCK - kk
    valid = (j >= 0) & (j <= N_STEPS)
    buckets = np.stack([np.where(valid, _bucket_table(dil)[np.clip(j, 0, N_STEPS)], -1)
                        for _, dil in DILATED]).astype(np.int32)
    pairs = N_HEADS // 2
    return pl.pallas_call(
        _bias_kernel,
        grid=(len(DILATED),),
        in_specs=[pl.BlockSpec(memory_space=pltpu.SMEM),
                  pl.BlockSpec((None, QBLOCK, 2 * QBLOCK), lambda g: (g, 0, 0))],
        out_specs=pl.BlockSpec((None, pairs, 2 * QBLOCK, 2 * QBLOCK), lambda g: (g, 0, 0, 0)),
        out_shape=jax.ShapeDtypeStruct((len(DILATED), pairs, 2 * QBLOCK, 2 * QBLOCK), F32),
        compiler_params=_params("parallel"),
        name="bias_table",
    )(rel_bias.astype(F32), jnp.asarray(buckets))


def _split_dot(a, b_bf16):
    hi = a.astype(BF16)
    lo = (a - hi.astype(F32)).astype(BF16)
    return (jnp.dot(hi, b_bf16, preferred_element_type=F32)
            + jnp.dot(lo, b_bf16, preferred_element_type=F32))


def _pool_groups(comb, u, cnt_fn, wp_ref, ps_ref):
    t = u.shape[0]
    outs = []
    run = comb
    width = 1
    for g, w in enumerate(POOL_WINDOWS):
        while width < w:
            run = run + pltpu.roll(run, width, 0)
            width *= 2
        cs = slice(g * POOL_GROUP_WIDTH, (g + 1) * POOL_GROUP_WIDTH)
        d = run[POOL_HALO:POOL_HALO + t, cs] / cnt_fn(w) - u[:, cs]
        y = jnp.dot(d.astype(BF16), wp_ref[g], preferred_element_type=F32)
        outs.append(y * ps_ref[:, cs])
    return outs


def _mix_kernel(o1_ref, o2_ref, o3_ref, l1_ref, l2_ref, l3_ref, u_ref, halo_ref,
                ex_ref, wp_ref, ps_ref, mix_ref, il_ref, ls_ref, *, seq):
    tm = u_ref.shape[0]
    n_slabs = ATTN_WIDTH // LANES
    lses = [l1_ref[0]]
    for bi, (o_ref, l_ref) in enumerate(((o2_ref, l2_ref), (o3_ref, l3_ref))):
        dil = o_ref.shape[0]
        n = tm // dil
        for r in range(dil):
            ls_ref[bi, pl.ds(r, n, stride=dil), :] = l_ref[r]
            for s in range(n_slabs):
                il_ref[bi, s, pl.ds(r, n, stride=dil), :] = o_ref[r, :, s * LANES:(s + 1) * LANES]
        lses.append(ls_ref[bi])
    l1, l2, l3 = lses
    m = jnp.maximum(jnp.maximum(l1, l2), l3)
    e1, e2, e3 = jnp.exp(l1 - m), jnp.exp(l2 - m), jnp.exp(l3 - m)
    inv = 1.0 / (e1 + e2 + e3)
    ex = ex_ref[...]
    w1, w2, w3 = (_split_dot(e * inv, ex) for e in (e1, e2, e3))
    for s in range(n_slabs):
        cs = slice(s * LANES, (s + 1) * LANES)
        attn = w1[:, cs] * o1_ref[0, :, cs] + w2[:, cs] * il_ref[0, s] + w3[:, cs] * il_ref[1, s]
        mix_ref[:, cs] = attn.astype(BF16)

    pos0 = (pl.program_id(0) * tm) % seq
    u = u_ref[...]
    halo = jnp.where(pos0 == 0, 0.0, halo_ref[...])
    comb = jnp.concatenate([halo, u], axis=0)
    pos = pos0 + lax.broadcasted_iota(jnp.int32, (tm, 1), 0)
    cnt_fn = lambda w: jnp.minimum(pos + 1, w).astype(F32)
    for g, y in enumerate(_pool_groups(comb, u, cnt_fn, wp_ref, ps_ref)):
        lo = ATTN_WIDTH + g * POOL_GROUP_WIDTH
        mix_ref[:, lo:lo + POOL_GROUP_WIDTH] = y.astype(BF16)


def _mix(o_list, l_list, u, expand, wp_b, pool_scale, seq, tm):
    n = u.shape[0]
    tiles = seq // tm
    row = lambda i: (i, 0)
    res = lambda i: (i // tiles, 0, i % tiles, 0)
    o_specs = [pl.BlockSpec((None, o.shape[1], tm // o.shape[1], ATTN_WIDTH), res) for o in o_list]
    l_specs = [pl.BlockSpec((None, l.shape[1], tm // l.shape[1], LANES), res) for l in l_list]
    halo = pl.BlockSpec((POOL_HALO, ATTN_WIDTH),
                        lambda i: (jnp.maximum(i * (tm // POOL_HALO) - 1, 0), 0))
    n_dilated = len(o_list) - 1
    return pl.pallas_call(
        functools.partial(_mix_kernel, seq=seq),
        grid=(n // tm,),
        in_specs=o_specs + l_specs + [pl.BlockSpec((tm, ATTN_WIDTH), row), halo,
                                      _const_spec(expand.shape), _const_spec(wp_b.shape),
                                      _const_spec((1, ATTN_WIDTH))],
        out_specs=pl.BlockSpec((tm, 2 * ATTN_WIDTH), row),
        out_shape=jax.ShapeDtypeStruct((n, 2 * ATTN_WIDTH), BF16),
        scratch_shapes=[pltpu.VMEM((n_dilated, ATTN_WIDTH // LANES, tm, LANES), F32),
                        pltpu.VMEM((n_dilated, tm, LANES), F32)],
        compiler_params=_params("parallel"),
        name="mix",
    )(*o_list, *l_list, u, u, expand, wp_b, pool_scale)


def _sample_attn_kernel(q_ref, kn_ref, vn_ref, k_ref, v_ref, bc_ref, bn_ref, o_ref):
    n_heads, t_new, _ = q_ref.shape
    n_br = bc_ref.shape[0]
    lane_max = lambda a: jnp.max(a, axis=1, keepdims=True)
    lane_sum = lambda a: jnp.sum(a, axis=1, keepdims=True)
    new_lane = lax.broadcasted_iota(jnp.int32, (t_new, t_new), 1)
    nt = (((1,), (1,)), ((), ()))
    for h in range(n_heads):
        q, kn, vn = q_ref[h], kn_ref[h], vn_ref[h]
        s_c = jnp.dot(q.astype(BF16), k_ref[h].astype(BF16), preferred_element_type=F32)
        s_n = jnp.zeros((t_new, t_new), F32)
        for tp in range(t_new):
            s_n = jnp.where(new_lane == tp, lane_sum(q * kn[tp:tp + 1, :]), s_n)
        sc = [s_c + bc_ref[br, h] for br in range(n_br)]
        sn = [s_n + bn_ref[br, h] for br in range(n_br)]
        m = functools.reduce(jnp.maximum, [lane_max(a) for a in sc + sn])
        p_c = functools.reduce(jnp.add, [jnp.exp(a - m) for a in sc])
        p_n = functools.reduce(jnp.add, [jnp.exp(a - m) for a in sn])
        acc = lax.dot_general(p_c.astype(BF16), v_ref[h].astype(BF16), nt,
                              preferred_element_type=F32)
        for tp in range(t_new):
            acc = acc + p_n[:, tp:tp + 1] * vn[tp:tp + 1, :]
        o_ref[h] = acc / (lane_sum(p_c) + lane_sum(p_n))


def _sample_attn(q, kn, vn, k_t, v_t, bias_c, bias_n, heads_per_step):
    nb, nh, t_new, hd = q.shape
    win = k_t.shape[-1]
    hg = heads_per_step
    n_br = bias_c.shape[0]
    small = pl.BlockSpec((None, hg, t_new, hd), lambda b, g: (b, g, 0, 0))
    wide = pl.BlockSpec((None, hg, hd, win), lambda b, g: (b, g, 0, 0))
    return pl.pallas_call(
        _sample_attn_kernel,
        grid=(nb, nh // hg),
        in_specs=[small, small, small, wide, wide,
                  pl.BlockSpec((n_br, hg, t_new, win), lambda b, g: (0, g, 0, 0)),
                  pl.BlockSpec((n_br, hg, t_new, t_new), lambda b, g: (0, g, 0, 0))],
        out_specs=small,
        out_shape=jax.ShapeDtypeStruct((nb, nh, t_new, hd), F32),
        compiler_params=_params("parallel", "parallel"),
        name="sample_attn",
    )(q, kn, vn, k_t, v_t, bias_c, bias_n)


def _sample_pool_kernel(u_ref, st_ref, wp_ref, ps_ref, pool_ref, sbuf_ref, comb_ref, *, start):
    t_new = u_ref.shape[0]
    u = u_ref[...]
    comb_ref[...] = jnp.zeros_like(comb_ref)
    comb_ref[1:POOL_HALO, :] = st_ref[...]
    comb_ref[POOL_HALO:POOL_HALO + t_new, :] = u
    pos = start + lax.broadcasted_iota(jnp.int32, (t_new, 1), 0)
    cnt_fn = lambda w: jnp.minimum(pos + 1, w).astype(F32)
    for g, y in enumerate(_pool_groups(comb_ref[...], u, cnt_fn, wp_ref, ps_ref)):
        pool_ref[:, g * POOL_GROUP_WIDTH:(g + 1) * POOL_GROUP_WIDTH] = y
    sbuf_ref[:POOL_BUF - t_new, :] = st_ref[t_new:, :]
    sbuf_ref[POOL_BUF - t_new:, :] = u


def _sample_pool(u, state, wp_b, pool_scale):
    nb, t_new, w = u.shape
    flat = lambda rows: pl.BlockSpec((None, rows, w), lambda b: (b, 0, 0))
    return pl.pallas_call(
        functools.partial(_sample_pool_kernel, start=PAST_LEN),
        grid=(nb,),
        in_specs=[flat(t_new), flat(POOL_BUF), _const_spec(wp_b.shape), _const_spec((1, w))],
        out_specs=[flat(t_new), flat(POOL_BUF)],
        out_shape=[jax.ShapeDtypeStruct((nb, t_new, w), F32),
                   jax.ShapeDtypeStruct((nb, POOL_BUF, w), F32)],
        scratch_shapes=[pltpu.VMEM((POOL_HALO + 8, w), F32)],
        compiler_params=_params("parallel"),
        name="sample_pool",
    )(u, state, wp_b, pool_scale)


def _pack_bf16_pairs(h):
    half = h.shape[1] // 2
    bits = lambda a: lax.bitcast_convert_type(a.astype(BF16).astype(F32), jnp.uint32)
    return (bits(h[:, :half]) >> 16) | (bits(h[:, half:]) & jnp.uint32(0xFFFF0000))


def _unpack_bf16_pairs(u):
    lo = lax.bitcast_convert_type(u << 16, F32).astype(BF16)
    hi = lax.bitcast_convert_type(u & jnp.uint32(0xFFFF0000), F32).astype(BF16)
    return lo, hi


def _outproj_kernel(x_ref, mix_ref, wo_ref, ln_ref, wr_ref, br_ref, tri_ref, cin_ref,
                    x1_ref, hp_ref, gate_ref, eid_ref, rank_ref, cout_ref, count_ref):
    @pl.when(pl.program_id(0) == 0)
    def _():
        count_ref[...] = cin_ref[...]

    x1 = x_ref[...] + jnp.dot(mix_ref[...].astype(BF16), wo_ref[...],
                              preferred_element_type=F32)
    x1_ref[...] = x1
    ms = jnp.mean(x1 * x1, axis=-1, keepdims=True)
    h = x1 * lax.rsqrt(ms + EPS) * ln_ref[...]
    hp_ref[...] = _pack_bf16_pairs(h)
    lg = jnp.dot(h.astype(BF16), wr_ref[...], preferred_element_type=F32) + br_ref[...]

    lane = lax.broadcasted_iota(jnp.int32, lg.shape, 1).astype(F32)
    big = float(LANES)
    row_max = lambda mask: jnp.max(jnp.where(mask, lg, -jnp.inf), axis=-1, keepdims=True)
    first = lambda mask: jnp.min(jnp.where(mask, lane, big), axis=-1, keepdims=True)
    is_g = lane < N_GROUPS
    mg = row_max(is_g)
    g_top = first(jnp.logical_and(is_g, lg == mg))
    den = jnp.sum(jnp.where(is_g, jnp.exp(lg - mg), 0.0), axis=-1, keepdims=True)
    p_top = 1.0 / den
    base = N_GROUPS + EXPERTS_PER_GROUP * g_top
    in_grp = jnp.logical_and(lane >= base, lane < base + EXPERTS_PER_GROUP)
    v1 = row_max(in_grp)
    i1 = first(jnp.logical_and(in_grp, lg == v1))
    rest = jnp.logical_and(in_grp, lane != i1)
    v2 = row_max(rest)
    i2 = first(jnp.logical_and(rest, lg == v2))
    e21 = jnp.exp(v2 - v1)
    s21 = 1.0 + e21
    gate1 = p_top * (1.0 / s21)
    gate2 = p_top * (e21 / s21)
    pick = lambda a, b: jnp.where(lane == 0.0, a, jnp.where(lane == 1.0, b, 0.0))
    gate_ref[...] = pick(gate1, gate2)
    e1, e2 = i1 - N_GROUPS, i2 - N_GROUPS
    eid_ref[...] = pick(e1, e2).astype(jnp.int32)

    hot1 = (lane == e1).astype(F32)
    hot2 = (lane == e2).astype(F32)
    hot = hot1 + hot2
    before = count_ref[...] + jnp.dot(tri_ref[...], hot.astype(BF16), preferred_element_type=F32)
    rank1 = jnp.sum(hot1 * before, axis=-1, keepdims=True)
    rank2 = jnp.sum(hot2 * before, axis=-1, keepdims=True)
    rank_ref[...] = pick(rank1, rank2).astype(jnp.int32)
    count_ref[...] = count_ref[...] + jnp.sum(hot, axis=0, keepdims=True)
    cout_ref[...] = count_ref[...]


def _outproj(x2d, mix, w_o_b, ln2, w_r, b_r, tri, counts_in, tm):
    n, d = x2d.shape
    row = lambda i: (i, 0)
    full = pl.BlockSpec((tm, d), row)
    stat = pl.BlockSpec((tm, LANES), row)
    stat_i = jax.ShapeDtypeStruct((n, LANES), jnp.int32)
    return pl.pallas_call(
        _outproj_kernel,
        grid=(n // tm,),
        in_specs=[full, full,
                  pl.BlockSpec(w_o_b.shape, lambda i: (0, 0), pipeline_mode=pl.Buffered(1)),
                  _const_spec((1, d)), _const_spec(w_r.shape),
                  _const_spec((1, LANES)), _const_spec((tm, tm)), _const_spec((1, LANES))],
        out_specs=[full, pl.BlockSpec((tm, d // 2), row), stat, stat, stat,
                   _const_spec((1, LANES))],
        out_shape=[jax.ShapeDtypeStruct((n, d), F32),
                   jax.ShapeDtypeStruct((n, d // 2), jnp.uint32),
                   jax.ShapeDtypeStruct((n, LANES), F32), stat_i, stat_i,
                   jax.ShapeDtypeStruct((1, LANES), F32)],
        scratch_shapes=[pltpu.VMEM((1, LANES), F32)],
        compiler_params=_params("arbitrary"),
        name="outproj",
    )(x2d, mix, w_o_b, ln2, w_r, b_r, tri[:tm, :tm], counts_in)


def _row_copy(src_hbm, dst_vmem, sem, src_row, dst_row):
    return pltpu.make_async_copy(src_hbm.at[pl.ds(src_row, 1)],
                                 dst_vmem.at[pl.ds(dst_row, 1)], sem)


def _dispatch_kernel(pos_ref, pos_s_ref, pad_start_ref, pad_len_ref, tail_ref, hp_ref, hps_ref,
                     hs_ref, sem_ref, zero_ref, *, tile_rows):
    i = pl.program_id(0)
    sem = sem_ref.at[0]

    def scatter(src_ref, dst_rows_ref):
        n_rows = src_ref.shape[0]

        def put(r, k):
            return _row_copy(src_ref, hs_ref, sem, r, dst_rows_ref[0, r * TOP_K + k])

        def issue(r, c):
            for k in range(TOP_K):
                put(r, k).start()
            return c

        def drain(r, c):
            for k in range(TOP_K):
                put(r, k).wait()
            return c

        lax.fori_loop(0, n_rows, issue, 0, unroll=DMA_UNROLL)
        lax.fori_loop(0, n_rows, drain, 0, unroll=DMA_UNROLL)

    scatter(hp_ref, pos_ref)

    @pl.when(i == pl.num_programs(0) - 1)
    def _():
        scatter(hps_ref, pos_s_ref)
        zero_ref[...] = jnp.zeros_like(zero_ref)
        for e in range(N_EXPERTS):
            def fill(r):
                return _row_copy(zero_ref, hs_ref, sem, 0, pad_start_ref[e] + r)

            def start(r, c):
                fill(r).start()
                return c

            def wait(r, c):
                fill(r).wait()
                return c

            lax.fori_loop(0, pad_len_ref[e], start, 0)
            lax.fori_loop(0, pad_len_ref[e], wait, 0)

        n_tiles = hs_ref.shape[0] // tile_rows

        def fill_tile(t):
            rows = pl.ds(pl.multiple_of(t * tile_rows, tile_rows), tile_rows)
            return pltpu.make_async_copy(zero_ref, hs_ref.at[rows], sem)

        def start_tile(t, c):
            fill_tile(t).start()
            return c

        def wait_tile(t, c):
            fill_tile(t).wait()
            return c

        lax.fori_loop(tail_ref[0], n_tiles, start_tile, 0)
        lax.fori_loop(tail_ref[0], n_tiles, wait_tile, 0)


def _tile_rows_spec(tm):
    return pl.BlockSpec((None, 1, TOP_K * tm), lambda i: (i, 0, 0), memory_space=pltpu.SMEM)


def _dispatch(rows_p, rows_s, pad_start, pad_len, tail, hp, hp_s, n_tiles, tile_rows, tm):
    n, half = hp.shape
    smem = pl.BlockSpec(memory_space=pltpu.SMEM)
    return pl.pallas_call(
        functools.partial(_dispatch_kernel, tile_rows=tile_rows),
        grid=(n // tm,),
        in_specs=[_tile_rows_spec(tm), smem, smem, smem, smem,
                  pl.BlockSpec((tm, half), lambda i: (i, 0)),
                  pl.BlockSpec(hp_s.shape, lambda i: (0, 0))],
        out_specs=pl.BlockSpec(memory_space=pl.ANY),
        out_shape=jax.ShapeDtypeStruct((n_tiles * tile_rows, half), jnp.uint32),
        scratch_shapes=[pltpu.SemaphoreType.DMA((1,)), pltpu.VMEM((tile_rows, half), jnp.uint32)],
        compiler_params=_params("arbitrary", unchecked=True),
        name="dispatch",
    )(rows_p, rows_s, pad_start, pad_len, tail, hp, hp_s)


def _expert_kernel(tile_e_ref, tile_on_ref, hs_ref, wg_ref, wu_ref, wd_ref, y_ref,
                   wg_b, wu_b, wd_b):
    i = pl.program_id(0)
    half = hs_ref.shape[1]

    @pl.when(jnp.logical_or(i == 0, tile_e_ref[i] != tile_e_ref[jnp.maximum(i - 1, 0)]))
    def _():
        wg_b[...] = wg_ref[...].astype(BF16)
        wu_b[...] = wu_ref[...].astype(BF16)
        wd_b[...] = wd_ref[...].astype(BF16)

    @pl.when(tile_on_ref[i] == 1)
    def _():
        lo, hi = _unpack_bf16_pairs(hs_ref[...])
        proj = lambda w_ref: (jnp.dot(lo, w_ref[:half], preferred_element_type=F32)
                              + jnp.dot(hi, w_ref[half:], preferred_element_type=F32))
        a = proj(wg_b)
        b = proj(wu_b)
        hid = a * jax.nn.sigmoid(a) * b
        y = jnp.dot(hid.astype(BF16), wd_b[...], preferred_element_type=F32)
        y_ref[...] = _pack_bf16_pairs(y)

    @pl.when(tile_on_ref[i] == 0)
    def _():
        y_ref[...] = jnp.zeros_like(y_ref)


def _experts(tile_e, tile_on, hs, w_gate, w_up, w_down, tm):
    n_tiles = tile_e.shape[0]
    half = hs.shape[1]
    d, f = w_gate.shape[1:]
    by_expert = lambda i, te, on: (te[i], 0, 0)
    grid_spec = pltpu.PrefetchScalarGridSpec(
        num_scalar_prefetch=2,
        grid=(n_tiles,),
        in_specs=[pl.BlockSpec((tm, half), lambda i, te, on: (i, 0)),
                  pl.BlockSpec((None, d, f), by_expert),
                  pl.BlockSpec((None, d, f), by_expert),
                  pl.BlockSpec((None, f, d), by_expert)],
        out_specs=pl.BlockSpec((tm, half), lambda i, te, on: (i, 0)),
        scratch_shapes=[pltpu.VMEM((d, f), BF16), pltpu.VMEM((d, f), BF16),
                        pltpu.VMEM((f, d), BF16)],
    )
    return pl.pallas_call(
        _expert_kernel,
        grid_spec=grid_spec,
        out_shape=jax.ShapeDtypeStruct((n_tiles * tm, half), jnp.uint32),
        compiler_params=_params("arbitrary"),
        name="experts",
    )(tile_e, tile_on, hs, w_gate, w_up, w_down)


def _combine_kernel(rows_ref, next_rows_ref, x1_ref, gate_ref, ys_hbm, y_ref, buf_ref, sem_ref):
    i = pl.program_id(0)
    n_tiles = pl.num_programs(0)
    tm = y_ref.shape[0]
    half = buf_ref.shape[-1]

    def issue(src_rows_ref, slot):
        def body(r, c):
            for k in range(TOP_K):
                _row_copy(ys_hbm, buf_ref.at[slot, k], sem_ref.at[slot],
                          src_rows_ref[0, r * TOP_K + k], r).start()
            return c
        lax.fori_loop(0, tm, body, 0, unroll=DMA_UNROLL)

    def drain(slot):
        def body(r, c):
            for k in range(TOP_K):
                _row_copy(ys_hbm, buf_ref.at[slot, k], sem_ref.at[slot], 0, r).wait()
            return c
        lax.fori_loop(0, tm, body, 0, unroll=DMA_UNROLL)

    slot = i % 2

    @pl.when(i == 0)
    def _():
        issue(rows_ref, 0)

    drain(slot)

    @pl.when(i + 1 < n_tiles)
    def _():
        issue(next_rows_ref, 1 - slot)

    gate = gate_ref[...]
    g0, g1 = gate[:, 0:1], gate[:, 1:2]
    lo0, hi0 = _unpack_bf16_pairs(buf_ref[slot, 0])
    lo1, hi1 = _unpack_bf16_pairs(buf_ref[slot, 1])
    y_ref[:, :half] = x1_ref[:, :half] + g0 * lo0.astype(F32) + g1 * lo1.astype(F32)
    y_ref[:, half:] = x1_ref[:, half:] + g0 * hi0.astype(F32) + g1 * hi1.astype(F32)


def _combine(rows, x1, gate, ys, tm):
    n, d = x1.shape
    half = ys.shape[1]
    last = n // tm - 1
    next_spec = pl.BlockSpec((None, 1, TOP_K * tm), lambda i: (jnp.minimum(i + 1, last), 0, 0),
                             memory_space=pltpu.SMEM)
    return pl.pallas_call(
        _combine_kernel,
        grid=(n // tm,),
        in_specs=[_tile_rows_spec(tm), next_spec,
                  pl.BlockSpec((tm, d), lambda i: (i, 0)),
                  pl.BlockSpec((tm, LANES), lambda i: (i, 0)),
                  pl.BlockSpec(memory_space=pl.ANY)],
        out_specs=pl.BlockSpec((tm, d), lambda i: (i, 0)),
        out_shape=jax.ShapeDtypeStruct((n, d), F32),
        scratch_shapes=[pltpu.VMEM((2, TOP_K, tm, half), jnp.uint32),
                        pltpu.SemaphoreType.DMA((2,))],
        compiler_params=_params("arbitrary", unchecked=True),
        name="combine",
    )(rows, rows, x1, gate, ys)


def _bucket_table(dilation):
    dist = np.arange(N_STEPS + 1, dtype=np.int64) * dilation
    max_exact = NUM_BUCKETS // 2
    df = np.maximum(dist, 1).astype(np.float32)
    large = max_exact + (np.log(df / np.float32(max_exact))
                         / np.float32(math.log(MAX_DISTANCE / max_exact))
                         * np.float32(NUM_BUCKETS - max_exact)).astype(np.int32)
    large = np.minimum(large, NUM_BUCKETS - 1)
    return np.where(dist < max_exact, dist, large).astype(np.int32)


def _sample_bias(rel_bias, t_new, win):
    buckets = np.stack([_bucket_table(dil) for _, dil in DILATED])
    by_step = rel_bias.astype(F32)[buckets]
    dil = jnp.asarray([d for _, d in DILATED], jnp.int32)[:, None, None, None]
    t = jnp.arange(t_new, dtype=jnp.int32)[None, :, None, None]
    j = jnp.arange(N_STEPS + 1, dtype=jnp.int32)[None, None, :, None]

    def spread(first, count):
        pos = first + jnp.arange(count, dtype=jnp.int32)[None, None, None, :]
        hit = pos == win + t - dil * j
        vals = jnp.einsum("bjh,btjp->bhtp", by_step, hit.astype(F32),
                          precision=lax.Precision.HIGHEST)
        return jnp.where(jnp.any(hit, axis=2)[:, None, :, :], vals, NEG_INF)

    return spread(0, win), spread(win, t_new)


def _tile_plan(counts, n_tiles, tm):
    tiles_e = (counts + tm - 1) // tm
    ends = jnp.cumsum(tiles_e)
    first_row = (ends - tiles_e) * tm
    tile = jnp.arange(n_tiles, dtype=jnp.int32)
    tile_e = jnp.sum((ends[None, :] <= tile[:, None]).astype(jnp.int32), axis=1)
    tile_on = (tile_e < N_EXPERTS).astype(jnp.int32)
    tile_e = jnp.minimum(tile_e, N_EXPERTS - 1)
    i32 = lambda a: a.astype(jnp.int32)
    return (i32(tile_e), tile_on, i32(first_row), i32(first_row + counts),
            i32(tiles_e * tm - counts), i32(ends[-1:]))


TM_PROJ = 256
TM_EXPERT = 256
SAMPLE_HEADS_PER_STEP = 8


def kernel(x_prompt, x_sample, cache_k, cache_v, state_pool, rel_bias, ln1_w, w_in,
           q_norm_w, k_norm_w, w_pool, pool_scale, w_o, ln2_w, w_router_group,
           b_router_group, w_router_expert, b_router_expert, w_gate, w_up, w_down):
    depth = w_in.shape[0]
    assert depth == 1
    batch, seq, d_model = x_prompt.shape
    nb, t_new, _ = x_sample.shape
    win = cache_k.shape[2]
    f_exp = w_gate.shape[-1]

    w_in_b = _to_bf16(w_in[0], 256)
    w_o_b = _to_bf16(w_o[0], 256)
    wp_b = _to_bf16(w_pool[0].reshape(-1, POOL_GROUP_WIDTH), 256).reshape(w_pool.shape[1:])
    ln1 = ln1_w[0][None, :]
    ln2 = ln2_w[0][None, :]
    q_gain = jnp.tile(q_norm_w[0], N_HEADS)[None, :] * SCALE
    k_gain = jnp.tile(k_norm_w[0], N_HEADS)[None, :]
    ps = pool_scale[0][None, :]
    blk = np.arange(MXU_DIM) // HEAD_DIM
    avg = jnp.asarray((blk[:, None] == blk[None, :]) / HEAD_DIM, BF16)
    head_of_col = np.arange(ATTN_WIDTH) // HEAD_DIM
    expand_np = (np.arange(LANES)[:, None] == head_of_col[None, :])
    expand = jnp.asarray(expand_np, BF16)
    w_r = jnp.concatenate([w_router_group[0], w_router_expert[0]], axis=1)
    w_r = jnp.pad(w_r, ((0, 0), (0, LANES - w_r.shape[1]))).astype(BF16)
    b_r = jnp.pad(jnp.concatenate([b_router_group[0], b_router_expert[0]]),
                  (0, LANES - N_GROUPS - N_EXPERTS))[None, :]

    xp = x_prompt.reshape(batch * seq, d_model)
    proj = _inproj(xp, ln1, w_in_b, q_gain, k_gain, avg, TM_PROJ, batch, seq)
    k, v, u = proj[:3]
    natural = tuple(a.reshape(batch, 1, seq, ATTN_WIDTH) for a in proj[3:6])
    qkv = [natural] + [tuple(proj[6 + 3 * bi:9 + 3 * bi]) for bi in range(len(DILATED) - 1)]
    bias_all = _prompt_bias(rel_bias)
    o_list, l_list = [], []
    for branch, (qd, kd, vd) in enumerate(qkv):
        o, lse = _attn_branch(qd, kd, vd, bias_all, branch)
        o_list.append(o)
        l_list.append(lse)
    mix_p = _mix(o_list, l_list, u, expand, wp_b, ps, seq, TM_PROJ)
    tri = jnp.asarray(np.tril(np.ones((TM_PROJ, TM_PROJ)), -1), BF16)
    x1_p, hp_p, gate_p, eid_p, rank_p, counts_p = _outproj(
        xp, mix_p, w_o_b, ln2, w_r, b_r, tri, jnp.zeros((1, LANES), F32), TM_PROJ)

    n_s = nb * t_new
    xs = x_sample.reshape(n_s, d_model)
    k_s, v_s, u_s, q_s, _, _ = _inproj(xs, ln1, w_in_b, q_gain, k_gain, avg, n_s)
    heads_s = (nb, t_new, N_HEADS, HEAD_DIM)
    k_s5, v_s5 = k_s.reshape(heads_s), v_s.reshape(heads_s)
    to_lanes = lambda a: jnp.transpose(a, (0, 2, 3, 1))
    by_head = lambda a: jnp.transpose(a, (0, 2, 1, 3))
    bias_c, bias_n = _sample_bias(rel_bias, t_new, win)
    attn_h = _sample_attn(by_head(q_s.astype(F32).reshape(heads_s)), by_head(k_s5), by_head(v_s5),
                          to_lanes(cache_k[0]), to_lanes(cache_v[0]),
                          bias_c, bias_n, SAMPLE_HEADS_PER_STEP)
    attn_s = by_head(attn_h).reshape(n_s, ATTN_WIDTH)
    pool_s, sbuf = _sample_pool(u_s.reshape(nb, t_new, ATTN_WIDTH), state_pool[0], wp_b, ps)
    mix_s = jnp.concatenate([attn_s, pool_s.reshape(n_s, ATTN_WIDTH)], axis=1)
    x1_s, hp_s, gate_s, eid_s, rank_s, counts = _outproj(
        xs, mix_s, w_o_b, ln2, w_r, b_r, tri, counts_p, n_s)

    n_p = batch * seq
    n_pairs = (n_p + n_s) * TOP_K
    tm_e = TM_EXPERT
    n_tiles = -(-n_pairs // tm_e) + N_EXPERTS
    counts_i = counts[0, :N_EXPERTS].astype(jnp.int32)
    tile_e, tile_on, first_row, pad_start, pad_len, tail = _tile_plan(counts_i, n_tiles, tm_e)
    rows = lambda eid, rank, tm: (first_row[eid[:, :TOP_K]] + rank[:, :TOP_K]).reshape(-1, 1, TOP_K * tm)
    rows_p, rows_s = rows(eid_p, rank_p, TM_PROJ), rows(eid_s, rank_s, n_s)
    hs = _dispatch(rows_p, rows_s[0], pad_start, pad_len, tail, hp_p, hp_s, n_tiles, tm_e, TM_PROJ)
    ys = _experts(tile_e, tile_on, hs, w_gate[0], w_up[0], w_down[0], tm_e)
    y_p = _combine(rows_p, x1_p, gate_p, ys, TM_PROJ)
    y_s = _combine(rows_s, x1_s, gate_s, ys, n_s)

    keep = min(MAX_DISTANCE, seq)
    from_lanes = lambda a: jnp.transpose(a, (0, 3, 1, 2))[None, :, -keep:]
    return (y_p.reshape(batch, seq, d_model),
            y_s.reshape(nb, t_new, d_model),
            from_lanes(k), from_lanes(v),
            u.reshape(batch, seq, ATTN_WIDTH)[None, :, -POOL_BUF:],
            k_s5[None], v_s5[None], sbuf[None])
```

```python
import functools
import math

import numpy as np
import jax
import jax.numpy as jnp
from jax import lax
from jax.experimental import pallas as pl
from jax.experimental.pallas import tpu as pltpu

F32 = jnp.float32
BF16 = jnp.bfloat16

N_HEADS = 16
HEAD_DIM = 64
ATTN_WIDTH = N_HEADS * HEAD_DIM
POOL_WINDOWS = (2, 4, 8, 16)
POOL_GROUP_WIDTH = 256
POOL_BUF = max(POOL_WINDOWS) - 1
POOL_HALO = POOL_BUF + 1
DILATED = ((128, 1), (512, 4), (2048, 16))
N_STEPS = 128
QBLOCK = 128
NUM_BUCKETS = 32
MAX_DISTANCE = 2048
PAST_LEN = 16384
N_GROUPS = 4
EXPERTS_PER_GROUP = 4
N_EXPERTS = N_GROUPS * EXPERTS_PER_GROUP
TOP_K = 2
EPS = 1e-6
SCALE = HEAD_DIM ** -0.5
NEG_INF = -1e30
LANES = 128
MXU_DIM = 256
VMEM_LIMIT = 56 * 1024 * 1024


DMA_UNROLL = 8


def _params(*sem, unchecked=False):
    return pltpu.CompilerParams(dimension_semantics=sem, vmem_limit_bytes=VMEM_LIMIT,
                                disable_bounds_checks=unchecked)


def _const_spec(shape):
    zeros = (0,) * len(shape)
    return pl.BlockSpec(shape, lambda *_: zeros)


def _cast_kernel(x_ref, o_ref):
    o_ref[...] = x_ref[...].astype(o_ref.dtype)


def _to_bf16(w2d, block_rows):
    rows, cols = w2d.shape
    return pl.pallas_call(
        _cast_kernel,
        grid=(rows // block_rows,),
        in_specs=[pl.BlockSpec((block_rows, cols), lambda i: (i, 0))],
        out_specs=pl.BlockSpec((block_rows, cols), lambda i: (i, 0)),
        out_shape=jax.ShapeDtypeStruct((rows, cols), BF16),
        compiler_params=_params("parallel"),
        name="cast_bf16",
    )(w2d)


def _inproj_kernel(x_ref, ln_ref, w_ref, qg_ref, kg_ref, avg_ref, *refs, dilated):
    k_ref, v_ref, u_ref, q_ref, kb_ref, vb_ref = refs[:6]
    tm = x_ref.shape[0]
    x = x_ref[...]
    ms = jnp.mean(x * x, axis=-1, keepdims=True)
    h = (x * lax.rsqrt(ms + EPS) * ln_ref[...]).astype(BF16)
    n_chunks = ATTN_WIDTH // MXU_DIM

    def section(s):
        return jnp.dot(h, w_ref[:, s * ATTN_WIDTH:(s + 1) * ATTN_WIDTH],
                       preferred_element_type=F32)

    def head_norm(z, g_ref, c):
        zc = z[:, c * MXU_DIM:(c + 1) * MXU_DIM]
        msh = jnp.dot((zc * zc).astype(BF16), avg_ref[...], preferred_element_type=F32)
        return zc * lax.rsqrt(msh + EPS) * g_ref[:, c * MXU_DIM:(c + 1) * MXU_DIM]

    def emit(chunks, which, f32_ref, bf_ref):
        heads_per_chunk = MXU_DIM // HEAD_DIM
        for c, zc in enumerate(chunks):
            cs = slice(c * MXU_DIM, (c + 1) * MXU_DIM)
            bf_ref[:, cs] = zc.astype(BF16)
            if f32_ref is None:
                continue
            if dilated:
                zt = zc.T
                for j in range(heads_per_chunk):
                    f32_ref[c * heads_per_chunk + j] = zt[j * HEAD_DIM:(j + 1) * HEAD_DIM, :]
            else:
                f32_ref[:, cs] = zc
        if not dilated:
            return
        stage_ref = refs[-1]
        for c, zc in enumerate(chunks):
            for half in range(MXU_DIM // LANES):
                stage_ref[2 * c + half] = zc[:, half * LANES:(half + 1) * LANES]
        for bi, (_, dil) in enumerate(DILATED[1:]):
            out_ref = refs[6 + 3 * bi + which]
            n = tm // dil
            for r in range(dil):
                for s in range(ATTN_WIDTH // LANES):
                    rows = stage_ref[s, pl.ds(r, n, stride=dil), :]
                    out_ref[r, :, s * LANES:(s + 1) * LANES] = rows.astype(BF16)

    zq = section(0)
    emit([head_norm(zq, qg_ref, c) for c in range(n_chunks)], 0, None, q_ref)
    zk = section(1)
    emit([head_norm(zk, kg_ref, c) for c in range(n_chunks)], 1, k_ref, kb_ref)
    zv = section(2)
    emit([zv[:, c * MXU_DIM:(c + 1) * MXU_DIM] for c in range(n_chunks)], 2, v_ref, vb_ref)
    u_ref[...] = section(3)


def _inproj(x2d, ln1, w_in_b, q_gain, k_gain, avg, tm, batch=None, seq=None):
    n, d = x2d.shape
    dilated = seq is not None
    row = lambda i: (i, 0)
    wide = pl.BlockSpec((tm, ATTN_WIDTH), row)
    f32o = jax.ShapeDtypeStruct((n, ATTN_WIDTH), F32)
    bf16o = jax.ShapeDtypeStruct((n, ATTN_WIDTH), BF16)
    out_specs = [wide] * 6
    out_shape = [f32o, f32o, f32o, bf16o, bf16o, bf16o]
    scratch = []
    if dilated:
        tiles = seq // tm
        kv_t = pl.BlockSpec((None, N_HEADS, HEAD_DIM, tm), lambda i: (i // tiles, 0, 0, i % tiles))
        out_specs[:2] = [kv_t, kv_t]
        out_shape[:2] = [jax.ShapeDtypeStruct((batch, N_HEADS, HEAD_DIM, seq), F32)] * 2
        for _, dil in DILATED[1:]:
            spec = pl.BlockSpec((None, dil, tm // dil, ATTN_WIDTH),
                                lambda i: (i // tiles, 0, i % tiles, 0))
            out_specs += [spec] * 3
            out_shape += [jax.ShapeDtypeStruct((batch, dil, seq // dil, ATTN_WIDTH), BF16)] * 3
        scratch = [pltpu.VMEM((ATTN_WIDTH // LANES, tm, LANES), F32)]
    return pl.pallas_call(
        functools.partial(_inproj_kernel, dilated=dilated),
        grid=(n // tm,),
        in_specs=[pl.BlockSpec((tm, d), row),
                  _const_spec((1, d)),
                  pl.BlockSpec(w_in_b.shape, lambda i: (0, 0), pipeline_mode=pl.Buffered(1)),
                  _const_spec((1, ATTN_WIDTH)), _const_spec((1, ATTN_WIDTH)),
                  _const_spec((MXU_DIM, MXU_DIM))],
        out_specs=out_specs,
        out_shape=out_shape,
        scratch_shapes=scratch,
        compiler_params=_params("parallel"),
        name="inproj",
    )(x2d, ln1, w_in_b, q_gain, k_gain, avg)


def _attn_kernel(q_ref, kp_ref, kc_ref, vp_ref, vc_ref, bias_ref, o_ref, lse_ref,
                 s_ref, p_ref):
    i = pl.program_id(2)
    lane = lax.broadcasted_iota(jnp.int32, (QBLOCK, LANES), 1)
    low = lane < HEAD_DIM
    keep_low = low.astype(F32).astype(BF16)
    keep_high = (1.0 - low.astype(F32)).astype(BF16)
    nt = (((1,), (1,)), ((), ()))
    n_pairs = N_HEADS // 2

    def run(with_prev):
        k0 = 0 if with_prev else QBLOCK
        nk = 2 * QBLOCK - k0
        ones = jnp.ones((nk, LANES), BF16)
        for hp in range(n_pairs):
            cs = slice(hp * LANES, (hp + 1) * LANES)
            qp = q_ref[:, cs]
            q2 = jnp.concatenate([qp * keep_low, qp * keep_high], axis=0)
            keys = kc_ref[:, cs]
            if with_prev:
                keys = jnp.concatenate([kp_ref[:, cs], keys], axis=0)
            s2 = lax.dot_general(q2, keys, nt, preferred_element_type=F32)
            s_ref[hp, :, k0:] = s2 + bias_ref[hp, :, k0:]

        m_all = jnp.zeros((QBLOCK, LANES), F32)
        for hp in range(n_pairs):
            for sub in range(2):
                rows = slice(sub * QBLOCK, (sub + 1) * QBLOCK)
                m = jnp.max(s_ref[hp, rows, k0:], axis=-1, keepdims=True)
                p_ref[hp, rows, k0:] = jnp.exp(s_ref[hp, rows, k0:] - m).astype(BF16)
                m_all = jnp.where(lane == 2 * hp + sub, m, m_all)

        den_all = jnp.ones((QBLOCK, LANES), F32)
        for hp in range(n_pairs):
            cs = slice(hp * LANES, (hp + 1) * LANES)
            vals = vc_ref[:, cs]
            if with_prev:
                vals = jnp.concatenate([vp_ref[:, cs], vals], axis=0)
            r = jnp.dot(p_ref[hp, :, k0:], jnp.concatenate([vals, ones], axis=1),
                        preferred_element_type=F32)
            den0, den1 = r[:QBLOCK, LANES:], r[QBLOCK:, LANES:]
            o_ref[:, cs] = jnp.where(low, r[:QBLOCK, :LANES] / den0, r[QBLOCK:, :LANES] / den1)
            den_all = jnp.where(lane == 2 * hp, den0, den_all)
            den_all = jnp.where(lane == 2 * hp + 1, den1, den_all)
        lse_ref[...] = m_all + jnp.log(den_all)

    @pl.when(i == 0)
    def _():
        run(False)

    @pl.when(i > 0)
    def _():
        run(True)


def _attn_branch(q, kb, vb, bias_all, branch):
    batch, dil, sub, _ = q.shape
    cur = lambda b, r, i: (b, r, i, 0)
    prev = lambda b, r, i: (b, r, jnp.maximum(i - 1, 0), 0)
    wide_c = pl.BlockSpec((None, None, QBLOCK, ATTN_WIDTH), cur)
    wide_p = pl.BlockSpec((None, None, QBLOCK, ATTN_WIDTH), prev)
    pairs = N_HEADS // 2
    bias_spec = pl.BlockSpec((None, pairs, 2 * QBLOCK, 2 * QBLOCK), lambda b, r, i: (branch, 0, 0, 0))
    return pl.pallas_call(
        _attn_kernel,
        grid=(batch, dil, sub // QBLOCK),
        in_specs=[wide_c, wide_p, wide_c, wide_p, wide_c, bias_spec],
        out_specs=[wide_c, pl.BlockSpec((None, None, QBLOCK, LANES), cur)],
        out_shape=[jax.ShapeDtypeStruct((batch, dil, sub, ATTN_WIDTH), F32),
                   jax.ShapeDtypeStruct((batch, dil, sub, LANES), F32)],
        scratch_shapes=[pltpu.VMEM((pairs, 2 * QBLOCK, 2 * QBLOCK), F32),
                        pltpu.VMEM((pairs, 2 * QBLOCK, 2 * QBLOCK), BF16)],
        compiler_params=_params("parallel", "parallel", "parallel"),
        name=f"attn_d{dil}",
    )(q, kb, kb, vb, vb, bias_all)


def _bias_kernel(rb_ref, bucket_ref, out_ref):
    bucket = bucket_ref[...]
    for h in range(N_HEADS):
        acc = jnp.full(bucket.shape, NEG_INF, F32)
        for b in range(NUM_BUCKETS):
            acc = jnp.where(bucket == b, rb_ref[b, h], acc)
        out_ref[h // 2, (h % 2) * QBLOCK:(h % 2 + 1) * QBLOCK, :] = acc


def _prompt_bias(rel_bias):
    qi = np.arange(QBLOCK)[:, None]
    kk = np.arange(2 * QBLOCK)[None, :]
    j = qi + QBLOCK - kk
    valid = (j >= 0) & (j <= N_STEPS)
    buckets = np.stack([np.where(valid, _bucket_table(dil)[np.clip(j, 0, N_STEPS)], -1)
                        for _, dil in DILATED]).astype(np.int32)
    pairs = N_HEADS // 2
    return pl.pallas_call(
        _bias_kernel,
        grid=(len(DILATED),),
        in_specs=[pl.BlockSpec(memory_space=pltpu.SMEM),
                  pl.BlockSpec((None, QBLOCK, 2 * QBLOCK), lambda g: (g, 0, 0))],
        out_specs=pl.BlockSpec((None, pairs, 2 * QBLOCK, 2 * QBLOCK), lambda g: (g, 0, 0, 0)),
        out_shape=jax.ShapeDtypeStruct((len(DILATED), pairs, 2 * QBLOCK, 2 * QBLOCK), F32),
        compiler_params=_params("parallel"),
        name="bias_table",
    )(rel_bias.astype(F32), jnp.asarray(buckets))


def _split_dot(a, b_bf16):
    hi = a.astype(BF16)
    lo = (a - hi.astype(F32)).astype(BF16)
    return (jnp.dot(hi, b_bf16, preferred_element_type=F32)
            + jnp.dot(lo, b_bf16, preferred_element_type=F32))


def _pool_groups(comb, u, cnt_fn, wp_ref, ps_ref):
    t = u.shape[0]
    outs = []
    run = comb
    width = 1
    for g, w in enumerate(POOL_WINDOWS):
        while width < w:
            run = run + pltpu.roll(run, width, 0)
            width *= 2
        cs = slice(g * POOL_GROUP_WIDTH, (g + 1) * POOL_GROUP_WIDTH)
        d = run[POOL_HALO:POOL_HALO + t, cs] / cnt_fn(w) - u[:, cs]
        y = jnp.dot(d.astype(BF16), wp_ref[g], preferred_element_type=F32)
        outs.append(y * ps_ref[:, cs])
    return outs


def _mix_kernel(o1_ref, o2_ref, o3_ref, l1_ref, l2_ref, l3_ref, u_ref, halo_ref,
                ex_ref, wp_ref, ps_ref, mix_ref, il_ref, ls_ref, *, seq):
    tm = u_ref.shape[0]
    n_slabs = ATTN_WIDTH // LANES
    lses = [l1_ref[0]]
    for bi, (o_ref, l_ref) in enumerate(((o2_ref, l2_ref), (o3_ref, l3_ref))):
        dil = o_ref.shape[0]
        n = tm // dil
        for r in range(dil):
            ls_ref[bi, pl.ds(r, n, stride=dil), :] = l_ref[r]
            for s in range(n_slabs):
                il_ref[bi, s, pl.ds(r, n, stride=dil), :] = o_ref[r, :, s * LANES:(s + 1) * LANES]
        lses.append(ls_ref[bi])
    l1, l2, l3 = lses
    m = jnp.maximum(jnp.maximum(l1, l2), l3)
    e1, e2, e3 = jnp.exp(l1 - m), jnp.exp(l2 - m), jnp.exp(l3 - m)
    inv = 1.0 / (e1 + e2 + e3)
    ex = ex_ref[...]
    w1, w2, w3 = (_split_dot(e * inv, ex) for e in (e1, e2, e3))
    for s in range(n_slabs):
        cs = slice(s * LANES, (s + 1) * LANES)
        attn = w1[:, cs] * o1_ref[0, :, cs] + w2[:, cs] * il_ref[0, s] + w3[:, cs] * il_ref[1, s]
        mix_ref[:, cs] = attn.astype(BF16)

    pos0 = (pl.program_id(0) * tm) % seq
    u = u_ref[...]
    halo = jnp.where(pos0 == 0, 0.0, halo_ref[...])
    comb = jnp.concatenate([halo, u], axis=0)
    pos = pos0 + lax.broadcasted_iota(jnp.int32, (tm, 1), 0)
    cnt_fn = lambda w: jnp.minimum(pos + 1, w).astype(F32)
    for g, y in enumerate(_pool_groups(comb, u, cnt_fn, wp_ref, ps_ref)):
        lo = ATTN_WIDTH + g * POOL_GROUP_WIDTH
        mix_ref[:, lo:lo + POOL_GROUP_WIDTH] = y.astype(BF16)


def _mix(o_list, l_list, u, expand, wp_b, pool_scale, seq, tm):
    n = u.shape[0]
    tiles = seq // tm
    row = lambda i: (i, 0)
    res = lambda i: (i // tiles, 0, i % tiles, 0)
    o_specs = [pl.BlockSpec((None, o.shape[1], tm // o.shape[1], ATTN_WIDTH), res) for o in o_list]
    l_specs = [pl.BlockSpec((None, l.shape[1], tm // l.shape[1], LANES), res) for l in l_list]
    halo = pl.BlockSpec((POOL_HALO, ATTN_WIDTH),
                        lambda i: (jnp.maximum(i * (tm // POOL_HALO) - 1, 0), 0))
    n_dilated = len(o_list) - 1
    return pl.pallas_call(
        functools.partial(_mix_kernel, seq=seq),
        grid=(n // tm,),
        in_specs=o_specs + l_specs + [pl.BlockSpec((tm, ATTN_WIDTH), row), halo,
                                      _const_spec(expand.shape), _const_spec(wp_b.shape),
                                      _const_spec((1, ATTN_WIDTH))],
        out_specs=pl.BlockSpec((tm, 2 * ATTN_WIDTH), row),
        out_shape=jax.ShapeDtypeStruct((n, 2 * ATTN_WIDTH), BF16),
        scratch_shapes=[pltpu.VMEM((n_dilated, ATTN_WIDTH // LANES, tm, LANES), F32),
                        pltpu.VMEM((n_dilated, tm, LANES), F32)],
        compiler_params=_params("parallel"),
        name="mix",
    )(*o_list, *l_list, u, u, expand, wp_b, pool_scale)


def _sample_attn_kernel(q_ref, kn_ref, vn_ref, k_ref, v_ref, bc_ref, bn_ref, o_ref):
    n_heads, t_new, _ = q_ref.shape
    n_br = bc_ref.shape[0]
    lane_max = lambda a: jnp.max(a, axis=1, keepdims=True)
    lane_sum = lambda a: jnp.sum(a, axis=1, keepdims=True)
    new_lane = lax.broadcasted_iota(jnp.int32, (t_new, t_new), 1)
    nt = (((1,), (1,)), ((), ()))
    for h in range(n_heads):
        q, kn, vn = q_ref[h], kn_ref[h], vn_ref[h]
        s_c = jnp.dot(q.astype(BF16), k_ref[h].astype(BF16), preferred_element_type=F32)
        s_n = jnp.zeros((t_new, t_new), F32)
        for tp in range(t_new):
            s_n = jnp.where(new_lane == tp, lane_sum(q * kn[tp:tp + 1, :]), s_n)
        sc = [s_c + bc_ref[br, h] for br in range(n_br)]
        sn = [s_n + bn_ref[br, h] for br in range(n_br)]
        m = functools.reduce(jnp.maximum, [lane_max(a) for a in sc + sn])
        p_c = functools.reduce(jnp.add, [jnp.exp(a - m) for a in sc])
        p_n = functools.reduce(jnp.add, [jnp.exp(a - m) for a in sn])
        acc = lax.dot_general(p_c.astype(BF16), v_ref[h].astype(BF16), nt,
                              preferred_element_type=F32)
        for tp in range(t_new):
            acc = acc + p_n[:, tp:tp + 1] * vn[tp:tp + 1, :]
        o_ref[h] = acc / (lane_sum(p_c) + lane_sum(p_n))


def _sample_attn(q, kn, vn, k_t, v_t, bias_c, bias_n, heads_per_step):
    nb, nh, t_new, hd = q.shape
    win = k_t.shape[-1]
    hg = heads_per_step
    n_br = bias_c.shape[0]
    small = pl.BlockSpec((None, hg, t_new, hd), lambda b, g: (b, g, 0, 0))
    wide = pl.BlockSpec((None, hg, hd, win), lambda b, g: (b, g, 0, 0))
    return pl.pallas_call(
        _sample_attn_kernel,
        grid=(nb, nh // hg),
        in_specs=[small, small, small, wide, wide,
                  pl.BlockSpec((n_br, hg, t_new, win), lambda b, g: (0, g, 0, 0)),
                  pl.BlockSpec((n_br, hg, t_new, t_new), lambda b, g: (0, g, 0, 0))],
        out_specs=small,
        out_shape=jax.ShapeDtypeStruct((nb, nh, t_new, hd), F32),
        compiler_params=_params("parallel", "parallel"),
        name="sample_attn",
    )(q, kn, vn, k_t, v_t, bias_c, bias_n)


def _sample_pool_kernel(u_ref, st_ref, wp_ref, ps_ref, pool_ref, sbuf_ref, comb_ref, *, start):
    t_new = u_ref.shape[0]
    u = u_ref[...]
    comb_ref[...] = jnp.zeros_like(comb_ref)
    comb_ref[1:POOL_HALO, :] = st_ref[...]
    comb_ref[POOL_HALO:POOL_HALO + t_new, :] = u
    pos = start + lax.broadcasted_iota(jnp.int32, (t_new, 1), 0)
    cnt_fn = lambda w: jnp.minimum(pos + 1, w).astype(F32)
    for g, y in enumerate(_pool_groups(comb_ref[...], u, cnt_fn, wp_ref, ps_ref)):
        pool_ref[:, g * POOL_GROUP_WIDTH:(g + 1) * POOL_GROUP_WIDTH] = y
    sbuf_ref[:POOL_BUF - t_new, :] = st_ref[t_new:, :]
    sbuf_ref[POOL_BUF - t_new:, :] = u


def _sample_pool(u, state, wp_b, pool_scale):
    nb, t_new, w = u.shape
    flat = lambda rows: pl.BlockSpec((None, rows, w), lambda b: (b, 0, 0))
    return pl.pallas_call(
        functools.partial(_sample_pool_kernel, start=PAST_LEN),
        grid=(nb,),
        in_specs=[flat(t_new), flat(POOL_BUF), _const_spec(wp_b.shape), _const_spec((1, w))],
        out_specs=[flat(t_new), flat(POOL_BUF)],
        out_shape=[jax.ShapeDtypeStruct((nb, t_new, w), F32),
                   jax.ShapeDtypeStruct((nb, POOL_BUF, w), F32)],
        scratch_shapes=[pltpu.VMEM((POOL_HALO + 8, w), F32)],
        compiler_params=_params("parallel"),
        name="sample_pool",
    )(u, state, wp_b, pool_scale)


def _pack_bf16_pairs(h):
    half = h.shape[1] // 2
    bits = lambda a: lax.bitcast_convert_type(a.astype(BF16).astype(F32), jnp.uint32)
    return (bits(h[:, :half]) >> 16) | (bits(h[:, half:]) & jnp.uint32(0xFFFF0000))


def _unpack_bf16_pairs(u):
    lo = lax.bitcast_convert_type(u << 16, F32).astype(BF16)
    hi = lax.bitcast_convert_type(u & jnp.uint32(0xFFFF0000), F32).astype(BF16)
    return lo, hi


def _outproj_kernel(x_ref, mix_ref, wo_ref, ln_ref, wr_ref, br_ref, tri_ref, cin_ref,
                    x1_ref, hp_ref, gate_ref, eid_ref, rank_ref, cout_ref, count_ref):
    @pl.when(pl.program_id(0) == 0)
    def _():
        count_ref[...] = cin_ref[...]

    x1 = x_ref[...] + jnp.dot(mix_ref[...].astype(BF16), wo_ref[...],
                              preferred_element_type=F32)
    x1_ref[...] = x1
    ms = jnp.mean(x1 * x1, axis=-1, keepdims=True)
    h = x1 * lax.rsqrt(ms + EPS) * ln_ref[...]
    hp_ref[...] = _pack_bf16_pairs(h)
    lg = jnp.dot(h.astype(BF16), wr_ref[...], preferred_element_type=F32) + br_ref[...]

    lane = lax.broadcasted_iota(jnp.int32, lg.shape, 1).astype(F32)
    big = float(LANES)
    row_max = lambda mask: jnp.max(jnp.where(mask, lg, -jnp.inf), axis=-1, keepdims=True)
    first = lambda mask: jnp.min(jnp.where(mask, lane, big), axis=-1, keepdims=True)
    is_g = lane < N_GROUPS
    mg = row_max(is_g)
    g_top = first(jnp.logical_and(is_g, lg == mg))
    den = jnp.sum(jnp.where(is_g, jnp.exp(lg - mg), 0.0), axis=-1, keepdims=True)
    p_top = 1.0 / den
    base = N_GROUPS + EXPERTS_PER_GROUP * g_top
    in_grp = jnp.logical_and(lane >= base, lane < base + EXPERTS_PER_GROUP)
    v1 = row_max(in_grp)
    i1 = first(jnp.logical_and(in_grp, lg == v1))
    rest = jnp.logical_and(in_grp, lane != i1)
    v2 = row_max(rest)
    i2 = first(jnp.logical_and(rest, lg == v2))
    e21 = jnp.exp(v2 - v1)
    s21 = 1.0 + e21
    gate1 = p_top * (1.0 / s21)
    gate2 = p_top * (e21 / s21)
    pick = lambda a, b: jnp.where(lane == 0.0, a, jnp.where(lane == 1.0, b, 0.0))
    gate_ref[...] = pick(gate1, gate2)
    e1, e2 = i1 - N_GROUPS, i2 - N_GROUPS
    eid_ref[...] = pick(e1, e2).astype(jnp.int32)

    hot1 = (lane == e1).astype(F32)
    hot2 = (lane == e2).astype(F32)
    hot = hot1 + hot2
    before = count_ref[...] + jnp.dot(tri_ref[...], hot.astype(BF16), preferred_element_type=F32)
    rank1 = jnp.sum(hot1 * before, axis=-1, keepdims=True)
    rank2 = jnp.sum(hot2 * before, axis=-1, keepdims=True)
    rank_ref[...] = pick(rank1, rank2).astype(jnp.int32)
    count_ref[...] = count_ref[...] + jnp.sum(hot, axis=0, keepdims=True)
    cout_ref[...] = count_ref[...]


def _outproj(x2d, mix, w_o_b, ln2, w_r, b_r, tri, counts_in, tm):
    n, d = x2d.shape
    row = lambda i: (i, 0)
    full = pl.BlockSpec((tm, d), row)
    stat = pl.BlockSpec((tm, LANES), row)
    stat_i = jax.ShapeDtypeStruct((n, LANES), jnp.int32)
    return pl.pallas_call(
        _outproj_kernel,
        grid=(n // tm,),
        in_specs=[full, full,
                  pl.BlockSpec(w_o_b.shape, lambda i: (0, 0), pipeline_mode=pl.Buffered(1)),
                  _const_spec((1, d)), _const_spec(w_r.shape),
                  _const_spec((1, LANES)), _const_spec((tm, tm)), _const_spec((1, LANES))],
        out_specs=[full, pl.BlockSpec((tm, d // 2), row), stat, stat, stat,
                   _const_spec((1, LANES))],
        out_shape=[jax.ShapeDtypeStruct((n, d), F32),
                   jax.ShapeDtypeStruct((n, d // 2), jnp.uint32),
                   jax.ShapeDtypeStruct((n, LANES), F32), stat_i, stat_i,
                   jax.ShapeDtypeStruct((1, LANES), F32)],
        scratch_shapes=[pltpu.VMEM((1, LANES), F32)],
        compiler_params=_params("arbitrary"),
        name="outproj",
    )(x2d, mix, w_o_b, ln2, w_r, b_r, tri[:tm, :tm], counts_in)


def _row_copy(src_hbm, dst_vmem, sem, src_row, dst_row):
    return pltpu.make_async_copy(src_hbm.at[pl.ds(src_row, 1)],
                                 dst_vmem.at[pl.ds(dst_row, 1)], sem)


def _dispatch_kernel(pos_ref, pos_s_ref, pad_start_ref, pad_len_ref, tail_ref, hp_ref, hps_ref,
                     hs_ref, sem_ref, zero_ref, *, tile_rows):
    i = pl.program_id(0)
    sem = sem_ref.at[0]

    def scatter(src_ref, dst_rows_ref):
        n_rows = src_ref.shape[0]

        def put(r, k):
            return _row_copy(src_ref, hs_ref, sem, r, dst_rows_ref[0, r * TOP_K + k])

        def issue(r, c):
            for k in range(TOP_K):
                put(r, k).start()
            return c

        def drain(r, c):
            for k in range(TOP_K):
                put(r, k).wait()
            return c

        lax.fori_loop(0, n_rows, issue, 0, unroll=DMA_UNROLL)
        lax.fori_loop(0, n_rows, drain, 0, unroll=DMA_UNROLL)

    scatter(hp_ref, pos_ref)

    @pl.when(i == pl.num_programs(0) - 1)
    def _():
        scatter(hps_ref, pos_s_ref)
        zero_ref[...] = jnp.zeros_like(zero_ref)
        for e in range(N_EXPERTS):
            def fill(r):
                return _row_copy(zero_ref, hs_ref, sem, 0, pad_start_ref[e] + r)

            def start(r, c):
                fill(r).start()
                return c

            def wait(r, c):
                fill(r).wait()
                return c

            lax.fori_loop(0, pad_len_ref[e], start, 0)
            lax.fori_loop(0, pad_len_ref[e], wait, 0)

        n_tiles = hs_ref.shape[0] // tile_rows

        def fill_tile(t):
            rows = pl.ds(pl.multiple_of(t * tile_rows, tile_rows), tile_rows)
            return pltpu.make_async_copy(zero_ref, hs_ref.at[rows], sem)

        def start_tile(t, c):
            fill_tile(t).start()
            return c

        def wait_tile(t, c):
            fill_tile(t).wait()
            return c

        lax.fori_loop(tail_ref[0], n_tiles, start_tile, 0)
        lax.fori_loop(tail_ref[0], n_tiles, wait_tile, 0)


def _tile_rows_spec(tm):
    return pl.BlockSpec((None, 1, TOP_K * tm), lambda i: (i, 0, 0), memory_space=pltpu.SMEM)


def _dispatch(rows_p, rows_s, pad_start, pad_len, tail, hp, hp_s, n_tiles, tile_rows, tm):
    n, half = hp.shape
    smem = pl.BlockSpec(memory_space=pltpu.SMEM)
    return pl.pallas_call(
        functools.partial(_dispatch_kernel, tile_rows=tile_rows),
        grid=(n // tm,),
        in_specs=[_tile_rows_spec(tm), smem, smem, smem, smem,
                  pl.BlockSpec((tm, half), lambda i: (i, 0)),
                  pl.BlockSpec(hp_s.shape, lambda i: (0, 0))],
        out_specs=pl.BlockSpec(memory_space=pl.ANY),
        out_shape=jax.ShapeDtypeStruct((n_tiles * tile_rows, half), jnp.uint32),
        scratch_shapes=[pltpu.SemaphoreType.DMA((1,)), pltpu.VMEM((tile_rows, half), jnp.uint32)],
        compiler_params=_params("arbitrary", unchecked=True),
        name="dispatch",
    )(rows_p, rows_s, pad_start, pad_len, tail, hp, hp_s)


def _expert_kernel(tile_e_ref, tile_on_ref, hs_ref, wg_ref, wu_ref, wd_ref, y_ref,
                   wg_b, wu_b, wd_b):
    i = pl.program_id(0)
    half = hs_ref.shape[1]

    @pl.when(jnp.logical_or(i == 0, tile_e_ref[i] != tile_e_ref[jnp.maximum(i - 1, 0)]))
    def _():
        wg_b[...] = wg_ref[...].astype(BF16)
        wu_b[...] = wu_ref[...].astype(BF16)
        wd_b[...] = wd_ref[...].astype(BF16)

    @pl.when(tile_on_ref[i] == 1)
    def _():
        lo, hi = _unpack_bf16_pairs(hs_ref[...])
        proj = lambda w_ref: (jnp.dot(lo, w_ref[:half], preferred_element_type=F32)
                              + jnp.dot(hi, w_ref[half:], preferred_element_type=F32))
        a = proj(wg_b)
        b = proj(wu_b)
        hid = a * jax.nn.sigmoid(a) * b
        y = jnp.dot(hid.astype(BF16), wd_b[...], preferred_element_type=F32)
        y_ref[...] = _pack_bf16_pairs(y)

    @pl.when(tile_on_ref[i] == 0)
    def _():
        y_ref[...] = jnp.zeros_like(y_ref)


def _experts(tile_e, tile_on, hs, w_gate, w_up, w_down, tm):
    n_tiles = tile_e.shape[0]
    half = hs.shape[1]
    d, f = w_gate.shape[1:]
    by_expert = lambda i, te, on: (te[i], 0, 0)
    grid_spec = pltpu.PrefetchScalarGridSpec(
        num_scalar_prefetch=2,
        grid=(n_tiles,),
        in_specs=[pl.BlockSpec((tm, half), lambda i, te, on: (i, 0)),
                  pl.BlockSpec((None, d, f), by_expert),
                  pl.BlockSpec((None, d, f), by_expert),
                  pl.BlockSpec((None, f, d), by_expert)],
        out_specs=pl.BlockSpec((tm, half), lambda i, te, on: (i, 0)),
        scratch_shapes=[pltpu.VMEM((d, f), BF16), pltpu.VMEM((d, f), BF16),
                        pltpu.VMEM((f, d), BF16)],
    )
    return pl.pallas_call(
        _expert_kernel,
        grid_spec=grid_spec,
        out_shape=jax.ShapeDtypeStruct((n_tiles * tm, half), jnp.uint32),
        compiler_params=_params("arbitrary"),
        name="experts",
    )(tile_e, tile_on, hs, w_gate, w_up, w_down)


def _combine_kernel(rows_ref, next_rows_ref, x1_ref, gate_ref, ys_hbm, y_ref, buf_ref, sem_ref):
    i = pl.program_id(0)
    n_tiles = pl.num_programs(0)
    tm = y_ref.shape[0]
    half = buf_ref.shape[-1]

    def issue(src_rows_ref, slot):
        def body(r, c):
            for k in range(TOP_K):
                _row_copy(ys_hbm, buf_ref.at[slot, k], sem_ref.at[slot],
                          src_rows_ref[0, r * TOP_K + k], r).start()
            return c
        lax.fori_loop(0, tm, body, 0, unroll=DMA_UNROLL)

    def drain(slot):
        def body(r, c):
            for k in range(TOP_K):
                _row_copy(ys_hbm, buf_ref.at[slot, k], sem_ref.at[slot], 0, r).wait()
            return c
        lax.fori_loop(0, tm, body, 0, unroll=DMA_UNROLL)

    slot = i % 2

    @pl.when(i == 0)
    def _():
        issue(rows_ref, 0)

    drain(slot)

    @pl.when(i + 1 < n_tiles)
    def _():
        issue(next_rows_ref, 1 - slot)

    gate = gate_ref[...]
    g0, g1 = gate[:, 0:1], gate[:, 1:2]
    lo0, hi0 = _unpack_bf16_pairs(buf_ref[slot, 0])
    lo1, hi1 = _unpack_bf16_pairs(buf_ref[slot, 1])
    y_ref[:, :half] = x1_ref[:, :half] + g0 * lo0.astype(F32) + g1 * lo1.astype(F32)
    y_ref[:, half:] = x1_ref[:, half:] + g0 * hi0.astype(F32) + g1 * hi1.astype(F32)


def _combine(rows, x1, gate, ys, tm):
    n, d = x1.shape
    half = ys.shape[1]
    last = n // tm - 1
    next_spec = pl.BlockSpec((None, 1, TOP_K * tm), lambda i: (jnp.minimum(i + 1, last), 0, 0),
                             memory_space=pltpu.SMEM)
    return pl.pallas_call(
        _combine_kernel,
        grid=(n // tm,),
        in_specs=[_tile_rows_spec(tm), next_spec,
                  pl.BlockSpec((tm, d), lambda i: (i, 0)),
                  pl.BlockSpec((tm, LANES), lambda i: (i, 0)),
                  pl.BlockSpec(memory_space=pl.ANY)],
        out_specs=pl.BlockSpec((tm, d), lambda i: (i, 0)),
        out_shape=jax.ShapeDtypeStruct((n, d), F32),
        scratch_shapes=[pltpu.VMEM((2, TOP_K, tm, half), jnp.uint32),
                        pltpu.SemaphoreType.DMA((2,))],
        compiler_params=_params("arbitrary", unchecked=True),
        name="combine",
    )(rows, rows, x1, gate, ys)


def _bucket_table(dilation):
    dist = np.arange(N_STEPS + 1, dtype=np.int64) * dilation
    max_exact = NUM_BUCKETS // 2
    df = np.maximum(dist, 1).astype(np.float32)
    large = max_exact + (np.log(df / np.float32(max_exact))
                         / np.float32(math.log(MAX_DISTANCE / max_exact))
                         * np.float32(NUM_BUCKETS - max_exact)).astype(np.int32)
    large = np.minimum(large, NUM_BUCKETS - 1)
    return np.where(dist < max_exact, dist, large).astype(np.int32)


def _sample_bias(rel_bias, t_new, win):
    buckets = np.stack([_bucket_table(dil) for _, dil in DILATED])
    by_step = jnp.transpose(rel_bias.astype(F32)[buckets], (0, 2, 1))
    nh = by_step.shape[1]
    neg = lambda *shape: jnp.full(shape, NEG_INF, F32)
    t = np.arange(t_new)
    cached, fresh = [], []
    for br, (_, dil) in enumerate(DILATED):
        rev = by_step[br, :, ::-1][:, :N_STEPS]
        if dil == 1:
            rows = [jnp.concatenate([neg(nh, win - N_STEPS + q), rev[:, :N_STEPS - q]], axis=1)
                    for q in range(t_new)]
            cached.append(jnp.stack(rows, axis=1))
        else:
            own = (t[:, None, None] == np.arange(dil)[None, None, :])
            band = jnp.where(own[None], rev[:, None, :, None], NEG_INF)
            band = band.reshape(nh, t_new, dil * N_STEPS)
            cached.append(jnp.concatenate([neg(nh, t_new, win - dil * N_STEPS), band], axis=2))
        step = t[:, None] - t[None, :]
        ok = (step >= 0) & (step % dil == 0)
        vals = by_step[br][:, np.where(ok, step // dil, 0)]
        fresh.append(jnp.where(ok[None], vals, NEG_INF))
    return jnp.stack(cached), jnp.stack(fresh)


def _tile_plan(counts, n_tiles, tm):
    tiles_e = (counts + tm - 1) // tm
    ends = jnp.cumsum(tiles_e)
    first_row = (ends - tiles_e) * tm
    tile = jnp.arange(n_tiles, dtype=jnp.int32)
    tile_e = jnp.sum((ends[None, :] <= tile[:, None]).astype(jnp.int32), axis=1)
    tile_on = (tile_e < N_EXPERTS).astype(jnp.int32)
    tile_e = jnp.minimum(tile_e, N_EXPERTS - 1)
    i32 = lambda a: a.astype(jnp.int32)
    return (i32(tile_e), tile_on, i32(first_row), i32(first_row + counts),
            i32(tiles_e * tm - counts), i32(ends[-1:]))


TM_PROJ = 256
TM_TOKEN = 512
TM_EXPERT = 256
SAMPLE_HEADS_PER_STEP = 8


def kernel(x_prompt, x_sample, cache_k, cache_v, state_pool, rel_bias, ln1_w, w_in,
           q_norm_w, k_norm_w, w_pool, pool_scale, w_o, ln2_w, w_router_group,
           b_router_group, w_router_expert, b_router_expert, w_gate, w_up, w_down):
    depth = w_in.shape[0]
    assert depth == 1
    batch, seq, d_model = x_prompt.shape
    nb, t_new, _ = x_sample.shape
    win = cache_k.shape[2]
    f_exp = w_gate.shape[-1]

    w_in_b = _to_bf16(w_in[0], 256)
    w_o_b = _to_bf16(w_o[0], 256)
    wp_b = _to_bf16(w_pool[0].reshape(-1, POOL_GROUP_WIDTH), 256).reshape(w_pool.shape[1:])
    ln1 = ln1_w[0][None, :]
    ln2 = ln2_w[0][None, :]
    q_gain = jnp.tile(q_norm_w[0], N_HEADS)[None, :] * SCALE
    k_gain = jnp.tile(k_norm_w[0], N_HEADS)[None, :]
    ps = pool_scale[0][None, :]
    blk = np.arange(MXU_DIM) // HEAD_DIM
    avg = jnp.asarray((blk[:, None] == blk[None, :]) / HEAD_DIM, BF16)
    head_of_col = np.arange(ATTN_WIDTH) // HEAD_DIM
    expand_np = (np.arange(LANES)[:, None] == head_of_col[None, :])
    expand = jnp.asarray(expand_np, BF16)
    w_r = jnp.concatenate([w_router_group[0], w_router_expert[0]], axis=1)
    w_r = jnp.pad(w_r, ((0, 0), (0, LANES - w_r.shape[1]))).astype(BF16)
    b_r = jnp.pad(jnp.concatenate([b_router_group[0], b_router_expert[0]]),
                  (0, LANES - N_GROUPS - N_EXPERTS))[None, :]

    xp = x_prompt.reshape(batch * seq, d_model)
    proj = _inproj(xp, ln1, w_in_b, q_gain, k_gain, avg, TM_PROJ, batch, seq)
    k, v, u = proj[:3]
    natural = tuple(a.reshape(batch, 1, seq, ATTN_WIDTH) for a in proj[3:6])
    qkv = [natural] + [tuple(proj[6 + 3 * bi:9 + 3 * bi]) for bi in range(len(DILATED) - 1)]
    bias_all = _prompt_bias(rel_bias)
    o_list, l_list = [], []
    for branch, (qd, kd, vd) in enumerate(qkv):
        o, lse = _attn_branch(qd, kd, vd, bias_all, branch)
        o_list.append(o)
        l_list.append(lse)
    mix_p = _mix(o_list, l_list, u, expand, wp_b, ps, seq, TM_TOKEN)
    tri = jnp.asarray(np.tril(np.ones((TM_TOKEN, TM_TOKEN)), -1), BF16)
    x1_p, hp_p, gate_p, eid_p, rank_p, counts_p = _outproj(
        xp, mix_p, w_o_b, ln2, w_r, b_r, tri, jnp.zeros((1, LANES), F32), TM_TOKEN)

    n_s = nb * t_new
    xs = x_sample.reshape(n_s, d_model)
    k_s, v_s, u_s, q_s, _, _ = _inproj(xs, ln1, w_in_b, q_gain, k_gain, avg, n_s)
    heads_s = (nb, t_new, N_HEADS, HEAD_DIM)
    k_s5, v_s5 = k_s.reshape(heads_s), v_s.reshape(heads_s)
    to_lanes = lambda a: jnp.transpose(a, (0, 2, 3, 1))
    by_head = lambda a: jnp.transpose(a, (0, 2, 1, 3))
    bias_c, bias_n = _sample_bias(rel_bias, t_new, win)
    attn_h = _sample_attn(by_head(q_s.astype(F32).reshape(heads_s)), by_head(k_s5), by_head(v_s5),
                          to_lanes(cache_k[0]), to_lanes(cache_v[0]),
                          bias_c, bias_n, SAMPLE_HEADS_PER_STEP)
    attn_s = by_head(attn_h).reshape(n_s, ATTN_WIDTH)
    pool_s, sbuf = _sample_pool(u_s.reshape(nb, t_new, ATTN_WIDTH), state_pool[0], wp_b, ps)
    mix_s = jnp.concatenate([attn_s, pool_s.reshape(n_s, ATTN_WIDTH)], axis=1)
    x1_s, hp_s, gate_s, eid_s, rank_s, counts = _outproj(
        xs, mix_s, w_o_b, ln2, w_r, b_r, tri, counts_p, n_s)

    n_p = batch * seq
    n_pairs = (n_p + n_s) * TOP_K
    tm_e = TM_EXPERT
    n_tiles = -(-n_pairs // tm_e) + N_EXPERTS
    counts_i = counts[0, :N_EXPERTS].astype(jnp.int32)
    tile_e, tile_on, first_row, pad_start, pad_len, tail = _tile_plan(counts_i, n_tiles, tm_e)
    rows = lambda eid, rank, tm: (first_row[eid[:, :TOP_K]] + rank[:, :TOP_K]).reshape(-1, 1, TOP_K * tm)
    rows_p, rows_s = rows(eid_p, rank_p, TM_TOKEN), rows(eid_s, rank_s, n_s)
    hs = _dispatch(rows_p, rows_s[0], pad_start, pad_len, tail, hp_p, hp_s, n_tiles, tm_e, TM_TOKEN)
    ys = _experts(tile_e, tile_on, hs, w_gate[0], w_up[0], w_down[0], tm_e)
    y_p = _combine(rows_p, x1_p, gate_p, ys, TM_TOKEN)
    y_s = _combine(rows_s, x1_s, gate_s, ys, n_s)

    keep = min(MAX_DISTANCE, seq)
    from_lanes = lambda a: jnp.transpose(a, (0, 3, 1, 2))[None, :, -keep:]
    return (y_p.reshape(batch, seq, d_model),
            y_s.reshape(nb, t_new, d_model),
            from_lanes(k), from_lanes(v),
            u.reshape(batch, seq, ATTN_WIDTH)[None, :, -POOL_BUF:],
            k_s5[None], v_s5[None], sbuf[None])
```

```python
import functools
import math

import numpy as np
import jax
import jax.numpy as jnp
from jax import lax
from jax.experimental import pallas as pl
from jax.experimental.pallas import tpu as pltpu

F32 = jnp.float32
BF16 = jnp.bfloat16

N_HEADS = 16
HEAD_DIM = 64
ATTN_WIDTH = N_HEADS * HEAD_DIM
POOL_WINDOWS = (2, 4, 8, 16)
POOL_GROUP_WIDTH = 256
POOL_BUF = max(POOL_WINDOWS) - 1
POOL_HALO = POOL_BUF + 1
DILATED = ((128, 1), (512, 4), (2048, 16))
N_STEPS = 128
QBLOCK = 128
NUM_BUCKETS = 32
MAX_DISTANCE = 2048
PAST_LEN = 16384
N_GROUPS = 4
EXPERTS_PER_GROUP = 4
N_EXPERTS = N_GROUPS * EXPERTS_PER_GROUP
TOP_K = 2
EPS = 1e-6
SCALE = HEAD_DIM ** -0.5
NEG_INF = -1e30
LANES = 128
MXU_DIM = 256
VMEM_LIMIT = 56 * 1024 * 1024


DMA_UNROLL = 8


def _params(*sem, unchecked=False):
    return pltpu.CompilerParams(dimension_semantics=sem, vmem_limit_bytes=VMEM_LIMIT,
                                disable_bounds_checks=unchecked)


def _const_spec(shape):
    zeros = (0,) * len(shape)
    return pl.BlockSpec(shape, lambda *_: zeros)


def _cast_kernel(x_ref, o_ref):
    o_ref[...] = x_ref[...].astype(o_ref.dtype)


def _to_bf16(w2d, block_rows):
    rows, cols = w2d.shape
    return pl.pallas_call(
        _cast_kernel,
        grid=(rows // block_rows,),
        in_specs=[pl.BlockSpec((block_rows, cols), lambda i: (i, 0))],
        out_specs=pl.BlockSpec((block_rows, cols), lambda i: (i, 0)),
        out_shape=jax.ShapeDtypeStruct((rows, cols), BF16),
        compiler_params=_params("parallel"),
        name="cast_bf16",
    )(w2d)


def _inproj_kernel(x_ref, ln_ref, w_ref, qg_ref, kg_ref, avg_ref, *refs, dilated):
    k_ref, v_ref, u_ref, q_ref, kb_ref, vb_ref = refs[:6]
    tm = x_ref.shape[0]
    x = x_ref[...]
    ms = jnp.mean(x * x, axis=-1, keepdims=True)
    h = (x * lax.rsqrt(ms + EPS) * ln_ref[...]).astype(BF16)
    n_chunks = ATTN_WIDTH // MXU_DIM

    def section(s):
        return jnp.dot(h, w_ref[:, s * ATTN_WIDTH:(s + 1) * ATTN_WIDTH],
                       preferred_element_type=F32)

    def head_norm(z, g_ref, c):
        zc = z[:, c * MXU_DIM:(c + 1) * MXU_DIM]
        msh = jnp.dot((zc * zc).astype(BF16), avg_ref[...], preferred_element_type=F32)
        return zc * lax.rsqrt(msh + EPS) * g_ref[:, c * MXU_DIM:(c + 1) * MXU_DIM]

    def emit(chunks, which, f32_ref, bf_ref):
        heads_per_chunk = MXU_DIM // HEAD_DIM
        for c, zc in enumerate(chunks):
            cs = slice(c * MXU_DIM, (c + 1) * MXU_DIM)
            bf_ref[:, cs] = zc.astype(BF16)
            if f32_ref is None:
                continue
            if dilated:
                zt = zc.T
                for j in range(heads_per_chunk):
                    f32_ref[c * heads_per_chunk + j] = zt[j * HEAD_DIM:(j + 1) * HEAD_DIM, :]
            else:
                f32_ref[:, cs] = zc
        if not dilated:
            return
        stage_ref = refs[-1]
        for c, zc in enumerate(chunks):
            for half in range(MXU_DIM // LANES):
                stage_ref[2 * c + half] = zc[:, half * LANES:(half + 1) * LANES]
        for bi, (_, dil) in enumerate(DILATED[1:]):
            out_ref = refs[6 + 3 * bi + which]
            n = tm // dil
            for r in range(dil):
                for s in range(ATTN_WIDTH // LANES):
                    rows = stage_ref[s, pl.ds(r, n, stride=dil), :]
                    out_ref[r, :, s * LANES:(s + 1) * LANES] = rows.astype(BF16)

    zq = section(0)
    emit([head_norm(zq, qg_ref, c) for c in range(n_chunks)], 0, None, q_ref)
    zk = section(1)
    emit([head_norm(zk, kg_ref, c) for c in range(n_chunks)], 1, k_ref, kb_ref)
    zv = section(2)
    emit([zv[:, c * MXU_DIM:(c + 1) * MXU_DIM] for c in range(n_chunks)], 2, v_ref, vb_ref)
    u_ref[...] = section(3)


def _inproj(x2d, ln1, w_in_b, q_gain, k_gain, avg, tm, batch=None, seq=None):
    n, d = x2d.shape
    dilated = seq is not None
    row = lambda i: (i, 0)
    wide = pl.BlockSpec((tm, ATTN_WIDTH), row)
    f32o = jax.ShapeDtypeStruct((n, ATTN_WIDTH), F32)
    bf16o = jax.ShapeDtypeStruct((n, ATTN_WIDTH), BF16)
    out_specs = [wide] * 6
    out_shape = [f32o, f32o, f32o, bf16o, bf16o, bf16o]
    scratch = []
    if dilated:
        tiles = seq // tm
        kv_t = pl.BlockSpec((None, N_HEADS, HEAD_DIM, tm), lambda i: (i // tiles, 0, 0, i % tiles))
        out_specs[:2] = [kv_t, kv_t]
        out_shape[:2] = [jax.ShapeDtypeStruct((batch, N_HEADS, HEAD_DIM, seq), F32)] * 2
        for _, dil in DILATED[1:]:
            spec = pl.BlockSpec((None, dil, tm // dil, ATTN_WIDTH),
                                lambda i: (i // tiles, 0, i % tiles, 0))
            out_specs += [spec] * 3
            out_shape += [jax.ShapeDtypeStruct((batch, dil, seq // dil, ATTN_WIDTH), BF16)] * 3
        scratch = [pltpu.VMEM((ATTN_WIDTH // LANES, tm, LANES), F32)]
    return pl.pallas_call(
        functools.partial(_inproj_kernel, dilated=dilated),
        grid=(n // tm,),
        in_specs=[pl.BlockSpec((tm, d), row),
                  _const_spec((1, d)),
                  pl.BlockSpec(w_in_b.shape, lambda i: (0, 0), pipeline_mode=pl.Buffered(1)),
                  _const_spec((1, ATTN_WIDTH)), _const_spec((1, ATTN_WIDTH)),
                  _const_spec((MXU_DIM, MXU_DIM))],
        out_specs=out_specs,
        out_shape=out_shape,
        scratch_shapes=scratch,
        compiler_params=_params("parallel"),
        name="inproj",
    )(x2d, ln1, w_in_b, q_gain, k_gain, avg)


def _attn_kernel(q_ref, kp_ref, kc_ref, vp_ref, vc_ref, bias_ref, o_ref, lse_ref,
                 s_ref, p_ref):
    i = pl.program_id(2)
    lane = lax.broadcasted_iota(jnp.int32, (QBLOCK, LANES), 1)
    low = lane < HEAD_DIM
    keep_low = low.astype(F32).astype(BF16)
    keep_high = (1.0 - low.astype(F32)).astype(BF16)
    nt = (((1,), (1,)), ((), ()))
    n_pairs = N_HEADS // 2

    def run(with_prev):
        k0 = 0 if with_prev else QBLOCK
        nk = 2 * QBLOCK - k0
        ones = jnp.ones((nk, LANES), BF16)
        for hp in range(n_pairs):
            cs = slice(hp * LANES, (hp + 1) * LANES)
            qp = q_ref[:, cs]
            q2 = jnp.concatenate([qp * keep_low, qp * keep_high], axis=0)
            keys = kc_ref[:, cs]
            if with_prev:
                keys = jnp.concatenate([kp_ref[:, cs], keys], axis=0)
            s2 = lax.dot_general(q2, keys, nt, preferred_element_type=F32)
            s_ref[hp, :, k0:] = s2 + bias_ref[hp, :, k0:]

        m_all = jnp.zeros((QBLOCK, LANES), F32)
        for hp in range(n_pairs):
            for sub in range(2):
                rows = slice(sub * QBLOCK, (sub + 1) * QBLOCK)
                m = jnp.max(s_ref[hp, rows, k0:], axis=-1, keepdims=True)
                p_ref[hp, rows, k0:] = jnp.exp(s_ref[hp, rows, k0:] - m).astype(BF16)
                m_all = jnp.where(lane == 2 * hp + sub, m, m_all)

        den_all = jnp.ones((QBLOCK, LANES), F32)
        for hp in range(n_pairs):
            cs = slice(hp * LANES, (hp + 1) * LANES)
            vals = vc_ref[:, cs]
            if with_prev:
                vals = jnp.concatenate([vp_ref[:, cs], vals], axis=0)
            r = jnp.dot(p_ref[hp, :, k0:], jnp.concatenate([vals, ones], axis=1),
                        preferred_element_type=F32)
            den0, den1 = r[:QBLOCK, LANES:], r[QBLOCK:, LANES:]
            o_ref[:, cs] = jnp.where(low, r[:QBLOCK, :LANES] / den0, r[QBLOCK:, :LANES] / den1)
            den_all = jnp.where(lane == 2 * hp, den0, den_all)
            den_all = jnp.where(lane == 2 * hp + 1, den1, den_all)
        lse_ref[...] = m_all + jnp.log(den_all)

    @pl.when(i == 0)
    def _():
        run(False)

    @pl.when(i > 0)
    def _():
        run(True)


def _attn_branch(q, kb, vb, bias_all, branch):
    batch, dil, sub, _ = q.shape
    cur = lambda b, r, i: (b, r, i, 0)
    prev = lambda b, r, i: (b, r, jnp.maximum(i - 1, 0), 0)
    wide_c = pl.BlockSpec((None, None, QBLOCK, ATTN_WIDTH), cur)
    wide_p = pl.BlockSpec((None, None, QBLOCK, ATTN_WIDTH), prev)
    pairs = N_HEADS // 2
    bias_spec = pl.BlockSpec((None, pairs, 2 * QBLOCK, 2 * QBLOCK), lambda b, r, i: (branch, 0, 0, 0))
    return pl.pallas_call(
        _attn_kernel,
        grid=(batch, dil, sub // QBLOCK),
        in_specs=[wide_c, wide_p, wide_c, wide_p, wide_c, bias_spec],
        out_specs=[wide_c, pl.BlockSpec((None, None, QBLOCK, LANES), cur)],
        out_shape=[jax.ShapeDtypeStruct((batch, dil, sub, ATTN_WIDTH), F32),
                   jax.ShapeDtypeStruct((batch, dil, sub, LANES), F32)],
        scratch_shapes=[pltpu.VMEM((pairs, 2 * QBLOCK, 2 * QBLOCK), F32),
                        pltpu.VMEM((pairs, 2 * QBLOCK, 2 * QBLOCK), BF16)],
        compiler_params=_params("parallel", "parallel", "parallel"),
        name=f"attn_d{dil}",
    )(q, kb, kb, vb, vb, bias_all)


def _bias_kernel(base_ref, out_ref):
    width = base_ref.shape[1]
    for h in range(N_HEADS):
        rows = jnp.broadcast_to(base_ref[h:h + 1, :], (QBLOCK, width))
        band = pltpu.roll(rows, 0, 1, stride=1, stride_axis=0)
        out_ref[h // 2, (h % 2) * QBLOCK:(h % 2 + 1) * QBLOCK, :] = band[:, :2 * QBLOCK]


def _prompt_bias(rel_bias):
    width = 3 * QBLOCK
    steps = QBLOCK - np.arange(width)
    valid = (steps >= 0) & (steps <= N_STEPS)
    buckets = np.stack([_bucket_table(dil)[np.clip(steps, 0, N_STEPS)] for _, dil in DILATED])
    base = jnp.where(valid[None, :, None], rel_bias.astype(F32)[buckets], NEG_INF)
    base = jnp.transpose(base, (0, 2, 1))
    pairs = N_HEADS // 2
    return pl.pallas_call(
        _bias_kernel,
        grid=(len(DILATED),),
        in_specs=[pl.BlockSpec((None, N_HEADS, width), lambda g: (g, 0, 0))],
        out_specs=pl.BlockSpec((None, pairs, 2 * QBLOCK, 2 * QBLOCK), lambda g: (g, 0, 0, 0)),
        out_shape=jax.ShapeDtypeStruct((len(DILATED), pairs, 2 * QBLOCK, 2 * QBLOCK), F32),
        compiler_params=_params("parallel"),
        name="bias_table",
    )(base)


def _split_dot(a, b_bf16):
    hi = a.astype(BF16)
    lo = (a - hi.astype(F32)).astype(BF16)
    return (jnp.dot(hi, b_bf16, preferred_element_type=F32)
            + jnp.dot(lo, b_bf16, preferred_element_type=F32))


def _pool_groups(comb, u, cnt_fn, wp_ref, ps_ref):
    t = u.shape[0]
    outs = []
    run = comb
    width = 1
    for g, w in enumerate(POOL_WINDOWS):
        while width < w:
            run = run + pltpu.roll(run, width, 0)
            width *= 2
        cs = slice(g * POOL_GROUP_WIDTH, (g + 1) * POOL_GROUP_WIDTH)
        d = run[POOL_HALO:POOL_HALO + t, cs] / cnt_fn(w) - u[:, cs]
        y = jnp.dot(d.astype(BF16), wp_ref[g], preferred_element_type=F32)
        outs.append(y * ps_ref[:, cs])
    return outs


def _mix_kernel(o1_ref, o2_ref, o3_ref, l1_ref, l2_ref, l3_ref, u_ref, halo_ref,
                ex_ref, wp_ref, ps_ref, mix_ref, il_ref, ls_ref, *, seq):
    tm = u_ref.shape[0]
    n_slabs = ATTN_WIDTH // LANES
    lses = [l1_ref[0]]
    for bi, (o_ref, l_ref) in enumerate(((o2_ref, l2_ref), (o3_ref, l3_ref))):
        dil = o_ref.shape[0]
        n = tm // dil
        for r in range(dil):
            ls_ref[bi, pl.ds(r, n, stride=dil), :] = l_ref[r]
            for s in range(n_slabs):
                il_ref[bi, s, pl.ds(r, n, stride=dil), :] = o_ref[r, :, s * LANES:(s + 1) * LANES]
        lses.append(ls_ref[bi])
    l1, l2, l3 = lses
    m = jnp.maximum(jnp.maximum(l1, l2), l3)
    e1, e2, e3 = jnp.exp(l1 - m), jnp.exp(l2 - m), jnp.exp(l3 - m)
    inv = 1.0 / (e1 + e2 + e3)
    ex = ex_ref[...]
    w1, w2, w3 = (_split_dot(e * inv, ex) for e in (e1, e2, e3))
    for s in range(n_slabs):
        cs = slice(s * LANES, (s + 1) * LANES)
        attn = w1[:, cs] * o1_ref[0, :, cs] + w2[:, cs] * il_ref[0, s] + w3[:, cs] * il_ref[1, s]
        mix_ref[:, cs] = attn.astype(BF16)

    pos0 = (pl.program_id(0) * tm) % seq
    u = u_ref[...]
    halo = jnp.where(pos0 == 0, 0.0, halo_ref[...])
    comb = jnp.concatenate([halo, u], axis=0)
    pos = pos0 + lax.broadcasted_iota(jnp.int32, (tm, 1), 0)
    cnt_fn = lambda w: jnp.minimum(pos + 1, w).astype(F32)
    for g, y in enumerate(_pool_groups(comb, u, cnt_fn, wp_ref, ps_ref)):
        lo = ATTN_WIDTH + g * POOL_GROUP_WIDTH
        mix_ref[:, lo:lo + POOL_GROUP_WIDTH] = y.astype(BF16)


def _mix(o_list, l_list, u, expand, wp_b, pool_scale, seq, tm):
    n = u.shape[0]
    tiles = seq // tm
    row = lambda i: (i, 0)
    res = lambda i: (i // tiles, 0, i % tiles, 0)
    o_specs = [pl.BlockSpec((None, o.shape[1], tm // o.shape[1], ATTN_WIDTH), res) for o in o_list]
    l_specs = [pl.BlockSpec((None, l.shape[1], tm // l.shape[1], LANES), res) for l in l_list]
    halo = pl.BlockSpec((POOL_HALO, ATTN_WIDTH),
                        lambda i: (jnp.maximum(i * (tm // POOL_HALO) - 1, 0), 0))
    n_dilated = len(o_list) - 1
    return pl.pallas_call(
        functools.partial(_mix_kernel, seq=seq),
        grid=(n // tm,),
        in_specs=o_specs + l_specs + [pl.BlockSpec((tm, ATTN_WIDTH), row), halo,
                                      _const_spec(expand.shape), _const_spec(wp_b.shape),
                                      _const_spec((1, ATTN_WIDTH))],
        out_specs=pl.BlockSpec((tm, 2 * ATTN_WIDTH), row),
        out_shape=jax.ShapeDtypeStruct((n, 2 * ATTN_WIDTH), BF16),
        scratch_shapes=[pltpu.VMEM((n_dilated, ATTN_WIDTH // LANES, tm, LANES), F32),
                        pltpu.VMEM((n_dilated, tm, LANES), F32)],
        compiler_params=_params("parallel"),
        name="mix",
    )(*o_list, *l_list, u, u, expand, wp_b, pool_scale)


def _sample_attn_kernel(q_ref, kn_ref, vn_ref, k_ref, v_ref, bc_ref, bn_ref, o_ref):
    n_heads, t_new, _ = q_ref.shape
    n_br = bc_ref.shape[0]
    lane_max = lambda a: jnp.max(a, axis=1, keepdims=True)
    lane_sum = lambda a: jnp.sum(a, axis=1, keepdims=True)
    new_lane = lax.broadcasted_iota(jnp.int32, (t_new, t_new), 1)
    nt = (((1,), (1,)), ((), ()))
    for h in range(n_heads):
        q, kn, vn = q_ref[h], kn_ref[h], vn_ref[h]
        s_c = jnp.dot(q.astype(BF16), k_ref[h].astype(BF16), preferred_element_type=F32)
        s_n = jnp.zeros((t_new, t_new), F32)
        for tp in range(t_new):
            s_n = jnp.where(new_lane == tp, lane_sum(q * kn[tp:tp + 1, :]), s_n)
        sc = [s_c + bc_ref[br, h] for br in range(n_br)]
        sn = [s_n + bn_ref[br, h] for br in range(n_br)]
        m = functools.reduce(jnp.maximum, [lane_max(a) for a in sc + sn])
        p_c = functools.reduce(jnp.add, [jnp.exp(a - m) for a in sc])
        p_n = functools.reduce(jnp.add, [jnp.exp(a - m) for a in sn])
        acc = lax.dot_general(p_c.astype(BF16), v_ref[h].astype(BF16), nt,
                              preferred_element_type=F32)
        for tp in range(t_new):
            acc = acc + p_n[:, tp:tp + 1] * vn[tp:tp + 1, :]
        o_ref[h] = acc / (lane_sum(p_c) + lane_sum(p_n))


def _sample_attn(q, kn, vn, k_t, v_t, bias_c, bias_n, heads_per_step):
    nb, nh, t_new, hd = q.shape
    win = k_t.shape[-1]
    hg = heads_per_step
    n_br = bias_c.shape[0]
    small = pl.BlockSpec((None, hg, t_new, hd), lambda b, g: (b, g, 0, 0))
    wide = pl.BlockSpec((None, hg, hd, win), lambda b, g: (b, g, 0, 0))
    return pl.pallas_call(
        _sample_attn_kernel,
        grid=(nb, nh // hg),
        in_specs=[small, small, small, wide, wide,
                  pl.BlockSpec((n_br, hg, t_new, win), lambda b, g: (0, g, 0, 0)),
                  pl.BlockSpec((n_br, hg, t_new, t_new), lambda b, g: (0, g, 0, 0))],
        out_specs=small,
        out_shape=jax.ShapeDtypeStruct((nb, nh, t_new, hd), F32),
        compiler_params=_params("parallel", "parallel"),
        name="sample_attn",
    )(q, kn, vn, k_t, v_t, bias_c, bias_n)


def _sample_pool_kernel(u_ref, st_ref, wp_ref, ps_ref, pool_ref, sbuf_ref, comb_ref, *, start):
    t_new = u_ref.shape[0]
    u = u_ref[...]
    comb_ref[...] = jnp.zeros_like(comb_ref)
    comb_ref[1:POOL_HALO, :] = st_ref[...]
    comb_ref[POOL_HALO:POOL_HALO + t_new, :] = u
    pos = start + lax.broadcasted_iota(jnp.int32, (t_new, 1), 0)
    cnt_fn = lambda w: jnp.minimum(pos + 1, w).astype(F32)
    for g, y in enumerate(_pool_groups(comb_ref[...], u, cnt_fn, wp_ref, ps_ref)):
        pool_ref[:, g * POOL_GROUP_WIDTH:(g + 1) * POOL_GROUP_WIDTH] = y
    sbuf_ref[:POOL_BUF - t_new, :] = st_ref[t_new:, :]
    sbuf_ref[POOL_BUF - t_new:, :] = u


def _sample_pool(u, state, wp_b, pool_scale):
    nb, t_new, w = u.shape
    flat = lambda rows: pl.BlockSpec((None, rows, w), lambda b: (b, 0, 0))
    return pl.pallas_call(
        functools.partial(_sample_pool_kernel, start=PAST_LEN),
        grid=(nb,),
        in_specs=[flat(t_new), flat(POOL_BUF), _const_spec(wp_b.shape), _const_spec((1, w))],
        out_specs=[flat(t_new), flat(POOL_BUF)],
        out_shape=[jax.ShapeDtypeStruct((nb, t_new, w), F32),
                   jax.ShapeDtypeStruct((nb, POOL_BUF, w), F32)],
        scratch_shapes=[pltpu.VMEM((POOL_HALO + 8, w), F32)],
        compiler_params=_params("parallel"),
        name="sample_pool",
    )(u, state, wp_b, pool_scale)


def _pack_bf16_pairs(h):
    half = h.shape[1] // 2
    bits = lambda a: lax.bitcast_convert_type(a.astype(BF16).astype(F32), jnp.uint32)
    return (bits(h[:, :half]) >> 16) | (bits(h[:, half:]) & jnp.uint32(0xFFFF0000))


def _unpack_bf16_pairs(u):
    lo = lax.bitcast_convert_type(u << 16, F32).astype(BF16)
    hi = lax.bitcast_convert_type(u & jnp.uint32(0xFFFF0000), F32).astype(BF16)
    return lo, hi


def _outproj_kernel(x_ref, mix_ref, wo_ref, ln_ref, wr_ref, br_ref, tri_ref, cin_ref,
                    x1_ref, hp_ref, gate_ref, route_ref, cout_ref, count_ref):
    @pl.when(pl.program_id(0) == 0)
    def _():
        count_ref[...] = cin_ref[...]

    x1 = x_ref[...] + jnp.dot(mix_ref[...].astype(BF16), wo_ref[...],
                              preferred_element_type=F32)
    x1_ref[...] = x1
    ms = jnp.mean(x1 * x1, axis=-1, keepdims=True)
    h = x1 * lax.rsqrt(ms + EPS) * ln_ref[...]
    hp_ref[...] = _pack_bf16_pairs(h)
    lg = jnp.dot(h.astype(BF16), wr_ref[...], preferred_element_type=F32) + br_ref[...]

    lane = lax.broadcasted_iota(jnp.int32, lg.shape, 1).astype(F32)
    big = float(LANES)
    row_max = lambda mask: jnp.max(jnp.where(mask, lg, -jnp.inf), axis=-1, keepdims=True)
    first = lambda mask: jnp.min(jnp.where(mask, lane, big), axis=-1, keepdims=True)
    is_g = lane < N_GROUPS
    mg = row_max(is_g)
    g_top = first(jnp.logical_and(is_g, lg == mg))
    den = jnp.sum(jnp.where(is_g, jnp.exp(lg - mg), 0.0), axis=-1, keepdims=True)
    p_top = 1.0 / den
    base = N_GROUPS + EXPERTS_PER_GROUP * g_top
    in_grp = jnp.logical_and(lane >= base, lane < base + EXPERTS_PER_GROUP)
    v1 = row_max(in_grp)
    i1 = first(jnp.logical_and(in_grp, lg == v1))
    rest = jnp.logical_and(in_grp, lane != i1)
    v2 = row_max(rest)
    i2 = first(jnp.logical_and(rest, lg == v2))
    e21 = jnp.exp(v2 - v1)
    s21 = 1.0 + e21
    gate1 = p_top * (1.0 / s21)
    gate2 = p_top * (e21 / s21)
    gate_ref[...] = jnp.where(lane == 0.0, gate1, jnp.where(lane == 1.0, gate2, 0.0))
    e1, e2 = i1 - N_GROUPS, i2 - N_GROUPS

    hot1 = (lane == e1).astype(F32)
    hot2 = (lane == e2).astype(F32)
    hot = hot1 + hot2
    before = count_ref[...] + jnp.dot(tri_ref[...], hot.astype(BF16), preferred_element_type=F32)
    rank1 = jnp.sum(hot1 * before, axis=-1, keepdims=True)
    rank2 = jnp.sum(hot2 * before, axis=-1, keepdims=True)
    count_ref[...] = count_ref[...] + jnp.sum(hot, axis=0, keepdims=True)
    cout_ref[...] = count_ref[...]

    cols = jnp.zeros_like(lg)
    for r, col in enumerate((e1, e2, rank1, rank2)):
        cols = jnp.where(lane == float(r), col, cols)
    route_ref[...] = cols.T[:route_ref.shape[0], :]


def _outproj(x2d, mix, w_o_b, ln2, w_r, b_r, tri, counts_in, tm):
    n, d = x2d.shape
    row = lambda i: (i, 0)
    full = pl.BlockSpec((tm, d), row)
    stat = pl.BlockSpec((tm, LANES), row)
    route_rows = 8
    return pl.pallas_call(
        _outproj_kernel,
        grid=(n // tm,),
        in_specs=[full, full,
                  pl.BlockSpec(w_o_b.shape, lambda i: (0, 0), pipeline_mode=pl.Buffered(1)),
                  _const_spec((1, d)), _const_spec(w_r.shape),
                  _const_spec((1, LANES)), _const_spec((tm, tm)), _const_spec((1, LANES))],
        out_specs=[full, pl.BlockSpec((tm, d // 2), row), stat,
                   pl.BlockSpec((route_rows, tm), lambda i: (0, i)), _const_spec((1, LANES))],
        out_shape=[jax.ShapeDtypeStruct((n, d), F32),
                   jax.ShapeDtypeStruct((n, d // 2), jnp.uint32),
                   jax.ShapeDtypeStruct((n, LANES), F32),
                   jax.ShapeDtypeStruct((route_rows, n), F32),
                   jax.ShapeDtypeStruct((1, LANES), F32)],
        scratch_shapes=[pltpu.VMEM((1, LANES), F32)],
        compiler_params=_params("arbitrary"),
        name="outproj",
    )(x2d, mix, w_o_b, ln2, w_r, b_r, tri[:tm, :tm], counts_in)


def _row_copy(src_hbm, dst_vmem, sem, src_row, dst_row):
    return pltpu.make_async_copy(src_hbm.at[pl.ds(src_row, 1)],
                                 dst_vmem.at[pl.ds(dst_row, 1)], sem)


def _dispatch_kernel(pos_ref, pos_s_ref, pad_start_ref, pad_len_ref, tail_ref, hp_ref, hps_ref,
                     hs_ref, sem_ref, zero_ref, *, tile_rows):
    i = pl.program_id(0)
    sem = sem_ref.at[0]

    def scatter(src_ref, dst_rows_ref):
        n_rows = src_ref.shape[0]

        def put(r, k):
            return _row_copy(src_ref, hs_ref, sem, r, dst_rows_ref[0, k * n_rows + r])

        def issue(r, c):
            for k in range(TOP_K):
                put(r, k).start()
            return c

        def drain(r, c):
            for k in range(TOP_K):
                put(r, k).wait()
            return c

        lax.fori_loop(0, n_rows, issue, 0, unroll=DMA_UNROLL)
        lax.fori_loop(0, n_rows, drain, 0, unroll=DMA_UNROLL)

    scatter(hp_ref, pos_ref)

    @pl.when(i == pl.num_programs(0) - 1)
    def _():
        scatter(hps_ref, pos_s_ref)
        zero_ref[...] = jnp.zeros_like(zero_ref)
        for e in range(N_EXPERTS):
            def fill(r):
                return _row_copy(zero_ref, hs_ref, sem, 0, pad_start_ref[e] + r)

            def start(r, c):
                fill(r).start()
                return c

            def wait(r, c):
                fill(r).wait()
                return c

            lax.fori_loop(0, pad_len_ref[e], start, 0)
            lax.fori_loop(0, pad_len_ref[e], wait, 0)

        n_tiles = hs_ref.shape[0] // tile_rows

        def fill_tile(t):
            rows = pl.ds(pl.multiple_of(t * tile_rows, tile_rows), tile_rows)
            return pltpu.make_async_copy(zero_ref, hs_ref.at[rows], sem)

        def start_tile(t, c):
            fill_tile(t).start()
            return c

        def wait_tile(t, c):
            fill_tile(t).wait()
            return c

        lax.fori_loop(tail_ref[0], n_tiles, start_tile, 0)
        lax.fori_loop(tail_ref[0], n_tiles, wait_tile, 0)


def _tile_rows_spec(tm):
    return pl.BlockSpec((None, 1, TOP_K * tm), lambda i: (i, 0, 0), memory_space=pltpu.SMEM)


def _dispatch(rows_p, rows_s, pad_start, pad_len, tail, hp, hp_s, n_tiles, tile_rows, tm):
    n, half = hp.shape
    smem = pl.BlockSpec(memory_space=pltpu.SMEM)
    return pl.pallas_call(
        functools.partial(_dispatch_kernel, tile_rows=tile_rows),
        grid=(n // tm,),
        in_specs=[_tile_rows_spec(tm), smem, smem, smem, smem,
                  pl.BlockSpec((tm, half), lambda i: (i, 0)),
                  pl.BlockSpec(hp_s.shape, lambda i: (0, 0))],
        out_specs=pl.BlockSpec(memory_space=pl.ANY),
        out_shape=jax.ShapeDtypeStruct((n_tiles * tile_rows, half), jnp.uint32),
        scratch_shapes=[pltpu.SemaphoreType.DMA((1,)), pltpu.VMEM((tile_rows, half), jnp.uint32)],
        compiler_params=_params("arbitrary", unchecked=True),
        name="dispatch",
    )(rows_p, rows_s, pad_start, pad_len, tail, hp, hp_s)


def _expert_kernel(tile_e_ref, tile_on_ref, hs_ref, wg_ref, wu_ref, wd_ref, y_ref,
                   wg_b, wu_b, wd_b):
    i = pl.program_id(0)
    half = hs_ref.shape[1]

    @pl.when(jnp.logical_or(i == 0, tile_e_ref[i] != tile_e_ref[jnp.maximum(i - 1, 0)]))
    def _():
        wg_b[...] = wg_ref[...].astype(BF16)
        wu_b[...] = wu_ref[...].astype(BF16)
        wd_b[...] = wd_ref[...].astype(BF16)

    @pl.when(tile_on_ref[i] == 1)
    def _():
        lo, hi = _unpack_bf16_pairs(hs_ref[...])
        proj = lambda w_ref: (jnp.dot(lo, w_ref[:half], preferred_element_type=F32)
                              + jnp.dot(hi, w_ref[half:], preferred_element_type=F32))
        a = proj(wg_b)
        b = proj(wu_b)
        hid = a * jax.nn.sigmoid(a) * b
        y = jnp.dot(hid.astype(BF16), wd_b[...], preferred_element_type=F32)
        y_ref[...] = _pack_bf16_pairs(y)

    @pl.when(tile_on_ref[i] == 0)
    def _():
        y_ref[...] = jnp.zeros_like(y_ref)


def _experts(tile_e, tile_on, hs, w_gate, w_up, w_down, tm):
    n_tiles = tile_e.shape[0]
    half = hs.shape[1]
    d, f = w_gate.shape[1:]
    by_expert = lambda i, te, on: (te[i], 0, 0)
    grid_spec = pltpu.PrefetchScalarGridSpec(
        num_scalar_prefetch=2,
        grid=(n_tiles,),
        in_specs=[pl.BlockSpec((tm, half), lambda i, te, on: (i, 0)),
                  pl.BlockSpec((None, d, f), by_expert),
                  pl.BlockSpec((None, d, f), by_expert),
                  pl.BlockSpec((None, f, d), by_expert)],
        out_specs=pl.BlockSpec((tm, half), lambda i, te, on: (i, 0)),
        scratch_shapes=[pltpu.VMEM((d, f), BF16), pltpu.VMEM((d, f), BF16),
                        pltpu.VMEM((f, d), BF16)],
    )
    return pl.pallas_call(
        _expert_kernel,
        grid_spec=grid_spec,
        out_shape=jax.ShapeDtypeStruct((n_tiles * tm, half), jnp.uint32),
        compiler_params=_params("arbitrary"),
        name="experts",
    )(tile_e, tile_on, hs, w_gate, w_up, w_down)


def _combine_kernel(rows_ref, next_rows_ref, x1_ref, gate_ref, ys_hbm, y_ref, buf_ref, sem_ref):
    i = pl.program_id(0)
    n_tiles = pl.num_programs(0)
    tm = y_ref.shape[0]
    half = buf_ref.shape[-1]

    def issue(src_rows_ref, slot):
        def body(r, c):
            for k in range(TOP_K):
                _row_copy(ys_hbm, buf_ref.at[slot, k], sem_ref.at[slot],
                          src_rows_ref[0, k * tm + r], r).start()
            return c
        lax.fori_loop(0, tm, body, 0, unroll=DMA_UNROLL)

    def drain(slot):
        def body(r, c):
            for k in range(TOP_K):
                _row_copy(ys_hbm, buf_ref.at[slot, k], sem_ref.at[slot], 0, r).wait()
            return c
        lax.fori_loop(0, tm, body, 0, unroll=DMA_UNROLL)

    slot = i % 2

    @pl.when(i == 0)
    def _():
        issue(rows_ref, 0)

    drain(slot)

    @pl.when(i + 1 < n_tiles)
    def _():
        issue(next_rows_ref, 1 - slot)

    gate = gate_ref[...]
    g0, g1 = gate[:, 0:1], gate[:, 1:2]
    lo0, hi0 = _unpack_bf16_pairs(buf_ref[slot, 0])
    lo1, hi1 = _unpack_bf16_pairs(buf_ref[slot, 1])
    y_ref[:, :half] = x1_ref[:, :half] + g0 * lo0.astype(F32) + g1 * lo1.astype(F32)
    y_ref[:, half:] = x1_ref[:, half:] + g0 * hi0.astype(F32) + g1 * hi1.astype(F32)


def _combine(rows, x1, gate, ys, tm):
    n, d = x1.shape
    half = ys.shape[1]
    last = n // tm - 1
    next_spec = pl.BlockSpec((None, 1, TOP_K * tm), lambda i: (jnp.minimum(i + 1, last), 0, 0),
                             memory_space=pltpu.SMEM)
    return pl.pallas_call(
        _combine_kernel,
        grid=(n // tm,),
        in_specs=[_tile_rows_spec(tm), next_spec,
                  pl.BlockSpec((tm, d), lambda i: (i, 0)),
                  pl.BlockSpec((tm, LANES), lambda i: (i, 0)),
                  pl.BlockSpec(memory_space=pl.ANY)],
        out_specs=pl.BlockSpec((tm, d), lambda i: (i, 0)),
        out_shape=jax.ShapeDtypeStruct((n, d), F32),
        scratch_shapes=[pltpu.VMEM((2, TOP_K, tm, half), jnp.uint32),
                        pltpu.SemaphoreType.DMA((2,))],
        compiler_params=_params("arbitrary", unchecked=True),
        name="combine",
    )(rows, rows, x1, gate, ys)


def _bucket_table(dilation):
    dist = np.arange(N_STEPS + 1, dtype=np.int64) * dilation
    max_exact = NUM_BUCKETS // 2
    df = np.maximum(dist, 1).astype(np.float32)
    large = max_exact + (np.log(df / np.float32(max_exact))
                         / np.float32(math.log(MAX_DISTANCE / max_exact))
                         * np.float32(NUM_BUCKETS - max_exact)).astype(np.int32)
    large = np.minimum(large, NUM_BUCKETS - 1)
    return np.where(dist < max_exact, dist, large).astype(np.int32)


def _sample_bias(rel_bias, t_new, win):
    buckets = np.stack([_bucket_table(dil) for _, dil in DILATED])
    by_step = jnp.transpose(rel_bias.astype(F32)[buckets], (0, 2, 1))
    nh = by_step.shape[1]
    neg = lambda *shape: jnp.full(shape, NEG_INF, F32)
    t = np.arange(t_new)
    cached, fresh = [], []
    for br, (_, dil) in enumerate(DILATED):
        rev = by_step[br, :, ::-1][:, :N_STEPS]
        if dil == 1:
            rows = [jnp.concatenate([neg(nh, win - N_STEPS + q), rev[:, :N_STEPS - q]], axis=1)
                    for q in range(t_new)]
            cached.append(jnp.stack(rows, axis=1))
        else:
            own = (t[:, None, None] == np.arange(dil)[None, None, :])
            band = jnp.where(own[None], rev[:, None, :, None], NEG_INF)
            band = band.reshape(nh, t_new, dil * N_STEPS)
            cached.append(jnp.concatenate([neg(nh, t_new, win - dil * N_STEPS), band], axis=2))
        step = t[:, None] - t[None, :]
        ok = (step >= 0) & (step % dil == 0)
        vals = by_step[br][:, np.where(ok, step // dil, 0)]
        fresh.append(jnp.where(ok[None], vals, NEG_INF))
    return jnp.stack(cached), jnp.stack(fresh)


def _tile_plan(counts, n_tiles, tm):
    tiles_e = (counts + tm - 1) // tm
    ends = jnp.cumsum(tiles_e)
    first_row = (ends - tiles_e) * tm
    tile = jnp.arange(n_tiles, dtype=jnp.int32)
    tile_e = jnp.sum((ends[None, :] <= tile[:, None]).astype(jnp.int32), axis=1)
    tile_on = (tile_e < N_EXPERTS).astype(jnp.int32)
    tile_e = jnp.minimum(tile_e, N_EXPERTS - 1)
    i32 = lambda a: a.astype(jnp.int32)
    return (i32(tile_e), tile_on, i32(first_row), i32(first_row + counts),
            i32(tiles_e * tm - counts), i32(ends[-1:]))


TM_PROJ = 256
TM_TOKEN = 512
TM_EXPERT = 512
SAMPLE_HEADS_PER_STEP = 8


def kernel(x_prompt, x_sample, cache_k, cache_v, state_pool, rel_bias, ln1_w, w_in,
           q_norm_w, k_norm_w, w_pool, pool_scale, w_o, ln2_w, w_router_group,
           b_router_group, w_router_expert, b_router_expert, w_gate, w_up, w_down):
    depth = w_in.shape[0]
    assert depth == 1
    batch, seq, d_model = x_prompt.shape
    nb, t_new, _ = x_sample.shape
    win = cache_k.shape[2]
    f_exp = w_gate.shape[-1]

    w_in_b = _to_bf16(w_in[0], 256)
    w_o_b = _to_bf16(w_o[0], 256)
    wp_b = _to_bf16(w_pool[0].reshape(-1, POOL_GROUP_WIDTH), 256).reshape(w_pool.shape[1:])
    ln1 = ln1_w[0][None, :]
    ln2 = ln2_w[0][None, :]
    q_gain = jnp.tile(q_norm_w[0], N_HEADS)[None, :] * SCALE
    k_gain = jnp.tile(k_norm_w[0], N_HEADS)[None, :]
    ps = pool_scale[0][None, :]
    blk = np.arange(MXU_DIM) // HEAD_DIM
    avg = jnp.asarray((blk[:, None] == blk[None, :]) / HEAD_DIM, BF16)
    head_of_col = np.arange(ATTN_WIDTH) // HEAD_DIM
    expand_np = (np.arange(LANES)[:, None] == head_of_col[None, :])
    expand = jnp.asarray(expand_np, BF16)
    w_r = jnp.concatenate([w_router_group[0], w_router_expert[0]], axis=1)
    w_r = jnp.pad(w_r, ((0, 0), (0, LANES - w_r.shape[1]))).astype(BF16)
    b_r = jnp.pad(jnp.concatenate([b_router_group[0], b_router_expert[0]]),
                  (0, LANES - N_GROUPS - N_EXPERTS))[None, :]

    xp = x_prompt.reshape(batch * seq, d_model)
    proj = _inproj(xp, ln1, w_in_b, q_gain, k_gain, avg, TM_PROJ, batch, seq)
    k, v, u = proj[:3]
    natural = tuple(a.reshape(batch, 1, seq, ATTN_WIDTH) for a in proj[3:6])
    qkv = [natural] + [tuple(proj[6 + 3 * bi:9 + 3 * bi]) for bi in range(len(DILATED) - 1)]
    bias_all = _prompt_bias(rel_bias)
    o_list, l_list = [], []
    for branch, (qd, kd, vd) in enumerate(qkv):
        o, lse = _attn_branch(qd, kd, vd, bias_all, branch)
        o_list.append(o)
        l_list.append(lse)
    mix_p = _mix(o_list, l_list, u, expand, wp_b, ps, seq, TM_TOKEN)
    tri = jnp.asarray(np.tril(np.ones((TM_TOKEN, TM_TOKEN)), -1), BF16)
    x1_p, hp_p, gate_p, route_p, counts_p = _outproj(
        xp, mix_p, w_o_b, ln2, w_r, b_r, tri, jnp.zeros((1, LANES), F32), TM_TOKEN)

    n_s = nb * t_new
    xs = x_sample.reshape(n_s, d_model)
    k_s, v_s, u_s, q_s, _, _ = _inproj(xs, ln1, w_in_b, q_gain, k_gain, avg, n_s)
    heads_s = (nb, t_new, N_HEADS, HEAD_DIM)
    k_s5, v_s5 = k_s.reshape(heads_s), v_s.reshape(heads_s)
    to_lanes = lambda a: jnp.transpose(a, (0, 2, 3, 1))
    by_head = lambda a: jnp.transpose(a, (0, 2, 1, 3))
    bias_c, bias_n = _sample_bias(rel_bias, t_new, win)
    attn_h = _sample_attn(by_head(q_s.astype(F32).reshape(heads_s)), by_head(k_s5), by_head(v_s5),
                          to_lanes(cache_k[0]), to_lanes(cache_v[0]),
                          bias_c, bias_n, SAMPLE_HEADS_PER_STEP)
    attn_s = by_head(attn_h).reshape(n_s, ATTN_WIDTH)
    pool_s, sbuf = _sample_pool(u_s.reshape(nb, t_new, ATTN_WIDTH), state_pool[0], wp_b, ps)
    mix_s = jnp.concatenate([attn_s, pool_s.reshape(n_s, ATTN_WIDTH)], axis=1)
    x1_s, hp_s, gate_s, route_s, counts = _outproj(
        xs, mix_s, w_o_b, ln2, w_r, b_r, tri, counts_p, n_s)

    n_p = batch * seq
    n_pairs = (n_p + n_s) * TOP_K
    tm_e = TM_EXPERT
    n_tiles = -(-n_pairs // tm_e) + N_EXPERTS
    counts_i = counts[0, :N_EXPERTS].astype(jnp.int32)
    tile_e, tile_on, first_row, pad_start, pad_len, tail = _tile_plan(counts_i, n_tiles, tm_e)

    def tile_rows(route, tm):
        ids = route[:TOP_K].astype(jnp.int32)
        base = jnp.zeros_like(ids)
        for e in range(N_EXPERTS):
            base = jnp.where(ids == e, first_row[e], base)
        rows = base + route[TOP_K:2 * TOP_K].astype(jnp.int32)
        rows = jnp.transpose(rows.reshape(TOP_K, -1, tm), (1, 0, 2))
        return rows.reshape(-1, 1, TOP_K * tm)

    rows_p, rows_s = tile_rows(route_p, TM_TOKEN), tile_rows(route_s, n_s)
    hs = _dispatch(rows_p, rows_s[0], pad_start, pad_len, tail, hp_p, hp_s, n_tiles, tm_e, TM_TOKEN)
    ys = _experts(tile_e, tile_on, hs, w_gate[0], w_up[0], w_down[0], tm_e)
    y_p = _combine(rows_p, x1_p, gate_p, ys, TM_TOKEN)
    y_s = _combine(rows_s, x1_s, gate_s, ys, n_s)

    keep = min(MAX_DISTANCE, seq)
    from_lanes = lambda a: jnp.transpose(a, (0, 3, 1, 2))[None, :, -keep:]
    return (y_p.reshape(batch, seq, d_model),
            y_s.reshape(nb, t_new, d_model),
            from_lanes(k), from_lanes(v),
            u.reshape(batch, seq, ATTN_WIDTH)[None, :, -POOL_BUF:],
            k_s5[None], v_s5[None], sbuf[None])
```

```python
import functools
import math

import numpy as np
import jax
import jax.numpy as jnp
from jax import lax
from jax.experimental import pallas as pl
from jax.experimental.pallas import tpu as pltpu

F32 = jnp.float32
BF16 = jnp.bfloat16

N_HEADS = 16
HEAD_DIM = 64
ATTN_WIDTH = N_HEADS * HEAD_DIM
POOL_WINDOWS = (2, 4, 8, 16)
POOL_GROUP_WIDTH = 256
POOL_BUF = max(POOL_WINDOWS) - 1
POOL_HALO = POOL_BUF + 1
DILATED = ((128, 1), (512, 4), (2048, 16))
N_STEPS = 128
QBLOCK = 128
NUM_BUCKETS = 32
MAX_DISTANCE = 2048
PAST_LEN = 16384
N_GROUPS = 4
EXPERTS_PER_GROUP = 4
N_EXPERTS = N_GROUPS * EXPERTS_PER_GROUP
TOP_K = 2
EPS = 1e-6
SCALE = HEAD_DIM ** -0.5
NEG_INF = -1e30
LANES = 128
SUBLANES = 8
MXU_DIM = 256
VMEM_LIMIT = 56 * 1024 * 1024


DMA_UNROLL = 8


def _params(*sem, unchecked=False):
    return pltpu.CompilerParams(dimension_semantics=sem, vmem_limit_bytes=VMEM_LIMIT,
                                disable_bounds_checks=unchecked)


def _const_spec(shape):
    zeros = (0,) * len(shape)
    return pl.BlockSpec(shape, lambda *_: zeros)


def _cast_kernel(x_ref, o_ref):
    o_ref[...] = x_ref[...].astype(o_ref.dtype)


def _to_bf16(w2d, block_rows):
    rows, cols = w2d.shape
    return pl.pallas_call(
        _cast_kernel,
        grid=(rows // block_rows,),
        in_specs=[pl.BlockSpec((block_rows, cols), lambda i: (i, 0))],
        out_specs=pl.BlockSpec((block_rows, cols), lambda i: (i, 0)),
        out_shape=jax.ShapeDtypeStruct((rows, cols), BF16),
        compiler_params=_params("parallel"),
        name="cast_bf16",
    )(w2d)


def _inproj_kernel(x_ref, ln_ref, w_ref, qg_ref, kg_ref, avg_ref, *refs, dilated):
    k_ref, v_ref, u_ref, q_ref, kb_ref, vb_ref = refs[:6]
    tm = x_ref.shape[0]
    x = x_ref[...]
    ms = jnp.mean(x * x, axis=-1, keepdims=True)
    h = (x * lax.rsqrt(ms + EPS) * ln_ref[...]).astype(BF16)
    n_chunks = ATTN_WIDTH // MXU_DIM

    def section(s):
        return jnp.dot(h, w_ref[:, s * ATTN_WIDTH:(s + 1) * ATTN_WIDTH],
                       preferred_element_type=F32)

    def head_norm(z, g_ref, c):
        zc = z[:, c * MXU_DIM:(c + 1) * MXU_DIM]
        msh = jnp.dot((zc * zc).astype(BF16), avg_ref[...], preferred_element_type=F32)
        return zc * lax.rsqrt(msh + EPS) * g_ref[:, c * MXU_DIM:(c + 1) * MXU_DIM]

    def emit(chunks, which, f32_ref, bf_ref):
        heads_per_chunk = MXU_DIM // HEAD_DIM
        for c, zc in enumerate(chunks):
            cs = slice(c * MXU_DIM, (c + 1) * MXU_DIM)
            bf_ref[:, cs] = zc.astype(BF16)
            if f32_ref is None:
                continue
            if dilated:
                zt = zc.T
                for j in range(heads_per_chunk):
                    f32_ref[c * heads_per_chunk + j] = zt[j * HEAD_DIM:(j + 1) * HEAD_DIM, :]
            else:
                f32_ref[:, cs] = zc
        if not dilated:
            return
        stage_ref = refs[-1]
        for c, zc in enumerate(chunks):
            for half in range(MXU_DIM // LANES):
                stage_ref[2 * c + half] = zc[:, half * LANES:(half + 1) * LANES]
        for bi, (_, dil) in enumerate(DILATED[1:]):
            out_ref = refs[6 + 3 * bi + which]
            n = tm // dil
            for r in range(dil):
                for s in range(ATTN_WIDTH // LANES):
                    rows = stage_ref[s, pl.ds(r, n, stride=dil), :]
                    out_ref[r, :, s * LANES:(s + 1) * LANES] = rows.astype(BF16)

    zq = section(0)
    emit([head_norm(zq, qg_ref, c) for c in range(n_chunks)], 0, None, q_ref)
    zk = section(1)
    emit([head_norm(zk, kg_ref, c) for c in range(n_chunks)], 1, k_ref, kb_ref)
    zv = section(2)
    emit([zv[:, c * MXU_DIM:(c + 1) * MXU_DIM] for c in range(n_chunks)], 2, v_ref, vb_ref)
    u_ref[...] = section(3)


def _inproj(x2d, ln1, w_in_b, q_gain, k_gain, avg, tm, batch=None, seq=None):
    n, d = x2d.shape
    dilated = seq is not None
    row = lambda i: (i, 0)
    wide = pl.BlockSpec((tm, ATTN_WIDTH), row)
    f32o = jax.ShapeDtypeStruct((n, ATTN_WIDTH), F32)
    bf16o = jax.ShapeDtypeStruct((n, ATTN_WIDTH), BF16)
    out_specs = [wide] * 6
    out_shape = [f32o, f32o, f32o, bf16o, bf16o, bf16o]
    scratch = []
    if dilated:
        tiles = seq // tm
        kv_t = pl.BlockSpec((None, N_HEADS, HEAD_DIM, tm), lambda i: (i // tiles, 0, 0, i % tiles))
        out_specs[:2] = [kv_t, kv_t]
        out_shape[:2] = [jax.ShapeDtypeStruct((batch, N_HEADS, HEAD_DIM, seq), F32)] * 2
        for _, dil in DILATED[1:]:
            spec = pl.BlockSpec((None, dil, tm // dil, ATTN_WIDTH),
                                lambda i: (i // tiles, 0, i % tiles, 0))
            out_specs += [spec] * 3
            out_shape += [jax.ShapeDtypeStruct((batch, dil, seq // dil, ATTN_WIDTH), BF16)] * 3
        scratch = [pltpu.VMEM((ATTN_WIDTH // LANES, tm, LANES), F32)]
    return pl.pallas_call(
        functools.partial(_inproj_kernel, dilated=dilated),
        grid=(n // tm,),
        in_specs=[pl.BlockSpec((tm, d), row),
                  _const_spec((1, d)),
                  pl.BlockSpec(w_in_b.shape, lambda i: (0, 0), pipeline_mode=pl.Buffered(1)),
                  _const_spec((1, ATTN_WIDTH)), _const_spec((1, ATTN_WIDTH)),
                  _const_spec((MXU_DIM, MXU_DIM))],
        out_specs=out_specs,
        out_shape=out_shape,
        scratch_shapes=scratch,
        compiler_params=_params("parallel"),
        name="inproj",
    )(x2d, ln1, w_in_b, q_gain, k_gain, avg)


def _attn_kernel(q_ref, kp_ref, kc_ref, vp_ref, vc_ref, bias_ref, o_ref, lse_ref,
                 s_ref, p_ref):
    i = pl.program_id(2)
    lane = lax.broadcasted_iota(jnp.int32, (QBLOCK, LANES), 1)
    low = lane < HEAD_DIM
    keep_low = low.astype(F32).astype(BF16)
    keep_high = (1.0 - low.astype(F32)).astype(BF16)
    nt = (((1,), (1,)), ((), ()))
    n_pairs = N_HEADS // 2

    def run(with_prev):
        k0 = 0 if with_prev else QBLOCK
        nk = 2 * QBLOCK - k0
        ones = jnp.ones((nk, LANES), BF16)
        for hp in range(n_pairs):
            cs = slice(hp * LANES, (hp + 1) * LANES)
            qp = q_ref[:, cs]
            q2 = jnp.concatenate([qp * keep_low, qp * keep_high], axis=0)
            keys = kc_ref[:, cs]
            if with_prev:
                keys = jnp.concatenate([kp_ref[:, cs], keys], axis=0)
            s2 = lax.dot_general(q2, keys, nt, preferred_element_type=F32)
            s_ref[hp, :, k0:] = s2 + bias_ref[hp, :, k0:]

        m_all = jnp.zeros((QBLOCK, LANES), F32)
        for hp in range(n_pairs):
            for sub in range(2):
                rows = slice(sub * QBLOCK, (sub + 1) * QBLOCK)
                m = jnp.max(s_ref[hp, rows, k0:], axis=-1, keepdims=True)
                p_ref[hp, rows, k0:] = jnp.exp(s_ref[hp, rows, k0:] - m).astype(BF16)
                m_all = jnp.where(lane == 2 * hp + sub, m, m_all)

        den_all = jnp.ones((QBLOCK, LANES), F32)
        for hp in range(n_pairs):
            cs = slice(hp * LANES, (hp + 1) * LANES)
            vals = vc_ref[:, cs]
            if with_prev:
                vals = jnp.concatenate([vp_ref[:, cs], vals], axis=0)
            r = jnp.dot(p_ref[hp, :, k0:], jnp.concatenate([vals, ones], axis=1),
                        preferred_element_type=F32)
            den0, den1 = r[:QBLOCK, LANES:], r[QBLOCK:, LANES:]
            o_ref[:, cs] = jnp.where(low, r[:QBLOCK, :LANES] / den0, r[QBLOCK:, :LANES] / den1)
            den_all = jnp.where(lane == 2 * hp, den0, den_all)
            den_all = jnp.where(lane == 2 * hp + 1, den1, den_all)
        lse_ref[...] = m_all + jnp.log(den_all)

    @pl.when(i == 0)
    def _():
        run(False)

    @pl.when(i > 0)
    def _():
        run(True)


def _attn_branch(q, kb, vb, bias_all, branch):
    batch, dil, sub, _ = q.shape
    cur = lambda b, r, i: (b, r, i, 0)
    prev = lambda b, r, i: (b, r, jnp.maximum(i - 1, 0), 0)
    wide_c = pl.BlockSpec((None, None, QBLOCK, ATTN_WIDTH), cur)
    wide_p = pl.BlockSpec((None, None, QBLOCK, ATTN_WIDTH), prev)
    pairs = N_HEADS // 2
    bias_spec = pl.BlockSpec((None, pairs, 2 * QBLOCK, 2 * QBLOCK), lambda b, r, i: (branch, 0, 0, 0))
    return pl.pallas_call(
        _attn_kernel,
        grid=(batch, dil, sub // QBLOCK),
        in_specs=[wide_c, wide_p, wide_c, wide_p, wide_c, bias_spec],
        out_specs=[wide_c, pl.BlockSpec((None, None, QBLOCK, LANES), cur)],
        out_shape=[jax.ShapeDtypeStruct((batch, dil, sub, ATTN_WIDTH), F32),
                   jax.ShapeDtypeStruct((batch, dil, sub, LANES), F32)],
        scratch_shapes=[pltpu.VMEM((pairs, 2 * QBLOCK, 2 * QBLOCK), F32),
                        pltpu.VMEM((pairs, 2 * QBLOCK, 2 * QBLOCK), BF16)],
        compiler_params=_params("parallel", "parallel", "parallel"),
        name=f"attn_d{dil}",
    )(q, kb, kb, vb, vb, bias_all)


def _bias_kernel(base_ref, out_ref):
    width = base_ref.shape[1]
    for h in range(N_HEADS):
        rows = jnp.broadcast_to(base_ref[h:h + 1, :], (QBLOCK, width))
        band = pltpu.roll(rows, 0, 1, stride=1, stride_axis=0)
        out_ref[h // 2, (h % 2) * QBLOCK:(h % 2 + 1) * QBLOCK, :] = band[:, :2 * QBLOCK]


def _prompt_bias(rel_bias):
    width = 3 * QBLOCK
    steps = QBLOCK - np.arange(width)
    valid = (steps >= 0) & (steps <= N_STEPS)
    buckets = np.stack([_bucket_table(dil)[np.clip(steps, 0, N_STEPS)] for _, dil in DILATED])
    base = jnp.where(valid[None, :, None], rel_bias.astype(F32)[buckets], NEG_INF)
    base = jnp.transpose(base, (0, 2, 1))
    pairs = N_HEADS // 2
    return pl.pallas_call(
        _bias_kernel,
        grid=(len(DILATED),),
        in_specs=[pl.BlockSpec((None, N_HEADS, width), lambda g: (g, 0, 0))],
        out_specs=pl.BlockSpec((None, pairs, 2 * QBLOCK, 2 * QBLOCK), lambda g: (g, 0, 0, 0)),
        out_shape=jax.ShapeDtypeStruct((len(DILATED), pairs, 2 * QBLOCK, 2 * QBLOCK), F32),
        compiler_params=_params("parallel"),
        name="bias_table",
    )(base)


def _split_dot(a, b_bf16):
    hi = a.astype(BF16)
    lo = (a - hi.astype(F32)).astype(BF16)
    return (jnp.dot(hi, b_bf16, preferred_element_type=F32)
            + jnp.dot(lo, b_bf16, preferred_element_type=F32))


def _pool_groups(comb, u, cnt_fn, wp_ref, ps_ref):
    t = u.shape[0]
    outs = []
    run = comb
    width = 1
    for g, w in enumerate(POOL_WINDOWS):
        while width < w:
            run = run + pltpu.roll(run, width, 0)
            width *= 2
        cs = slice(g * POOL_GROUP_WIDTH, (g + 1) * POOL_GROUP_WIDTH)
        d = run[POOL_HALO:POOL_HALO + t, cs] / cnt_fn(w) - u[:, cs]
        y = jnp.dot(d.astype(BF16), wp_ref[g], preferred_element_type=F32)
        outs.append(y * ps_ref[:, cs])
    return outs


def _mix_kernel(o1_ref, o2_ref, o3_ref, l1_ref, l2_ref, l3_ref, u_ref, halo_ref,
                ex_ref, wp_ref, ps_ref, mix_ref, il_ref, ls_ref, *, seq):
    tm = u_ref.shape[0]
    n_slabs = ATTN_WIDTH // LANES
    lses = [l1_ref[0]]
    for bi, (o_ref, l_ref) in enumerate(((o2_ref, l2_ref), (o3_ref, l3_ref))):
        dil = o_ref.shape[0]
        n = tm // dil
        for r in range(dil):
            ls_ref[bi, pl.ds(r, n, stride=dil), :] = l_ref[r]
            for s in range(n_slabs):
                il_ref[bi, s, pl.ds(r, n, stride=dil), :] = o_ref[r, :, s * LANES:(s + 1) * LANES]
        lses.append(ls_ref[bi])
    l1, l2, l3 = lses
    m = jnp.maximum(jnp.maximum(l1, l2), l3)
    e1, e2, e3 = jnp.exp(l1 - m), jnp.exp(l2 - m), jnp.exp(l3 - m)
    inv = 1.0 / (e1 + e2 + e3)
    ex = ex_ref[...]
    w1, w2, w3 = (_split_dot(e * inv, ex) for e in (e1, e2, e3))
    for s in range(n_slabs):
        cs = slice(s * LANES, (s + 1) * LANES)
        attn = w1[:, cs] * o1_ref[0, :, cs] + w2[:, cs] * il_ref[0, s] + w3[:, cs] * il_ref[1, s]
        mix_ref[:, cs] = attn.astype(BF16)

    pos0 = (pl.program_id(0) * tm) % seq
    u = u_ref[...]
    halo = jnp.where(pos0 == 0, 0.0, halo_ref[...])
    comb = jnp.concatenate([halo, u], axis=0)
    pos = pos0 + lax.broadcasted_iota(jnp.int32, (tm, 1), 0)
    cnt_fn = lambda w: jnp.minimum(pos + 1, w).astype(F32)
    for g, y in enumerate(_pool_groups(comb, u, cnt_fn, wp_ref, ps_ref)):
        lo = ATTN_WIDTH + g * POOL_GROUP_WIDTH
        mix_ref[:, lo:lo + POOL_GROUP_WIDTH] = y.astype(BF16)


def _mix(o_list, l_list, u, expand, wp_b, pool_scale, seq, tm):
    n = u.shape[0]
    tiles = seq // tm
    row = lambda i: (i, 0)
    res = lambda i: (i // tiles, 0, i % tiles, 0)
    o_specs = [pl.BlockSpec((None, o.shape[1], tm // o.shape[1], ATTN_WIDTH), res) for o in o_list]
    l_specs = [pl.BlockSpec((None, l.shape[1], tm // l.shape[1], LANES), res) for l in l_list]
    halo = pl.BlockSpec((POOL_HALO, ATTN_WIDTH),
                        lambda i: (jnp.maximum(i * (tm // POOL_HALO) - 1, 0), 0))
    n_dilated = len(o_list) - 1
    return pl.pallas_call(
        functools.partial(_mix_kernel, seq=seq),
        grid=(n // tm,),
        in_specs=o_specs + l_specs + [pl.BlockSpec((tm, ATTN_WIDTH), row), halo,
                                      _const_spec(expand.shape), _const_spec(wp_b.shape),
                                      _const_spec((1, ATTN_WIDTH))],
        out_specs=pl.BlockSpec((tm, 2 * ATTN_WIDTH), row),
        out_shape=jax.ShapeDtypeStruct((n, 2 * ATTN_WIDTH), BF16),
        scratch_shapes=[pltpu.VMEM((n_dilated, ATTN_WIDTH // LANES, tm, LANES), F32),
                        pltpu.VMEM((n_dilated, tm, LANES), F32)],
        compiler_params=_params("parallel"),
        name="mix",
    )(*o_list, *l_list, u, u, expand, wp_b, pool_scale)


def _sample_attn_kernel(q_ref, kn_ref, vn_ref, k_ref, v_ref, bc_ref, bn_ref, o_ref):
    n_heads, t_new, _ = q_ref.shape
    n_br = bc_ref.shape[0]
    lane_max = lambda a: jnp.max(a, axis=1, keepdims=True)
    lane_sum = lambda a: jnp.sum(a, axis=1, keepdims=True)
    new_lane = lax.broadcasted_iota(jnp.int32, (t_new, t_new), 1)
    nt = (((1,), (1,)), ((), ()))
    for h in range(n_heads):
        q, kn, vn = q_ref[h], kn_ref[h], vn_ref[h]
        s_c = jnp.dot(q.astype(BF16), k_ref[h].astype(BF16), preferred_element_type=F32)
        s_n = jnp.zeros((t_new, t_new), F32)
        for tp in range(t_new):
            s_n = jnp.where(new_lane == tp, lane_sum(q * kn[tp:tp + 1, :]), s_n)
        sc = [s_c + bc_ref[br, h] for br in range(n_br)]
        sn = [s_n + bn_ref[br, h] for br in range(n_br)]
        m = functools.reduce(jnp.maximum, [lane_max(a) for a in sc + sn])
        p_c = functools.reduce(jnp.add, [jnp.exp(a - m) for a in sc])
        p_n = functools.reduce(jnp.add, [jnp.exp(a - m) for a in sn])
        acc = lax.dot_general(p_c.astype(BF16), v_ref[h].astype(BF16), nt,
                              preferred_element_type=F32)
        for tp in range(t_new):
            acc = acc + p_n[:, tp:tp + 1] * vn[tp:tp + 1, :]
        o_ref[h] = acc / (lane_sum(p_c) + lane_sum(p_n))


def _sample_attn(q, kn, vn, k_t, v_t, bias_c, bias_n, heads_per_step):
    nb, nh, t_new, hd = q.shape
    win = k_t.shape[-1]
    hg = heads_per_step
    n_br = bias_c.shape[0]
    small = pl.BlockSpec((None, hg, t_new, hd), lambda b, g: (b, g, 0, 0))
    wide = pl.BlockSpec((None, hg, hd, win), lambda b, g: (b, g, 0, 0))
    return pl.pallas_call(
        _sample_attn_kernel,
        grid=(nb, nh // hg),
        in_specs=[small, small, small, wide, wide,
                  pl.BlockSpec((n_br, hg, t_new, win), lambda b, g: (0, g, 0, 0)),
                  pl.BlockSpec((n_br, hg, t_new, t_new), lambda b, g: (0, g, 0, 0))],
        out_specs=small,
        out_shape=jax.ShapeDtypeStruct((nb, nh, t_new, hd), F32),
        compiler_params=_params("parallel", "parallel"),
        name="sample_attn",
    )(q, kn, vn, k_t, v_t, bias_c, bias_n)


def _sample_pool_kernel(u_ref, st_ref, wp_ref, ps_ref, pool_ref, sbuf_ref, comb_ref, *, start):
    t_new = u_ref.shape[0]
    u = u_ref[...]
    comb_ref[...] = jnp.zeros_like(comb_ref)
    comb_ref[1:POOL_HALO, :] = st_ref[...]
    comb_ref[POOL_HALO:POOL_HALO + t_new, :] = u
    pos = start + lax.broadcasted_iota(jnp.int32, (t_new, 1), 0)
    cnt_fn = lambda w: jnp.minimum(pos + 1, w).astype(F32)
    for g, y in enumerate(_pool_groups(comb_ref[...], u, cnt_fn, wp_ref, ps_ref)):
        pool_ref[:, g * POOL_GROUP_WIDTH:(g + 1) * POOL_GROUP_WIDTH] = y
    sbuf_ref[:POOL_BUF - t_new, :] = st_ref[t_new:, :]
    sbuf_ref[POOL_BUF - t_new:, :] = u


def _sample_pool(u, state, wp_b, pool_scale):
    nb, t_new, w = u.shape
    flat = lambda rows: pl.BlockSpec((None, rows, w), lambda b: (b, 0, 0))
    return pl.pallas_call(
        functools.partial(_sample_pool_kernel, start=PAST_LEN),
        grid=(nb,),
        in_specs=[flat(t_new), flat(POOL_BUF), _const_spec(wp_b.shape), _const_spec((1, w))],
        out_specs=[flat(t_new), flat(POOL_BUF)],
        out_shape=[jax.ShapeDtypeStruct((nb, t_new, w), F32),
                   jax.ShapeDtypeStruct((nb, POOL_BUF, w), F32)],
        scratch_shapes=[pltpu.VMEM((POOL_HALO + 8, w), F32)],
        compiler_params=_params("parallel"),
        name="sample_pool",
    )(u, state, wp_b, pool_scale)


def _outproj_kernel(x_ref, mix_ref, wo_ref, ln_ref, wr_ref, br_ref, tri_ref, cin_ref,
                    x1_ref, h_ref, gate_ref, route_ref, cout_ref, count_ref):
    @pl.when(pl.program_id(0) == 0)
    def _():
        count_ref[...] = cin_ref[...]

    x1 = x_ref[...] + jnp.dot(mix_ref[...].astype(BF16), wo_ref[...],
                              preferred_element_type=F32)
    x1_ref[...] = x1
    ms = jnp.mean(x1 * x1, axis=-1, keepdims=True)
    h = x1 * lax.rsqrt(ms + EPS) * ln_ref[...]
    h_ref[...] = h
    lg = jnp.dot(h.astype(BF16), wr_ref[...], preferred_element_type=F32) + br_ref[...]

    lane = lax.broadcasted_iota(jnp.int32, lg.shape, 1).astype(F32)
    big = float(LANES)
    row_max = lambda mask: jnp.max(jnp.where(mask, lg, -jnp.inf), axis=-1, keepdims=True)
    first = lambda mask: jnp.min(jnp.where(mask, lane, big), axis=-1, keepdims=True)
    is_g = lane < N_GROUPS
    mg = row_max(is_g)
    g_top = first(jnp.logical_and(is_g, lg == mg))
    den = jnp.sum(jnp.where(is_g, jnp.exp(lg - mg), 0.0), axis=-1, keepdims=True)
    p_top = 1.0 / den
    base = N_GROUPS + EXPERTS_PER_GROUP * g_top
    in_grp = jnp.logical_and(lane >= base, lane < base + EXPERTS_PER_GROUP)
    v1 = row_max(in_grp)
    i1 = first(jnp.logical_and(in_grp, lg == v1))
    rest = jnp.logical_and(in_grp, lane != i1)
    v2 = row_max(rest)
    i2 = first(jnp.logical_and(rest, lg == v2))
    e21 = jnp.exp(v2 - v1)
    s21 = 1.0 + e21
    gate1 = p_top * (1.0 / s21)
    gate2 = p_top * (e21 / s21)
    gate_ref[...] = jnp.where(lane == 0.0, gate1, jnp.where(lane == 1.0, gate2, 0.0))
    e1, e2 = i1 - N_GROUPS, i2 - N_GROUPS

    hot1 = (lane == e1).astype(F32)
    hot2 = (lane == e2).astype(F32)
    hot = hot1 + hot2
    before = count_ref[...] + jnp.dot(tri_ref[...], hot.astype(BF16), preferred_element_type=F32)
    rank1 = jnp.sum(hot1 * before, axis=-1, keepdims=True)
    rank2 = jnp.sum(hot2 * before, axis=-1, keepdims=True)
    count_ref[...] = count_ref[...] + jnp.sum(hot, axis=0, keepdims=True)
    cout_ref[...] = count_ref[...]

    cols = jnp.zeros_like(lg)
    for r, col in enumerate((e1, e2, rank1, rank2)):
        cols = jnp.where(lane == float(r), col, cols)
    route_ref[...] = cols.T[:route_ref.shape[0], :]


def _outproj(x2d, mix, w_o_b, ln2, w_r, b_r, tri, counts_in, tm):
    n, d = x2d.shape
    row = lambda i: (i, 0)
    full = pl.BlockSpec((tm, d), row)
    stat = pl.BlockSpec((tm, LANES), row)
    route_rows = 8
    return pl.pallas_call(
        _outproj_kernel,
        grid=(n // tm,),
        in_specs=[full, full,
                  pl.BlockSpec(w_o_b.shape, lambda i: (0, 0), pipeline_mode=pl.Buffered(1)),
                  _const_spec((1, d)), _const_spec(w_r.shape),
                  _const_spec((1, LANES)), _const_spec((tm, tm)), _const_spec((1, LANES))],
        out_specs=[full, full, stat,
                   pl.BlockSpec((route_rows, tm), lambda i: (0, i)), _const_spec((1, LANES))],
        out_shape=[jax.ShapeDtypeStruct((n, d), F32),
                   jax.ShapeDtypeStruct((n, d), F32),
                   jax.ShapeDtypeStruct((n, LANES), F32),
                   jax.ShapeDtypeStruct((route_rows, n), F32),
                   jax.ShapeDtypeStruct((1, LANES), F32)],
        scratch_shapes=[pltpu.VMEM((1, LANES), F32)],
        compiler_params=_params("arbitrary"),
        name="outproj",
    )(x2d, mix, w_o_b, ln2, w_r, b_r, tri[:tm, :tm], counts_in)


def _row_copy(src_hbm, dst_vmem, sem, src_row, dst_row):
    return pltpu.make_async_copy(src_hbm.at[pl.ds(src_row, 1)],
                                 dst_vmem.at[pl.ds(dst_row, 1)], sem)


def _dispatch_kernel(pos_ref, pos_s_ref, pad_start_ref, pad_len_ref, tail_ref, h_ref, h_s_ref,
                     hs_ref, sem_ref, zero_ref, *, tile_rows):
    i = pl.program_id(0)
    sem = sem_ref.at[0]

    def scatter(src_ref, dst_rows_ref):
        n_rows = src_ref.shape[0]

        def put(r, k):
            return _row_copy(src_ref, hs_ref, sem, r, dst_rows_ref[0, k * n_rows + r])

        def issue(r, c):
            for k in range(TOP_K):
                put(r, k).start()
            return c

        def drain(r, c):
            for k in range(TOP_K):
                put(r, k).wait()
            return c

        lax.fori_loop(0, n_rows, issue, 0, unroll=DMA_UNROLL)
        lax.fori_loop(0, n_rows, drain, 0, unroll=DMA_UNROLL)

    scatter(h_ref, pos_ref)

    @pl.when(i == pl.num_programs(0) - 1)
    def _():
        scatter(h_s_ref, pos_s_ref)
        zero_ref[...] = jnp.zeros_like(zero_ref)

        sizes = [s for s in (tile_rows >> (b + 1) for b in range(tile_rows.bit_length()))
                 if s >= SUBLANES]

        def pad_copies(act):
            def single_rows(first, count):
                def body(r, c):
                    act(_row_copy(zero_ref, hs_ref, sem, 0, first + r))
                    return c
                lax.fori_loop(0, count, body, 0)

            for e in range(N_EXPERTS):
                start, length = pad_start_ref[e], pad_len_ref[e]
                head = jnp.minimum((-start) & (SUBLANES - 1), length)
                single_rows(start, head)
                body_len = length - head
                aligned = body_len & -SUBLANES
                offset = start + head
                for b, size in enumerate(sizes):
                    @pl.when((aligned & size) != 0)
                    def _():
                        rows = pl.ds(pl.multiple_of(offset, SUBLANES), size)
                        act(pltpu.make_async_copy(zero_ref.at[pl.ds(0, size)], hs_ref.at[rows],
                                                  sem_ref.at[1 + b]))
                    offset = offset + (aligned & size)
                single_rows(offset, body_len - aligned)

        pad_copies(lambda copy: copy.start())
        pad_copies(lambda copy: copy.wait())

        n_tiles = hs_ref.shape[0] // tile_rows

        def fill_tile(t):
            rows = pl.ds(pl.multiple_of(t * tile_rows, tile_rows), tile_rows)
            return pltpu.make_async_copy(zero_ref, hs_ref.at[rows], sem)

        def start_tile(t, c):
            fill_tile(t).start()
            return c

        def wait_tile(t, c):
            fill_tile(t).wait()
            return c

        lax.fori_loop(tail_ref[0], n_tiles, start_tile, 0)
        lax.fori_loop(tail_ref[0], n_tiles, wait_tile, 0)


def _tile_rows_spec(tm):
    return pl.BlockSpec((None, 1, TOP_K * tm), lambda i: (i, 0, 0), memory_space=pltpu.SMEM)


def _dispatch(rows_p, rows_s, pad_start, pad_len, tail, h, h_s, n_tiles, tile_rows, tm):
    n, d = h.shape
    smem = pl.BlockSpec(memory_space=pltpu.SMEM)
    return pl.pallas_call(
        functools.partial(_dispatch_kernel, tile_rows=tile_rows),
        grid=(n // tm,),
        in_specs=[_tile_rows_spec(tm), smem, smem, smem, smem,
                  pl.BlockSpec((tm, d), lambda i: (i, 0)),
                  pl.BlockSpec(h_s.shape, lambda i: (0, 0))],
        out_specs=pl.BlockSpec(memory_space=pl.ANY),
        out_shape=jax.ShapeDtypeStruct((n_tiles * tile_rows, d), h.dtype),
        scratch_shapes=[pltpu.SemaphoreType.DMA((tile_rows.bit_length(),)),
                        pltpu.VMEM((tile_rows, d), h.dtype)],
        compiler_params=_params("arbitrary", unchecked=True),
        name="dispatch",
    )(rows_p, rows_s, pad_start, pad_len, tail, h, h_s)


def _expert_kernel(tile_e_ref, tile_on_ref, hs_ref, wg_ref, wu_ref, wd_ref, y_ref,
                   wg_b, wu_b, wd_b):
    i = pl.program_id(0)

    @pl.when(jnp.logical_or(i == 0, tile_e_ref[i] != tile_e_ref[jnp.maximum(i - 1, 0)]))
    def _():
        wg_b[...] = wg_ref[...].astype(BF16)
        wu_b[...] = wu_ref[...].astype(BF16)
        wd_b[...] = wd_ref[...].astype(BF16)

    @pl.when(tile_on_ref[i] == 1)
    def _():
        hb = hs_ref[...].astype(BF16)
        a = jnp.dot(hb, wg_b[...], preferred_element_type=F32)
        b = jnp.dot(hb, wu_b[...], preferred_element_type=F32)
        hid = a * jax.nn.sigmoid(a) * b
        y_ref[...] = jnp.dot(hid.astype(BF16), wd_b[...], preferred_element_type=F32)

    @pl.when(tile_on_ref[i] == 0)
    def _():
        y_ref[...] = jnp.zeros_like(y_ref)


def _experts(tile_e, tile_on, hs, w_gate, w_up, w_down, tm):
    n_tiles = tile_e.shape[0]
    d, f = w_gate.shape[1:]
    by_expert = lambda i, te, on: (te[i], 0, 0)
    grid_spec = pltpu.PrefetchScalarGridSpec(
        num_scalar_prefetch=2,
        grid=(n_tiles,),
        in_specs=[pl.BlockSpec((tm, d), lambda i, te, on: (i, 0)),
                  pl.BlockSpec((None, d, f), by_expert),
                  pl.BlockSpec((None, d, f), by_expert),
                  pl.BlockSpec((None, f, d), by_expert)],
        out_specs=pl.BlockSpec((tm, d), lambda i, te, on: (i, 0)),
        scratch_shapes=[pltpu.VMEM((d, f), BF16), pltpu.VMEM((d, f), BF16),
                        pltpu.VMEM((f, d), BF16)],
    )
    return pl.pallas_call(
        _expert_kernel,
        grid_spec=grid_spec,
        out_shape=jax.ShapeDtypeStruct((n_tiles * tm, d), F32),
        compiler_params=_params("arbitrary"),
        name="experts",
    )(tile_e, tile_on, hs, w_gate, w_up, w_down)


def _combine_kernel(rows_ref, next_rows_ref, x1_ref, gate_ref, ys_hbm, y_ref, buf_ref, sem_ref):
    i = pl.program_id(0)
    n_tiles = pl.num_programs(0)
    tm = y_ref.shape[0]

    def issue(src_rows_ref, slot):
        def body(r, c):
            for k in range(TOP_K):
                _row_copy(ys_hbm, buf_ref.at[slot, k], sem_ref.at[slot],
                          src_rows_ref[0, k * tm + r], r).start()
            return c
        lax.fori_loop(0, tm, body, 0, unroll=DMA_UNROLL)

    def drain(slot):
        def body(r, c):
            for k in range(TOP_K):
                _row_copy(ys_hbm, buf_ref.at[slot, k], sem_ref.at[slot], 0, r).wait()
            return c
        lax.fori_loop(0, tm, body, 0, unroll=DMA_UNROLL)

    slot = i % 2

    @pl.when(i == 0)
    def _():
        issue(rows_ref, 0)

    drain(slot)

    @pl.when(i + 1 < n_tiles)
    def _():
        issue(next_rows_ref, 1 - slot)

    gate = gate_ref[...]
    y_ref[...] = (x1_ref[...] + gate[:, 0:1] * buf_ref[slot, 0]
                  + gate[:, 1:2] * buf_ref[slot, 1])


def _combine(rows, x1, gate, ys, tm):
    n, d = x1.shape
    last = n // tm - 1
    next_spec = pl.BlockSpec((None, 1, TOP_K * tm), lambda i: (jnp.minimum(i + 1, last), 0, 0),
                             memory_space=pltpu.SMEM)
    return pl.pallas_call(
        _combine_kernel,
        grid=(n // tm,),
        in_specs=[_tile_rows_spec(tm), next_spec,
                  pl.BlockSpec((tm, d), lambda i: (i, 0)),
                  pl.BlockSpec((tm, LANES), lambda i: (i, 0)),
                  pl.BlockSpec(memory_space=pl.ANY)],
        out_specs=pl.BlockSpec((tm, d), lambda i: (i, 0)),
        out_shape=jax.ShapeDtypeStruct((n, d), F32),
        scratch_shapes=[pltpu.VMEM((2, TOP_K, tm, d), F32), pltpu.SemaphoreType.DMA((2,))],
        compiler_params=_params("arbitrary", unchecked=True),
        name="combine",
    )(rows, rows, x1, gate, ys)


def _bucket_table(dilation):
    dist = np.arange(N_STEPS + 1, dtype=np.int64) * dilation
    max_exact = NUM_BUCKETS // 2
    df = np.maximum(dist, 1).astype(np.float32)
    large = max_exact + (np.log(df / np.float32(max_exact))
                         / np.float32(math.log(MAX_DISTANCE / max_exact))
                         * np.float32(NUM_BUCKETS - max_exact)).astype(np.int32)
    large = np.minimum(large, NUM_BUCKETS - 1)
    return np.where(dist < max_exact, dist, large).astype(np.int32)


def _sample_bias(rel_bias, t_new, win):
    buckets = np.stack([_bucket_table(dil) for _, dil in DILATED])
    by_step = jnp.transpose(rel_bias.astype(F32)[buckets], (0, 2, 1))
    nh = by_step.shape[1]
    neg = lambda *shape: jnp.full(shape, NEG_INF, F32)
    t = np.arange(t_new)
    cached, fresh = [], []
    for br, (_, dil) in enumerate(DILATED):
        rev = by_step[br, :, ::-1][:, :N_STEPS]
        if dil == 1:
            rows = [jnp.concatenate([neg(nh, win - N_STEPS + q), rev[:, :N_STEPS - q]], axis=1)
                    for q in range(t_new)]
            cached.append(jnp.stack(rows, axis=1))
        else:
            own = (t[:, None, None] == np.arange(dil)[None, None, :])
            band = jnp.where(own[None], rev[:, None, :, None], NEG_INF)
            band = band.reshape(nh, t_new, dil * N_STEPS)
            cached.append(jnp.concatenate([neg(nh, t_new, win - dil * N_STEPS), band], axis=2))
        step = t[:, None] - t[None, :]
        ok = (step >= 0) & (step % dil == 0)
        vals = by_step[br][:, np.where(ok, step // dil, 0)]
        fresh.append(jnp.where(ok[None], vals, NEG_INF))
    return jnp.stack(cached), jnp.stack(fresh)


def _tile_plan(counts, n_tiles, tm):
    tiles_e = (counts + tm - 1) // tm
    ends = jnp.cumsum(tiles_e)
    first_row = (ends - tiles_e) * tm
    tile = jnp.arange(n_tiles, dtype=jnp.int32)
    tile_e = jnp.sum((ends[None, :] <= tile[:, None]).astype(jnp.int32), axis=1)
    tile_on = (tile_e < N_EXPERTS).astype(jnp.int32)
    tile_e = jnp.minimum(tile_e, N_EXPERTS - 1)
    i32 = lambda a: a.astype(jnp.int32)
    return (i32(tile_e), tile_on, i32(first_row), i32(first_row + counts),
            i32(tiles_e * tm - counts), i32(ends[-1:]))


TM_PROJ = 256
TM_TOKEN = 512
TM_EXPERT = 256
SAMPLE_HEADS_PER_STEP = 8


def kernel(x_prompt, x_sample, cache_k, cache_v, state_pool, rel_bias, ln1_w, w_in,
           q_norm_w, k_norm_w, w_pool, pool_scale, w_o, ln2_w, w_router_group,
           b_router_group, w_router_expert, b_router_expert, w_gate, w_up, w_down):
    depth = w_in.shape[0]
    assert depth == 1
    batch, seq, d_model = x_prompt.shape
    nb, t_new, _ = x_sample.shape
    win = cache_k.shape[2]
    f_exp = w_gate.shape[-1]

    w_in_b = _to_bf16(w_in[0], 256)
    w_o_b = _to_bf16(w_o[0], 256)
    wp_b = _to_bf16(w_pool[0].reshape(-1, POOL_GROUP_WIDTH), 256).reshape(w_pool.shape[1:])
    ln1 = ln1_w[0][None, :]
    ln2 = ln2_w[0][None, :]
    q_gain = jnp.tile(q_norm_w[0], N_HEADS)[None, :] * SCALE
    k_gain = jnp.tile(k_norm_w[0], N_HEADS)[None, :]
    ps = pool_scale[0][None, :]
    blk = np.arange(MXU_DIM) // HEAD_DIM
    avg = jnp.asarray((blk[:, None] == blk[None, :]) / HEAD_DIM, BF16)
    head_of_col = np.arange(ATTN_WIDTH) // HEAD_DIM
    expand_np = (np.arange(LANES)[:, None] == head_of_col[None, :])
    expand = jnp.asarray(expand_np, BF16)
    w_r = jnp.concatenate([w_router_group[0], w_router_expert[0]], axis=1)
    w_r = jnp.pad(w_r, ((0, 0), (0, LANES - w_r.shape[1]))).astype(BF16)
    b_r = jnp.pad(jnp.concatenate([b_router_group[0], b_router_expert[0]]),
                  (0, LANES - N_GROUPS - N_EXPERTS))[None, :]

    xp = x_prompt.reshape(batch * seq, d_model)
    proj = _inproj(xp, ln1, w_in_b, q_gain, k_gain, avg, TM_PROJ, batch, seq)
    k, v, u = proj[:3]
    natural = tuple(a.reshape(batch, 1, seq, ATTN_WIDTH) for a in proj[3:6])
    qkv = [natural] + [tuple(proj[6 + 3 * bi:9 + 3 * bi]) for bi in range(len(DILATED) - 1)]
    bias_all = _prompt_bias(rel_bias)
    o_list, l_list = [], []
    for branch, (qd, kd, vd) in enumerate(qkv):
        o, lse = _attn_branch(qd, kd, vd, bias_all, branch)
        o_list.append(o)
        l_list.append(lse)
    mix_p = _mix(o_list, l_list, u, expand, wp_b, ps, seq, TM_TOKEN)
    tri = jnp.asarray(np.tril(np.ones((TM_TOKEN, TM_TOKEN)), -1), BF16)
    x1_p, hp_p, gate_p, route_p, counts_p = _outproj(
        xp, mix_p, w_o_b, ln2, w_r, b_r, tri, jnp.zeros((1, LANES), F32), TM_TOKEN)

    n_s = nb * t_new
    xs = x_sample.reshape(n_s, d_model)
    k_s, v_s, u_s, q_s, _, _ = _inproj(xs, ln1, w_in_b, q_gain, k_gain, avg, n_s)
    heads_s = (nb, t_new, N_HEADS, HEAD_DIM)
    k_s5, v_s5 = k_s.reshape(heads_s), v_s.reshape(heads_s)
    to_lanes = lambda a: jnp.transpose(a, (0, 2, 3, 1))
    by_head = lambda a: jnp.transpose(a, (0, 2, 1, 3))
    bias_c, bias_n = _sample_bias(rel_bias, t_new, win)
    attn_h = _sample_attn(by_head(q_s.astype(F32).reshape(heads_s)), by_head(k_s5), by_head(v_s5),
                          to_lanes(cache_k[0]), to_lanes(cache_v[0]),
                          bias_c, bias_n, SAMPLE_HEADS_PER_STEP)
    attn_s = by_head(attn_h).reshape(n_s, ATTN_WIDTH)
    pool_s, sbuf = _sample_pool(u_s.reshape(nb, t_new, ATTN_WIDTH), state_pool[0], wp_b, ps)
    mix_s = jnp.concatenate([attn_s, pool_s.reshape(n_s, ATTN_WIDTH)], axis=1)
    x1_s, hp_s, gate_s, route_s, counts = _outproj(
        xs, mix_s, w_o_b, ln2, w_r, b_r, tri, counts_p, n_s)

    n_p = batch * seq
    n_pairs = (n_p + n_s) * TOP_K
    tm_e = TM_EXPERT
    n_tiles = -(-n_pairs // tm_e) + N_EXPERTS
    counts_i = counts[0, :N_EXPERTS].astype(jnp.int32)
    tile_e, tile_on, first_row, pad_start, pad_len, tail = _tile_plan(counts_i, n_tiles, tm_e)

    route = jnp.concatenate([route_p, route_s], axis=1)
    ids = route[:TOP_K].astype(jnp.int32)
    base = jnp.zeros_like(ids)
    for e in range(N_EXPERTS):
        base = jnp.where(ids == e, first_row[e], base)
    rows = base + route[TOP_K:2 * TOP_K].astype(jnp.int32)

    def by_tile(rows, tm):
        return jnp.transpose(rows.reshape(TOP_K, -1, tm), (1, 0, 2)).reshape(-1, 1, TOP_K * tm)

    rows_p, rows_s = by_tile(rows[:, :n_p], TM_TOKEN), by_tile(rows[:, n_p:], n_s)
    hs = _dispatch(rows_p, rows_s[0], pad_start, pad_len, tail, hp_p, hp_s, n_tiles, tm_e, TM_TOKEN)
    ys = _experts(tile_e, tile_on, hs, w_gate[0], w_up[0], w_down[0], tm_e)
    y_p = _combine(rows_p, x1_p, gate_p, ys, TM_TOKEN)
    y_s = _combine(rows_s, x1_s, gate_s, ys, n_s)

    keep = min(MAX_DISTANCE, seq)
    from_lanes = lambda a: jnp.transpose(a, (0, 3, 1, 2))[None, :, -keep:]
    return (y_p.reshape(batch, seq, d_model),
            y_s.reshape(nb, t_new, d_model),
            from_lanes(k), from_lanes(v),
            u.reshape(batch, seq, ATTN_WIDTH)[None, :, -POOL_BUF:],
            k_s5[None], v_s5[None], sbuf[None])
```

```python
import functools
import math

import numpy as np
import jax
import jax.numpy as jnp
from jax import lax
from jax.experimental import pallas as pl
from jax.experimental.pallas import tpu as pltpu

F32 = jnp.float32
BF16 = jnp.bfloat16

N_HEADS = 16
HEAD_DIM = 64
ATTN_WIDTH = N_HEADS * HEAD_DIM
POOL_WINDOWS = (2, 4, 8, 16)
POOL_GROUP_WIDTH = 256
POOL_BUF = max(POOL_WINDOWS) - 1
POOL_HALO = POOL_BUF + 1
DILATED = ((128, 1), (512, 4), (2048, 16))
N_STEPS = 128
QBLOCK = 128
NUM_BUCKETS = 32
MAX_DISTANCE = 2048
PAST_LEN = 16384
N_GROUPS = 4
EXPERTS_PER_GROUP = 4
N_EXPERTS = N_GROUPS * EXPERTS_PER_GROUP
TOP_K = 2
EPS = 1e-6
SCALE = HEAD_DIM ** -0.5
NEG_INF = -1e30
LANES = 128
SUBLANES = 8
MXU_DIM = 256
VMEM_LIMIT = 60 * 1024 * 1024


DMA_UNROLL = 8


def _params(*sem, unchecked=False):
    return pltpu.CompilerParams(dimension_semantics=sem, vmem_limit_bytes=VMEM_LIMIT,
                                disable_bounds_checks=unchecked)


def _const_spec(shape):
    zeros = (0,) * len(shape)
    return pl.BlockSpec(shape, lambda *_: zeros)


def _cast_kernel(x_ref, o_ref):
    o_ref[...] = x_ref[...].astype(o_ref.dtype)


def _to_bf16(w2d, block_rows):
    rows, cols = w2d.shape
    return pl.pallas_call(
        _cast_kernel,
        grid=(rows // block_rows,),
        in_specs=[pl.BlockSpec((block_rows, cols), lambda i: (i, 0))],
        out_specs=pl.BlockSpec((block_rows, cols), lambda i: (i, 0)),
        out_shape=jax.ShapeDtypeStruct((rows, cols), BF16),
        compiler_params=_params("parallel"),
        name="cast_bf16",
    )(w2d)


def _inproj_kernel(x_ref, ln_ref, w_ref, qg_ref, kg_ref, avg_ref, *refs, dilated):
    k_ref, v_ref, u_ref, q_ref, kb_ref, vb_ref = refs[:6]
    tm = x_ref.shape[0]
    x = x_ref[...]
    ms = jnp.mean(x * x, axis=-1, keepdims=True)
    h = (x * lax.rsqrt(ms + EPS) * ln_ref[...]).astype(BF16)
    n_chunks = ATTN_WIDTH // MXU_DIM

    def section(s):
        return jnp.dot(h, w_ref[:, s * ATTN_WIDTH:(s + 1) * ATTN_WIDTH],
                       preferred_element_type=F32)

    def head_norm(z, g_ref, c):
        zc = z[:, c * MXU_DIM:(c + 1) * MXU_DIM]
        msh = jnp.dot((zc * zc).astype(BF16), avg_ref[...], preferred_element_type=F32)
        return zc * lax.rsqrt(msh + EPS) * g_ref[:, c * MXU_DIM:(c + 1) * MXU_DIM]

    def emit(chunks, which, f32_ref, bf_ref):
        heads_per_chunk = MXU_DIM // HEAD_DIM
        for c, zc in enumerate(chunks):
            cs = slice(c * MXU_DIM, (c + 1) * MXU_DIM)
            bf_ref[:, cs] = zc.astype(BF16)
            if f32_ref is None:
                continue
            if dilated:
                zt = zc.T
                for j in range(heads_per_chunk):
                    f32_ref[c * heads_per_chunk + j] = zt[j * HEAD_DIM:(j + 1) * HEAD_DIM, :]
            else:
                f32_ref[:, cs] = zc
        if not dilated:
            return
        stage_ref = refs[-3 + which]
        for c, zc in enumerate(chunks):
            for half in range(MXU_DIM // LANES):
                stage_ref[2 * c + half] = zc[:, half * LANES:(half + 1) * LANES]
        for bi, (_, dil) in enumerate(DILATED[1:]):
            out_ref = refs[6 + 3 * bi + which]
            n = tm // dil
            for r in range(dil):
                for s in range(ATTN_WIDTH // LANES):
                    rows = stage_ref[s, pl.ds(r, n, stride=dil), :]
                    out_ref[r, :, s * LANES:(s + 1) * LANES] = rows.astype(BF16)

    zq = section(0)
    emit([head_norm(zq, qg_ref, c) for c in range(n_chunks)], 0, None, q_ref)
    zk = section(1)
    emit([head_norm(zk, kg_ref, c) for c in range(n_chunks)], 1, k_ref, kb_ref)
    zv = section(2)
    emit([zv[:, c * MXU_DIM:(c + 1) * MXU_DIM] for c in range(n_chunks)], 2, v_ref, vb_ref)
    u_ref[...] = section(3)


def _inproj(x2d, ln1, w_in_b, q_gain, k_gain, avg, tm, batch=None, seq=None):
    n, d = x2d.shape
    dilated = seq is not None
    row = lambda i: (i, 0)
    wide = pl.BlockSpec((tm, ATTN_WIDTH), row)
    f32o = jax.ShapeDtypeStruct((n, ATTN_WIDTH), F32)
    bf16o = jax.ShapeDtypeStruct((n, ATTN_WIDTH), BF16)
    out_specs = [wide] * 6
    out_shape = [f32o, f32o, f32o, bf16o, bf16o, bf16o]
    scratch = []
    if dilated:
        tiles = seq // tm
        kv_t = pl.BlockSpec((None, N_HEADS, HEAD_DIM, tm), lambda i: (i // tiles, 0, 0, i % tiles))
        out_specs[:2] = [kv_t, kv_t]
        out_shape[:2] = [jax.ShapeDtypeStruct((batch, N_HEADS, HEAD_DIM, seq), F32)] * 2
        for _, dil in DILATED[1:]:
            spec = pl.BlockSpec((None, dil, tm // dil, ATTN_WIDTH),
                                lambda i: (i // tiles, 0, i % tiles, 0))
            out_specs += [spec] * 3
            out_shape += [jax.ShapeDtypeStruct((batch, dil, seq // dil, ATTN_WIDTH), BF16)] * 3
        scratch = [pltpu.VMEM((ATTN_WIDTH // LANES, tm, LANES), F32)] * 3
    return pl.pallas_call(
        functools.partial(_inproj_kernel, dilated=dilated),
        grid=(n // tm,),
        in_specs=[pl.BlockSpec((tm, d), row),
                  _const_spec((1, d)),
                  pl.BlockSpec(w_in_b.shape, lambda i: (0, 0), pipeline_mode=pl.Buffered(1)),
                  _const_spec((1, ATTN_WIDTH)), _const_spec((1, ATTN_WIDTH)),
                  _const_spec((MXU_DIM, MXU_DIM))],
        out_specs=out_specs,
        out_shape=out_shape,
        scratch_shapes=scratch,
        compiler_params=_params("parallel"),
        name="inproj",
    )(x2d, ln1, w_in_b, q_gain, k_gain, avg)


def _attn_kernel(q_ref, kp_ref, kc_ref, vp_ref, vc_ref, bias_ref, o_ref, lse_ref,
                 s_ref, p_ref):
    i = pl.program_id(2)
    lane = lax.broadcasted_iota(jnp.int32, (QBLOCK, LANES), 1)
    low = lane < HEAD_DIM
    keep_low = low.astype(F32).astype(BF16)
    keep_high = (1.0 - low.astype(F32)).astype(BF16)
    nt = (((1,), (1,)), ((), ()))
    n_pairs = N_HEADS // 2

    def run(with_prev):
        k0 = 0 if with_prev else QBLOCK
        nk = 2 * QBLOCK - k0
        ones = jnp.ones((nk, LANES), BF16)
        for hp in range(n_pairs):
            cs = slice(hp * LANES, (hp + 1) * LANES)
            qp = q_ref[:, cs]
            q2 = jnp.concatenate([qp * keep_low, qp * keep_high], axis=0)
            keys = kc_ref[:, cs]
            if with_prev:
                keys = jnp.concatenate([kp_ref[:, cs], keys], axis=0)
            s2 = lax.dot_general(q2, keys, nt, preferred_element_type=F32)
            s_ref[hp, :, k0:] = s2 + bias_ref[hp, :, k0:]

        m_all = jnp.zeros((QBLOCK, LANES), F32)
        for hp in range(n_pairs):
            for sub in range(2):
                rows = slice(sub * QBLOCK, (sub + 1) * QBLOCK)
                m = jnp.max(s_ref[hp, rows, k0:], axis=-1, keepdims=True)
                p_ref[hp, rows, k0:] = jnp.exp(s_ref[hp, rows, k0:] - m).astype(BF16)
                m_all = jnp.where(lane == 2 * hp + sub, m, m_all)

        den_all = jnp.ones((QBLOCK, LANES), F32)
        for hp in range(n_pairs):
            cs = slice(hp * LANES, (hp + 1) * LANES)
            vals = vc_ref[:, cs]
            if with_prev:
                vals = jnp.concatenate([vp_ref[:, cs], vals], axis=0)
            r = jnp.dot(p_ref[hp, :, k0:], jnp.concatenate([vals, ones], axis=1),
                        preferred_element_type=F32)
            den0, den1 = r[:QBLOCK, LANES:], r[QBLOCK:, LANES:]
            o_ref[:, cs] = jnp.where(low, r[:QBLOCK, :LANES] / den0, r[QBLOCK:, :LANES] / den1)
            den_all = jnp.where(lane == 2 * hp, den0, den_all)
            den_all = jnp.where(lane == 2 * hp + 1, den1, den_all)
        lse_ref[...] = m_all + jnp.log(den_all)

    @pl.when(i == 0)
    def _():
        run(False)

    @pl.when(i > 0)
    def _():
        run(True)


def _attn_branch(q, kb, vb, bias_all, branch):
    batch, dil, sub, _ = q.shape
    cur = lambda b, r, i: (b, r, i, 0)
    prev = lambda b, r, i: (b, r, jnp.maximum(i - 1, 0), 0)
    wide_c = pl.BlockSpec((None, None, QBLOCK, ATTN_WIDTH), cur)
    wide_p = pl.BlockSpec((None, None, QBLOCK, ATTN_WIDTH), prev)
    pairs = N_HEADS // 2
    bias_spec = pl.BlockSpec((None, pairs, 2 * QBLOCK, 2 * QBLOCK), lambda b, r, i: (branch, 0, 0, 0))
    return pl.pallas_call(
        _attn_kernel,
        grid=(batch, dil, sub // QBLOCK),
        in_specs=[wide_c, wide_p, wide_c, wide_p, wide_c, bias_spec],
        out_specs=[wide_c, pl.BlockSpec((None, None, QBLOCK, LANES), cur)],
        out_shape=[jax.ShapeDtypeStruct((batch, dil, sub, ATTN_WIDTH), F32),
                   jax.ShapeDtypeStruct((batch, dil, sub, LANES), F32)],
        scratch_shapes=[pltpu.VMEM((pairs, 2 * QBLOCK, 2 * QBLOCK), F32),
                        pltpu.VMEM((pairs, 2 * QBLOCK, 2 * QBLOCK), BF16)],
        compiler_params=_params("parallel", "parallel", "parallel"),
        name=f"attn_d{dil}",
    )(q, kb, kb, vb, vb, bias_all)


def _bias_kernel(base_ref, out_ref):
    width = base_ref.shape[1]
    for h in range(N_HEADS):
        rows = jnp.broadcast_to(base_ref[h:h + 1, :], (QBLOCK, width))
        band = pltpu.roll(rows, 0, 1, stride=1, stride_axis=0)
        out_ref[h // 2, (h % 2) * QBLOCK:(h % 2 + 1) * QBLOCK, :] = band[:, :2 * QBLOCK]


def _prompt_bias(rel_bias):
    width = 3 * QBLOCK
    steps = QBLOCK - np.arange(width)
    valid = (steps >= 0) & (steps <= N_STEPS)
    buckets = np.stack([_bucket_table(dil)[np.clip(steps, 0, N_STEPS)] for _, dil in DILATED])
    base = jnp.where(valid[None, :, None], rel_bias.astype(F32)[buckets], NEG_INF)
    base = jnp.transpose(base, (0, 2, 1))
    pairs = N_HEADS // 2
    return pl.pallas_call(
        _bias_kernel,
        grid=(len(DILATED),),
        in_specs=[pl.BlockSpec((None, N_HEADS, width), lambda g: (g, 0, 0))],
        out_specs=pl.BlockSpec((None, pairs, 2 * QBLOCK, 2 * QBLOCK), lambda g: (g, 0, 0, 0)),
        out_shape=jax.ShapeDtypeStruct((len(DILATED), pairs, 2 * QBLOCK, 2 * QBLOCK), F32),
        compiler_params=_params("parallel"),
        name="bias_table",
    )(base)


def _split_dot(a, b_bf16):
    hi = a.astype(BF16)
    lo = (a - hi.astype(F32)).astype(BF16)
    return (jnp.dot(hi, b_bf16, preferred_element_type=F32)
            + jnp.dot(lo, b_bf16, preferred_element_type=F32))


def _pool_groups(comb, u, cnt_fn, wp_ref, ps_ref):
    t = u.shape[0]
    outs = []
    run = comb
    width = 1
    for g, w in enumerate(POOL_WINDOWS):
        while width < w:
            run = run + pltpu.roll(run, width, 0)
            width *= 2
        cs = slice(g * POOL_GROUP_WIDTH, (g + 1) * POOL_GROUP_WIDTH)
        d = run[POOL_HALO:POOL_HALO + t, cs] / cnt_fn(w) - u[:, cs]
        y = jnp.dot(d.astype(BF16), wp_ref[g], preferred_element_type=F32)
        outs.append(y * ps_ref[:, cs])
    return outs


def _mix_kernel(o1_ref, o2_ref, o3_ref, l1_ref, l2_ref, l3_ref, u_ref, halo_ref,
                ex_ref, wp_ref, ps_ref, mix_ref, il_ref, ls_ref, *, seq):
    tm = u_ref.shape[0]
    n_slabs = ATTN_WIDTH // LANES
    lses = [l1_ref[0]]
    for bi, (o_ref, l_ref) in enumerate(((o2_ref, l2_ref), (o3_ref, l3_ref))):
        dil = o_ref.shape[0]
        n = tm // dil
        for r in range(dil):
            ls_ref[bi, pl.ds(r, n, stride=dil), :] = l_ref[r]
            for s in range(n_slabs):
                il_ref[bi, s, pl.ds(r, n, stride=dil), :] = o_ref[r, :, s * LANES:(s + 1) * LANES]
        lses.append(ls_ref[bi])
    l1, l2, l3 = lses
    m = jnp.maximum(jnp.maximum(l1, l2), l3)
    e1, e2, e3 = jnp.exp(l1 - m), jnp.exp(l2 - m), jnp.exp(l3 - m)
    inv = 1.0 / (e1 + e2 + e3)
    ex = ex_ref[...]
    w1, w2, w3 = (_split_dot(e * inv, ex) for e in (e1, e2, e3))
    for s in range(n_slabs):
        cs = slice(s * LANES, (s + 1) * LANES)
        attn = w1[:, cs] * o1_ref[0, :, cs] + w2[:, cs] * il_ref[0, s] + w3[:, cs] * il_ref[1, s]
        mix_ref[:, cs] = attn.astype(BF16)

    pos0 = (pl.program_id(0) * tm) % seq
    u = u_ref[...]
    halo = jnp.where(pos0 == 0, 0.0, halo_ref[...])
    comb = jnp.concatenate([halo, u], axis=0)
    pos = pos0 + lax.broadcasted_iota(jnp.int32, (tm, 1), 0)
    cnt_fn = lambda w: jnp.minimum(pos + 1, w).astype(F32)
    for g, y in enumerate(_pool_groups(comb, u, cnt_fn, wp_ref, ps_ref)):
        lo = ATTN_WIDTH + g * POOL_GROUP_WIDTH
        mix_ref[:, lo:lo + POOL_GROUP_WIDTH] = y.astype(BF16)


def _mix(o_list, l_list, u, expand, wp_b, pool_scale, seq, tm):
    n = u.shape[0]
    tiles = seq // tm
    row = lambda i: (i, 0)
    res = lambda i: (i // tiles, 0, i % tiles, 0)
    o_specs = [pl.BlockSpec((None, o.shape[1], tm // o.shape[1], ATTN_WIDTH), res) for o in o_list]
    l_specs = [pl.BlockSpec((None, l.shape[1], tm // l.shape[1], LANES), res) for l in l_list]
    halo = pl.BlockSpec((POOL_HALO, ATTN_WIDTH),
                        lambda i: (jnp.maximum(i * (tm // POOL_HALO) - 1, 0), 0))
    n_dilated = len(o_list) - 1
    return pl.pallas_call(
        functools.partial(_mix_kernel, seq=seq),
        grid=(n // tm,),
        in_specs=o_specs + l_specs + [pl.BlockSpec((tm, ATTN_WIDTH), row), halo,
                                      _const_spec(expand.shape), _const_spec(wp_b.shape),
                                      _const_spec((1, ATTN_WIDTH))],
        out_specs=pl.BlockSpec((tm, 2 * ATTN_WIDTH), row),
        out_shape=jax.ShapeDtypeStruct((n, 2 * ATTN_WIDTH), BF16),
        scratch_shapes=[pltpu.VMEM((n_dilated, ATTN_WIDTH // LANES, tm, LANES), F32),
                        pltpu.VMEM((n_dilated, tm, LANES), F32)],
        compiler_params=_params("parallel"),
        name="mix",
    )(*o_list, *l_list, u, u, expand, wp_b, pool_scale)


def _sample_attn_kernel(q_ref, kn_ref, vn_ref, k_ref, v_ref, bc_ref, bn_ref, o_ref):
    n_heads, t_new, _ = q_ref.shape
    n_br = bc_ref.shape[0]
    lane_max = lambda a: jnp.max(a, axis=1, keepdims=True)
    lane_sum = lambda a: jnp.sum(a, axis=1, keepdims=True)
    new_lane = lax.broadcasted_iota(jnp.int32, (t_new, t_new), 1)
    nt = (((1,), (1,)), ((), ()))
    for h in range(n_heads):
        q, kn, vn = q_ref[h], kn_ref[h], vn_ref[h]
        s_c = jnp.dot(q.astype(BF16), k_ref[h].astype(BF16), preferred_element_type=F32)
        s_n = jnp.zeros((t_new, t_new), F32)
        for tp in range(t_new):
            s_n = jnp.where(new_lane == tp, lane_sum(q * kn[tp:tp + 1, :]), s_n)
        sc = [s_c + bc_ref[br, h] for br in range(n_br)]
        sn = [s_n + bn_ref[br, h] for br in range(n_br)]
        m = functools.reduce(jnp.maximum, [lane_max(a) for a in sc + sn])
        p_c = functools.reduce(jnp.add, [jnp.exp(a - m) for a in sc])
        p_n = functools.reduce(jnp.add, [jnp.exp(a - m) for a in sn])
        acc = lax.dot_general(p_c.astype(BF16), v_ref[h].astype(BF16), nt,
                              preferred_element_type=F32)
        for tp in range(t_new):
            acc = acc + p_n[:, tp:tp + 1] * vn[tp:tp + 1, :]
        o_ref[h] = acc / (lane_sum(p_c) + lane_sum(p_n))


def _sample_attn(q, kn, vn, k_t, v_t, bias_c, bias_n, heads_per_step):
    nb, nh, t_new, hd = q.shape
    win = k_t.shape[-1]
    hg = heads_per_step
    n_br = bias_c.shape[0]
    small = pl.BlockSpec((None, hg, t_new, hd), lambda b, g: (b, g, 0, 0))
    wide = pl.BlockSpec((None, hg, hd, win), lambda b, g: (b, g, 0, 0))
    return pl.pallas_call(
        _sample_attn_kernel,
        grid=(nb, nh // hg),
        in_specs=[small, small, small, wide, wide,
                  pl.BlockSpec((n_br, hg, t_new, win), lambda b, g: (0, g, 0, 0)),
                  pl.BlockSpec((n_br, hg, t_new, t_new), lambda b, g: (0, g, 0, 0))],
        out_specs=small,
        out_shape=jax.ShapeDtypeStruct((nb, nh, t_new, hd), F32),
        compiler_params=_params("parallel", "parallel"),
        name="sample_attn",
    )(q, kn, vn, k_t, v_t, bias_c, bias_n)


def _sample_pool_kernel(u_ref, st_ref, wp_ref, ps_ref, pool_ref, sbuf_ref, comb_ref, *, start):
    t_new = u_ref.shape[0]
    u = u_ref[...]
    comb_ref[...] = jnp.zeros_like(comb_ref)
    comb_ref[1:POOL_HALO, :] = st_ref[...]
    comb_ref[POOL_HALO:POOL_HALO + t_new, :] = u
    pos = start + lax.broadcasted_iota(jnp.int32, (t_new, 1), 0)
    cnt_fn = lambda w: jnp.minimum(pos + 1, w).astype(F32)
    for g, y in enumerate(_pool_groups(comb_ref[...], u, cnt_fn, wp_ref, ps_ref)):
        pool_ref[:, g * POOL_GROUP_WIDTH:(g + 1) * POOL_GROUP_WIDTH] = y
    sbuf_ref[:POOL_BUF - t_new, :] = st_ref[t_new:, :]
    sbuf_ref[POOL_BUF - t_new:, :] = u


def _sample_pool(u, state, wp_b, pool_scale):
    nb, t_new, w = u.shape
    flat = lambda rows: pl.BlockSpec((None, rows, w), lambda b: (b, 0, 0))
    return pl.pallas_call(
        functools.partial(_sample_pool_kernel, start=PAST_LEN),
        grid=(nb,),
        in_specs=[flat(t_new), flat(POOL_BUF), _const_spec(wp_b.shape), _const_spec((1, w))],
        out_specs=[flat(t_new), flat(POOL_BUF)],
        out_shape=[jax.ShapeDtypeStruct((nb, t_new, w), F32),
                   jax.ShapeDtypeStruct((nb, POOL_BUF, w), F32)],
        scratch_shapes=[pltpu.VMEM((POOL_HALO + 8, w), F32)],
        compiler_params=_params("parallel"),
        name="sample_pool",
    )(u, state, wp_b, pool_scale)


def _outproj_kernel(x_ref, mix_ref, wo_ref, ln_ref, wr_ref, br_ref, tri_ref, cin_ref,
                    x1_ref, h_ref, gate_ref, route_ref, cout_ref, count_ref):
    @pl.when(pl.program_id(0) == 0)
    def _():
        count_ref[...] = cin_ref[...]

    x1 = x_ref[...] + jnp.dot(mix_ref[...].astype(BF16), wo_ref[...],
                              preferred_element_type=F32)
    x1_ref[...] = x1
    ms = jnp.mean(x1 * x1, axis=-1, keepdims=True)
    h = x1 * lax.rsqrt(ms + EPS) * ln_ref[...]
    h_ref[...] = h
    lg = jnp.dot(h.astype(BF16), wr_ref[...], preferred_element_type=F32) + br_ref[...]

    lane = lax.broadcasted_iota(jnp.int32, lg.shape, 1).astype(F32)
    big = float(LANES)
    row_max = lambda mask: jnp.max(jnp.where(mask, lg, -jnp.inf), axis=-1, keepdims=True)
    first = lambda mask: jnp.min(jnp.where(mask, lane, big), axis=-1, keepdims=True)
    is_g = lane < N_GROUPS
    mg = row_max(is_g)
    g_top = first(jnp.logical_and(is_g, lg == mg))
    den = jnp.sum(jnp.where(is_g, jnp.exp(lg - mg), 0.0), axis=-1, keepdims=True)
    p_top = 1.0 / den
    base = N_GROUPS + EXPERTS_PER_GROUP * g_top
    in_grp = jnp.logical_and(lane >= base, lane < base + EXPERTS_PER_GROUP)
    v1 = row_max(in_grp)
    i1 = first(jnp.logical_and(in_grp, lg == v1))
    rest = jnp.logical_and(in_grp, lane != i1)
    v2 = row_max(rest)
    i2 = first(jnp.logical_and(rest, lg == v2))
    e21 = jnp.exp(v2 - v1)
    s21 = 1.0 + e21
    gate1 = p_top * (1.0 / s21)
    gate2 = p_top * (e21 / s21)
    gate_ref[...] = jnp.where(lane == 0.0, gate1, jnp.where(lane == 1.0, gate2, 0.0))
    e1, e2 = i1 - N_GROUPS, i2 - N_GROUPS

    hot1 = (lane == e1).astype(F32)
    hot2 = (lane == e2).astype(F32)
    hot = hot1 + hot2
    before = count_ref[...] + jnp.dot(tri_ref[...], hot.astype(BF16), preferred_element_type=F32)
    rank1 = jnp.sum(hot1 * before, axis=-1, keepdims=True)
    rank2 = jnp.sum(hot2 * before, axis=-1, keepdims=True)
    count_ref[...] = count_ref[...] + jnp.sum(hot, axis=0, keepdims=True)
    cout_ref[...] = count_ref[...]

    cols = jnp.zeros_like(lg)
    for r, col in enumerate((e1, e2, rank1, rank2)):
        cols = jnp.where(lane == float(r), col, cols)
    route_ref[...] = cols.T[:route_ref.shape[0], :]


def _outproj(x2d, mix, w_o_b, ln2, w_r, b_r, tri, counts_in, tm):
    n, d = x2d.shape
    row = lambda i: (i, 0)
    full = pl.BlockSpec((tm, d), row)
    stat = pl.BlockSpec((tm, LANES), row)
    route_rows = 8
    return pl.pallas_call(
        _outproj_kernel,
        grid=(n // tm,),
        in_specs=[full, full,
                  pl.BlockSpec(w_o_b.shape, lambda i: (0, 0), pipeline_mode=pl.Buffered(1)),
                  _const_spec((1, d)), _const_spec(w_r.shape),
                  _const_spec((1, LANES)), _const_spec((tm, tm)), _const_spec((1, LANES))],
        out_specs=[full, full, stat,
                   pl.BlockSpec((route_rows, tm), lambda i: (0, i)), _const_spec((1, LANES))],
        out_shape=[jax.ShapeDtypeStruct((n, d), F32),
                   jax.ShapeDtypeStruct((n, d), F32),
                   jax.ShapeDtypeStruct((n, LANES), F32),
                   jax.ShapeDtypeStruct((route_rows, n), F32),
                   jax.ShapeDtypeStruct((1, LANES), F32)],
        scratch_shapes=[pltpu.VMEM((1, LANES), F32)],
        compiler_params=_params("arbitrary"),
        name="outproj",
    )(x2d, mix, w_o_b, ln2, w_r, b_r, tri[:tm, :tm], counts_in)


def _row_copy(src_hbm, dst_vmem, sem, src_row, dst_row):
    return pltpu.make_async_copy(src_hbm.at[pl.ds(src_row, 1)],
                                 dst_vmem.at[pl.ds(dst_row, 1)], sem)


def _dispatch_kernel(pos_ref, pos_s_ref, pad_start_ref, pad_len_ref, tail_ref, h_ref, h_s_ref,
                     hs_ref, sem_ref, zero_ref, *, tile_rows):
    i = pl.program_id(0)
    sem = sem_ref.at[0]

    def scatter(src_ref, dst_rows_ref):
        n_rows = src_ref.shape[0]

        def put(r, k):
            return _row_copy(src_ref, hs_ref, sem, r, dst_rows_ref[0, k * n_rows + r])

        def issue(r, c):
            for k in range(TOP_K):
                put(r, k).start()
            return c

        def drain(r, c):
            for k in range(TOP_K):
                put(r, k).wait()
            return c

        lax.fori_loop(0, n_rows, issue, 0, unroll=DMA_UNROLL)
        lax.fori_loop(0, n_rows, drain, 0, unroll=DMA_UNROLL)

    scatter(h_ref, pos_ref)

    @pl.when(i == pl.num_programs(0) - 1)
    def _():
        scatter(h_s_ref, pos_s_ref)
        zero_ref[...] = jnp.zeros_like(zero_ref)

        sizes = [s for s in (tile_rows >> (b + 1) for b in range(tile_rows.bit_length()))
                 if s >= SUBLANES]

        def pad_copies(act):
            def single_rows(first, count):
                def body(r, c):
                    act(_row_copy(zero_ref, hs_ref, sem, 0, first + r))
                    return c
                lax.fori_loop(0, count, body, 0)

            for e in range(N_EXPERTS):
                start, length = pad_start_ref[e], pad_len_ref[e]
                head = jnp.minimum((-start) & (SUBLANES - 1), length)
                single_rows(start, head)
                body_len = length - head
                aligned = body_len & -SUBLANES
                offset = start + head
                for b, size in enumerate(sizes):
                    @pl.when((aligned & size) != 0)
                    def _():
                        rows = pl.ds(pl.multiple_of(offset, SUBLANES), size)
                        act(pltpu.make_async_copy(zero_ref.at[pl.ds(0, size)], hs_ref.at[rows],
                                                  sem_ref.at[1 + b]))
                    offset = offset + (aligned & size)
                single_rows(offset, body_len - aligned)

        pad_copies(lambda copy: copy.start())
        pad_copies(lambda copy: copy.wait())

        n_tiles = hs_ref.shape[0] // tile_rows

        def fill_tile(t):
            rows = pl.ds(pl.multiple_of(t * tile_rows, tile_rows), tile_rows)
            return pltpu.make_async_copy(zero_ref, hs_ref.at[rows], sem)

        def start_tile(t, c):
            fill_tile(t).start()
            return c

        def wait_tile(t, c):
            fill_tile(t).wait()
            return c

        lax.fori_loop(tail_ref[0], n_tiles, start_tile, 0)
        lax.fori_loop(tail_ref[0], n_tiles, wait_tile, 0)


def _tile_rows_spec(tm):
    return pl.BlockSpec((None, 1, TOP_K * tm), lambda i: (i, 0, 0), memory_space=pltpu.SMEM)


def _dispatch(rows_p, rows_s, pad_start, pad_len, tail, h, h_s, n_tiles, tile_rows, tm):
    n, d = h.shape
    smem = pl.BlockSpec(memory_space=pltpu.SMEM)
    return pl.pallas_call(
        functools.partial(_dispatch_kernel, tile_rows=tile_rows),
        grid=(n // tm,),
        in_specs=[_tile_rows_spec(tm), smem, smem, smem, smem,
                  pl.BlockSpec((tm, d), lambda i: (i, 0)),
                  pl.BlockSpec(h_s.shape, lambda i: (0, 0))],
        out_specs=pl.BlockSpec(memory_space=pl.ANY),
        out_shape=jax.ShapeDtypeStruct((n_tiles * tile_rows, d), h.dtype),
        scratch_shapes=[pltpu.SemaphoreType.DMA((tile_rows.bit_length(),)),
                        pltpu.VMEM((tile_rows, d), h.dtype)],
        compiler_params=_params("arbitrary", unchecked=True),
        name="dispatch",
    )(rows_p, rows_s, pad_start, pad_len, tail, h, h_s)


def _expert_kernel(tile_e_ref, tile_on_ref, hs_ref, wg_ref, wu_ref, wd_ref, y_ref,
                   wg_b, wu_b, wd_b):
    i = pl.program_id(0)

    @pl.when(jnp.logical_or(i == 0, tile_e_ref[i] != tile_e_ref[jnp.maximum(i - 1, 0)]))
    def _():
        wg_b[...] = wg_ref[...].astype(BF16)
        wu_b[...] = wu_ref[...].astype(BF16)
        wd_b[...] = wd_ref[...].astype(BF16)

    @pl.when(tile_on_ref[i] == 1)
    def _():
        hb = hs_ref[...].astype(BF16)
        a = jnp.dot(hb, wg_b[...], preferred_element_type=F32)
        b = jnp.dot(hb, wu_b[...], preferred_element_type=F32)
        hid = a * jax.nn.sigmoid(a) * b
        y_ref[...] = jnp.dot(hid.astype(BF16), wd_b[...], preferred_element_type=F32)

    @pl.when(tile_on_ref[i] == 0)
    def _():
        y_ref[...] = jnp.zeros_like(y_ref)


def _experts(tile_e, tile_on, hs, w_gate, w_up, w_down, tm):
    n_tiles = tile_e.shape[0]
    d, f = w_gate.shape[1:]
    by_expert = lambda i, te, on: (te[i], 0, 0)
    grid_spec = pltpu.PrefetchScalarGridSpec(
        num_scalar_prefetch=2,
        grid=(n_tiles,),
        in_specs=[pl.BlockSpec((tm, d), lambda i, te, on: (i, 0)),
                  pl.BlockSpec((None, d, f), by_expert),
                  pl.BlockSpec((None, d, f), by_expert),
                  pl.BlockSpec((None, f, d), by_expert)],
        out_specs=pl.BlockSpec((tm, d), lambda i, te, on: (i, 0)),
        scratch_shapes=[pltpu.VMEM((d, f), BF16), pltpu.VMEM((d, f), BF16),
                        pltpu.VMEM((f, d), BF16)],
    )
    return pl.pallas_call(
        _expert_kernel,
        grid_spec=grid_spec,
        out_shape=jax.ShapeDtypeStruct((n_tiles * tm, d), F32),
        compiler_params=_params("arbitrary"),
        name="experts",
    )(tile_e, tile_on, hs, w_gate, w_up, w_down)


def _combine_kernel(rows_ref, next_rows_ref, x1_ref, gate_ref, ys_hbm, y_ref, buf_ref, sem_ref):
    i = pl.program_id(0)
    n_tiles = pl.num_programs(0)
    tm = y_ref.shape[0]

    def issue(src_rows_ref, slot):
        def body(r, c):
            for k in range(TOP_K):
                _row_copy(ys_hbm, buf_ref.at[slot, k], sem_ref.at[slot],
                          src_rows_ref[0, k * tm + r], r).start()
            return c
        lax.fori_loop(0, tm, body, 0, unroll=DMA_UNROLL)

    def drain(slot):
        def body(r, c):
            for k in range(TOP_K):
                _row_copy(ys_hbm, buf_ref.at[slot, k], sem_ref.at[slot], 0, r).wait()
            return c
        lax.fori_loop(0, tm, body, 0, unroll=DMA_UNROLL)

    slot = i % 2

    @pl.when(i == 0)
    def _():
        issue(rows_ref, 0)

    drain(slot)

    @pl.when(i + 1 < n_tiles)
    def _():
        issue(next_rows_ref, 1 - slot)

    gate = gate_ref[...]
    y_ref[...] = (x1_ref[...] + gate[:, 0:1] * buf_ref[slot, 0]
                  + gate[:, 1:2] * buf_ref[slot, 1])


def _combine(rows, x1, gate, ys, tm):
    n, d = x1.shape
    last = n // tm - 1
    next_spec = pl.BlockSpec((None, 1, TOP_K * tm), lambda i: (jnp.minimum(i + 1, last), 0, 0),
                             memory_space=pltpu.SMEM)
    return pl.pallas_call(
        _combine_kernel,
        grid=(n // tm,),
        in_specs=[_tile_rows_spec(tm), next_spec,
                  pl.BlockSpec((tm, d), lambda i: (i, 0)),
                  pl.BlockSpec((tm, LANES), lambda i: (i, 0)),
                  pl.BlockSpec(memory_space=pl.ANY)],
        out_specs=pl.BlockSpec((tm, d), lambda i: (i, 0)),
        out_shape=jax.ShapeDtypeStruct((n, d), F32),
        scratch_shapes=[pltpu.VMEM((2, TOP_K, tm, d), F32), pltpu.SemaphoreType.DMA((2,))],
        compiler_params=_params("arbitrary", unchecked=True),
        name="combine",
    )(rows, rows, x1, gate, ys)


def _bucket_table(dilation):
    dist = np.arange(N_STEPS + 1, dtype=np.int64) * dilation
    max_exact = NUM_BUCKETS // 2
    df = np.maximum(dist, 1).astype(np.float32)
    large = max_exact + (np.log(df / np.float32(max_exact))
                         / np.float32(math.log(MAX_DISTANCE / max_exact))
                         * np.float32(NUM_BUCKETS - max_exact)).astype(np.int32)
    large = np.minimum(large, NUM_BUCKETS - 1)
    return np.where(dist < max_exact, dist, large).astype(np.int32)


def _sample_bias(rel_bias, t_new, win):
    buckets = np.stack([_bucket_table(dil) for _, dil in DILATED])
    by_step = jnp.transpose(rel_bias.astype(F32)[buckets], (0, 2, 1))
    nh = by_step.shape[1]
    neg = lambda *shape: jnp.full(shape, NEG_INF, F32)
    t = np.arange(t_new)
    cached, fresh = [], []
    for br, (_, dil) in enumerate(DILATED):
        rev = by_step[br, :, ::-1][:, :N_STEPS]
        if dil == 1:
            rows = [jnp.concatenate([neg(nh, win - N_STEPS + q), rev[:, :N_STEPS - q]], axis=1)
                    for q in range(t_new)]
            cached.append(jnp.stack(rows, axis=1))
        else:
            own = (t[:, None, None] == np.arange(dil)[None, None, :])
            band = jnp.where(own[None], rev[:, None, :, None], NEG_INF)
            band = band.reshape(nh, t_new, dil * N_STEPS)
            cached.append(jnp.concatenate([neg(nh, t_new, win - dil * N_STEPS), band], axis=2))
        step = t[:, None] - t[None, :]
        ok = (step >= 0) & (step % dil == 0)
        vals = by_step[br][:, np.where(ok, step // dil, 0)]
        fresh.append(jnp.where(ok[None], vals, NEG_INF))
    return jnp.stack(cached), jnp.stack(fresh)


def _tile_plan(counts, n_tiles, tm):
    tiles_e = (counts + tm - 1) // tm
    ends = jnp.cumsum(tiles_e)
    first_row = (ends - tiles_e) * tm
    tile = jnp.arange(n_tiles, dtype=jnp.int32)
    tile_e = jnp.sum((ends[None, :] <= tile[:, None]).astype(jnp.int32), axis=1)
    tile_on = (tile_e < N_EXPERTS).astype(jnp.int32)
    tile_e = jnp.minimum(tile_e, N_EXPERTS - 1)
    i32 = lambda a: a.astype(jnp.int32)
    return (i32(tile_e), tile_on, i32(first_row), i32(first_row + counts),
            i32(tiles_e * tm - counts), i32(ends[-1:]))


TM_PROJ = 256
TM_TOKEN = 512
TM_EXPERT = 512
SAMPLE_HEADS_PER_STEP = 8


def kernel(x_prompt, x_sample, cache_k, cache_v, state_pool, rel_bias, ln1_w, w_in,
           q_norm_w, k_norm_w, w_pool, pool_scale, w_o, ln2_w, w_router_group,
           b_router_group, w_router_expert, b_router_expert, w_gate, w_up, w_down):
    depth = w_in.shape[0]
    assert depth == 1
    batch, seq, d_model = x_prompt.shape
    nb, t_new, _ = x_sample.shape
    win = cache_k.shape[2]
    f_exp = w_gate.shape[-1]

    w_in_b = _to_bf16(w_in[0], 256)
    w_o_b = _to_bf16(w_o[0], 256)
    wp_b = _to_bf16(w_pool[0].reshape(-1, POOL_GROUP_WIDTH), 256).reshape(w_pool.shape[1:])
    ln1 = ln1_w[0][None, :]
    ln2 = ln2_w[0][None, :]
    q_gain = jnp.tile(q_norm_w[0], N_HEADS)[None, :] * SCALE
    k_gain = jnp.tile(k_norm_w[0], N_HEADS)[None, :]
    ps = pool_scale[0][None, :]
    blk = np.arange(MXU_DIM) // HEAD_DIM
    avg = jnp.asarray((blk[:, None] == blk[None, :]) / HEAD_DIM, BF16)
    head_of_col = np.arange(ATTN_WIDTH) // HEAD_DIM
    expand_np = (np.arange(LANES)[:, None] == head_of_col[None, :])
    expand = jnp.asarray(expand_np, BF16)
    w_r = jnp.concatenate([w_router_group[0], w_router_expert[0]], axis=1)
    w_r = jnp.pad(w_r, ((0, 0), (0, LANES - w_r.shape[1]))).astype(BF16)
    b_r = jnp.pad(jnp.concatenate([b_router_group[0], b_router_expert[0]]),
                  (0, LANES - N_GROUPS - N_EXPERTS))[None, :]

    xp = x_prompt.reshape(batch * seq, d_model)
    proj = _inproj(xp, ln1, w_in_b, q_gain, k_gain, avg, TM_PROJ, batch, seq)
    k, v, u = proj[:3]
    natural = tuple(a.reshape(batch, 1, seq, ATTN_WIDTH) for a in proj[3:6])
    qkv = [natural] + [tuple(proj[6 + 3 * bi:9 + 3 * bi]) for bi in range(len(DILATED) - 1)]
    bias_all = _prompt_bias(rel_bias)
    o_list, l_list = [], []
    for branch, (qd, kd, vd) in enumerate(qkv):
        o, lse = _attn_branch(qd, kd, vd, bias_all, branch)
        o_list.append(o)
        l_list.append(lse)
    mix_p = _mix(o_list, l_list, u, expand, wp_b, ps, seq, TM_TOKEN)
    tri = jnp.asarray(np.tril(np.ones((TM_TOKEN, TM_TOKEN)), -1), BF16)
    x1_p, hp_p, gate_p, route_p, counts_p = _outproj(
        xp, mix_p, w_o_b, ln2, w_r, b_r, tri, jnp.zeros((1, LANES), F32), TM_TOKEN)

    n_s = nb * t_new
    xs = x_sample.reshape(n_s, d_model)
    k_s, v_s, u_s, q_s, _, _ = _inproj(xs, ln1, w_in_b, q_gain, k_gain, avg, n_s)
    heads_s = (nb, t_new, N_HEADS, HEAD_DIM)
    k_s5, v_s5 = k_s.reshape(heads_s), v_s.reshape(heads_s)
    to_lanes = lambda a: jnp.transpose(a, (0, 2, 3, 1))
    by_head = lambda a: jnp.transpose(a, (0, 2, 1, 3))
    bias_c, bias_n = _sample_bias(rel_bias, t_new, win)
    attn_h = _sample_attn(by_head(q_s.astype(F32).reshape(heads_s)), by_head(k_s5), by_head(v_s5),
                          to_lanes(cache_k[0]), to_lanes(cache_v[0]),
                          bias_c, bias_n, SAMPLE_HEADS_PER_STEP)
    attn_s = by_head(attn_h).reshape(n_s, ATTN_WIDTH)
    pool_s, sbuf = _sample_pool(u_s.reshape(nb, t_new, ATTN_WIDTH), state_pool[0], wp_b, ps)
    mix_s = jnp.concatenate([attn_s, pool_s.reshape(n_s, ATTN_WIDTH)], axis=1)
    x1_s, hp_s, gate_s, route_s, counts = _outproj(
        xs, mix_s, w_o_b, ln2, w_r, b_r, tri, counts_p, n_s)

    n_p = batch * seq
    n_pairs = (n_p + n_s) * TOP_K
    tm_e = TM_EXPERT
    n_tiles = -(-n_pairs // tm_e) + N_EXPERTS
    counts_i = counts[0, :N_EXPERTS].astype(jnp.int32)
    tile_e, tile_on, first_row, pad_start, pad_len, tail = _tile_plan(counts_i, n_tiles, tm_e)

    route = jnp.concatenate([route_p, route_s], axis=1)
    ids = route[:TOP_K].astype(jnp.int32)
    base = jnp.zeros_like(ids)
    for e in range(N_EXPERTS):
        base = jnp.where(ids == e, first_row[e], base)
    rows = base + route[TOP_K:2 * TOP_K].astype(jnp.int32)

    def by_tile(rows, tm):
        return jnp.transpose(rows.reshape(TOP_K, -1, tm), (1, 0, 2)).reshape(-1, 1, TOP_K * tm)

    rows_p, rows_s = by_tile(rows[:, :n_p], TM_TOKEN), by_tile(rows[:, n_p:], n_s)
    hs = _dispatch(rows_p, rows_s[0], pad_start, pad_len, tail, hp_p, hp_s, n_tiles, tm_e, TM_TOKEN)
    ys = _experts(tile_e, tile_on, hs, w_gate[0], w_up[0], w_down[0], tm_e)
    y_p = _combine(rows_p, x1_p, gate_p, ys, TM_TOKEN)
    y_s = _combine(rows_s, x1_s, gate_s, ys, n_s)

    keep = min(MAX_DISTANCE, seq)
    from_lanes = lambda a: jnp.transpose(a, (0, 3, 1, 2))[None, :, -keep:]
    return (y_p.reshape(batch, seq, d_model),
            y_s.reshape(nb, t_new, d_model),
            from_lanes(k), from_lanes(v),
            u.reshape(batch, seq, ATTN_WIDTH)[None, :, -POOL_BUF:],
            k_s5[None], v_s5[None], sbuf[None])
```

```python
import functools
import math

import numpy as np
import jax
import jax.numpy as jnp
from jax import lax
from jax.experimental import pallas as pl
from jax.experimental.pallas import tpu as pltpu

F32 = jnp.float32
BF16 = jnp.bfloat16

N_HEADS = 16
HEAD_DIM = 64
ATTN_WIDTH = N_HEADS * HEAD_DIM
POOL_WINDOWS = (2, 4, 8, 16)
POOL_GROUP_WIDTH = 256
POOL_BUF = max(POOL_WINDOWS) - 1
POOL_HALO = POOL_BUF + 1
DILATED = ((128, 1), (512, 4), (2048, 16))
N_STEPS = 128
QBLOCK = 128
NUM_BUCKETS = 32
MAX_DISTANCE = 2048
PAST_LEN = 16384
N_GROUPS = 4
EXPERTS_PER_GROUP = 4
N_EXPERTS = N_GROUPS * EXPERTS_PER_GROUP
TOP_K = 2
EPS = 1e-6
SCALE = HEAD_DIM ** -0.5
NEG_INF = -1e30
LANES = 128
SUBLANES = 8
MXU_DIM = 256
VMEM_LIMIT = 56 * 1024 * 1024


DMA_UNROLL = 8


def _params(*sem, unchecked=False):
    return pltpu.CompilerParams(dimension_semantics=sem, vmem_limit_bytes=VMEM_LIMIT,
                                disable_bounds_checks=unchecked)


def _const_spec(shape):
    zeros = (0,) * len(shape)
    return pl.BlockSpec(shape, lambda *_: zeros)


def _cast_kernel(x_ref, o_ref):
    o_ref[...] = x_ref[...].astype(o_ref.dtype)


def _to_bf16(w2d, block_rows):
    rows, cols = w2d.shape
    return pl.pallas_call(
        _cast_kernel,
        grid=(rows // block_rows,),
        in_specs=[pl.BlockSpec((block_rows, cols), lambda i: (i, 0))],
        out_specs=pl.BlockSpec((block_rows, cols), lambda i: (i, 0)),
        out_shape=jax.ShapeDtypeStruct((rows, cols), BF16),
        compiler_params=_params("parallel"),
        name="cast_bf16",
    )(w2d)


def _inproj_kernel(x_ref, ln_ref, w_ref, qg_ref, kg_ref, avg_ref, *refs, dilated):
    k_ref, v_ref, u_ref, q_ref, kb_ref, vb_ref = refs[:6]
    tm = x_ref.shape[0]
    x = x_ref[...]
    ms = jnp.mean(x * x, axis=-1, keepdims=True)
    h = (x * lax.rsqrt(ms + EPS) * ln_ref[...]).astype(BF16)
    n_chunks = ATTN_WIDTH // MXU_DIM

    def section(s):
        return jnp.dot(h, w_ref[:, s * ATTN_WIDTH:(s + 1) * ATTN_WIDTH],
                       preferred_element_type=F32)

    def head_norm(z, g_ref, c):
        zc = z[:, c * MXU_DIM:(c + 1) * MXU_DIM]
        msh = jnp.dot((zc * zc).astype(BF16), avg_ref[...], preferred_element_type=F32)
        return zc * lax.rsqrt(msh + EPS) * g_ref[:, c * MXU_DIM:(c + 1) * MXU_DIM]

    def emit(chunks, which, f32_ref, bf_ref):
        heads_per_chunk = MXU_DIM // HEAD_DIM
        for c, zc in enumerate(chunks):
            cs = slice(c * MXU_DIM, (c + 1) * MXU_DIM)
            bf_ref[:, cs] = zc.astype(BF16)
            if f32_ref is None:
                continue
            if dilated:
                zt = zc.T
                for j in range(heads_per_chunk):
                    f32_ref[c * heads_per_chunk + j] = zt[j * HEAD_DIM:(j + 1) * HEAD_DIM, :]
            else:
                f32_ref[:, cs] = zc
        if not dilated:
            return
        stage_ref = refs[-1]
        for c, zc in enumerate(chunks):
            for half in range(MXU_DIM // LANES):
                stage_ref[2 * c + half] = zc[:, half * LANES:(half + 1) * LANES]
        for bi, (_, dil) in enumerate(DILATED[1:]):
            out_ref = refs[6 + 3 * bi + which]
            n = tm // dil
            for r in range(dil):
                for s in range(ATTN_WIDTH // LANES):
                    rows = stage_ref[s, pl.ds(r, n, stride=dil), :]
                    out_ref[r, :, s * LANES:(s + 1) * LANES] = rows.astype(BF16)

    zq = section(0)
    emit([head_norm(zq, qg_ref, c) for c in range(n_chunks)], 0, None, q_ref)
    zk = section(1)
    emit([head_norm(zk, kg_ref, c) for c in range(n_chunks)], 1, k_ref, kb_ref)
    zv = section(2)
    emit([zv[:, c * MXU_DIM:(c + 1) * MXU_DIM] for c in range(n_chunks)], 2, v_ref, vb_ref)
    u_ref[...] = section(3)


def _inproj(x2d, ln1, w_in_b, q_gain, k_gain, avg, tm, batch=None, seq=None):
    n, d = x2d.shape
    dilated = seq is not None
    row = lambda i: (i, 0)
    wide = pl.BlockSpec((tm, ATTN_WIDTH), row)
    f32o = jax.ShapeDtypeStruct((n, ATTN_WIDTH), F32)
    bf16o = jax.ShapeDtypeStruct((n, ATTN_WIDTH), BF16)
    out_specs = [wide] * 6
    out_shape = [f32o, f32o, f32o, bf16o, bf16o, bf16o]
    scratch = []
    if dilated:
        tiles = seq // tm
        kv_t = pl.BlockSpec((None, N_HEADS, HEAD_DIM, tm), lambda i: (i // tiles, 0, 0, i % tiles))
        out_specs[:2] = [kv_t, kv_t]
        out_shape[:2] = [jax.ShapeDtypeStruct((batch, N_HEADS, HEAD_DIM, seq), F32)] * 2
        for _, dil in DILATED[1:]:
            spec = pl.BlockSpec((None, dil, tm // dil, ATTN_WIDTH),
                                lambda i: (i // tiles, 0, i % tiles, 0))
            out_specs += [spec] * 3
            out_shape += [jax.ShapeDtypeStruct((batch, dil, seq // dil, ATTN_WIDTH), BF16)] * 3
        scratch = [pltpu.VMEM((ATTN_WIDTH // LANES, tm, LANES), F32)]
    return pl.pallas_call(
        functools.partial(_inproj_kernel, dilated=dilated),
        grid=(n // tm,),
        in_specs=[pl.BlockSpec((tm, d), row),
                  _const_spec((1, d)),
                  pl.BlockSpec(w_in_b.shape, lambda i: (0, 0), pipeline_mode=pl.Buffered(1)),
                  _const_spec((1, ATTN_WIDTH)), _const_spec((1, ATTN_WIDTH)),
                  _const_spec((MXU_DIM, MXU_DIM))],
        out_specs=out_specs,
        out_shape=out_shape,
        scratch_shapes=scratch,
        compiler_params=_params("parallel"),
        name="inproj",
    )(x2d, ln1, w_in_b, q_gain, k_gain, avg)


def _attn_kernel(q_ref, kp_ref, kc_ref, vp_ref, vc_ref, bias_ref, o_ref, lse_ref,
                 s_ref, p_ref):
    i = pl.program_id(2)
    lane = lax.broadcasted_iota(jnp.int32, (QBLOCK, LANES), 1)
    low = lane < HEAD_DIM
    keep_low = low.astype(F32).astype(BF16)
    keep_high = (1.0 - low.astype(F32)).astype(BF16)
    nt = (((1,), (1,)), ((), ()))
    n_pairs = N_HEADS // 2

    def run(with_prev):
        k0 = 0 if with_prev else QBLOCK
        nk = 2 * QBLOCK - k0
        ones = jnp.ones((nk, LANES), BF16)
        for hp in range(n_pairs):
            cs = slice(hp * LANES, (hp + 1) * LANES)
            qp = q_ref[:, cs]
            q2 = jnp.concatenate([qp * keep_low, qp * keep_high], axis=0)
            keys = kc_ref[:, cs]
            if with_prev:
                keys = jnp.concatenate([kp_ref[:, cs], keys], axis=0)
            s2 = lax.dot_general(q2, keys, nt, preferred_element_type=F32)
            s_ref[hp, :, k0:] = s2 + bias_ref[hp, :, k0:]

        m_all = jnp.zeros((QBLOCK, LANES), F32)
        for hp in range(n_pairs):
            for sub in range(2):
                rows = slice(sub * QBLOCK, (sub + 1) * QBLOCK)
                m = jnp.max(s_ref[hp, rows, k0:], axis=-1, keepdims=True)
                p_ref[hp, rows, k0:] = jnp.exp(s_ref[hp, rows, k0:] - m).astype(BF16)
                m_all = jnp.where(lane == 2 * hp + sub, m, m_all)

        den_all = jnp.ones((QBLOCK, LANES), F32)
        for hp in range(n_pairs):
            cs = slice(hp * LANES, (hp + 1) * LANES)
            vals = vc_ref[:, cs]
            if with_prev:
                vals = jnp.concatenate([vp_ref[:, cs], vals], axis=0)
            r = jnp.dot(p_ref[hp, :, k0:], jnp.concatenate([vals, ones], axis=1),
                        preferred_element_type=F32)
            den0, den1 = r[:QBLOCK, LANES:], r[QBLOCK:, LANES:]
            o_ref[:, cs] = jnp.where(low, r[:QBLOCK, :LANES] / den0, r[QBLOCK:, :LANES] / den1)
            den_all = jnp.where(lane == 2 * hp, den0, den_all)
            den_all = jnp.where(lane == 2 * hp + 1, den1, den_all)
        lse_ref[...] = m_all + jnp.log(den_all)

    @pl.when(i == 0)
    def _():
        run(False)

    @pl.when(i > 0)
    def _():
        run(True)


def _attn_branch(q, kb, vb, bias_all, branch):
    batch, dil, sub, _ = q.shape
    cur = lambda b, r, i: (b, r, i, 0)
    prev = lambda b, r, i: (b, r, jnp.maximum(i - 1, 0), 0)
    wide_c = pl.BlockSpec((None, None, QBLOCK, ATTN_WIDTH), cur)
    wide_p = pl.BlockSpec((None, None, QBLOCK, ATTN_WIDTH), prev)
    pairs = N_HEADS // 2
    bias_spec = pl.BlockSpec((None, pairs, 2 * QBLOCK, 2 * QBLOCK), lambda b, r, i: (branch, 0, 0, 0))
    return pl.pallas_call(
        _attn_kernel,
        grid=(batch, dil, sub // QBLOCK),
        in_specs=[wide_c, wide_p, wide_c, wide_p, wide_c, bias_spec],
        out_specs=[wide_c, pl.BlockSpec((None, None, QBLOCK, LANES), cur)],
        out_shape=[jax.ShapeDtypeStruct((batch, dil, sub, ATTN_WIDTH), F32),
                   jax.ShapeDtypeStruct((batch, dil, sub, LANES), F32)],
        scratch_shapes=[pltpu.VMEM((pairs, 2 * QBLOCK, 2 * QBLOCK), F32),
                        pltpu.VMEM((pairs, 2 * QBLOCK, 2 * QBLOCK), BF16)],
        compiler_params=_params("parallel", "parallel", "parallel"),
        name=f"attn_d{dil}",
    )(q, kb, kb, vb, vb, bias_all)


def _bias_kernel(base_ref, out_ref):
    width = base_ref.shape[1]
    for h in range(N_HEADS):
        rows = jnp.broadcast_to(base_ref[h:h + 1, :], (QBLOCK, width))
        band = pltpu.roll(rows, 0, 1, stride=1, stride_axis=0)
        out_ref[h // 2, (h % 2) * QBLOCK:(h % 2 + 1) * QBLOCK, :] = band[:, :2 * QBLOCK]


def _prompt_bias(rel_bias):
    width = 3 * QBLOCK
    steps = QBLOCK - np.arange(width)
    valid = (steps >= 0) & (steps <= N_STEPS)
    buckets = np.stack([_bucket_table(dil)[np.clip(steps, 0, N_STEPS)] for _, dil in DILATED])
    base = jnp.where(valid[None, :, None], rel_bias.astype(F32)[buckets], NEG_INF)
    base = jnp.transpose(base, (0, 2, 1))
    pairs = N_HEADS // 2
    return pl.pallas_call(
        _bias_kernel,
        grid=(len(DILATED),),
        in_specs=[pl.BlockSpec((None, N_HEADS, width), lambda g: (g, 0, 0))],
        out_specs=pl.BlockSpec((None, pairs, 2 * QBLOCK, 2 * QBLOCK), lambda g: (g, 0, 0, 0)),
        out_shape=jax.ShapeDtypeStruct((len(DILATED), pairs, 2 * QBLOCK, 2 * QBLOCK), F32),
        compiler_params=_params("parallel"),
        name="bias_table",
    )(base)


def _split_dot(a, b_bf16):
    hi = a.astype(BF16)
    lo = (a - hi.astype(F32)).astype(BF16)
    return (jnp.dot(hi, b_bf16, preferred_element_type=F32)
            + jnp.dot(lo, b_bf16, preferred_element_type=F32))


def _pool_groups(comb, u, cnt_fn, wp_ref, ps_ref):
    t = u.shape[0]
    outs = []
    run = comb
    width = 1
    for g, w in enumerate(POOL_WINDOWS):
        while width < w:
            run = run + pltpu.roll(run, width, 0)
            width *= 2
        cs = slice(g * POOL_GROUP_WIDTH, (g + 1) * POOL_GROUP_WIDTH)
        d = run[POOL_HALO:POOL_HALO + t, cs] / cnt_fn(w) - u[:, cs]
        y = jnp.dot(d.astype(BF16), wp_ref[g], preferred_element_type=F32)
        outs.append(y * ps_ref[:, cs])
    return outs


def _mix_kernel(o1_ref, o2_ref, o3_ref, l1_ref, l2_ref, l3_ref, u_ref, halo_ref,
                ex_ref, wp_ref, ps_ref, mix_ref, il_ref, ls_ref, *, seq):
    tm = u_ref.shape[0]
    n_slabs = ATTN_WIDTH // LANES
    lses = [l1_ref[0]]
    for bi, (o_ref, l_ref) in enumerate(((o2_ref, l2_ref), (o3_ref, l3_ref))):
        dil = o_ref.shape[0]
        n = tm // dil
        for r in range(dil):
            ls_ref[bi, pl.ds(r, n, stride=dil), :] = l_ref[r]
            for s in range(n_slabs):
                il_ref[bi, s, pl.ds(r, n, stride=dil), :] = o_ref[r, :, s * LANES:(s + 1) * LANES]
        lses.append(ls_ref[bi])
    l1, l2, l3 = lses
    m = jnp.maximum(jnp.maximum(l1, l2), l3)
    e1, e2, e3 = jnp.exp(l1 - m), jnp.exp(l2 - m), jnp.exp(l3 - m)
    inv = 1.0 / (e1 + e2 + e3)
    ex = ex_ref[...]
    w1, w2, w3 = (_split_dot(e * inv, ex) for e in (e1, e2, e3))
    for s in range(n_slabs):
        cs = slice(s * LANES, (s + 1) * LANES)
        attn = w1[:, cs] * o1_ref[0, :, cs] + w2[:, cs] * il_ref[0, s] + w3[:, cs] * il_ref[1, s]
        mix_ref[:, cs] = attn.astype(BF16)

    pos0 = (pl.program_id(0) * tm) % seq
    u = u_ref[...]
    halo = jnp.where(pos0 == 0, 0.0, halo_ref[...])
    comb = jnp.concatenate([halo, u], axis=0)
    pos = pos0 + lax.broadcasted_iota(jnp.int32, (tm, 1), 0)
    cnt_fn = lambda w: jnp.minimum(pos + 1, w).astype(F32)
    for g, y in enumerate(_pool_groups(comb, u, cnt_fn, wp_ref, ps_ref)):
        lo = ATTN_WIDTH + g * POOL_GROUP_WIDTH
        mix_ref[:, lo:lo + POOL_GROUP_WIDTH] = y.astype(BF16)


def _mix(o_list, l_list, u, expand, wp_b, pool_scale, seq, tm):
    n = u.shape[0]
    tiles = seq // tm
    row = lambda i: (i, 0)
    res = lambda i: (i // tiles, 0, i % tiles, 0)
    o_specs = [pl.BlockSpec((None, o.shape[1], tm // o.shape[1], ATTN_WIDTH), res) for o in o_list]
    l_specs = [pl.BlockSpec((None, l.shape[1], tm // l.shape[1], LANES), res) for l in l_list]
    halo = pl.BlockSpec((POOL_HALO, ATTN_WIDTH),
                        lambda i: (jnp.maximum(i * (tm // POOL_HALO) - 1, 0), 0))
    n_dilated = len(o_list) - 1
    return pl.pallas_call(
        functools.partial(_mix_kernel, seq=seq),
        grid=(n // tm,),
        in_specs=o_specs + l_specs + [pl.BlockSpec((tm, ATTN_WIDTH), row), halo,
                                      _const_spec(expand.shape), _const_spec(wp_b.shape),
                                      _const_spec((1, ATTN_WIDTH))],
        out_specs=pl.BlockSpec((tm, 2 * ATTN_WIDTH), row),
        out_shape=jax.ShapeDtypeStruct((n, 2 * ATTN_WIDTH), BF16),
        scratch_shapes=[pltpu.VMEM((n_dilated, ATTN_WIDTH // LANES, tm, LANES), F32),
                        pltpu.VMEM((n_dilated, tm, LANES), F32)],
        compiler_params=_params("parallel"),
        name="mix",
    )(*o_list, *l_list, u, u, expand, wp_b, pool_scale)


def _sample_attn_kernel(q_ref, kn_ref, vn_ref, k_ref, v_ref, bc_ref, bn_ref, o_ref):
    n_heads, t_new, _ = q_ref.shape
    n_br = bc_ref.shape[0]
    lane_max = lambda a: jnp.max(a, axis=1, keepdims=True)
    lane_sum = lambda a: jnp.sum(a, axis=1, keepdims=True)
    new_lane = lax.broadcasted_iota(jnp.int32, (t_new, t_new), 1)
    nt = (((1,), (1,)), ((), ()))
    for h in range(n_heads):
        q, kn, vn = q_ref[h], kn_ref[h], vn_ref[h]
        s_c = jnp.dot(q.astype(BF16), k_ref[h].astype(BF16), preferred_element_type=F32)
        s_n = jnp.zeros((t_new, t_new), F32)
        for tp in range(t_new):
            s_n = jnp.where(new_lane == tp, lane_sum(q * kn[tp:tp + 1, :]), s_n)
        sc = [s_c + bc_ref[br, h] for br in range(n_br)]
        sn = [s_n + bn_ref[br, h] for br in range(n_br)]
        m = functools.reduce(jnp.maximum, [lane_max(a) for a in sc + sn])
        p_c = functools.reduce(jnp.add, [jnp.exp(a - m) for a in sc])
        p_n = functools.reduce(jnp.add, [jnp.exp(a - m) for a in sn])
        acc = lax.dot_general(p_c.astype(BF16), v_ref[h].astype(BF16), nt,
                              preferred_element_type=F32)
        for tp in range(t_new):
            acc = acc + p_n[:, tp:tp + 1] * vn[tp:tp + 1, :]
        o_ref[h] = acc / (lane_sum(p_c) + lane_sum(p_n))


def _sample_attn(q, kn, vn, k_t, v_t, bias_c, bias_n, heads_per_step):
    nb, nh, t_new, hd = q.shape
    win = k_t.shape[-1]
    hg = heads_per_step
    n_br = bias_c.shape[0]
    small = pl.BlockSpec((None, hg, t_new, hd), lambda b, g: (b, g, 0, 0))
    wide = pl.BlockSpec((None, hg, hd, win), lambda b, g: (b, g, 0, 0))
    return pl.pallas_call(
        _sample_attn_kernel,
        grid=(nb, nh // hg),
        in_specs=[small, small, small, wide, wide,
                  pl.BlockSpec((n_br, hg, t_new, win), lambda b, g: (0, g, 0, 0)),
                  pl.BlockSpec((n_br, hg, t_new, t_new), lambda b, g: (0, g, 0, 0))],
        out_specs=small,
        out_shape=jax.ShapeDtypeStruct((nb, nh, t_new, hd), F32),
        compiler_params=_params("parallel", "parallel"),
        name="sample_attn",
    )(q, kn, vn, k_t, v_t, bias_c, bias_n)


def _sample_pool_kernel(u_ref, st_ref, wp_ref, ps_ref, pool_ref, sbuf_ref, comb_ref, *, start):
    t_new = u_ref.shape[0]
    u = u_ref[...]
    comb_ref[...] = jnp.zeros_like(comb_ref)
    comb_ref[1:POOL_HALO, :] = st_ref[...]
    comb_ref[POOL_HALO:POOL_HALO + t_new, :] = u
    pos = start + lax.broadcasted_iota(jnp.int32, (t_new, 1), 0)
    cnt_fn = lambda w: jnp.minimum(pos + 1, w).astype(F32)
    for g, y in enumerate(_pool_groups(comb_ref[...], u, cnt_fn, wp_ref, ps_ref)):
        pool_ref[:, g * POOL_GROUP_WIDTH:(g + 1) * POOL_GROUP_WIDTH] = y
    sbuf_ref[:POOL_BUF - t_new, :] = st_ref[t_new:, :]
    sbuf_ref[POOL_BUF - t_new:, :] = u


def _sample_pool(u, state, wp_b, pool_scale):
    nb, t_new, w = u.shape
    flat = lambda rows: pl.BlockSpec((None, rows, w), lambda b: (b, 0, 0))
    return pl.pallas_call(
        functools.partial(_sample_pool_kernel, start=PAST_LEN),
        grid=(nb,),
        in_specs=[flat(t_new), flat(POOL_BUF), _const_spec(wp_b.shape), _const_spec((1, w))],
        out_specs=[flat(t_new), flat(POOL_BUF)],
        out_shape=[jax.ShapeDtypeStruct((nb, t_new, w), F32),
                   jax.ShapeDtypeStruct((nb, POOL_BUF, w), F32)],
        scratch_shapes=[pltpu.VMEM((POOL_HALO + 8, w), F32)],
        compiler_params=_params("parallel"),
        name="sample_pool",
    )(u, state, wp_b, pool_scale)


def _outproj_kernel(x_ref, mix_ref, wo_ref, ln_ref, wr_ref, br_ref, tri_ref, cin_ref,
                    x1_ref, h_ref, gate_ref, route_ref, cout_ref, count_ref):
    @pl.when(pl.program_id(0) == 0)
    def _():
        count_ref[...] = cin_ref[...]

    x1 = x_ref[...] + jnp.dot(mix_ref[...].astype(BF16), wo_ref[...],
                              preferred_element_type=F32)
    x1_ref[...] = x1
    ms = jnp.mean(x1 * x1, axis=-1, keepdims=True)
    h = x1 * lax.rsqrt(ms + EPS) * ln_ref[...]
    h_ref[...] = h
    lg = jnp.dot(h.astype(BF16), wr_ref[...], preferred_element_type=F32) + br_ref[...]

    lane = lax.broadcasted_iota(jnp.int32, lg.shape, 1).astype(F32)
    big = float(LANES)
    row_max = lambda mask: jnp.max(jnp.where(mask, lg, -jnp.inf), axis=-1, keepdims=True)
    first = lambda mask: jnp.min(jnp.where(mask, lane, big), axis=-1, keepdims=True)
    is_g = lane < N_GROUPS
    mg = row_max(is_g)
    g_top = first(jnp.logical_and(is_g, lg == mg))
    den = jnp.sum(jnp.where(is_g, jnp.exp(lg - mg), 0.0), axis=-1, keepdims=True)
    p_top = 1.0 / den
    base = N_GROUPS + EXPERTS_PER_GROUP * g_top
    in_grp = jnp.logical_and(lane >= base, lane < base + EXPERTS_PER_GROUP)
    v1 = row_max(in_grp)
    i1 = first(jnp.logical_and(in_grp, lg == v1))
    rest = jnp.logical_and(in_grp, lane != i1)
    v2 = row_max(rest)
    i2 = first(jnp.logical_and(rest, lg == v2))
    e21 = jnp.exp(v2 - v1)
    s21 = 1.0 + e21
    gate1 = p_top * (1.0 / s21)
    gate2 = p_top * (e21 / s21)
    gate_ref[...] = jnp.where(lane == 0.0, gate1, jnp.where(lane == 1.0, gate2, 0.0))
    e1, e2 = i1 - N_GROUPS, i2 - N_GROUPS

    hot1 = (lane == e1).astype(F32)
    hot2 = (lane == e2).astype(F32)
    hot = hot1 + hot2
    before = count_ref[...] + jnp.dot(tri_ref[...], hot.astype(BF16), preferred_element_type=F32)
    rank1 = jnp.sum(hot1 * before, axis=-1, keepdims=True)
    rank2 = jnp.sum(hot2 * before, axis=-1, keepdims=True)
    count_ref[...] = count_ref[...] + jnp.sum(hot, axis=0, keepdims=True)
    cout_ref[...] = count_ref[...]

    cols = jnp.zeros_like(lg)
    for r, col in enumerate((e1, e2, rank1, rank2)):
        cols = jnp.where(lane == float(r), col, cols)
    route_ref[...] = cols.T[:route_ref.shape[0], :]


def _outproj(x2d, mix, w_o_b, ln2, w_r, b_r, tri, counts_in, tm):
    n, d = x2d.shape
    row = lambda i: (i, 0)
    full = pl.BlockSpec((tm, d), row)
    stat = pl.BlockSpec((tm, LANES), row)
    route_rows = 8
    return pl.pallas_call(
        _outproj_kernel,
        grid=(n // tm,),
        in_specs=[full, full,
                  pl.BlockSpec(w_o_b.shape, lambda i: (0, 0), pipeline_mode=pl.Buffered(1)),
                  _const_spec((1, d)), _const_spec(w_r.shape),
                  _const_spec((1, LANES)), _const_spec((tm, tm)), _const_spec((1, LANES))],
        out_specs=[full, full, stat,
                   pl.BlockSpec((route_rows, tm), lambda i: (0, i)), _const_spec((1, LANES))],
        out_shape=[jax.ShapeDtypeStruct((n, d), F32),
                   jax.ShapeDtypeStruct((n, d), F32),
                   jax.ShapeDtypeStruct((n, LANES), F32),
                   jax.ShapeDtypeStruct((route_rows, n), F32),
                   jax.ShapeDtypeStruct((1, LANES), F32)],
        scratch_shapes=[pltpu.VMEM((1, LANES), F32)],
        compiler_params=_params("arbitrary"),
        name="outproj",
    )(x2d, mix, w_o_b, ln2, w_r, b_r, tri[:tm, :tm], counts_in)


def _row_copy(src_hbm, dst_vmem, sem, src_row, dst_row):
    return pltpu.make_async_copy(src_hbm.at[pl.ds(src_row, 1)],
                                 dst_vmem.at[pl.ds(dst_row, 1)], sem)


def _dispatch_kernel(pos_ref, pos_s_ref, pad_start_ref, pad_len_ref, tail_ref, h_ref, h_s_ref,
                     hs_ref, sem_ref, zero_ref, *, tile_rows):
    i = pl.program_id(0)
    sem = sem_ref.at[0]

    def scatter(src_ref, dst_rows_ref):
        n_rows = src_ref.shape[0]

        def put(r, k):
            return _row_copy(src_ref, hs_ref, sem, r, dst_rows_ref[0, k * n_rows + r])

        def issue(r, c):
            for k in range(TOP_K):
                put(r, k).start()
            return c

        def drain(r, c):
            for k in range(TOP_K):
                put(r, k).wait()
            return c

        lax.fori_loop(0, n_rows, issue, 0, unroll=DMA_UNROLL)
        lax.fori_loop(0, n_rows, drain, 0, unroll=DMA_UNROLL)

    scatter(h_ref, pos_ref)

    @pl.when(i == pl.num_programs(0) - 1)
    def _():
        scatter(h_s_ref, pos_s_ref)
        zero_ref[...] = jnp.zeros_like(zero_ref)

        sizes = [s for s in (tile_rows >> (b + 1) for b in range(tile_rows.bit_length()))
                 if s >= SUBLANES]

        def pad_copies(act):
            def single_rows(first, count):
                def body(r, c):
                    act(_row_copy(zero_ref, hs_ref, sem, 0, first + r))
                    return c
                lax.fori_loop(0, count, body, 0)

            for e in range(N_EXPERTS):
                start, length = pad_start_ref[e], pad_len_ref[e]
                head = jnp.minimum((-start) & (SUBLANES - 1), length)
                single_rows(start, head)
                body_len = length - head
                aligned = body_len & -SUBLANES
                offset = start + head
                for b, size in enumerate(sizes):
                    @pl.when((aligned & size) != 0)
                    def _():
                        rows = pl.ds(pl.multiple_of(offset, SUBLANES), size)
                        act(pltpu.make_async_copy(zero_ref.at[pl.ds(0, size)], hs_ref.at[rows],
                                                  sem_ref.at[1 + b]))
                    offset = offset + (aligned & size)
                single_rows(offset, body_len - aligned)

        pad_copies(lambda copy: copy.start())
        pad_copies(lambda copy: copy.wait())

        n_tiles = hs_ref.shape[0] // tile_rows

        def fill_tile(t):
            rows = pl.ds(pl.multiple_of(t * tile_rows, tile_rows), tile_rows)
            return pltpu.make_async_copy(zero_ref, hs_ref.at[rows], sem)

        def start_tile(t, c):
            fill_tile(t).start()
            return c

        def wait_tile(t, c):
            fill_tile(t).wait()
            return c

        lax.fori_loop(tail_ref[0], n_tiles, start_tile, 0)
        lax.fori_loop(tail_ref[0], n_tiles, wait_tile, 0)


def _tile_rows_spec(tm):
    return pl.BlockSpec((None, 1, TOP_K * tm), lambda i: (i, 0, 0), memory_space=pltpu.SMEM)


def _dispatch(rows_p, rows_s, pad_start, pad_len, tail, h, h_s, n_tiles, tile_rows, tm):
    n, d = h.shape
    smem = pl.BlockSpec(memory_space=pltpu.SMEM)
    return pl.pallas_call(
        functools.partial(_dispatch_kernel, tile_rows=tile_rows),
        grid=(n // tm,),
        in_specs=[_tile_rows_spec(tm), smem, smem, smem, smem,
                  pl.BlockSpec((tm, d), lambda i: (i, 0)),
                  pl.BlockSpec(h_s.shape, lambda i: (0, 0))],
        out_specs=pl.BlockSpec(memory_space=pl.ANY),
        out_shape=jax.ShapeDtypeStruct((n_tiles * tile_rows, d), h.dtype),
        scratch_shapes=[pltpu.SemaphoreType.DMA((tile_rows.bit_length(),)),
                        pltpu.VMEM((tile_rows, d), h.dtype)],
        compiler_params=_params("arbitrary", unchecked=True),
        name="dispatch",
    )(rows_p, rows_s, pad_start, pad_len, tail, h, h_s)


def _expert_kernel(tile_e_ref, tile_on_ref, next_e_ref, hs_ref, wg_hbm, wu_hbm, wd_hbm, y_ref,
                   wg_f, wu_f, wd_f, wg_b, wu_b, wd_b, sem_ref, slot_ref):
    i = pl.program_id(0)

    def fetch(expert, slot):
        return [pltpu.make_async_copy(src.at[expert], dst.at[slot], sem_ref.at[slot])
                for src, dst in ((wg_hbm, wg_f), (wu_hbm, wu_f), (wd_hbm, wd_f))]

    @pl.when(i == 0)
    def _():
        slot_ref[0] = 0
        for copy in fetch(tile_e_ref[0], 0):
            copy.start()

    @pl.when(jnp.logical_or(i == 0, tile_e_ref[i] != tile_e_ref[jnp.maximum(i - 1, 0)]))
    def _():
        slot = slot_ref[0]
        for copy in fetch(tile_e_ref[i], slot):
            copy.wait()
        wg_b[...] = wg_f[slot].astype(BF16)
        wu_b[...] = wu_f[slot].astype(BF16)
        wd_b[...] = wd_f[slot].astype(BF16)

        @pl.when(next_e_ref[i] >= 0)
        def _():
            for copy in fetch(next_e_ref[i], 1 - slot):
                copy.start()

        slot_ref[0] = 1 - slot

    @pl.when(tile_on_ref[i] == 1)
    def _():
        hb = hs_ref[...].astype(BF16)
        a = jnp.dot(hb, wg_b[...], preferred_element_type=F32)
        b = jnp.dot(hb, wu_b[...], preferred_element_type=F32)
        hid = a * jax.nn.sigmoid(a) * b
        y_ref[...] = jnp.dot(hid.astype(BF16), wd_b[...], preferred_element_type=F32)

    @pl.when(tile_on_ref[i] == 0)
    def _():
        y_ref[...] = jnp.zeros_like(y_ref)


def _experts(tile_e, tile_on, next_e, hs, w_gate, w_up, w_down, tm):
    n_tiles = tile_e.shape[0]
    d, f = w_gate.shape[1:]
    any_spec = pl.BlockSpec(memory_space=pl.ANY)
    grid_spec = pltpu.PrefetchScalarGridSpec(
        num_scalar_prefetch=3,
        grid=(n_tiles,),
        in_specs=[pl.BlockSpec((tm, d), lambda i, *_: (i, 0)), any_spec, any_spec, any_spec],
        out_specs=pl.BlockSpec((tm, d), lambda i, *_: (i, 0)),
        scratch_shapes=[pltpu.VMEM((2, d, f), F32), pltpu.VMEM((2, d, f), F32),
                        pltpu.VMEM((2, f, d), F32),
                        pltpu.VMEM((d, f), BF16), pltpu.VMEM((d, f), BF16),
                        pltpu.VMEM((f, d), BF16),
                        pltpu.SemaphoreType.DMA((2,)), pltpu.SMEM((1,), jnp.int32)],
    )
    return pl.pallas_call(
        _expert_kernel,
        grid_spec=grid_spec,
        out_shape=jax.ShapeDtypeStruct((n_tiles * tm, d), F32),
        compiler_params=_params("arbitrary"),
        name="experts",
    )(tile_e, tile_on, next_e, hs, w_gate, w_up, w_down)


def _combine_kernel(rows_ref, next_rows_ref, x1_ref, gate_ref, ys_hbm, y_ref, buf_ref, sem_ref):
    i = pl.program_id(0)
    n_tiles = pl.num_programs(0)
    tm = y_ref.shape[0]

    def issue(src_rows_ref, slot):
        def body(r, c):
            for k in range(TOP_K):
                _row_copy(ys_hbm, buf_ref.at[slot, k], sem_ref.at[slot],
                          src_rows_ref[0, k * tm + r], r).start()
            return c
        lax.fori_loop(0, tm, body, 0, unroll=DMA_UNROLL)

    def drain(slot):
        def body(r, c):
            for k in range(TOP_K):
                _row_copy(ys_hbm, buf_ref.at[slot, k], sem_ref.at[slot], 0, r).wait()
            return c
        lax.fori_loop(0, tm, body, 0, unroll=DMA_UNROLL)

    slot = i % 2

    @pl.when(i == 0)
    def _():
        issue(rows_ref, 0)

    drain(slot)

    @pl.when(i + 1 < n_tiles)
    def _():
        issue(next_rows_ref, 1 - slot)

    gate = gate_ref[...]
    y_ref[...] = (x1_ref[...] + gate[:, 0:1] * buf_ref[slot, 0]
                  + gate[:, 1:2] * buf_ref[slot, 1])


def _combine(rows, x1, gate, ys, tm):
    n, d = x1.shape
    last = n // tm - 1
    next_spec = pl.BlockSpec((None, 1, TOP_K * tm), lambda i: (jnp.minimum(i + 1, last), 0, 0),
                             memory_space=pltpu.SMEM)
    return pl.pallas_call(
        _combine_kernel,
        grid=(n // tm,),
        in_specs=[_tile_rows_spec(tm), next_spec,
                  pl.BlockSpec((tm, d), lambda i: (i, 0)),
                  pl.BlockSpec((tm, LANES), lambda i: (i, 0)),
                  pl.BlockSpec(memory_space=pl.ANY)],
        out_specs=pl.BlockSpec((tm, d), lambda i: (i, 0)),
        out_shape=jax.ShapeDtypeStruct((n, d), F32),
        scratch_shapes=[pltpu.VMEM((2, TOP_K, tm, d), F32), pltpu.SemaphoreType.DMA((2,))],
        compiler_params=_params("arbitrary", unchecked=True),
        name="combine",
    )(rows, rows, x1, gate, ys)


def _bucket_table(dilation):
    dist = np.arange(N_STEPS + 1, dtype=np.int64) * dilation
    max_exact = NUM_BUCKETS // 2
    df = np.maximum(dist, 1).astype(np.float32)
    large = max_exact + (np.log(df / np.float32(max_exact))
                         / np.float32(math.log(MAX_DISTANCE / max_exact))
                         * np.float32(NUM_BUCKETS - max_exact)).astype(np.int32)
    large = np.minimum(large, NUM_BUCKETS - 1)
    return np.where(dist < max_exact, dist, large).astype(np.int32)


def _sample_bias(rel_bias, t_new, win):
    buckets = np.stack([_bucket_table(dil) for _, dil in DILATED])
    by_step = jnp.transpose(rel_bias.astype(F32)[buckets], (0, 2, 1))
    nh = by_step.shape[1]
    neg = lambda *shape: jnp.full(shape, NEG_INF, F32)
    t = np.arange(t_new)
    cached, fresh = [], []
    for br, (_, dil) in enumerate(DILATED):
        rev = by_step[br, :, ::-1][:, :N_STEPS]
        if dil == 1:
            rows = [jnp.concatenate([neg(nh, win - N_STEPS + q), rev[:, :N_STEPS - q]], axis=1)
                    for q in range(t_new)]
            cached.append(jnp.stack(rows, axis=1))
        else:
            own = (t[:, None, None] == np.arange(dil)[None, None, :])
            band = jnp.where(own[None], rev[:, None, :, None], NEG_INF)
            band = band.reshape(nh, t_new, dil * N_STEPS)
            cached.append(jnp.concatenate([neg(nh, t_new, win - dil * N_STEPS), band], axis=2))
        step = t[:, None] - t[None, :]
        ok = (step >= 0) & (step % dil == 0)
        vals = by_step[br][:, np.where(ok, step // dil, 0)]
        fresh.append(jnp.where(ok[None], vals, NEG_INF))
    return jnp.stack(cached), jnp.stack(fresh)


def _tile_plan(counts, n_tiles, tm):
    tiles_e = (counts + tm - 1) // tm
    ends = jnp.cumsum(tiles_e)
    first_row = (ends - tiles_e) * tm
    tile = jnp.arange(n_tiles, dtype=jnp.int32)
    tile_e = jnp.sum((ends[None, :] <= tile[:, None]).astype(jnp.int32), axis=1)
    tile_on = (tile_e < N_EXPERTS).astype(jnp.int32)
    tile_e = jnp.minimum(tile_e, N_EXPERTS - 1)
    later = jnp.where(tile_e[None, :] > tile_e[:, None], tile_e[None, :], N_EXPERTS)
    next_e = jnp.min(later, axis=1)
    next_e = jnp.where(next_e == N_EXPERTS, -1, next_e)
    i32 = lambda a: a.astype(jnp.int32)
    return (i32(tile_e), tile_on, i32(next_e), i32(first_row), i32(first_row + counts),
            i32(tiles_e * tm - counts), i32(ends[-1:]))


TM_PROJ = 256
TM_TOKEN = 512
TM_EXPERT = 256
SAMPLE_HEADS_PER_STEP = 8


def kernel(x_prompt, x_sample, cache_k, cache_v, state_pool, rel_bias, ln1_w, w_in,
           q_norm_w, k_norm_w, w_pool, pool_scale, w_o, ln2_w, w_router_group,
           b_router_group, w_router_expert, b_router_expert, w_gate, w_up, w_down):
    depth = w_in.shape[0]
    assert depth == 1
    batch, seq, d_model = x_prompt.shape
    nb, t_new, _ = x_sample.shape
    win = cache_k.shape[2]
    f_exp = w_gate.shape[-1]

    w_in_b = _to_bf16(w_in[0], 256)
    w_o_b = _to_bf16(w_o[0], 256)
    wp_b = _to_bf16(w_pool[0].reshape(-1, POOL_GROUP_WIDTH), 256).reshape(w_pool.shape[1:])
    ln1 = ln1_w[0][None, :]
    ln2 = ln2_w[0][None, :]
    q_gain = jnp.tile(q_norm_w[0], N_HEADS)[None, :] * SCALE
    k_gain = jnp.tile(k_norm_w[0], N_HEADS)[None, :]
    ps = pool_scale[0][None, :]
    blk = np.arange(MXU_DIM) // HEAD_DIM
    avg = jnp.asarray((blk[:, None] == blk[None, :]) / HEAD_DIM, BF16)
    head_of_col = np.arange(ATTN_WIDTH) // HEAD_DIM
    expand_np = (np.arange(LANES)[:, None] == head_of_col[None, :])
    expand = jnp.asarray(expand_np, BF16)
    w_r = jnp.concatenate([w_router_group[0], w_router_expert[0]], axis=1)
    w_r = jnp.pad(w_r, ((0, 0), (0, LANES - w_r.shape[1]))).astype(BF16)
    b_r = jnp.pad(jnp.concatenate([b_router_group[0], b_router_expert[0]]),
                  (0, LANES - N_GROUPS - N_EXPERTS))[None, :]

    xp = x_prompt.reshape(batch * seq, d_model)
    proj = _inproj(xp, ln1, w_in_b, q_gain, k_gain, avg, TM_PROJ, batch, seq)
    k, v, u = proj[:3]
    natural = tuple(a.reshape(batch, 1, seq, ATTN_WIDTH) for a in proj[3:6])
    qkv = [natural] + [tuple(proj[6 + 3 * bi:9 + 3 * bi]) for bi in range(len(DILATED) - 1)]
    bias_all = _prompt_bias(rel_bias)
    o_list, l_list = [], []
    for branch, (qd, kd, vd) in enumerate(qkv):
        o, lse = _attn_branch(qd, kd, vd, bias_all, branch)
        o_list.append(o)
        l_list.append(lse)
    mix_p = _mix(o_list, l_list, u, expand, wp_b, ps, seq, TM_TOKEN)
    tri = jnp.asarray(np.tril(np.ones((TM_TOKEN, TM_TOKEN)), -1), BF16)
    x1_p, hp_p, gate_p, route_p, counts_p = _outproj(
        xp, mix_p, w_o_b, ln2, w_r, b_r, tri, jnp.zeros((1, LANES), F32), TM_TOKEN)

    n_s = nb * t_new
    xs = x_sample.reshape(n_s, d_model)
    k_s, v_s, u_s, q_s, _, _ = _inproj(xs, ln1, w_in_b, q_gain, k_gain, avg, n_s)
    heads_s = (nb, t_new, N_HEADS, HEAD_DIM)
    k_s5, v_s5 = k_s.reshape(heads_s), v_s.reshape(heads_s)
    to_lanes = lambda a: jnp.transpose(a, (0, 2, 3, 1))
    by_head = lambda a: jnp.transpose(a, (0, 2, 1, 3))
    bias_c, bias_n = _sample_bias(rel_bias, t_new, win)
    attn_h = _sample_attn(by_head(q_s.astype(F32).reshape(heads_s)), by_head(k_s5), by_head(v_s5),
                          to_lanes(cache_k[0]), to_lanes(cache_v[0]),
                          bias_c, bias_n, SAMPLE_HEADS_PER_STEP)
    attn_s = by_head(attn_h).reshape(n_s, ATTN_WIDTH)
    pool_s, sbuf = _sample_pool(u_s.reshape(nb, t_new, ATTN_WIDTH), state_pool[0], wp_b, ps)
    mix_s = jnp.concatenate([attn_s, pool_s.reshape(n_s, ATTN_WIDTH)], axis=1)
    x1_s, hp_s, gate_s, route_s, counts = _outproj(
        xs, mix_s, w_o_b, ln2, w_r, b_r, tri, counts_p, n_s)

    n_p = batch * seq
    n_pairs = (n_p + n_s) * TOP_K
    tm_e = TM_EXPERT
    n_tiles = -(-n_pairs // tm_e) + N_EXPERTS
    counts_i = counts[0, :N_EXPERTS].astype(jnp.int32)
    tile_e, tile_on, next_e, first_row, pad_start, pad_len, tail = _tile_plan(counts_i, n_tiles, tm_e)

    route = jnp.concatenate([route_p, route_s], axis=1)
    ids = route[:TOP_K].astype(jnp.int32)
    base = jnp.zeros_like(ids)
    for e in range(N_EXPERTS):
        base = jnp.where(ids == e, first_row[e], base)
    rows = base + route[TOP_K:2 * TOP_K].astype(jnp.int32)

    def by_tile(rows, tm):
        return jnp.transpose(rows.reshape(TOP_K, -1, tm), (1, 0, 2)).reshape(-1, 1, TOP_K * tm)

    rows_p, rows_s = by_tile(rows[:, :n_p], TM_TOKEN), by_tile(rows[:, n_p:], n_s)
    hs = _dispatch(rows_p, rows_s[0], pad_start, pad_len, tail, hp_p, hp_s, n_tiles, tm_e, TM_TOKEN)
    ys = _experts(tile_e, tile_on, next_e, hs, w_gate[0], w_up[0], w_down[0], tm_e)
    y_p = _combine(rows_p, x1_p, gate_p, ys, TM_TOKEN)
    y_s = _combine(rows_s, x1_s, gate_s, ys, n_s)

    keep = min(MAX_DISTANCE, seq)
    from_lanes = lambda a: jnp.transpose(a, (0, 3, 1, 2))[None, :, -keep:]
    return (y_p.reshape(batch, seq, d_model),
            y_s.reshape(nb, t_new, d_model),
            from_lanes(k), from_lanes(v),
            u.reshape(batch, seq, ATTN_WIDTH)[None, :, -POOL_BUF:],
            k_s5[None], v_s5[None], sbuf[None])
```

```python
import functools
import math

import numpy as np
import jax
import jax.numpy as jnp
from jax import lax
from jax.experimental import pallas as pl
from jax.experimental.pallas import tpu as pltpu

F32 = jnp.float32
BF16 = jnp.bfloat16

N_HEADS = 16
HEAD_DIM = 64
ATTN_WIDTH = N_HEADS * HEAD_DIM
POOL_WINDOWS = (2, 4, 8, 16)
POOL_GROUP_WIDTH = 256
POOL_BUF = max(POOL_WINDOWS) - 1
POOL_HALO = POOL_BUF + 1
DILATED = ((128, 1), (512, 4), (2048, 16))
N_STEPS = 128
QBLOCK = 128
NUM_BUCKETS = 32
MAX_DISTANCE = 2048
PAST_LEN = 16384
N_GROUPS = 4
EXPERTS_PER_GROUP = 4
N_EXPERTS = N_GROUPS * EXPERTS_PER_GROUP
TOP_K = 2
EPS = 1e-6
SCALE = HEAD_DIM ** -0.5
NEG_INF = -1e30
LANES = 128
SUBLANES = 8
MXU_DIM = 256
VMEM_LIMIT = 56 * 1024 * 1024


DMA_UNROLL = 8


def _params(*sem, unchecked=False):
    return pltpu.CompilerParams(dimension_semantics=sem, vmem_limit_bytes=VMEM_LIMIT,
                                disable_bounds_checks=unchecked)


def _const_spec(shape):
    zeros = (0,) * len(shape)
    return pl.BlockSpec(shape, lambda *_: zeros)


def _cast_kernel(x_ref, o_ref):
    o_ref[...] = x_ref[...].astype(o_ref.dtype)


def _to_bf16(w2d, block_rows):
    rows, cols = w2d.shape
    return pl.pallas_call(
        _cast_kernel,
        grid=(rows // block_rows,),
        in_specs=[pl.BlockSpec((block_rows, cols), lambda i: (i, 0))],
        out_specs=pl.BlockSpec((block_rows, cols), lambda i: (i, 0)),
        out_shape=jax.ShapeDtypeStruct((rows, cols), BF16),
        compiler_params=_params("parallel"),
        name="cast_bf16",
    )(w2d)


def _inproj_kernel(x_ref, ln_ref, w_ref, qg_ref, kg_ref, avg_ref, *refs, dilated):
    k_ref, v_ref, u_ref, q_ref, kb_ref, vb_ref = refs[:6]
    tm = x_ref.shape[0]
    x = x_ref[...]
    ms = jnp.mean(x * x, axis=-1, keepdims=True)
    h = (x * lax.rsqrt(ms + EPS) * ln_ref[...]).astype(BF16)
    n_chunks = ATTN_WIDTH // MXU_DIM

    def section(s):
        return jnp.dot(h, w_ref[:, s * ATTN_WIDTH:(s + 1) * ATTN_WIDTH],
                       preferred_element_type=F32)

    def head_norm(z, g_ref, c):
        zc = z[:, c * MXU_DIM:(c + 1) * MXU_DIM]
        msh = jnp.dot((zc * zc).astype(BF16), avg_ref[...], preferred_element_type=F32)
        return zc * lax.rsqrt(msh + EPS) * g_ref[:, c * MXU_DIM:(c + 1) * MXU_DIM]

    def emit(chunks, which, f32_ref, bf_ref):
        heads_per_chunk = MXU_DIM // HEAD_DIM
        for c, zc in enumerate(chunks):
            cs = slice(c * MXU_DIM, (c + 1) * MXU_DIM)
            bf_ref[:, cs] = zc.astype(BF16)
            if f32_ref is None:
                continue
            if dilated:
                zt = zc.T
                for j in range(heads_per_chunk):
                    f32_ref[c * heads_per_chunk + j] = zt[j * HEAD_DIM:(j + 1) * HEAD_DIM, :]
            else:
                f32_ref[:, cs] = zc
        if not dilated:
            return
        stage_ref = refs[-1]
        for c, zc in enumerate(chunks):
            for half in range(MXU_DIM // LANES):
                stage_ref[2 * c + half] = zc[:, half * LANES:(half + 1) * LANES]
        for bi, (_, dil) in enumerate(DILATED[1:]):
            out_ref = refs[6 + 3 * bi + which]
            n = tm // dil
            for r in range(dil):
                for s in range(ATTN_WIDTH // LANES):
                    rows = stage_ref[s, pl.ds(r, n, stride=dil), :]
                    out_ref[r, :, s * LANES:(s + 1) * LANES] = rows.astype(BF16)

    zq = section(0)
    emit([head_norm(zq, qg_ref, c) for c in range(n_chunks)], 0, None, q_ref)
    zk = section(1)
    emit([head_norm(zk, kg_ref, c) for c in range(n_chunks)], 1, k_ref, kb_ref)
    zv = section(2)
    emit([zv[:, c * MXU_DIM:(c + 1) * MXU_DIM] for c in range(n_chunks)], 2, v_ref, vb_ref)
    u_ref[...] = section(3)


def _inproj(x2d, ln1, w_in_b, q_gain, k_gain, avg, tm, batch=None, seq=None):
    n, d = x2d.shape
    dilated = seq is not None
    row = lambda i: (i, 0)
    wide = pl.BlockSpec((tm, ATTN_WIDTH), row)
    f32o = jax.ShapeDtypeStruct((n, ATTN_WIDTH), F32)
    bf16o = jax.ShapeDtypeStruct((n, ATTN_WIDTH), BF16)
    out_specs = [wide] * 6
    out_shape = [f32o, f32o, f32o, bf16o, bf16o, bf16o]
    scratch = []
    if dilated:
        tiles = seq // tm
        kv_t = pl.BlockSpec((None, N_HEADS, HEAD_DIM, tm), lambda i: (i // tiles, 0, 0, i % tiles))
        out_specs[:2] = [kv_t, kv_t]
        out_shape[:2] = [jax.ShapeDtypeStruct((batch, N_HEADS, HEAD_DIM, seq), F32)] * 2
        for _, dil in DILATED[1:]:
            spec = pl.BlockSpec((None, dil, tm // dil, ATTN_WIDTH),
                                lambda i: (i // tiles, 0, i % tiles, 0))
            out_specs += [spec] * 3
            out_shape += [jax.ShapeDtypeStruct((batch, dil, seq // dil, ATTN_WIDTH), BF16)] * 3
        scratch = [pltpu.VMEM((ATTN_WIDTH // LANES, tm, LANES), F32)]
    return pl.pallas_call(
        functools.partial(_inproj_kernel, dilated=dilated),
        grid=(n // tm,),
        in_specs=[pl.BlockSpec((tm, d), row),
                  _const_spec((1, d)),
                  pl.BlockSpec(w_in_b.shape, lambda i: (0, 0), pipeline_mode=pl.Buffered(1)),
                  _const_spec((1, ATTN_WIDTH)), _const_spec((1, ATTN_WIDTH)),
                  _const_spec((MXU_DIM, MXU_DIM))],
        out_specs=out_specs,
        out_shape=out_shape,
        scratch_shapes=scratch,
        compiler_params=_params("parallel"),
        name="inproj",
    )(x2d, ln1, w_in_b, q_gain, k_gain, avg)


def _attn_kernel(q_ref, kp_ref, kc_ref, vp_ref, vc_ref, bias_ref, o_ref, lse_ref,
                 s_ref, p_ref):
    i = pl.program_id(2)
    lane = lax.broadcasted_iota(jnp.int32, (QBLOCK, LANES), 1)
    low = lane < HEAD_DIM
    keep_low = low.astype(F32).astype(BF16)
    keep_high = (1.0 - low.astype(F32)).astype(BF16)
    nt = (((1,), (1,)), ((), ()))
    n_pairs = N_HEADS // 2

    def run(blk, with_prev):
        rq = slice(blk * QBLOCK, (blk + 1) * QBLOCK)
        before = slice((blk - 1) * QBLOCK, blk * QBLOCK)
        k0 = 0 if with_prev else QBLOCK
        nk = 2 * QBLOCK - k0
        ones = jnp.ones((nk, LANES), BF16)
        for hp in range(n_pairs):
            cs = slice(hp * LANES, (hp + 1) * LANES)
            qp = q_ref[rq, cs]
            q2 = jnp.concatenate([qp * keep_low, qp * keep_high], axis=0)
            keys = kc_ref[rq, cs]
            if with_prev:
                older = kp_ref[:, cs] if blk == 0 else kc_ref[before, cs]
                keys = jnp.concatenate([older, keys], axis=0)
            s2 = lax.dot_general(q2, keys, nt, preferred_element_type=F32)
            s_ref[hp, :, k0:] = s2 + bias_ref[hp, :, k0:]

        m_all = jnp.zeros((QBLOCK, LANES), F32)
        for hp in range(n_pairs):
            for sub in range(2):
                rows = slice(sub * QBLOCK, (sub + 1) * QBLOCK)
                m = jnp.max(s_ref[hp, rows, k0:], axis=-1, keepdims=True)
                p_ref[hp, rows, k0:] = jnp.exp(s_ref[hp, rows, k0:] - m).astype(BF16)
                m_all = jnp.where(lane == 2 * hp + sub, m, m_all)

        den_all = jnp.ones((QBLOCK, LANES), F32)
        for hp in range(n_pairs):
            cs = slice(hp * LANES, (hp + 1) * LANES)
            vals = vc_ref[rq, cs]
            if with_prev:
                older = vp_ref[:, cs] if blk == 0 else vc_ref[before, cs]
                vals = jnp.concatenate([older, vals], axis=0)
            r = jnp.dot(p_ref[hp, :, k0:], jnp.concatenate([vals, ones], axis=1),
                        preferred_element_type=F32)
            den0, den1 = r[:QBLOCK, LANES:], r[QBLOCK:, LANES:]
            o_ref[rq, cs] = jnp.where(low, r[:QBLOCK, :LANES] / den0, r[QBLOCK:, :LANES] / den1)
            den_all = jnp.where(lane == 2 * hp, den0, den_all)
            den_all = jnp.where(lane == 2 * hp + 1, den1, den_all)
        lse_ref[rq, :] = m_all + jnp.log(den_all)

    @pl.when(i == 0)
    def _():
        run(0, False)

    @pl.when(i > 0)
    def _():
        run(0, True)

    for blk in range(1, q_ref.shape[0] // QBLOCK):
        run(blk, True)


def _attn_branch(q, kb, vb, bias_all, branch):
    batch, dil, sub, _ = q.shape
    qb = min(ATTN_BLOCKS_PER_STEP, sub // QBLOCK)
    cur = lambda b, r, i: (b, r, i, 0)
    prev = lambda b, r, i: (b, r, jnp.maximum(i * qb - 1, 0), 0)
    wide_c = pl.BlockSpec((None, None, qb * QBLOCK, ATTN_WIDTH), cur)
    wide_p = pl.BlockSpec((None, None, QBLOCK, ATTN_WIDTH), prev)
    pairs = N_HEADS // 2
    bias_spec = pl.BlockSpec((None, pairs, 2 * QBLOCK, 2 * QBLOCK), lambda b, r, i: (branch, 0, 0, 0))
    return pl.pallas_call(
        _attn_kernel,
        grid=(batch, dil, sub // (qb * QBLOCK)),
        in_specs=[wide_c, wide_p, wide_c, wide_p, wide_c, bias_spec],
        out_specs=[wide_c, pl.BlockSpec((None, None, qb * QBLOCK, LANES), cur)],
        out_shape=[jax.ShapeDtypeStruct((batch, dil, sub, ATTN_WIDTH), F32),
                   jax.ShapeDtypeStruct((batch, dil, sub, LANES), F32)],
        scratch_shapes=[pltpu.VMEM((pairs, 2 * QBLOCK, 2 * QBLOCK), F32),
                        pltpu.VMEM((pairs, 2 * QBLOCK, 2 * QBLOCK), BF16)],
        compiler_params=_params("parallel", "parallel", "parallel"),
        name=f"attn_d{dil}",
    )(q, kb, kb, vb, vb, bias_all)


def _bias_kernel(base_ref, out_ref):
    width = base_ref.shape[1]
    for h in range(N_HEADS):
        rows = jnp.broadcast_to(base_ref[h:h + 1, :], (QBLOCK, width))
        band = pltpu.roll(rows, 0, 1, stride=1, stride_axis=0)
        out_ref[h // 2, (h % 2) * QBLOCK:(h % 2 + 1) * QBLOCK, :] = band[:, :2 * QBLOCK]


def _prompt_bias(rel_bias):
    width = 3 * QBLOCK
    steps = QBLOCK - np.arange(width)
    valid = (steps >= 0) & (steps <= N_STEPS)
    buckets = np.stack([_bucket_table(dil)[np.clip(steps, 0, N_STEPS)] for _, dil in DILATED])
    base = jnp.where(valid[None, :, None], rel_bias.astype(F32)[buckets], NEG_INF)
    base = jnp.transpose(base, (0, 2, 1))
    pairs = N_HEADS // 2
    return pl.pallas_call(
        _bias_kernel,
        grid=(len(DILATED),),
        in_specs=[pl.BlockSpec((None, N_HEADS, width), lambda g: (g, 0, 0))],
        out_specs=pl.BlockSpec((None, pairs, 2 * QBLOCK, 2 * QBLOCK), lambda g: (g, 0, 0, 0)),
        out_shape=jax.ShapeDtypeStruct((len(DILATED), pairs, 2 * QBLOCK, 2 * QBLOCK), F32),
        compiler_params=_params("parallel"),
        name="bias_table",
    )(base)


def _split_dot(a, b_bf16):
    hi = a.astype(BF16)
    lo = (a - hi.astype(F32)).astype(BF16)
    return (jnp.dot(hi, b_bf16, preferred_element_type=F32)
            + jnp.dot(lo, b_bf16, preferred_element_type=F32))


def _pool_groups(comb, u, cnt_fn, wp_ref, ps_ref):
    t = u.shape[0]
    outs = []
    run = comb
    width = 1
    for g, w in enumerate(POOL_WINDOWS):
        while width < w:
            run = run + pltpu.roll(run, width, 0)
            width *= 2
        cs = slice(g * POOL_GROUP_WIDTH, (g + 1) * POOL_GROUP_WIDTH)
        d = run[POOL_HALO:POOL_HALO + t, cs] / cnt_fn(w) - u[:, cs]
        y = jnp.dot(d.astype(BF16), wp_ref[g], preferred_element_type=F32)
        outs.append(y * ps_ref[:, cs])
    return outs


def _mix_kernel(o1_ref, o2_ref, o3_ref, l1_ref, l2_ref, l3_ref, u_ref, halo_ref,
                ex_ref, wp_ref, ps_ref, mix_ref, il_ref, ls_ref, *, seq):
    tm = u_ref.shape[0]
    n_slabs = ATTN_WIDTH // LANES
    lses = [l1_ref[0]]
    for bi, (o_ref, l_ref) in enumerate(((o2_ref, l2_ref), (o3_ref, l3_ref))):
        dil = o_ref.shape[0]
        n = tm // dil
        for r in range(dil):
            ls_ref[bi, pl.ds(r, n, stride=dil), :] = l_ref[r]
            for s in range(n_slabs):
                il_ref[bi, s, pl.ds(r, n, stride=dil), :] = o_ref[r, :, s * LANES:(s + 1) * LANES]
        lses.append(ls_ref[bi])
    l1, l2, l3 = lses
    m = jnp.maximum(jnp.maximum(l1, l2), l3)
    e1, e2, e3 = jnp.exp(l1 - m), jnp.exp(l2 - m), jnp.exp(l3 - m)
    inv = 1.0 / (e1 + e2 + e3)
    ex = ex_ref[...]
    w1, w2, w3 = (_split_dot(e * inv, ex) for e in (e1, e2, e3))
    for s in range(n_slabs):
        cs = slice(s * LANES, (s + 1) * LANES)
        attn = w1[:, cs] * o1_ref[0, :, cs] + w2[:, cs] * il_ref[0, s] + w3[:, cs] * il_ref[1, s]
        mix_ref[:, cs] = attn.astype(BF16)

    pos0 = (pl.program_id(0) * tm) % seq
    u = u_ref[...]
    halo = jnp.where(pos0 == 0, 0.0, halo_ref[...])
    comb = jnp.concatenate([halo, u], axis=0)
    pos = pos0 + lax.broadcasted_iota(jnp.int32, (tm, 1), 0)
    cnt_fn = lambda w: jnp.minimum(pos + 1, w).astype(F32)
    for g, y in enumerate(_pool_groups(comb, u, cnt_fn, wp_ref, ps_ref)):
        lo = ATTN_WIDTH + g * POOL_GROUP_WIDTH
        mix_ref[:, lo:lo + POOL_GROUP_WIDTH] = y.astype(BF16)


def _mix(o_list, l_list, u, expand, wp_b, pool_scale, seq, tm):
    n = u.shape[0]
    tiles = seq // tm
    row = lambda i: (i, 0)
    res = lambda i: (i // tiles, 0, i % tiles, 0)
    o_specs = [pl.BlockSpec((None, o.shape[1], tm // o.shape[1], ATTN_WIDTH), res) for o in o_list]
    l_specs = [pl.BlockSpec((None, l.shape[1], tm // l.shape[1], LANES), res) for l in l_list]
    halo = pl.BlockSpec((POOL_HALO, ATTN_WIDTH),
                        lambda i: (jnp.maximum(i * (tm // POOL_HALO) - 1, 0), 0))
    n_dilated = len(o_list) - 1
    return pl.pallas_call(
        functools.partial(_mix_kernel, seq=seq),
        grid=(n // tm,),
        in_specs=o_specs + l_specs + [pl.BlockSpec((tm, ATTN_WIDTH), row), halo,
                                      _const_spec(expand.shape), _const_spec(wp_b.shape),
                                      _const_spec((1, ATTN_WIDTH))],
        out_specs=pl.BlockSpec((tm, 2 * ATTN_WIDTH), row),
        out_shape=jax.ShapeDtypeStruct((n, 2 * ATTN_WIDTH), BF16),
        scratch_shapes=[pltpu.VMEM((n_dilated, ATTN_WIDTH // LANES, tm, LANES), F32),
                        pltpu.VMEM((n_dilated, tm, LANES), F32)],
        compiler_params=_params("parallel"),
        name="mix",
    )(*o_list, *l_list, u, u, expand, wp_b, pool_scale)


def _sample_attn_kernel(q_ref, kn_ref, vn_ref, k_ref, v_ref, bc_ref, bn_ref, o_ref):
    n_heads, t_new, _ = q_ref.shape
    n_br = bc_ref.shape[0]
    lane_max = lambda a: jnp.max(a, axis=1, keepdims=True)
    lane_sum = lambda a: jnp.sum(a, axis=1, keepdims=True)
    new_lane = lax.broadcasted_iota(jnp.int32, (t_new, t_new), 1)
    nt = (((1,), (1,)), ((), ()))
    for h in range(n_heads):
        q, kn, vn = q_ref[h], kn_ref[h], vn_ref[h]
        s_c = jnp.dot(q.astype(BF16), k_ref[h].astype(BF16), preferred_element_type=F32)
        s_n = jnp.zeros((t_new, t_new), F32)
        for tp in range(t_new):
            s_n = jnp.where(new_lane == tp, lane_sum(q * kn[tp:tp + 1, :]), s_n)
        sc = [s_c + bc_ref[br, h] for br in range(n_br)]
        sn = [s_n + bn_ref[br, h] for br in range(n_br)]
        m = functools.reduce(jnp.maximum, [lane_max(a) for a in sc + sn])
        p_c = functools.reduce(jnp.add, [jnp.exp(a - m) for a in sc])
        p_n = functools.reduce(jnp.add, [jnp.exp(a - m) for a in sn])
        acc = lax.dot_general(p_c.astype(BF16), v_ref[h].astype(BF16), nt,
                              preferred_element_type=F32)
        for tp in range(t_new):
            acc = acc + p_n[:, tp:tp + 1] * vn[tp:tp + 1, :]
        o_ref[h] = acc / (lane_sum(p_c) + lane_sum(p_n))


def _sample_attn(q, kn, vn, k_t, v_t, bias_c, bias_n, heads_per_step):
    nb, nh, t_new, hd = q.shape
    win = k_t.shape[-1]
    hg = heads_per_step
    n_br = bias_c.shape[0]
    small = pl.BlockSpec((None, hg, t_new, hd), lambda b, g: (b, g, 0, 0))
    wide = pl.BlockSpec((None, hg, hd, win), lambda b, g: (b, g, 0, 0))
    return pl.pallas_call(
        _sample_attn_kernel,
        grid=(nb, nh // hg),
        in_specs=[small, small, small, wide, wide,
                  pl.BlockSpec((n_br, hg, t_new, win), lambda b, g: (0, g, 0, 0)),
                  pl.BlockSpec((n_br, hg, t_new, t_new), lambda b, g: (0, g, 0, 0))],
        out_specs=small,
        out_shape=jax.ShapeDtypeStruct((nb, nh, t_new, hd), F32),
        compiler_params=_params("parallel", "parallel"),
        name="sample_attn",
    )(q, kn, vn, k_t, v_t, bias_c, bias_n)


def _sample_pool_kernel(u_ref, st_ref, wp_ref, ps_ref, pool_ref, sbuf_ref, comb_ref, *, start):
    t_new = u_ref.shape[0]
    u = u_ref[...]
    comb_ref[...] = jnp.zeros_like(comb_ref)
    comb_ref[1:POOL_HALO, :] = st_ref[...]
    comb_ref[POOL_HALO:POOL_HALO + t_new, :] = u
    pos = start + lax.broadcasted_iota(jnp.int32, (t_new, 1), 0)
    cnt_fn = lambda w: jnp.minimum(pos + 1, w).astype(F32)
    for g, y in enumerate(_pool_groups(comb_ref[...], u, cnt_fn, wp_ref, ps_ref)):
        pool_ref[:, g * POOL_GROUP_WIDTH:(g + 1) * POOL_GROUP_WIDTH] = y
    sbuf_ref[:POOL_BUF - t_new, :] = st_ref[t_new:, :]
    sbuf_ref[POOL_BUF - t_new:, :] = u


def _sample_pool(u, state, wp_b, pool_scale):
    nb, t_new, w = u.shape
    flat = lambda rows: pl.BlockSpec((None, rows, w), lambda b: (b, 0, 0))
    return pl.pallas_call(
        functools.partial(_sample_pool_kernel, start=PAST_LEN),
        grid=(nb,),
        in_specs=[flat(t_new), flat(POOL_BUF), _const_spec(wp_b.shape), _const_spec((1, w))],
        out_specs=[flat(t_new), flat(POOL_BUF)],
        out_shape=[jax.ShapeDtypeStruct((nb, t_new, w), F32),
                   jax.ShapeDtypeStruct((nb, POOL_BUF, w), F32)],
        scratch_shapes=[pltpu.VMEM((POOL_HALO + 8, w), F32)],
        compiler_params=_params("parallel"),
        name="sample_pool",
    )(u, state, wp_b, pool_scale)


def _outproj_kernel(x_ref, mix_ref, wo_ref, ln_ref, wr_ref, br_ref, tri_ref, cin_ref,
                    x1_ref, h_ref, gate_ref, route_ref, cout_ref, count_ref):
    @pl.when(pl.program_id(0) == 0)
    def _():
        count_ref[...] = cin_ref[...]

    x1 = x_ref[...] + jnp.dot(mix_ref[...].astype(BF16), wo_ref[...],
                              preferred_element_type=F32)
    x1_ref[...] = x1
    ms = jnp.mean(x1 * x1, axis=-1, keepdims=True)
    h = x1 * lax.rsqrt(ms + EPS) * ln_ref[...]
    h_ref[...] = h
    lg = jnp.dot(h.astype(BF16), wr_ref[...], preferred_element_type=F32) + br_ref[...]

    lane = lax.broadcasted_iota(jnp.int32, lg.shape, 1).astype(F32)
    big = float(LANES)
    row_max = lambda mask: jnp.max(jnp.where(mask, lg, -jnp.inf), axis=-1, keepdims=True)
    first = lambda mask: jnp.min(jnp.where(mask, lane, big), axis=-1, keepdims=True)
    is_g = lane < N_GROUPS
    mg = row_max(is_g)
    g_top = first(jnp.logical_and(is_g, lg == mg))
    den = jnp.sum(jnp.where(is_g, jnp.exp(lg - mg), 0.0), axis=-1, keepdims=True)
    p_top = 1.0 / den
    base = N_GROUPS + EXPERTS_PER_GROUP * g_top
    in_grp = jnp.logical_and(lane >= base, lane < base + EXPERTS_PER_GROUP)
    v1 = row_max(in_grp)
    i1 = first(jnp.logical_and(in_grp, lg == v1))
    rest = jnp.logical_and(in_grp, lane != i1)
    v2 = row_max(rest)
    i2 = first(jnp.logical_and(rest, lg == v2))
    e21 = jnp.exp(v2 - v1)
    s21 = 1.0 + e21
    gate1 = p_top * (1.0 / s21)
    gate2 = p_top * (e21 / s21)
    gate_ref[...] = jnp.where(lane == 0.0, gate1, jnp.where(lane == 1.0, gate2, 0.0))
    e1, e2 = i1 - N_GROUPS, i2 - N_GROUPS

    hot1 = (lane == e1).astype(F32)
    hot2 = (lane == e2).astype(F32)
    hot = hot1 + hot2
    before = count_ref[...] + jnp.dot(tri_ref[...], hot.astype(BF16), preferred_element_type=F32)
    rank1 = jnp.sum(hot1 * before, axis=-1, keepdims=True)
    rank2 = jnp.sum(hot2 * before, axis=-1, keepdims=True)
    count_ref[...] = count_ref[...] + jnp.sum(hot, axis=0, keepdims=True)
    cout_ref[...] = count_ref[...]

    cols = jnp.zeros_like(lg)
    for r, col in enumerate((e1, e2, rank1, rank2)):
        cols = jnp.where(lane == float(r), col, cols)
    route_ref[...] = cols.T[:route_ref.shape[0], :]


def _outproj(x2d, mix, w_o_b, ln2, w_r, b_r, tri, counts_in, tm):
    n, d = x2d.shape
    row = lambda i: (i, 0)
    full = pl.BlockSpec((tm, d), row)
    stat = pl.BlockSpec((tm, LANES), row)
    route_rows = 8
    return pl.pallas_call(
        _outproj_kernel,
        grid=(n // tm,),
        in_specs=[full, full,
                  pl.BlockSpec(w_o_b.shape, lambda i: (0, 0), pipeline_mode=pl.Buffered(1)),
                  _const_spec((1, d)), _const_spec(w_r.shape),
                  _const_spec((1, LANES)), _const_spec((tm, tm)), _const_spec((1, LANES))],
        out_specs=[full, full, stat,
                   pl.BlockSpec((route_rows, tm), lambda i: (0, i)), _const_spec((1, LANES))],
        out_shape=[jax.ShapeDtypeStruct((n, d), F32),
                   jax.ShapeDtypeStruct((n, d), F32),
                   jax.ShapeDtypeStruct((n, LANES), F32),
                   jax.ShapeDtypeStruct((route_rows, n), F32),
                   jax.ShapeDtypeStruct((1, LANES), F32)],
        scratch_shapes=[pltpu.VMEM((1, LANES), F32)],
        compiler_params=_params("arbitrary"),
        name="outproj",
    )(x2d, mix, w_o_b, ln2, w_r, b_r, tri[:tm, :tm], counts_in)


def _row_copy(src_hbm, dst_vmem, sem, src_row, dst_row):
    return pltpu.make_async_copy(src_hbm.at[pl.ds(src_row, 1)],
                                 dst_vmem.at[pl.ds(dst_row, 1)], sem)


def _dispatch_kernel(pos_ref, pos_s_ref, pad_start_ref, pad_len_ref, tail_ref, h_ref, h_s_ref,
                     hs_ref, sem_ref, zero_ref, *, tile_rows):
    i = pl.program_id(0)
    sem = sem_ref.at[0]

    def scatter(src_ref, dst_rows_ref):
        n_rows = src_ref.shape[0]

        def put(r, k):
            return _row_copy(src_ref, hs_ref, sem, r, dst_rows_ref[0, k * n_rows + r])

        def issue(r, c):
            for k in range(TOP_K):
                put(r, k).start()
            return c

        def drain(r, c):
            for k in range(TOP_K):
                put(r, k).wait()
            return c

        lax.fori_loop(0, n_rows, issue, 0, unroll=DMA_UNROLL)
        lax.fori_loop(0, n_rows, drain, 0, unroll=DMA_UNROLL)

    scatter(h_ref, pos_ref)

    @pl.when(i == pl.num_programs(0) - 1)
    def _():
        scatter(h_s_ref, pos_s_ref)
        zero_ref[...] = jnp.zeros_like(zero_ref)

        sizes = [s for s in (tile_rows >> (b + 1) for b in range(tile_rows.bit_length()))
                 if s >= SUBLANES]

        def pad_copies(act):
            def single_rows(first, count):
                def body(r, c):
                    act(_row_copy(zero_ref, hs_ref, sem, 0, first + r))
                    return c
                lax.fori_loop(0, count, body, 0)

            for e in range(N_EXPERTS):
                start, length = pad_start_ref[e], pad_len_ref[e]
                head = jnp.minimum((-start) & (SUBLANES - 1), length)
                single_rows(start, head)
                body_len = length - head
                aligned = body_len & -SUBLANES
                offset = start + head
                for b, size in enumerate(sizes):
                    @pl.when((aligned & size) != 0)
                    def _():
                        rows = pl.ds(pl.multiple_of(offset, SUBLANES), size)
                        act(pltpu.make_async_copy(zero_ref.at[pl.ds(0, size)], hs_ref.at[rows],
                                                  sem_ref.at[1 + b]))
                    offset = offset + (aligned & size)
                single_rows(offset, body_len - aligned)

        pad_copies(lambda copy: copy.start())
        pad_copies(lambda copy: copy.wait())

        n_tiles = hs_ref.shape[0] // tile_rows

        def fill_tile(t):
            rows = pl.ds(pl.multiple_of(t * tile_rows, tile_rows), tile_rows)
            return pltpu.make_async_copy(zero_ref, hs_ref.at[rows], sem)

        def start_tile(t, c):
            fill_tile(t).start()
            return c

        def wait_tile(t, c):
            fill_tile(t).wait()
            return c

        lax.fori_loop(tail_ref[0], n_tiles, start_tile, 0)
        lax.fori_loop(tail_ref[0], n_tiles, wait_tile, 0)


def _tile_rows_spec(tm):
    return pl.BlockSpec((None, 1, TOP_K * tm), lambda i: (i, 0, 0), memory_space=pltpu.SMEM)


def _dispatch(rows_p, rows_s, pad_start, pad_len, tail, h, h_s, n_tiles, tile_rows, tm):
    n, d = h.shape
    smem = pl.BlockSpec(memory_space=pltpu.SMEM)
    return pl.pallas_call(
        functools.partial(_dispatch_kernel, tile_rows=tile_rows),
        grid=(n // tm,),
        in_specs=[_tile_rows_spec(tm), smem, smem, smem, smem,
                  pl.BlockSpec((tm, d), lambda i: (i, 0)),
                  pl.BlockSpec(h_s.shape, lambda i: (0, 0))],
        out_specs=pl.BlockSpec(memory_space=pl.ANY),
        out_shape=jax.ShapeDtypeStruct((n_tiles * tile_rows, d), h.dtype),
        scratch_shapes=[pltpu.SemaphoreType.DMA((tile_rows.bit_length(),)),
                        pltpu.VMEM((tile_rows, d), h.dtype)],
        compiler_params=_params("arbitrary", unchecked=True),
        name="dispatch",
    )(rows_p, rows_s, pad_start, pad_len, tail, h, h_s)


def _expert_kernel(tile_e_ref, tile_on_ref, next_e_ref, hs_ref, wg_hbm, wu_hbm, wd_hbm, y_ref,
                   wg_f, wu_f, wd_f, wg_b, wu_b, wd_b, sem_ref, slot_ref):
    i = pl.program_id(0)

    def fetch(expert, slot):
        return [pltpu.make_async_copy(src.at[expert], dst.at[slot], sem_ref.at[slot])
                for src, dst in ((wg_hbm, wg_f), (wu_hbm, wu_f), (wd_hbm, wd_f))]

    @pl.when(i == 0)
    def _():
        slot_ref[0] = 0
        for copy in fetch(tile_e_ref[0], 0):
            copy.start()

    @pl.when(jnp.logical_or(i == 0, tile_e_ref[i] != tile_e_ref[jnp.maximum(i - 1, 0)]))
    def _():
        slot = slot_ref[0]
        for copy in fetch(tile_e_ref[i], slot):
            copy.wait()
        wg_b[...] = wg_f[slot].astype(BF16)
        wu_b[...] = wu_f[slot].astype(BF16)
        wd_b[...] = wd_f[slot].astype(BF16)

        @pl.when(next_e_ref[i] >= 0)
        def _():
            for copy in fetch(next_e_ref[i], 1 - slot):
                copy.start()

        slot_ref[0] = 1 - slot

    @pl.when(tile_on_ref[i] == 1)
    def _():
        hb = hs_ref[...].astype(BF16)
        a = jnp.dot(hb, wg_b[...], preferred_element_type=F32)
        b = jnp.dot(hb, wu_b[...], preferred_element_type=F32)
        hid = a * jax.nn.sigmoid(a) * b
        y_ref[...] = jnp.dot(hid.astype(BF16), wd_b[...], preferred_element_type=F32)

    @pl.when(tile_on_ref[i] == 0)
    def _():
        y_ref[...] = jnp.zeros_like(y_ref)


def _experts(tile_e, tile_on, next_e, hs, w_gate, w_up, w_down, tm):
    n_tiles = tile_e.shape[0]
    d, f = w_gate.shape[1:]
    any_spec = pl.BlockSpec(memory_space=pl.ANY)
    grid_spec = pltpu.PrefetchScalarGridSpec(
        num_scalar_prefetch=3,
        grid=(n_tiles,),
        in_specs=[pl.BlockSpec((tm, d), lambda i, *_: (i, 0)), any_spec, any_spec, any_spec],
        out_specs=pl.BlockSpec((tm, d), lambda i, *_: (i, 0)),
        scratch_shapes=[pltpu.VMEM((2, d, f), F32), pltpu.VMEM((2, d, f), F32),
                        pltpu.VMEM((2, f, d), F32),
                        pltpu.VMEM((d, f), BF16), pltpu.VMEM((d, f), BF16),
                        pltpu.VMEM((f, d), BF16),
                        pltpu.SemaphoreType.DMA((2,)), pltpu.SMEM((1,), jnp.int32)],
    )
    return pl.pallas_call(
        _expert_kernel,
        grid_spec=grid_spec,
        out_shape=jax.ShapeDtypeStruct((n_tiles * tm, d), F32),
        compiler_params=_params("arbitrary"),
        name="experts",
    )(tile_e, tile_on, next_e, hs, w_gate, w_up, w_down)


def _combine_kernel(rows_ref, next_rows_ref, x1_ref, gate_ref, ys_hbm, y_ref, buf_ref, sem_ref):
    i = pl.program_id(0)
    n_tiles = pl.num_programs(0)
    tm = y_ref.shape[0]

    def issue(src_rows_ref, slot):
        def body(r, c):
            for k in range(TOP_K):
                _row_copy(ys_hbm, buf_ref.at[slot, k], sem_ref.at[slot],
                          src_rows_ref[0, k * tm + r], r).start()
            return c
        lax.fori_loop(0, tm, body, 0, unroll=DMA_UNROLL)

    def drain(slot):
        def body(r, c):
            for k in range(TOP_K):
                _row_copy(ys_hbm, buf_ref.at[slot, k], sem_ref.at[slot], 0, r).wait()
            return c
        lax.fori_loop(0, tm, body, 0, unroll=DMA_UNROLL)

    slot = i % 2

    @pl.when(i == 0)
    def _():
        issue(rows_ref, 0)

    drain(slot)

    @pl.when(i + 1 < n_tiles)
    def _():
        issue(next_rows_ref, 1 - slot)

    gate = gate_ref[...]
    y_ref[...] = (x1_ref[...] + gate[:, 0:1] * buf_ref[slot, 0]
                  + gate[:, 1:2] * buf_ref[slot, 1])


def _combine(rows, x1, gate, ys, tm):
    n, d = x1.shape
    last = n // tm - 1
    next_spec = pl.BlockSpec((None, 1, TOP_K * tm), lambda i: (jnp.minimum(i + 1, last), 0, 0),
                             memory_space=pltpu.SMEM)
    return pl.pallas_call(
        _combine_kernel,
        grid=(n // tm,),
        in_specs=[_tile_rows_spec(tm), next_spec,
                  pl.BlockSpec((tm, d), lambda i: (i, 0)),
                  pl.BlockSpec((tm, LANES), lambda i: (i, 0)),
                  pl.BlockSpec(memory_space=pl.ANY)],
        out_specs=pl.BlockSpec((tm, d), lambda i: (i, 0)),
        out_shape=jax.ShapeDtypeStruct((n, d), F32),
        scratch_shapes=[pltpu.VMEM((2, TOP_K, tm, d), F32), pltpu.SemaphoreType.DMA((2,))],
        compiler_params=_params("arbitrary", unchecked=True),
        name="combine",
    )(rows, rows, x1, gate, ys)


def _bucket_table(dilation):
    dist = np.arange(N_STEPS + 1, dtype=np.int64) * dilation
    max_exact = NUM_BUCKETS // 2
    df = np.maximum(dist, 1).astype(np.float32)
    large = max_exact + (np.log(df / np.float32(max_exact))
                         / np.float32(math.log(MAX_DISTANCE / max_exact))
                         * np.float32(NUM_BUCKETS - max_exact)).astype(np.int32)
    large = np.minimum(large, NUM_BUCKETS - 1)
    return np.where(dist < max_exact, dist, large).astype(np.int32)


def _sample_bias(rel_bias, t_new, win):
    buckets = np.stack([_bucket_table(dil) for _, dil in DILATED])
    by_step = jnp.transpose(rel_bias.astype(F32)[buckets], (0, 2, 1))
    nh = by_step.shape[1]
    neg = lambda *shape: jnp.full(shape, NEG_INF, F32)
    t = np.arange(t_new)
    cached, fresh = [], []
    for br, (_, dil) in enumerate(DILATED):
        rev = by_step[br, :, ::-1][:, :N_STEPS]
        if dil == 1:
            rows = [jnp.concatenate([neg(nh, win - N_STEPS + q), rev[:, :N_STEPS - q]], axis=1)
                    for q in range(t_new)]
            cached.append(jnp.stack(rows, axis=1))
        else:
            own = (t[:, None, None] == np.arange(dil)[None, None, :])
            band = jnp.where(own[None], rev[:, None, :, None], NEG_INF)
            band = band.reshape(nh, t_new, dil * N_STEPS)
            cached.append(jnp.concatenate([neg(nh, t_new, win - dil * N_STEPS), band], axis=2))
        step = t[:, None] - t[None, :]
        ok = (step >= 0) & (step % dil == 0)
        vals = by_step[br][:, np.where(ok, step // dil, 0)]
        fresh.append(jnp.where(ok[None], vals, NEG_INF))
    return jnp.stack(cached), jnp.stack(fresh)


def _tile_plan(counts, n_tiles, tm):
    tiles_e = (counts + tm - 1) // tm
    ends = jnp.cumsum(tiles_e)
    first_row = (ends - tiles_e) * tm
    tile = jnp.arange(n_tiles, dtype=jnp.int32)
    tile_e = jnp.sum((ends[None, :] <= tile[:, None]).astype(jnp.int32), axis=1)
    tile_on = (tile_e < N_EXPERTS).astype(jnp.int32)
    tile_e = jnp.minimum(tile_e, N_EXPERTS - 1)
    later = jnp.where(tile_e[None, :] > tile_e[:, None], tile_e[None, :], N_EXPERTS)
    next_e = jnp.min(later, axis=1)
    next_e = jnp.where(next_e == N_EXPERTS, -1, next_e)
    i32 = lambda a: a.astype(jnp.int32)
    return (i32(tile_e), tile_on, i32(next_e), i32(first_row), i32(first_row + counts),
            i32(tiles_e * tm - counts), i32(ends[-1:]))


TM_PROJ = 256
TM_TOKEN = 512
TM_EXPERT = 256
SAMPLE_HEADS_PER_STEP = 8
ATTN_BLOCKS_PER_STEP = 2


def kernel(x_prompt, x_sample, cache_k, cache_v, state_pool, rel_bias, ln1_w, w_in,
           q_norm_w, k_norm_w, w_pool, pool_scale, w_o, ln2_w, w_router_group,
           b_router_group, w_router_expert, b_router_expert, w_gate, w_up, w_down):
    depth = w_in.shape[0]
    assert depth == 1
    batch, seq, d_model = x_prompt.shape
    nb, t_new, _ = x_sample.shape
    win = cache_k.shape[2]
    f_exp = w_gate.shape[-1]

    w_in_b = _to_bf16(w_in[0], 256)
    w_o_b = _to_bf16(w_o[0], 256)
    wp_b = _to_bf16(w_pool[0].reshape(-1, POOL_GROUP_WIDTH), 256).reshape(w_pool.shape[1:])
    ln1 = ln1_w[0][None, :]
    ln2 = ln2_w[0][None, :]
    q_gain = jnp.tile(q_norm_w[0], N_HEADS)[None, :] * SCALE
    k_gain = jnp.tile(k_norm_w[0], N_HEADS)[None, :]
    ps = pool_scale[0][None, :]
    blk = np.arange(MXU_DIM) // HEAD_DIM
    avg = jnp.asarray((blk[:, None] == blk[None, :]) / HEAD_DIM, BF16)
    head_of_col = np.arange(ATTN_WIDTH) // HEAD_DIM
    expand_np = (np.arange(LANES)[:, None] == head_of_col[None, :])
    expand = jnp.asarray(expand_np, BF16)
    w_r = jnp.concatenate([w_router_group[0], w_router_expert[0]], axis=1)
    w_r = jnp.pad(w_r, ((0, 0), (0, LANES - w_r.shape[1]))).astype(BF16)
    b_r = jnp.pad(jnp.concatenate([b_router_group[0], b_router_expert[0]]),
                  (0, LANES - N_GROUPS - N_EXPERTS))[None, :]

    xp = x_prompt.reshape(batch * seq, d_model)
    proj = _inproj(xp, ln1, w_in_b, q_gain, k_gain, avg, TM_PROJ, batch, seq)
    k, v, u = proj[:3]
    natural = tuple(a.reshape(batch, 1, seq, ATTN_WIDTH) for a in proj[3:6])
    qkv = [natural] + [tuple(proj[6 + 3 * bi:9 + 3 * bi]) for bi in range(len(DILATED) - 1)]
    bias_all = _prompt_bias(rel_bias)
    o_list, l_list = [], []
    for branch, (qd, kd, vd) in enumerate(qkv):
        o, lse = _attn_branch(qd, kd, vd, bias_all, branch)
        o_list.append(o)
        l_list.append(lse)
    mix_p = _mix(o_list, l_list, u, expand, wp_b, ps, seq, TM_TOKEN)
    tri = jnp.asarray(np.tril(np.ones((TM_TOKEN, TM_TOKEN)), -1), BF16)
    x1_p, hp_p, gate_p, route_p, counts_p = _outproj(
        xp, mix_p, w_o_b, ln2, w_r, b_r, tri, jnp.zeros((1, LANES), F32), TM_TOKEN)

    n_s = nb * t_new
    xs = x_sample.reshape(n_s, d_model)
    k_s, v_s, u_s, q_s, _, _ = _inproj(xs, ln1, w_in_b, q_gain, k_gain, avg, n_s)
    heads_s = (nb, t_new, N_HEADS, HEAD_DIM)
    k_s5, v_s5 = k_s.reshape(heads_s), v_s.reshape(heads_s)
    to_lanes = lambda a: jnp.transpose(a, (0, 2, 3, 1))
    by_head = lambda a: jnp.transpose(a, (0, 2, 1, 3))
    bias_c, bias_n = _sample_bias(rel_bias, t_new, win)
    attn_h = _sample_attn(by_head(q_s.astype(F32).reshape(heads_s)), by_head(k_s5), by_head(v_s5),
                          to_lanes(cache_k[0]), to_lanes(cache_v[0]),
                          bias_c, bias_n, SAMPLE_HEADS_PER_STEP)
    attn_s = by_head(attn_h).reshape(n_s, ATTN_WIDTH)
    pool_s, sbuf = _sample_pool(u_s.reshape(nb, t_new, ATTN_WIDTH), state_pool[0], wp_b, ps)
    mix_s = jnp.concatenate([attn_s, pool_s.reshape(n_s, ATTN_WIDTH)], axis=1)
    x1_s, hp_s, gate_s, route_s, counts = _outproj(
        xs, mix_s, w_o_b, ln2, w_r, b_r, tri, counts_p, n_s)

    n_p = batch * seq
    n_pairs = (n_p + n_s) * TOP_K
    tm_e = TM_EXPERT
    n_tiles = -(-n_pairs // tm_e) + N_EXPERTS
    counts_i = counts[0, :N_EXPERTS].astype(jnp.int32)
    tile_e, tile_on, next_e, first_row, pad_start, pad_len, tail = _tile_plan(counts_i, n_tiles, tm_e)

    route = jnp.concatenate([route_p, route_s], axis=1)
    ids = route[:TOP_K].astype(jnp.int32)
    base = jnp.zeros_like(ids)
    for e in range(N_EXPERTS):
        base = jnp.where(ids == e, first_row[e], base)
    rows = base + route[TOP_K:2 * TOP_K].astype(jnp.int32)

    def by_tile(rows, tm):
        return jnp.transpose(rows.reshape(TOP_K, -1, tm), (1, 0, 2)).reshape(-1, 1, TOP_K * tm)

    rows_p, rows_s = by_tile(rows[:, :n_p], TM_TOKEN), by_tile(rows[:, n_p:], n_s)
    hs = _dispatch(rows_p, rows_s[0], pad_start, pad_len, tail, hp_p, hp_s, n_tiles, tm_e, TM_TOKEN)
    ys = _experts(tile_e, tile_on, next_e, hs, w_gate[0], w_up[0], w_down[0], tm_e)
    y_p = _combine(rows_p, x1_p, gate_p, ys, TM_TOKEN)
    y_s = _combine(rows_s, x1_s, gate_s, ys, n_s)

    keep = min(MAX_DISTANCE, seq)
    from_lanes = lambda a: jnp.transpose(a, (0, 3, 1, 2))[None, :, -keep:]
    return (y_p.reshape(batch, seq, d_model),
            y_s.reshape(nb, t_new, d_model),
            from_lanes(k), from_lanes(v),
            u.reshape(batch, seq, ATTN_WIDTH)[None, :, -POOL_BUF:],
            k_s5[None], v_s5[None], sbuf[None])
```

```python
import functools
import math

import numpy as np
import jax
import jax.numpy as jnp
from jax import lax
from jax.experimental import pallas as pl
from jax.experimental.pallas import tpu as pltpu

F32 = jnp.float32
BF16 = jnp.bfloat16

N_HEADS = 16
HEAD_DIM = 64
ATTN_WIDTH = N_HEADS * HEAD_DIM
POOL_WINDOWS = (2, 4, 8, 16)
POOL_GROUP_WIDTH = 256
POOL_BUF = max(POOL_WINDOWS) - 1
POOL_HALO = POOL_BUF + 1
DILATED = ((128, 1), (512, 4), (2048, 16))
N_STEPS = 128
QBLOCK = 128
NUM_BUCKETS = 32
MAX_DISTANCE = 2048
PAST_LEN = 16384
N_GROUPS = 4
EXPERTS_PER_GROUP = 4
N_EXPERTS = N_GROUPS * EXPERTS_PER_GROUP
TOP_K = 2
EPS = 1e-6
SCALE = HEAD_DIM ** -0.5
NEG_INF = -1e30
LANES = 128
SUBLANES = 8
MXU_DIM = 256
VMEM_LIMIT = 56 * 1024 * 1024


DMA_UNROLL = 8


def _params(*sem, unchecked=False):
    return pltpu.CompilerParams(dimension_semantics=sem, vmem_limit_bytes=VMEM_LIMIT,
                                disable_bounds_checks=unchecked)


def _const_spec(shape):
    zeros = (0,) * len(shape)
    return pl.BlockSpec(shape, lambda *_: zeros)


def _cast_kernel(x_ref, o_ref):
    o_ref[...] = x_ref[...].astype(o_ref.dtype)


def _to_bf16(w2d, block_rows):
    rows, cols = w2d.shape
    return pl.pallas_call(
        _cast_kernel,
        grid=(rows // block_rows,),
        in_specs=[pl.BlockSpec((block_rows, cols), lambda i: (i, 0))],
        out_specs=pl.BlockSpec((block_rows, cols), lambda i: (i, 0)),
        out_shape=jax.ShapeDtypeStruct((rows, cols), BF16),
        compiler_params=_params("parallel"),
        name="cast_bf16",
    )(w2d)


def _inproj_kernel(x_ref, ln_ref, w_ref, qg_ref, kg_ref, avg_ref, *refs, dilated):
    k_ref, v_ref, u_ref, q_ref, kb_ref, vb_ref = refs[:6]
    tm = x_ref.shape[0]
    x = x_ref[...]
    ms = jnp.mean(x * x, axis=-1, keepdims=True)
    h = (x * lax.rsqrt(ms + EPS) * ln_ref[...]).astype(BF16)
    n_chunks = ATTN_WIDTH // MXU_DIM

    def section(s):
        return jnp.dot(h, w_ref[:, s * ATTN_WIDTH:(s + 1) * ATTN_WIDTH],
                       preferred_element_type=F32)

    def head_norm(z, g_ref, c):
        zc = z[:, c * MXU_DIM:(c + 1) * MXU_DIM]
        msh = jnp.dot((zc * zc).astype(BF16), avg_ref[...], preferred_element_type=F32)
        return zc * lax.rsqrt(msh + EPS) * g_ref[:, c * MXU_DIM:(c + 1) * MXU_DIM]

    def emit(chunks, which, f32_ref, bf_ref):
        heads_per_chunk = MXU_DIM // HEAD_DIM
        for c, zc in enumerate(chunks):
            cs = slice(c * MXU_DIM, (c + 1) * MXU_DIM)
            bf_ref[:, cs] = zc.astype(BF16)
            if f32_ref is None:
                continue
            if dilated:
                zt = zc.T
                for j in range(heads_per_chunk):
                    f32_ref[c * heads_per_chunk + j] = zt[j * HEAD_DIM:(j + 1) * HEAD_DIM, :]
            else:
                f32_ref[:, cs] = zc
        if not dilated:
            return
        stage_ref = refs[-1]
        for c, zc in enumerate(chunks):
            for half in range(MXU_DIM // LANES):
                stage_ref[2 * c + half] = zc[:, half * LANES:(half + 1) * LANES]
        for bi, (_, dil) in enumerate(DILATED[1:]):
            out_ref = refs[6 + 3 * bi + which]
            n = tm // dil
            for r in range(dil):
                for s in range(ATTN_WIDTH // LANES):
                    rows = stage_ref[s, pl.ds(r, n, stride=dil), :]
                    out_ref[r, :, s * LANES:(s + 1) * LANES] = rows.astype(BF16)

    zq = section(0)
    emit([head_norm(zq, qg_ref, c) for c in range(n_chunks)], 0, None, q_ref)
    zk = section(1)
    emit([head_norm(zk, kg_ref, c) for c in range(n_chunks)], 1, k_ref, kb_ref)
    zv = section(2)
    emit([zv[:, c * MXU_DIM:(c + 1) * MXU_DIM] for c in range(n_chunks)], 2, v_ref, vb_ref)
    u_ref[...] = section(3)


def _inproj(x2d, ln1, w_in_b, q_gain, k_gain, avg, tm, batch=None, seq=None):
    n, d = x2d.shape
    dilated = seq is not None
    row = lambda i: (i, 0)
    wide = pl.BlockSpec((tm, ATTN_WIDTH), row)
    f32o = jax.ShapeDtypeStruct((n, ATTN_WIDTH), F32)
    bf16o = jax.ShapeDtypeStruct((n, ATTN_WIDTH), BF16)
    out_specs = [wide] * 6
    out_shape = [f32o, f32o, f32o, bf16o, bf16o, bf16o]
    scratch = []
    if dilated:
        tiles = seq // tm
        kv_t = pl.BlockSpec((None, N_HEADS, HEAD_DIM, tm), lambda i: (i // tiles, 0, 0, i % tiles))
        out_specs[:2] = [kv_t, kv_t]
        out_shape[:2] = [jax.ShapeDtypeStruct((batch, N_HEADS, HEAD_DIM, seq), F32)] * 2
        for _, dil in DILATED[1:]:
            spec = pl.BlockSpec((None, dil, tm // dil, ATTN_WIDTH),
                                lambda i: (i // tiles, 0, i % tiles, 0))
            out_specs += [spec] * 3
            out_shape += [jax.ShapeDtypeStruct((batch, dil, seq // dil, ATTN_WIDTH), BF16)] * 3
        scratch = [pltpu.VMEM((ATTN_WIDTH // LANES, tm, LANES), F32)]
    return pl.pallas_call(
        functools.partial(_inproj_kernel, dilated=dilated),
        grid=(n // tm,),
        in_specs=[pl.BlockSpec((tm, d), row),
                  _const_spec((1, d)),
                  pl.BlockSpec(w_in_b.shape, lambda i: (0, 0), pipeline_mode=pl.Buffered(1)),
                  _const_spec((1, ATTN_WIDTH)), _const_spec((1, ATTN_WIDTH)),
                  _const_spec((MXU_DIM, MXU_DIM))],
        out_specs=out_specs,
        out_shape=out_shape,
        scratch_shapes=scratch,
        compiler_params=_params("parallel"),
        name="inproj",
    )(x2d, ln1, w_in_b, q_gain, k_gain, avg)


def _attn_kernel(q_ref, kp_ref, kc_ref, vp_ref, vc_ref, bias_ref, o_ref, lse_ref,
                 s_ref, p_ref):
    i = pl.program_id(2)
    lane = lax.broadcasted_iota(jnp.int32, (QBLOCK, LANES), 1)
    low = lane < HEAD_DIM
    keep_low = low.astype(F32).astype(BF16)
    keep_high = (1.0 - low.astype(F32)).astype(BF16)
    nt = (((1,), (1,)), ((), ()))
    n_pairs = N_HEADS // 2

    def run(blk, with_prev):
        rq = slice(blk * QBLOCK, (blk + 1) * QBLOCK)
        before = slice((blk - 1) * QBLOCK, blk * QBLOCK)
        k0 = 0 if with_prev else QBLOCK
        nk = 2 * QBLOCK - k0
        ones = jnp.ones((nk, LANES), BF16)
        for hp in range(n_pairs):
            cs = slice(hp * LANES, (hp + 1) * LANES)
            qp = q_ref[rq, cs]
            q2 = jnp.concatenate([qp * keep_low, qp * keep_high], axis=0)
            keys = kc_ref[rq, cs]
            if with_prev:
                older = kp_ref[:, cs] if blk == 0 else kc_ref[before, cs]
                keys = jnp.concatenate([older, keys], axis=0)
            s2 = lax.dot_general(q2, keys, nt, preferred_element_type=F32)
            s_ref[hp, :, k0:] = s2 + bias_ref[hp, :, k0:]

        m_all = jnp.zeros((QBLOCK, LANES), F32)
        for hp in range(n_pairs):
            for sub in range(2):
                rows = slice(sub * QBLOCK, (sub + 1) * QBLOCK)
                m = jnp.max(s_ref[hp, rows, k0:], axis=-1, keepdims=True)
                p_ref[hp, rows, k0:] = jnp.exp(s_ref[hp, rows, k0:] - m).astype(BF16)
                m_all = jnp.where(lane == 2 * hp + sub, m, m_all)

        den_all = jnp.ones((QBLOCK, LANES), F32)
        for hp in range(n_pairs):
            cs = slice(hp * LANES, (hp + 1) * LANES)
            vals = vc_ref[rq, cs]
            if with_prev:
                older = vp_ref[:, cs] if blk == 0 else vc_ref[before, cs]
                vals = jnp.concatenate([older, vals], axis=0)
            r = jnp.dot(p_ref[hp, :, k0:], jnp.concatenate([vals, ones], axis=1),
                        preferred_element_type=F32)
            den0, den1 = r[:QBLOCK, LANES:], r[QBLOCK:, LANES:]
            o_ref[rq, cs] = jnp.where(low, r[:QBLOCK, :LANES] / den0, r[QBLOCK:, :LANES] / den1)
            den_all = jnp.where(lane == 2 * hp, den0, den_all)
            den_all = jnp.where(lane == 2 * hp + 1, den1, den_all)
        lse_ref[rq, :] = m_all + jnp.log(den_all)

    @pl.when(i == 0)
    def _():
        run(0, False)

    @pl.when(i > 0)
    def _():
        run(0, True)

    for blk in range(1, q_ref.shape[0] // QBLOCK):
        run(blk, True)


def _attn_branch(q, kb, vb, bias_all, branch):
    batch, dil, sub, _ = q.shape
    qb = min(ATTN_BLOCKS_PER_STEP, sub // QBLOCK)
    cur = lambda b, r, i: (b, r, i, 0)
    prev = lambda b, r, i: (b, r, jnp.maximum(i * qb - 1, 0), 0)
    wide_c = pl.BlockSpec((None, None, qb * QBLOCK, ATTN_WIDTH), cur)
    wide_p = pl.BlockSpec((None, None, QBLOCK, ATTN_WIDTH), prev)
    pairs = N_HEADS // 2
    bias_spec = pl.BlockSpec((None, pairs, 2 * QBLOCK, 2 * QBLOCK), lambda b, r, i: (branch, 0, 0, 0))
    return pl.pallas_call(
        _attn_kernel,
        grid=(batch, dil, sub // (qb * QBLOCK)),
        in_specs=[wide_c, wide_p, wide_c, wide_p, wide_c, bias_spec],
        out_specs=[wide_c, pl.BlockSpec((None, None, qb * QBLOCK, LANES), cur)],
        out_shape=[jax.ShapeDtypeStruct((batch, dil, sub, ATTN_WIDTH), F32),
                   jax.ShapeDtypeStruct((batch, dil, sub, LANES), F32)],
        scratch_shapes=[pltpu.VMEM((pairs, 2 * QBLOCK, 2 * QBLOCK), F32),
                        pltpu.VMEM((pairs, 2 * QBLOCK, 2 * QBLOCK), BF16)],
        compiler_params=_params("parallel", "parallel", "parallel"),
        name=f"attn_d{dil}",
    )(q, kb, kb, vb, vb, bias_all)


def _bias_kernel(base_ref, out_ref):
    width = base_ref.shape[1]
    for h in range(N_HEADS):
        rows = jnp.broadcast_to(base_ref[h:h + 1, :], (QBLOCK, width))
        band = pltpu.roll(rows, 0, 1, stride=1, stride_axis=0)
        out_ref[h // 2, (h % 2) * QBLOCK:(h % 2 + 1) * QBLOCK, :] = band[:, :2 * QBLOCK]


def _prompt_bias(rel_bias):
    width = 3 * QBLOCK
    steps = QBLOCK - np.arange(width)
    valid = (steps >= 0) & (steps <= N_STEPS)
    buckets = np.stack([_bucket_table(dil)[np.clip(steps, 0, N_STEPS)] for _, dil in DILATED])
    base = jnp.where(valid[None, :, None], rel_bias.astype(F32)[buckets], NEG_INF)
    base = jnp.transpose(base, (0, 2, 1))
    pairs = N_HEADS // 2
    return pl.pallas_call(
        _bias_kernel,
        grid=(len(DILATED),),
        in_specs=[pl.BlockSpec((None, N_HEADS, width), lambda g: (g, 0, 0))],
        out_specs=pl.BlockSpec((None, pairs, 2 * QBLOCK, 2 * QBLOCK), lambda g: (g, 0, 0, 0)),
        out_shape=jax.ShapeDtypeStruct((len(DILATED), pairs, 2 * QBLOCK, 2 * QBLOCK), F32),
        compiler_params=_params("parallel"),
        name="bias_table",
    )(base)


def _split_dot(a, b_bf16):
    hi = a.astype(BF16)
    lo = (a - hi.astype(F32)).astype(BF16)
    return (jnp.dot(hi, b_bf16, preferred_element_type=F32)
            + jnp.dot(lo, b_bf16, preferred_element_type=F32))


def _pool_groups(comb, u, cnt_fn, wp_ref, ps_ref):
    t = u.shape[0]
    outs = []
    run = comb
    width = 1
    for g, w in enumerate(POOL_WINDOWS):
        while width < w:
            run = run + pltpu.roll(run, width, 0)
            width *= 2
        cs = slice(g * POOL_GROUP_WIDTH, (g + 1) * POOL_GROUP_WIDTH)
        d = run[POOL_HALO:POOL_HALO + t, cs] / cnt_fn(w) - u[:, cs]
        y = jnp.dot(d.astype(BF16), wp_ref[g], preferred_element_type=F32)
        outs.append(y * ps_ref[:, cs])
    return outs


def _mix_kernel(o1_ref, o2_ref, o3_ref, l1_ref, l2_ref, l3_ref, u_ref, halo_ref,
                ex_ref, wp_ref, ps_ref, mix_ref, il_ref, ls_ref, *, seq):
    tm = u_ref.shape[0]
    n_slabs = ATTN_WIDTH // LANES
    lses = [l1_ref[0]]
    for bi, (o_ref, l_ref) in enumerate(((o2_ref, l2_ref), (o3_ref, l3_ref))):
        dil = o_ref.shape[0]
        n = tm // dil
        for r in range(dil):
            ls_ref[bi, pl.ds(r, n, stride=dil), :] = l_ref[r]
            for s in range(n_slabs):
                il_ref[bi, s, pl.ds(r, n, stride=dil), :] = o_ref[r, :, s * LANES:(s + 1) * LANES]
        lses.append(ls_ref[bi])
    l1, l2, l3 = lses
    m = jnp.maximum(jnp.maximum(l1, l2), l3)
    e1, e2, e3 = jnp.exp(l1 - m), jnp.exp(l2 - m), jnp.exp(l3 - m)
    inv = 1.0 / (e1 + e2 + e3)
    ex = ex_ref[...]
    w1, w2, w3 = (_split_dot(e * inv, ex) for e in (e1, e2, e3))
    for s in range(n_slabs):
        cs = slice(s * LANES, (s + 1) * LANES)
        attn = w1[:, cs] * o1_ref[0, :, cs] + w2[:, cs] * il_ref[0, s] + w3[:, cs] * il_ref[1, s]
        mix_ref[:, cs] = attn.astype(BF16)

    pos0 = (pl.program_id(0) * tm) % seq
    u = u_ref[...]
    halo = jnp.where(pos0 == 0, 0.0, halo_ref[...])
    comb = jnp.concatenate([halo, u], axis=0)
    pos = pos0 + lax.broadcasted_iota(jnp.int32, (tm, 1), 0)
    cnt_fn = lambda w: jnp.minimum(pos + 1, w).astype(F32)
    for g, y in enumerate(_pool_groups(comb, u, cnt_fn, wp_ref, ps_ref)):
        lo = ATTN_WIDTH + g * POOL_GROUP_WIDTH
        mix_ref[:, lo:lo + POOL_GROUP_WIDTH] = y.astype(BF16)


def _mix(o_list, l_list, u, expand, wp_b, pool_scale, seq, tm):
    n = u.shape[0]
    tiles = seq // tm
    row = lambda i: (i, 0)
    res = lambda i: (i // tiles, 0, i % tiles, 0)
    o_specs = [pl.BlockSpec((None, o.shape[1], tm // o.shape[1], ATTN_WIDTH), res) for o in o_list]
    l_specs = [pl.BlockSpec((None, l.shape[1], tm // l.shape[1], LANES), res) for l in l_list]
    halo = pl.BlockSpec((POOL_HALO, ATTN_WIDTH),
                        lambda i: (jnp.maximum(i * (tm // POOL_HALO) - 1, 0), 0))
    n_dilated = len(o_list) - 1
    return pl.pallas_call(
        functools.partial(_mix_kernel, seq=seq),
        grid=(n // tm,),
        in_specs=o_specs + l_specs + [pl.BlockSpec((tm, ATTN_WIDTH), row), halo,
                                      _const_spec(expand.shape), _const_spec(wp_b.shape),
                                      _const_spec((1, ATTN_WIDTH))],
        out_specs=pl.BlockSpec((tm, 2 * ATTN_WIDTH), row),
        out_shape=jax.ShapeDtypeStruct((n, 2 * ATTN_WIDTH), BF16),
        scratch_shapes=[pltpu.VMEM((n_dilated, ATTN_WIDTH // LANES, tm, LANES), F32),
                        pltpu.VMEM((n_dilated, tm, LANES), F32)],
        compiler_params=_params("parallel"),
        name="mix",
    )(*o_list, *l_list, u, u, expand, wp_b, pool_scale)


def _sample_attn_kernel(q_ref, kn_ref, vn_ref, k_ref, v_ref, bc_ref, bn_ref, o_ref):
    n_heads, t_new, _ = q_ref.shape
    n_br = bc_ref.shape[0]
    lane_max = lambda a: jnp.max(a, axis=1, keepdims=True)
    lane_sum = lambda a: jnp.sum(a, axis=1, keepdims=True)
    new_lane = lax.broadcasted_iota(jnp.int32, (t_new, t_new), 1)
    nt = (((1,), (1,)), ((), ()))
    for h in range(n_heads):
        q, kn, vn = q_ref[h], kn_ref[h], vn_ref[h]
        s_c = jnp.dot(q.astype(BF16), k_ref[h].astype(BF16), preferred_element_type=F32)
        s_n = jnp.zeros((t_new, t_new), F32)
        for tp in range(t_new):
            s_n = jnp.where(new_lane == tp, lane_sum(q * kn[tp:tp + 1, :]), s_n)
        sc = [s_c + bc_ref[br, h] for br in range(n_br)]
        sn = [s_n + bn_ref[br, h] for br in range(n_br)]
        m = functools.reduce(jnp.maximum, [lane_max(a) for a in sc + sn])
        p_c = functools.reduce(jnp.add, [jnp.exp(a - m) for a in sc])
        p_n = functools.reduce(jnp.add, [jnp.exp(a - m) for a in sn])
        acc = lax.dot_general(p_c.astype(BF16), v_ref[h].astype(BF16), nt,
                              preferred_element_type=F32)
        for tp in range(t_new):
            acc = acc + p_n[:, tp:tp + 1] * vn[tp:tp + 1, :]
        o_ref[h] = acc / (lane_sum(p_c) + lane_sum(p_n))


def _sample_attn(q, kn, vn, k_t, v_t, bias_c, bias_n, heads_per_step):
    nb, nh, t_new, hd = q.shape
    win = k_t.shape[-1]
    hg = heads_per_step
    n_br = bias_c.shape[0]
    small = pl.BlockSpec((None, hg, t_new, hd), lambda b, g: (b, g, 0, 0))
    wide = pl.BlockSpec((None, hg, hd, win), lambda b, g: (b, g, 0, 0))
    return pl.pallas_call(
        _sample_attn_kernel,
        grid=(nb, nh // hg),
        in_specs=[small, small, small, wide, wide,
                  pl.BlockSpec((n_br, hg, t_new, win), lambda b, g: (0, g, 0, 0)),
                  pl.BlockSpec((n_br, hg, t_new, t_new), lambda b, g: (0, g, 0, 0))],
        out_specs=small,
        out_shape=jax.ShapeDtypeStruct((nb, nh, t_new, hd), F32),
        compiler_params=_params("parallel", "parallel"),
        name="sample_attn",
    )(q, kn, vn, k_t, v_t, bias_c, bias_n)


def _sample_pool_kernel(u_ref, st_ref, wp_ref, ps_ref, pool_ref, sbuf_ref, comb_ref, *, start):
    t_new = u_ref.shape[0]
    u = u_ref[...]
    comb_ref[...] = jnp.zeros_like(comb_ref)
    comb_ref[1:POOL_HALO, :] = st_ref[...]
    comb_ref[POOL_HALO:POOL_HALO + t_new, :] = u
    pos = start + lax.broadcasted_iota(jnp.int32, (t_new, 1), 0)
    cnt_fn = lambda w: jnp.minimum(pos + 1, w).astype(F32)
    for g, y in enumerate(_pool_groups(comb_ref[...], u, cnt_fn, wp_ref, ps_ref)):
        pool_ref[:, g * POOL_GROUP_WIDTH:(g + 1) * POOL_GROUP_WIDTH] = y
    sbuf_ref[:POOL_BUF - t_new, :] = st_ref[t_new:, :]
    sbuf_ref[POOL_BUF - t_new:, :] = u


def _sample_pool(u, state, wp_b, pool_scale):
    nb, t_new, w = u.shape
    flat = lambda rows: pl.BlockSpec((None, rows, w), lambda b: (b, 0, 0))
    return pl.pallas_call(
        functools.partial(_sample_pool_kernel, start=PAST_LEN),
        grid=(nb,),
        in_specs=[flat(t_new), flat(POOL_BUF), _const_spec(wp_b.shape), _const_spec((1, w))],
        out_specs=[flat(t_new), flat(POOL_BUF)],
        out_shape=[jax.ShapeDtypeStruct((nb, t_new, w), F32),
                   jax.ShapeDtypeStruct((nb, POOL_BUF, w), F32)],
        scratch_shapes=[pltpu.VMEM((POOL_HALO + 8, w), F32)],
        compiler_params=_params("parallel"),
        name="sample_pool",
    )(u, state, wp_b, pool_scale)


def _outproj_kernel(x_ref, mix_ref, wo_ref, ln_ref, wr_ref, br_ref, tri_ref, cin_ref,
                    x1_ref, h_ref, gate_ref, route_ref, cout_ref, count_ref):
    @pl.when(pl.program_id(0) == 0)
    def _():
        count_ref[...] = cin_ref[...]

    x1 = x_ref[...] + jnp.dot(mix_ref[...].astype(BF16), wo_ref[...],
                              preferred_element_type=F32)
    x1_ref[...] = x1
    ms = jnp.mean(x1 * x1, axis=-1, keepdims=True)
    h = x1 * lax.rsqrt(ms + EPS) * ln_ref[...]
    h_ref[...] = h
    lg = jnp.dot(h.astype(BF16), wr_ref[...], preferred_element_type=F32) + br_ref[...]

    lane = lax.broadcasted_iota(jnp.int32, lg.shape, 1).astype(F32)
    big = float(LANES)
    row_max = lambda mask: jnp.max(jnp.where(mask, lg, -jnp.inf), axis=-1, keepdims=True)
    first = lambda mask: jnp.min(jnp.where(mask, lane, big), axis=-1, keepdims=True)
    is_g = lane < N_GROUPS
    mg = row_max(is_g)
    g_top = first(jnp.logical_and(is_g, lg == mg))
    den = jnp.sum(jnp.where(is_g, jnp.exp(lg - mg), 0.0), axis=-1, keepdims=True)
    p_top = 1.0 / den
    base = N_GROUPS + EXPERTS_PER_GROUP * g_top
    in_grp = jnp.logical_and(lane >= base, lane < base + EXPERTS_PER_GROUP)
    v1 = row_max(in_grp)
    i1 = first(jnp.logical_and(in_grp, lg == v1))
    rest = jnp.logical_and(in_grp, lane != i1)
    v2 = row_max(rest)
    i2 = first(jnp.logical_and(rest, lg == v2))
    e21 = jnp.exp(v2 - v1)
    s21 = 1.0 + e21
    gate1 = p_top * (1.0 / s21)
    gate2 = p_top * (e21 / s21)
    gate_ref[...] = jnp.where(lane == 0.0, gate1, jnp.where(lane == 1.0, gate2, 0.0))
    e1, e2 = i1 - N_GROUPS, i2 - N_GROUPS

    hot1 = (lane == e1).astype(F32)
    hot2 = (lane == e2).astype(F32)
    hot = hot1 + hot2
    before = count_ref[...] + jnp.dot(tri_ref[...], hot.astype(BF16), preferred_element_type=F32)
    rank1 = jnp.sum(hot1 * before, axis=-1, keepdims=True)
    rank2 = jnp.sum(hot2 * before, axis=-1, keepdims=True)
    count_ref[...] = count_ref[...] + jnp.sum(hot, axis=0, keepdims=True)
    cout_ref[...] = count_ref[...]

    cols = jnp.zeros_like(lg)
    for r, col in enumerate((e1, e2, rank1, rank2)):
        cols = jnp.where(lane == float(r), col, cols)
    route_ref[...] = cols.T[:route_ref.shape[0], :]


def _outproj(x2d, mix, w_o_b, ln2, w_r, b_r, tri, counts_in, tm):
    n, d = x2d.shape
    row = lambda i: (i, 0)
    full = pl.BlockSpec((tm, d), row)
    stat = pl.BlockSpec((tm, LANES), row)
    route_rows = 8
    return pl.pallas_call(
        _outproj_kernel,
        grid=(n // tm,),
        in_specs=[full, full,
                  pl.BlockSpec(w_o_b.shape, lambda i: (0, 0), pipeline_mode=pl.Buffered(1)),
                  _const_spec((1, d)), _const_spec(w_r.shape),
                  _const_spec((1, LANES)), _const_spec((tm, tm)), _const_spec((1, LANES))],
        out_specs=[full, full, stat,
                   pl.BlockSpec((route_rows, tm), lambda i: (0, i)), _const_spec((1, LANES))],
        out_shape=[jax.ShapeDtypeStruct((n, d), F32),
                   jax.ShapeDtypeStruct((n, d), F32),
                   jax.ShapeDtypeStruct((n, LANES), F32),
                   jax.ShapeDtypeStruct((route_rows, n), F32),
                   jax.ShapeDtypeStruct((1, LANES), F32)],
        scratch_shapes=[pltpu.VMEM((1, LANES), F32)],
        compiler_params=_params("arbitrary"),
        name="outproj",
    )(x2d, mix, w_o_b, ln2, w_r, b_r, tri[:tm, :tm], counts_in)


def _row_copy(src_hbm, dst_vmem, sem, src_row, dst_row):
    return pltpu.make_async_copy(src_hbm.at[pl.ds(src_row, 1)],
                                 dst_vmem.at[pl.ds(dst_row, 1)], sem)


def _dispatch_kernel(pos_ref, pos_s_ref, pad_start_ref, pad_len_ref, tail_ref, h_ref, h_s_ref,
                     hs_ref, sem_ref, zero_ref, *, tile_rows):
    i = pl.program_id(0)
    sem = sem_ref.at[0]

    def scatter(src_ref, dst_rows_ref):
        n_rows = src_ref.shape[0]

        def put(r, k):
            return _row_copy(src_ref, hs_ref, sem, r, dst_rows_ref[0, k * n_rows + r])

        def issue(r, c):
            for k in range(TOP_K):
                put(r, k).start()
            return c

        def drain(r, c):
            for k in range(TOP_K):
                put(r, k).wait()
            return c

        lax.fori_loop(0, n_rows, issue, 0, unroll=DMA_UNROLL)
        lax.fori_loop(0, n_rows, drain, 0, unroll=DMA_UNROLL)

    scatter(h_ref, pos_ref)

    @pl.when(i == pl.num_programs(0) - 1)
    def _():
        scatter(h_s_ref, pos_s_ref)
        zero_ref[...] = jnp.zeros_like(zero_ref)

        sizes = [s for s in (tile_rows >> (b + 1) for b in range(tile_rows.bit_length()))
                 if s >= SUBLANES]

        def pad_copies(act):
            def single_rows(first, count):
                def body(r, c):
                    act(_row_copy(zero_ref, hs_ref, sem, 0, first + r))
                    return c
                lax.fori_loop(0, count, body, 0)

            for e in range(N_EXPERTS):
                start, length = pad_start_ref[e], pad_len_ref[e]
                head = jnp.minimum((-start) & (SUBLANES - 1), length)
                single_rows(start, head)
                body_len = length - head
                aligned = body_len & -SUBLANES
                offset = start + head
                for b, size in enumerate(sizes):
                    @pl.when((aligned & size) != 0)
                    def _():
                        rows = pl.ds(pl.multiple_of(offset, SUBLANES), size)
                        act(pltpu.make_async_copy(zero_ref.at[pl.ds(0, size)], hs_ref.at[rows],
                                                  sem_ref.at[1 + b]))
                    offset = offset + (aligned & size)
                single_rows(offset, body_len - aligned)

        pad_copies(lambda copy: copy.start())
        pad_copies(lambda copy: copy.wait())

        n_tiles = hs_ref.shape[0] // tile_rows

        def fill_tile(t):
            rows = pl.ds(pl.multiple_of(t * tile_rows, tile_rows), tile_rows)
            return pltpu.make_async_copy(zero_ref, hs_ref.at[rows], sem)

        def start_tile(t, c):
            fill_tile(t).start()
            return c

        def wait_tile(t, c):
            fill_tile(t).wait()
            return c

        lax.fori_loop(tail_ref[0], n_tiles, start_tile, 0)
        lax.fori_loop(tail_ref[0], n_tiles, wait_tile, 0)


def _tile_rows_spec(tm):
    return pl.BlockSpec((None, 1, TOP_K * tm), lambda i: (i, 0, 0), memory_space=pltpu.SMEM)


def _dispatch(rows_p, rows_s, pad_start, pad_len, tail, h, h_s, n_tiles, tile_rows, tm):
    n, d = h.shape
    smem = pl.BlockSpec(memory_space=pltpu.SMEM)
    return pl.pallas_call(
        functools.partial(_dispatch_kernel, tile_rows=tile_rows),
        grid=(n // tm,),
        in_specs=[_tile_rows_spec(tm), smem, smem, smem, smem,
                  pl.BlockSpec((tm, d), lambda i: (i, 0)),
                  pl.BlockSpec(h_s.shape, lambda i: (0, 0))],
        out_specs=pl.BlockSpec(memory_space=pl.ANY),
        out_shape=jax.ShapeDtypeStruct((n_tiles * tile_rows, d), h.dtype),
        scratch_shapes=[pltpu.SemaphoreType.DMA((tile_rows.bit_length(),)),
                        pltpu.VMEM((tile_rows, d), h.dtype)],
        compiler_params=_params("arbitrary", unchecked=True),
        name="dispatch",
    )(rows_p, rows_s, pad_start, pad_len, tail, h, h_s)


def _expert_kernel(tile_e_ref, tile_on_ref, next_e_ref, hs_ref, wg_hbm, wu_hbm, wd_hbm, y_ref,
                   wg_f, wu_f, wd_f, wg_b, wu_b, wd_b, sem_ref, slot_ref):
    i = pl.program_id(0)

    def fetch(expert, slot):
        return [pltpu.make_async_copy(src.at[expert], dst.at[slot], sem_ref.at[slot])
                for src, dst in ((wg_hbm, wg_f), (wu_hbm, wu_f), (wd_hbm, wd_f))]

    @pl.when(i == 0)
    def _():
        slot_ref[0] = 0
        for copy in fetch(tile_e_ref[0], 0):
            copy.start()

    @pl.when(jnp.logical_or(i == 0, tile_e_ref[i] != tile_e_ref[jnp.maximum(i - 1, 0)]))
    def _():
        slot = slot_ref[0]
        for copy in fetch(tile_e_ref[i], slot):
            copy.wait()
        wg_b[...] = wg_f[slot].astype(BF16)
        wu_b[...] = wu_f[slot].astype(BF16)
        wd_b[...] = wd_f[slot].astype(BF16)

        @pl.when(next_e_ref[i] >= 0)
        def _():
            for copy in fetch(next_e_ref[i], 1 - slot):
                copy.start()

        slot_ref[0] = 1 - slot

    @pl.when(tile_on_ref[i] == 1)
    def _():
        hb = hs_ref[...].astype(BF16)
        a = jnp.dot(hb, wg_b[...], preferred_element_type=F32)
        b = jnp.dot(hb, wu_b[...], preferred_element_type=F32)
        hid = a * jax.nn.sigmoid(a) * b
        y_ref[...] = jnp.dot(hid.astype(BF16), wd_b[...], preferred_element_type=F32)

    @pl.when(tile_on_ref[i] == 0)
    def _():
        y_ref[...] = jnp.zeros_like(y_ref)


def _experts(tile_e, tile_on, next_e, hs, w_gate, w_up, w_down, tm):
    n_tiles = tile_e.shape[0]
    d, f = w_gate.shape[1:]
    any_spec = pl.BlockSpec(memory_space=pl.ANY)
    grid_spec = pltpu.PrefetchScalarGridSpec(
        num_scalar_prefetch=3,
        grid=(n_tiles,),
        in_specs=[pl.BlockSpec((tm, d), lambda i, *_: (i, 0)), any_spec, any_spec, any_spec],
        out_specs=pl.BlockSpec((tm, d), lambda i, *_: (i, 0)),
        scratch_shapes=[pltpu.VMEM((2, d, f), F32), pltpu.VMEM((2, d, f), F32),
                        pltpu.VMEM((2, f, d), F32),
                        pltpu.VMEM((d, f), BF16), pltpu.VMEM((d, f), BF16),
                        pltpu.VMEM((f, d), BF16),
                        pltpu.SemaphoreType.DMA((2,)), pltpu.SMEM((1,), jnp.int32)],
    )
    return pl.pallas_call(
        _expert_kernel,
        grid_spec=grid_spec,
        out_shape=jax.ShapeDtypeStruct((n_tiles * tm, d), F32),
        compiler_params=_params("arbitrary"),
        name="experts",
    )(tile_e, tile_on, next_e, hs, w_gate, w_up, w_down)


def _combine_kernel(rows_ref, next_rows_ref, x1_ref, gate_ref, ys_hbm, y_ref, buf_ref, sem_ref):
    i = pl.program_id(0)
    n_tiles = pl.num_programs(0)
    tm = y_ref.shape[0]

    def issue(src_rows_ref, slot):
        def body(r, c):
            for k in range(TOP_K):
                _row_copy(ys_hbm, buf_ref.at[slot, k], sem_ref.at[slot],
                          src_rows_ref[0, k * tm + r], r).start()
            return c
        lax.fori_loop(0, tm, body, 0, unroll=DMA_UNROLL)

    def drain(slot):
        def body(r, c):
            for k in range(TOP_K):
                _row_copy(ys_hbm, buf_ref.at[slot, k], sem_ref.at[slot], 0, r).wait()
            return c
        lax.fori_loop(0, tm, body, 0, unroll=DMA_UNROLL)

    slot = i % 2

    @pl.when(i == 0)
    def _():
        issue(rows_ref, 0)

    drain(slot)

    @pl.when(i + 1 < n_tiles)
    def _():
        issue(next_rows_ref, 1 - slot)

    gate = gate_ref[...]
    y_ref[...] = (x1_ref[...] + gate[:, 0:1] * buf_ref[slot, 0]
                  + gate[:, 1:2] * buf_ref[slot, 1])


def _combine(rows, x1, gate, ys, tm):
    n, d = x1.shape
    last = n // tm - 1
    next_spec = pl.BlockSpec((None, 1, TOP_K * tm), lambda i: (jnp.minimum(i + 1, last), 0, 0),
                             memory_space=pltpu.SMEM)
    return pl.pallas_call(
        _combine_kernel,
        grid=(n // tm,),
        in_specs=[_tile_rows_spec(tm), next_spec,
                  pl.BlockSpec((tm, d), lambda i: (i, 0)),
                  pl.BlockSpec((tm, LANES), lambda i: (i, 0)),
                  pl.BlockSpec(memory_space=pl.ANY)],
        out_specs=pl.BlockSpec((tm, d), lambda i: (i, 0)),
        out_shape=jax.ShapeDtypeStruct((n, d), F32),
        scratch_shapes=[pltpu.VMEM((2, TOP_K, tm, d), F32), pltpu.SemaphoreType.DMA((2,))],
        compiler_params=_params("arbitrary", unchecked=True),
        name="combine",
    )(rows, rows, x1, gate, ys)


def _bucket_table(dilation):
    dist = np.arange(N_STEPS + 1, dtype=np.int64) * dilation
    max_exact = NUM_BUCKETS // 2
    df = np.maximum(dist, 1).astype(np.float32)
    large = max_exact + (np.log(df / np.float32(max_exact))
                         / np.float32(math.log(MAX_DISTANCE / max_exact))
                         * np.float32(NUM_BUCKETS - max_exact)).astype(np.int32)
    large = np.minimum(large, NUM_BUCKETS - 1)
    return np.where(dist < max_exact, dist, large).astype(np.int32)


def _sample_bias(rel_bias, t_new, win):
    buckets = np.stack([_bucket_table(dil) for _, dil in DILATED])
    by_step = jnp.transpose(rel_bias.astype(F32)[buckets], (0, 2, 1))
    nh = by_step.shape[1]
    neg = lambda *shape: jnp.full(shape, NEG_INF, F32)
    t = np.arange(t_new)
    cached, fresh = [], []
    for br, (_, dil) in enumerate(DILATED):
        rev = by_step[br, :, ::-1][:, :N_STEPS]
        if dil == 1:
            rows = [jnp.concatenate([neg(nh, win - N_STEPS + q), rev[:, :N_STEPS - q]], axis=1)
                    for q in range(t_new)]
            cached.append(jnp.stack(rows, axis=1))
        else:
            own = (t[:, None, None] == np.arange(dil)[None, None, :])
            band = jnp.where(own[None], rev[:, None, :, None], NEG_INF)
            band = band.reshape(nh, t_new, dil * N_STEPS)
            cached.append(jnp.concatenate([neg(nh, t_new, win - dil * N_STEPS), band], axis=2))
        step = t[:, None] - t[None, :]
        ok = (step >= 0) & (step % dil == 0)
        vals = by_step[br][:, np.where(ok, step // dil, 0)]
        fresh.append(jnp.where(ok[None], vals, NEG_INF))
    return jnp.stack(cached), jnp.stack(fresh)


def _tile_plan(counts, n_tiles, tm):
    tiles_e = (counts + tm - 1) // tm
    ends = jnp.cumsum(tiles_e)
    first_row = (ends - tiles_e) * tm
    tile = jnp.arange(n_tiles, dtype=jnp.int32)
    tile_e = jnp.sum((ends[None, :] <= tile[:, None]).astype(jnp.int32), axis=1)
    tile_on = (tile_e < N_EXPERTS).astype(jnp.int32)
    tile_e = jnp.minimum(tile_e, N_EXPERTS - 1)
    later = jnp.where(tile_e[None, :] > tile_e[:, None], tile_e[None, :], N_EXPERTS)
    next_e = jnp.min(later, axis=1)
    next_e = jnp.where(next_e == N_EXPERTS, -1, next_e)
    i32 = lambda a: a.astype(jnp.int32)
    return (i32(tile_e), tile_on, i32(next_e), i32(first_row), i32(first_row + counts),
            i32(tiles_e * tm - counts), i32(ends[-1:]))


TM_PROJ = 256
TM_TOKEN = 512
TM_EXPERT = 256
SAMPLE_HEADS_PER_STEP = 8
ATTN_BLOCKS_PER_STEP = 4


def kernel(x_prompt, x_sample, cache_k, cache_v, state_pool, rel_bias, ln1_w, w_in,
           q_norm_w, k_norm_w, w_pool, pool_scale, w_o, ln2_w, w_router_group,
           b_router_group, w_router_expert, b_router_expert, w_gate, w_up, w_down):
    depth = w_in.shape[0]
    assert depth == 1
    batch, seq, d_model = x_prompt.shape
    nb, t_new, _ = x_sample.shape
    win = cache_k.shape[2]
    f_exp = w_gate.shape[-1]

    w_in_b = _to_bf16(w_in[0], 256)
    w_o_b = _to_bf16(w_o[0], 256)
    wp_b = _to_bf16(w_pool[0].reshape(-1, POOL_GROUP_WIDTH), 256).reshape(w_pool.shape[1:])
    ln1 = ln1_w[0][None, :]
    ln2 = ln2_w[0][None, :]
    q_gain = jnp.tile(q_norm_w[0], N_HEADS)[None, :] * SCALE
    k_gain = jnp.tile(k_norm_w[0], N_HEADS)[None, :]
    ps = pool_scale[0][None, :]
    blk = np.arange(MXU_DIM) // HEAD_DIM
    avg = jnp.asarray((blk[:, None] == blk[None, :]) / HEAD_DIM, BF16)
    head_of_col = np.arange(ATTN_WIDTH) // HEAD_DIM
    expand_np = (np.arange(LANES)[:, None] == head_of_col[None, :])
    expand = jnp.asarray(expand_np, BF16)
    w_r = jnp.concatenate([w_router_group[0], w_router_expert[0]], axis=1)
    w_r = jnp.pad(w_r, ((0, 0), (0, LANES - w_r.shape[1]))).astype(BF16)
    b_r = jnp.pad(jnp.concatenate([b_router_group[0], b_router_expert[0]]),
                  (0, LANES - N_GROUPS - N_EXPERTS))[None, :]

    xp = x_prompt.reshape(batch * seq, d_model)
    proj = _inproj(xp, ln1, w_in_b, q_gain, k_gain, avg, TM_PROJ, batch, seq)
    k, v, u = proj[:3]
    natural = tuple(a.reshape(batch, 1, seq, ATTN_WIDTH) for a in proj[3:6])
    qkv = [natural] + [tuple(proj[6 + 3 * bi:9 + 3 * bi]) for bi in range(len(DILATED) - 1)]
    bias_all = _prompt_bias(rel_bias)
    o_list, l_list = [], []
    for branch, (qd, kd, vd) in enumerate(qkv):
        o, lse = _attn_branch(qd, kd, vd, bias_all, branch)
        o_list.append(o)
        l_list.append(lse)
    mix_p = _mix(o_list, l_list, u, expand, wp_b, ps, seq, TM_TOKEN)
    tri = jnp.asarray(np.tril(np.ones((TM_TOKEN, TM_TOKEN)), -1), BF16)
    x1_p, hp_p, gate_p, route_p, counts_p = _outproj(
        xp, mix_p, w_o_b, ln2, w_r, b_r, tri, jnp.zeros((1, LANES), F32), TM_TOKEN)

    n_s = nb * t_new
    xs = x_sample.reshape(n_s, d_model)
    k_s, v_s, u_s, q_s, _, _ = _inproj(xs, ln1, w_in_b, q_gain, k_gain, avg, n_s)
    heads_s = (nb, t_new, N_HEADS, HEAD_DIM)
    k_s5, v_s5 = k_s.reshape(heads_s), v_s.reshape(heads_s)
    to_lanes = lambda a: jnp.transpose(a, (0, 2, 3, 1))
    by_head = lambda a: jnp.transpose(a, (0, 2, 1, 3))
    bias_c, bias_n = _sample_bias(rel_bias, t_new, win)
    attn_h = _sample_attn(by_head(q_s.astype(F32).reshape(heads_s)), by_head(k_s5), by_head(v_s5),
                          to_lanes(cache_k[0]), to_lanes(cache_v[0]),
                          bias_c, bias_n, SAMPLE_HEADS_PER_STEP)
    attn_s = by_head(attn_h).reshape(n_s, ATTN_WIDTH)
    pool_s, sbuf = _sample_pool(u_s.reshape(nb, t_new, ATTN_WIDTH), state_pool[0], wp_b, ps)
    mix_s = jnp.concatenate([attn_s, pool_s.reshape(n_s, ATTN_WIDTH)], axis=1)
    x1_s, hp_s, gate_s, route_s, counts = _outproj(
        xs, mix_s, w_o_b, ln2, w_r, b_r, tri, counts_p, n_s)

    n_p = batch * seq
    n_pairs = (n_p + n_s) * TOP_K
    tm_e = TM_EXPERT
    n_tiles = -(-n_pairs // tm_e) + N_EXPERTS
    counts_i = counts[0, :N_EXPERTS].astype(jnp.int32)
    tile_e, tile_on, next_e, first_row, pad_start, pad_len, tail = _tile_plan(counts_i, n_tiles, tm_e)

    route = jnp.concatenate([route_p, route_s], axis=1)
    ids = route[:TOP_K].astype(jnp.int32)
    base = jnp.zeros_like(ids)
    for e in range(N_EXPERTS):
        base = jnp.where(ids == e, first_row[e], base)
    rows = base + route[TOP_K:2 * TOP_K].astype(jnp.int32)

    def by_tile(rows, tm):
        return jnp.transpose(rows.reshape(TOP_K, -1, tm), (1, 0, 2)).reshape(-1, 1, TOP_K * tm)

    rows_p, rows_s = by_tile(rows[:, :n_p], TM_TOKEN), by_tile(rows[:, n_p:], n_s)
    hs = _dispatch(rows_p, rows_s[0], pad_start, pad_len, tail, hp_p, hp_s, n_tiles, tm_e, TM_TOKEN)
    ys = _experts(tile_e, tile_on, next_e, hs, w_gate[0], w_up[0], w_down[0], tm_e)
    y_p = _combine(rows_p, x1_p, gate_p, ys, TM_TOKEN)
    y_s = _combine(rows_s, x1_s, gate_s, ys, n_s)

    keep = min(MAX_DISTANCE, seq)
    from_lanes = lambda a: jnp.transpose(a, (0, 3, 1, 2))[None, :, -keep:]
    return (y_p.reshape(batch, seq, d_model),
            y_s.reshape(nb, t_new, d_model),
            from_lanes(k), from_lanes(v),
            u.reshape(batch, seq, ATTN_WIDTH)[None, :, -POOL_BUF:],
            k_s5[None], v_s5[None], sbuf[None])
```

```python
import functools
import math

import numpy as np
import jax
import jax.numpy as jnp
from jax import lax
from jax.experimental import pallas as pl
from jax.experimental.pallas import tpu as pltpu

F32 = jnp.float32
BF16 = jnp.bfloat16

N_HEADS = 16
HEAD_DIM = 64
ATTN_WIDTH = N_HEADS * HEAD_DIM
POOL_WINDOWS = (2, 4, 8, 16)
POOL_GROUP_WIDTH = 256
POOL_BUF = max(POOL_WINDOWS) - 1
POOL_HALO = POOL_BUF + 1
DILATED = ((128, 1), (512, 4), (2048, 16))
N_STEPS = 128
QBLOCK = 128
NUM_BUCKETS = 32
MAX_DISTANCE = 2048
PAST_LEN = 16384
N_GROUPS = 4
EXPERTS_PER_GROUP = 4
N_EXPERTS = N_GROUPS * EXPERTS_PER_GROUP
TOP_K = 2
EPS = 1e-6
SCALE = HEAD_DIM ** -0.5
NEG_INF = -1e30
LANES = 128
SUBLANES = 8
MXU_DIM = 256
VMEM_LIMIT = 56 * 1024 * 1024


DMA_UNROLL = 8


def _params(*sem, unchecked=False):
    return pltpu.CompilerParams(dimension_semantics=sem, vmem_limit_bytes=VMEM_LIMIT,
                                disable_bounds_checks=unchecked)


def _const_spec(shape):
    zeros = (0,) * len(shape)
    return pl.BlockSpec(shape, lambda *_: zeros)


def _cast_kernel(x_ref, o_ref):
    o_ref[...] = x_ref[...].astype(o_ref.dtype)


def _to_bf16(w2d, block_rows):
    rows, cols = w2d.shape
    return pl.pallas_call(
        _cast_kernel,
        grid=(rows // block_rows,),
        in_specs=[pl.BlockSpec((block_rows, cols), lambda i: (i, 0))],
        out_specs=pl.BlockSpec((block_rows, cols), lambda i: (i, 0)),
        out_shape=jax.ShapeDtypeStruct((rows, cols), BF16),
        compiler_params=_params("parallel"),
        name="cast_bf16",
    )(w2d)


def _inproj_kernel(x_ref, ln_ref, w_ref, qg_ref, kg_ref, avg_ref, *refs, dilated):
    k_ref, v_ref, u_ref, q_ref, kb_ref, vb_ref = refs[:6]
    tm = x_ref.shape[0]
    x = x_ref[...]
    ms = jnp.mean(x * x, axis=-1, keepdims=True)
    h = (x * lax.rsqrt(ms + EPS) * ln_ref[...]).astype(BF16)
    n_chunks = ATTN_WIDTH // MXU_DIM

    def section(s):
        return jnp.dot(h, w_ref[:, s * ATTN_WIDTH:(s + 1) * ATTN_WIDTH],
                       preferred_element_type=F32)

    def head_norm(z, g_ref, c):
        zc = z[:, c * MXU_DIM:(c + 1) * MXU_DIM]
        msh = jnp.dot((zc * zc).astype(BF16), avg_ref[...], preferred_element_type=F32)
        return zc * lax.rsqrt(msh + EPS) * g_ref[:, c * MXU_DIM:(c + 1) * MXU_DIM]

    def emit(chunks, which, f32_ref, bf_ref):
        heads_per_chunk = MXU_DIM // HEAD_DIM
        for c, zc in enumerate(chunks):
            cs = slice(c * MXU_DIM, (c + 1) * MXU_DIM)
            bf_ref[:, cs] = zc.astype(BF16)
            if f32_ref is None:
                continue
            if dilated:
                zt = zc.T
                for j in range(heads_per_chunk):
                    f32_ref[c * heads_per_chunk + j] = zt[j * HEAD_DIM:(j + 1) * HEAD_DIM, :]
            else:
                f32_ref[:, cs] = zc
        if not dilated:
            return
        stage_ref = refs[-1]
        for c, zc in enumerate(chunks):
            for half in range(MXU_DIM // LANES):
                stage_ref[2 * c + half] = zc[:, half * LANES:(half + 1) * LANES]
        for bi, (_, dil) in enumerate(DILATED[1:]):
            out_ref = refs[6 + 3 * bi + which]
            n = tm // dil
            for r in range(dil):
                for s in range(ATTN_WIDTH // LANES):
                    rows = stage_ref[s, pl.ds(r, n, stride=dil), :]
                    out_ref[r, :, s * LANES:(s + 1) * LANES] = rows.astype(BF16)

    zq = section(0)
    emit([head_norm(zq, qg_ref, c) for c in range(n_chunks)], 0, None, q_ref)
    zk = section(1)
    emit([head_norm(zk, kg_ref, c) for c in range(n_chunks)], 1, k_ref, kb_ref)
    zv = section(2)
    emit([zv[:, c * MXU_DIM:(c + 1) * MXU_DIM] for c in range(n_chunks)], 2, v_ref, vb_ref)
    u_ref[...] = section(3)


def _inproj(x2d, ln1, w_in_b, q_gain, k_gain, avg, tm, batch=None, seq=None):
    n, d = x2d.shape
    dilated = seq is not None
    row = lambda i: (i, 0)
    wide = pl.BlockSpec((tm, ATTN_WIDTH), row)
    f32o = jax.ShapeDtypeStruct((n, ATTN_WIDTH), F32)
    bf16o = jax.ShapeDtypeStruct((n, ATTN_WIDTH), BF16)
    out_specs = [wide] * 6
    out_shape = [f32o, f32o, f32o, bf16o, bf16o, bf16o]
    scratch = []
    if dilated:
        tiles = seq // tm
        kv_t = pl.BlockSpec((None, N_HEADS, HEAD_DIM, tm), lambda i: (i // tiles, 0, 0, i % tiles))
        out_specs[:2] = [kv_t, kv_t]
        out_shape[:2] = [jax.ShapeDtypeStruct((batch, N_HEADS, HEAD_DIM, seq), F32)] * 2
        for _, dil in DILATED[1:]:
            spec = pl.BlockSpec((None, dil, tm // dil, ATTN_WIDTH),
                                lambda i: (i // tiles, 0, i % tiles, 0))
            out_specs += [spec] * 3
            out_shape += [jax.ShapeDtypeStruct((batch, dil, seq // dil, ATTN_WIDTH), BF16)] * 3
        scratch = [pltpu.VMEM((ATTN_WIDTH // LANES, tm, LANES), F32)]
    return pl.pallas_call(
        functools.partial(_inproj_kernel, dilated=dilated),
        grid=(n // tm,),
        in_specs=[pl.BlockSpec((tm, d), row),
                  _const_spec((1, d)),
                  pl.BlockSpec(w_in_b.shape, lambda i: (0, 0), pipeline_mode=pl.Buffered(1)),
                  _const_spec((1, ATTN_WIDTH)), _const_spec((1, ATTN_WIDTH)),
                  _const_spec((MXU_DIM, MXU_DIM))],
        out_specs=out_specs,
        out_shape=out_shape,
        scratch_shapes=scratch,
        compiler_params=_params("parallel"),
        name="inproj",
    )(x2d, ln1, w_in_b, q_gain, k_gain, avg)


def _attn_kernel(q_ref, kp_ref, kc_ref, vp_ref, vc_ref, bias_ref, o_ref, lse_ref,
                 s_ref, p_ref, *, whole):
    i = pl.program_id(2)
    lane = lax.broadcasted_iota(jnp.int32, (QBLOCK, LANES), 1)
    low = lane < HEAD_DIM
    keep_low = low.astype(F32).astype(BF16)
    keep_high = (1.0 - low.astype(F32)).astype(BF16)
    nt = (((1,), (1,)), ((), ()))
    n_pairs = N_HEADS // 2

    def run(rr, blk, with_prev):
        rq = slice(blk * QBLOCK, (blk + 1) * QBLOCK)
        before = slice((blk - 1) * QBLOCK, blk * QBLOCK)
        k0 = 0 if with_prev else QBLOCK
        nk = 2 * QBLOCK - k0
        ones = jnp.ones((nk, LANES), BF16)
        for hp in range(n_pairs):
            cs = slice(hp * LANES, (hp + 1) * LANES)
            qp = q_ref[rr, rq, cs]
            q2 = jnp.concatenate([qp * keep_low, qp * keep_high], axis=0)
            keys = kc_ref[rr, rq, cs]
            if with_prev:
                older = kp_ref[rr, :, cs] if blk == 0 else kc_ref[rr, before, cs]
                keys = jnp.concatenate([older, keys], axis=0)
            s2 = lax.dot_general(q2, keys, nt, preferred_element_type=F32)
            s_ref[hp, :, k0:] = s2 + bias_ref[hp, :, k0:]

        m_all = jnp.zeros((QBLOCK, LANES), F32)
        for hp in range(n_pairs):
            for sub in range(2):
                rows = slice(sub * QBLOCK, (sub + 1) * QBLOCK)
                m = jnp.max(s_ref[hp, rows, k0:], axis=-1, keepdims=True)
                p_ref[hp, rows, k0:] = jnp.exp(s_ref[hp, rows, k0:] - m).astype(BF16)
                m_all = jnp.where(lane == 2 * hp + sub, m, m_all)

        den_all = jnp.ones((QBLOCK, LANES), F32)
        for hp in range(n_pairs):
            cs = slice(hp * LANES, (hp + 1) * LANES)
            vals = vc_ref[rr, rq, cs]
            if with_prev:
                older = vp_ref[rr, :, cs] if blk == 0 else vc_ref[rr, before, cs]
                vals = jnp.concatenate([older, vals], axis=0)
            r = jnp.dot(p_ref[hp, :, k0:], jnp.concatenate([vals, ones], axis=1),
                        preferred_element_type=F32)
            den0, den1 = r[:QBLOCK, LANES:], r[QBLOCK:, LANES:]
            o_ref[rr, rq, cs] = jnp.where(low, r[:QBLOCK, :LANES] / den0,
                                          r[QBLOCK:, :LANES] / den1)
            den_all = jnp.where(lane == 2 * hp, den0, den_all)
            den_all = jnp.where(lane == 2 * hp + 1, den1, den_all)
        lse_ref[rr, rq, :] = m_all + jnp.log(den_all)

    for rr in range(q_ref.shape[0]):
        if whole:
            run(rr, 0, False)
        else:
            @pl.when(i == 0)
            def _():
                run(rr, 0, False)

            @pl.when(i > 0)
            def _():
                run(rr, 0, True)

        for blk in range(1, q_ref.shape[1] // QBLOCK):
            run(rr, blk, True)


def _attn_branch(q, kb, vb, bias_all, branch):
    batch, dil, sub, _ = q.shape
    qb = min(ATTN_BLOCKS_PER_STEP, sub // QBLOCK)
    whole = qb * QBLOCK == sub
    res = min(dil, ATTN_BLOCKS_PER_STEP // qb) if whole else 1
    cur = lambda b, r, i: (b, r, i, 0)
    prev = lambda b, r, i: (b, r, jnp.maximum(i * qb - 1, 0), 0)
    wide_c = pl.BlockSpec((None, res, qb * QBLOCK, ATTN_WIDTH), cur)
    wide_p = pl.BlockSpec((None, res, QBLOCK, ATTN_WIDTH), prev)
    pairs = N_HEADS // 2
    bias_spec = pl.BlockSpec((None, pairs, 2 * QBLOCK, 2 * QBLOCK), lambda b, r, i: (branch, 0, 0, 0))
    return pl.pallas_call(
        functools.partial(_attn_kernel, whole=whole),
        grid=(batch, dil // res, sub // (qb * QBLOCK)),
        in_specs=[wide_c, wide_p, wide_c, wide_p, wide_c, bias_spec],
        out_specs=[wide_c, pl.BlockSpec((None, res, qb * QBLOCK, LANES), cur)],
        out_shape=[jax.ShapeDtypeStruct((batch, dil, sub, ATTN_WIDTH), F32),
                   jax.ShapeDtypeStruct((batch, dil, sub, LANES), F32)],
        scratch_shapes=[pltpu.VMEM((pairs, 2 * QBLOCK, 2 * QBLOCK), F32),
                        pltpu.VMEM((pairs, 2 * QBLOCK, 2 * QBLOCK), BF16)],
        compiler_params=_params("parallel", "parallel", "parallel"),
        name=f"attn_d{dil}",
    )(q, kb, kb, vb, vb, bias_all)


def _bias_kernel(base_ref, out_ref):
    width = base_ref.shape[1]
    for h in range(N_HEADS):
        rows = jnp.broadcast_to(base_ref[h:h + 1, :], (QBLOCK, width))
        band = pltpu.roll(rows, 0, 1, stride=1, stride_axis=0)
        out_ref[h // 2, (h % 2) * QBLOCK:(h % 2 + 1) * QBLOCK, :] = band[:, :2 * QBLOCK]


def _prompt_bias(rel_bias):
    width = 3 * QBLOCK
    steps = QBLOCK - np.arange(width)
    valid = (steps >= 0) & (steps <= N_STEPS)
    buckets = np.stack([_bucket_table(dil)[np.clip(steps, 0, N_STEPS)] for _, dil in DILATED])
    base = jnp.where(valid[None, :, None], rel_bias.astype(F32)[buckets], NEG_INF)
    base = jnp.transpose(base, (0, 2, 1))
    pairs = N_HEADS // 2
    return pl.pallas_call(
        _bias_kernel,
        grid=(len(DILATED),),
        in_specs=[pl.BlockSpec((None, N_HEADS, width), lambda g: (g, 0, 0))],
        out_specs=pl.BlockSpec((None, pairs, 2 * QBLOCK, 2 * QBLOCK), lambda g: (g, 0, 0, 0)),
        out_shape=jax.ShapeDtypeStruct((len(DILATED), pairs, 2 * QBLOCK, 2 * QBLOCK), F32),
        compiler_params=_params("parallel"),
        name="bias_table",
    )(base)


def _split_dot(a, b_bf16):
    hi = a.astype(BF16)
    lo = (a - hi.astype(F32)).astype(BF16)
    return (jnp.dot(hi, b_bf16, preferred_element_type=F32)
            + jnp.dot(lo, b_bf16, preferred_element_type=F32))


def _pool_groups(comb, u, cnt_fn, wp_ref, ps_ref):
    t = u.shape[0]
    outs = []
    run = comb
    width = 1
    for g, w in enumerate(POOL_WINDOWS):
        while width < w:
            run = run + pltpu.roll(run, width, 0)
            width *= 2
        cs = slice(g * POOL_GROUP_WIDTH, (g + 1) * POOL_GROUP_WIDTH)
        d = run[POOL_HALO:POOL_HALO + t, cs] / cnt_fn(w) - u[:, cs]
        y = jnp.dot(d.astype(BF16), wp_ref[g], preferred_element_type=F32)
        outs.append(y * ps_ref[:, cs])
    return outs


def _mix_kernel(o1_ref, o2_ref, o3_ref, l1_ref, l2_ref, l3_ref, u_ref, halo_ref,
                ex_ref, wp_ref, ps_ref, mix_ref, il_ref, ls_ref, *, seq):
    tm = u_ref.shape[0]
    n_slabs = ATTN_WIDTH // LANES
    lses = [l1_ref[0]]
    for bi, (o_ref, l_ref) in enumerate(((o2_ref, l2_ref), (o3_ref, l3_ref))):
        dil = o_ref.shape[0]
        n = tm // dil
        for r in range(dil):
            ls_ref[bi, pl.ds(r, n, stride=dil), :] = l_ref[r]
            for s in range(n_slabs):
                il_ref[bi, s, pl.ds(r, n, stride=dil), :] = o_ref[r, :, s * LANES:(s + 1) * LANES]
        lses.append(ls_ref[bi])
    l1, l2, l3 = lses
    m = jnp.maximum(jnp.maximum(l1, l2), l3)
    e1, e2, e3 = jnp.exp(l1 - m), jnp.exp(l2 - m), jnp.exp(l3 - m)
    inv = 1.0 / (e1 + e2 + e3)
    ex = ex_ref[...]
    w1, w2, w3 = (_split_dot(e * inv, ex) for e in (e1, e2, e3))
    for s in range(n_slabs):
        cs = slice(s * LANES, (s + 1) * LANES)
        attn = w1[:, cs] * o1_ref[0, :, cs] + w2[:, cs] * il_ref[0, s] + w3[:, cs] * il_ref[1, s]
        mix_ref[:, cs] = attn.astype(BF16)

    pos0 = (pl.program_id(0) * tm) % seq
    u = u_ref[...]
    halo = jnp.where(pos0 == 0, 0.0, halo_ref[...])
    comb = jnp.concatenate([halo, u], axis=0)
    pos = pos0 + lax.broadcasted_iota(jnp.int32, (tm, 1), 0)
    cnt_fn = lambda w: jnp.minimum(pos + 1, w).astype(F32)
    for g, y in enumerate(_pool_groups(comb, u, cnt_fn, wp_ref, ps_ref)):
        lo = ATTN_WIDTH + g * POOL_GROUP_WIDTH
        mix_ref[:, lo:lo + POOL_GROUP_WIDTH] = y.astype(BF16)


def _mix(o_list, l_list, u, expand, wp_b, pool_scale, seq, tm):
    n = u.shape[0]
    tiles = seq // tm
    row = lambda i: (i, 0)
    res = lambda i: (i // tiles, 0, i % tiles, 0)
    o_specs = [pl.BlockSpec((None, o.shape[1], tm // o.shape[1], ATTN_WIDTH), res) for o in o_list]
    l_specs = [pl.BlockSpec((None, l.shape[1], tm // l.shape[1], LANES), res) for l in l_list]
    halo = pl.BlockSpec((POOL_HALO, ATTN_WIDTH),
                        lambda i: (jnp.maximum(i * (tm // POOL_HALO) - 1, 0), 0))
    n_dilated = len(o_list) - 1
    return pl.pallas_call(
        functools.partial(_mix_kernel, seq=seq),
        grid=(n // tm,),
        in_specs=o_specs + l_specs + [pl.BlockSpec((tm, ATTN_WIDTH), row), halo,
                                      _const_spec(expand.shape), _const_spec(wp_b.shape),
                                      _const_spec((1, ATTN_WIDTH))],
        out_specs=pl.BlockSpec((tm, 2 * ATTN_WIDTH), row),
        out_shape=jax.ShapeDtypeStruct((n, 2 * ATTN_WIDTH), BF16),
        scratch_shapes=[pltpu.VMEM((n_dilated, ATTN_WIDTH // LANES, tm, LANES), F32),
                        pltpu.VMEM((n_dilated, tm, LANES), F32)],
        compiler_params=_params("parallel"),
        name="mix",
    )(*o_list, *l_list, u, u, expand, wp_b, pool_scale)


def _sample_attn_kernel(q_ref, kn_ref, vn_ref, k_ref, v_ref, bc_ref, bn_ref, o_ref):
    n_heads, t_new, _ = q_ref.shape
    n_br = bc_ref.shape[0]
    lane_max = lambda a: jnp.max(a, axis=1, keepdims=True)
    lane_sum = lambda a: jnp.sum(a, axis=1, keepdims=True)
    new_lane = lax.broadcasted_iota(jnp.int32, (t_new, t_new), 1)
    nt = (((1,), (1,)), ((), ()))
    for h in range(n_heads):
        q, kn, vn = q_ref[h], kn_ref[h], vn_ref[h]
        s_c = jnp.dot(q.astype(BF16), k_ref[h].astype(BF16), preferred_element_type=F32)
        s_n = jnp.zeros((t_new, t_new), F32)
        for tp in range(t_new):
            s_n = jnp.where(new_lane == tp, lane_sum(q * kn[tp:tp + 1, :]), s_n)
        sc = [s_c + bc_ref[br, h] for br in range(n_br)]
        sn = [s_n + bn_ref[br, h] for br in range(n_br)]
        m = functools.reduce(jnp.maximum, [lane_max(a) for a in sc + sn])
        p_c = functools.reduce(jnp.add, [jnp.exp(a - m) for a in sc])
        p_n = functools.reduce(jnp.add, [jnp.exp(a - m) for a in sn])
        acc = lax.dot_general(p_c.astype(BF16), v_ref[h].astype(BF16), nt,
                              preferred_element_type=F32)
        for tp in range(t_new):
            acc = acc + p_n[:, tp:tp + 1] * vn[tp:tp + 1, :]
        o_ref[h] = acc / (lane_sum(p_c) + lane_sum(p_n))


def _sample_attn(q, kn, vn, k_t, v_t, bias_c, bias_n, heads_per_step):
    nb, nh, t_new, hd = q.shape
    win = k_t.shape[-1]
    hg = heads_per_step
    n_br = bias_c.shape[0]
    small = pl.BlockSpec((None, hg, t_new, hd), lambda b, g: (b, g, 0, 0))
    wide = pl.BlockSpec((None, hg, hd, win), lambda b, g: (b, g, 0, 0))
    return pl.pallas_call(
        _sample_attn_kernel,
        grid=(nb, nh // hg),
        in_specs=[small, small, small, wide, wide,
                  pl.BlockSpec((n_br, hg, t_new, win), lambda b, g: (0, g, 0, 0)),
                  pl.BlockSpec((n_br, hg, t_new, t_new), lambda b, g: (0, g, 0, 0))],
        out_specs=small,
        out_shape=jax.ShapeDtypeStruct((nb, nh, t_new, hd), F32),
        compiler_params=_params("parallel", "parallel"),
        name="sample_attn",
    )(q, kn, vn, k_t, v_t, bias_c, bias_n)


def _sample_pool_kernel(u_ref, st_ref, wp_ref, ps_ref, pool_ref, sbuf_ref, comb_ref, *, start):
    t_new = u_ref.shape[0]
    u = u_ref[...]
    comb_ref[...] = jnp.zeros_like(comb_ref)
    comb_ref[1:POOL_HALO, :] = st_ref[...]
    comb_ref[POOL_HALO:POOL_HALO + t_new, :] = u
    pos = start + lax.broadcasted_iota(jnp.int32, (t_new, 1), 0)
    cnt_fn = lambda w: jnp.minimum(pos + 1, w).astype(F32)
    for g, y in enumerate(_pool_groups(comb_ref[...], u, cnt_fn, wp_ref, ps_ref)):
        pool_ref[:, g * POOL_GROUP_WIDTH:(g + 1) * POOL_GROUP_WIDTH] = y
    sbuf_ref[:POOL_BUF - t_new, :] = st_ref[t_new:, :]
    sbuf_ref[POOL_BUF - t_new:, :] = u


def _sample_pool(u, state, wp_b, pool_scale):
    nb, t_new, w = u.shape
    flat = lambda rows: pl.BlockSpec((None, rows, w), lambda b: (b, 0, 0))
    return pl.pallas_call(
        functools.partial(_sample_pool_kernel, start=PAST_LEN),
        grid=(nb,),
        in_specs=[flat(t_new), flat(POOL_BUF), _const_spec(wp_b.shape), _const_spec((1, w))],
        out_specs=[flat(t_new), flat(POOL_BUF)],
        out_shape=[jax.ShapeDtypeStruct((nb, t_new, w), F32),
                   jax.ShapeDtypeStruct((nb, POOL_BUF, w), F32)],
        scratch_shapes=[pltpu.VMEM((POOL_HALO + 8, w), F32)],
        compiler_params=_params("parallel"),
        name="sample_pool",
    )(u, state, wp_b, pool_scale)


def _outproj_kernel(x_ref, mix_ref, wo_ref, ln_ref, wr_ref, br_ref, tri_ref, cin_ref,
                    x1_ref, h_ref, gate_ref, route_ref, cout_ref, count_ref):
    @pl.when(pl.program_id(0) == 0)
    def _():
        count_ref[...] = cin_ref[...]

    x1 = x_ref[...] + jnp.dot(mix_ref[...].astype(BF16), wo_ref[...],
                              preferred_element_type=F32)
    x1_ref[...] = x1
    ms = jnp.mean(x1 * x1, axis=-1, keepdims=True)
    h = x1 * lax.rsqrt(ms + EPS) * ln_ref[...]
    h_ref[...] = h
    lg = jnp.dot(h.astype(BF16), wr_ref[...], preferred_element_type=F32) + br_ref[...]

    lane = lax.broadcasted_iota(jnp.int32, lg.shape, 1).astype(F32)
    big = float(LANES)
    row_max = lambda mask: jnp.max(jnp.where(mask, lg, -jnp.inf), axis=-1, keepdims=True)
    first = lambda mask: jnp.min(jnp.where(mask, lane, big), axis=-1, keepdims=True)
    is_g = lane < N_GROUPS
    mg = row_max(is_g)
    g_top = first(jnp.logical_and(is_g, lg == mg))
    den = jnp.sum(jnp.where(is_g, jnp.exp(lg - mg), 0.0), axis=-1, keepdims=True)
    p_top = 1.0 / den
    base = N_GROUPS + EXPERTS_PER_GROUP * g_top
    in_grp = jnp.logical_and(lane >= base, lane < base + EXPERTS_PER_GROUP)
    v1 = row_max(in_grp)
    i1 = first(jnp.logical_and(in_grp, lg == v1))
    rest = jnp.logical_and(in_grp, lane != i1)
    v2 = row_max(rest)
    i2 = first(jnp.logical_and(rest, lg == v2))
    e21 = jnp.exp(v2 - v1)
    s21 = 1.0 + e21
    gate1 = p_top * (1.0 / s21)
    gate2 = p_top * (e21 / s21)
    gate_ref[...] = jnp.where(lane == 0.0, gate1, jnp.where(lane == 1.0, gate2, 0.0))
    e1, e2 = i1 - N_GROUPS, i2 - N_GROUPS

    hot1 = (lane == e1).astype(F32)
    hot2 = (lane == e2).astype(F32)
    hot = hot1 + hot2
    before = count_ref[...] + jnp.dot(tri_ref[...], hot.astype(BF16), preferred_element_type=F32)
    rank1 = jnp.sum(hot1 * before, axis=-1, keepdims=True)
    rank2 = jnp.sum(hot2 * before, axis=-1, keepdims=True)
    count_ref[...] = count_ref[...] + jnp.sum(hot, axis=0, keepdims=True)
    cout_ref[...] = count_ref[...]

    cols = jnp.zeros_like(lg)
    for r, col in enumerate((e1, e2, rank1, rank2)):
        cols = jnp.where(lane == float(r), col, cols)
    route_ref[...] = cols.T[:route_ref.shape[0], :]


def _outproj(x2d, mix, w_o_b, ln2, w_r, b_r, tri, counts_in, tm):
    n, d = x2d.shape
    row = lambda i: (i, 0)
    full = pl.BlockSpec((tm, d), row)
    stat = pl.BlockSpec((tm, LANES), row)
    route_rows = 8
    return pl.pallas_call(
        _outproj_kernel,
        grid=(n // tm,),
        in_specs=[full, full,
                  pl.BlockSpec(w_o_b.shape, lambda i: (0, 0), pipeline_mode=pl.Buffered(1)),
                  _const_spec((1, d)), _const_spec(w_r.shape),
                  _const_spec((1, LANES)), _const_spec((tm, tm)), _const_spec((1, LANES))],
        out_specs=[full, full, stat,
                   pl.BlockSpec((route_rows, tm), lambda i: (0, i)), _const_spec((1, LANES))],
        out_shape=[jax.ShapeDtypeStruct((n, d), F32),
                   jax.ShapeDtypeStruct((n, d), F32),
                   jax.ShapeDtypeStruct((n, LANES), F32),
                   jax.ShapeDtypeStruct((route_rows, n), F32),
                   jax.ShapeDtypeStruct((1, LANES), F32)],
        scratch_shapes=[pltpu.VMEM((1, LANES), F32)],
        compiler_params=_params("arbitrary"),
        name="outproj",
    )(x2d, mix, w_o_b, ln2, w_r, b_r, tri[:tm, :tm], counts_in)


def _row_copy(src_hbm, dst_vmem, sem, src_row, dst_row):
    return pltpu.make_async_copy(src_hbm.at[pl.ds(src_row, 1)],
                                 dst_vmem.at[pl.ds(dst_row, 1)], sem)


def _dispatch_kernel(pos_ref, pos_s_ref, pad_start_ref, pad_len_ref, tail_ref, h_ref, h_s_ref,
                     hs_ref, sem_ref, zero_ref, *, tile_rows):
    i = pl.program_id(0)
    sem = sem_ref.at[0]

    def scatter(src_ref, dst_rows_ref):
        n_rows = src_ref.shape[0]

        def put(r, k):
            return _row_copy(src_ref, hs_ref, sem, r, dst_rows_ref[0, k * n_rows + r])

        def issue(r, c):
            for k in range(TOP_K):
                put(r, k).start()
            return c

        def drain(r, c):
            for k in range(TOP_K):
                put(r, k).wait()
            return c

        lax.fori_loop(0, n_rows, issue, 0, unroll=DMA_UNROLL)
        lax.fori_loop(0, n_rows, drain, 0, unroll=DMA_UNROLL)

    scatter(h_ref, pos_ref)

    @pl.when(i == pl.num_programs(0) - 1)
    def _():
        scatter(h_s_ref, pos_s_ref)
        zero_ref[...] = jnp.zeros_like(zero_ref)

        sizes = [s for s in (tile_rows >> (b + 1) for b in range(tile_rows.bit_length()))
                 if s >= SUBLANES]

        def pad_copies(act):
            def single_rows(first, count):
                def body(r, c):
                    act(_row_copy(zero_ref, hs_ref, sem, 0, first + r))
                    return c
                lax.fori_loop(0, count, body, 0)

            for e in range(N_EXPERTS):
                start, length = pad_start_ref[e], pad_len_ref[e]
                head = jnp.minimum((-start) & (SUBLANES - 1), length)
                single_rows(start, head)
                body_len = length - head
                aligned = body_len & -SUBLANES
                offset = start + head
                for b, size in enumerate(sizes):
                    @pl.when((aligned & size) != 0)
                    def _():
                        rows = pl.ds(pl.multiple_of(offset, SUBLANES), size)
                        act(pltpu.make_async_copy(zero_ref.at[pl.ds(0, size)], hs_ref.at[rows],
                                                  sem_ref.at[1 + b]))
                    offset = offset + (aligned & size)
                single_rows(offset, body_len - aligned)

        pad_copies(lambda copy: copy.start())
        pad_copies(lambda copy: copy.wait())

        n_tiles = hs_ref.shape[0] // tile_rows

        def fill_tile(t):
            rows = pl.ds(pl.multiple_of(t * tile_rows, tile_rows), tile_rows)
            return pltpu.make_async_copy(zero_ref, hs_ref.at[rows], sem)

        def start_tile(t, c):
            fill_tile(t).start()
            return c

        def wait_tile(t, c):
            fill_tile(t).wait()
            return c

        lax.fori_loop(tail_ref[0], n_tiles, start_tile, 0)
        lax.fori_loop(tail_ref[0], n_tiles, wait_tile, 0)


def _tile_rows_spec(tm):
    return pl.BlockSpec((None, 1, TOP_K * tm), lambda i: (i, 0, 0), memory_space=pltpu.SMEM)


def _dispatch(rows_p, rows_s, pad_start, pad_len, tail, h, h_s, n_tiles, tile_rows, tm):
    n, d = h.shape
    smem = pl.BlockSpec(memory_space=pltpu.SMEM)
    return pl.pallas_call(
        functools.partial(_dispatch_kernel, tile_rows=tile_rows),
        grid=(n // tm,),
        in_specs=[_tile_rows_spec(tm), smem, smem, smem, smem,
                  pl.BlockSpec((tm, d), lambda i: (i, 0)),
                  pl.BlockSpec(h_s.shape, lambda i: (0, 0))],
        out_specs=pl.BlockSpec(memory_space=pl.ANY),
        out_shape=jax.ShapeDtypeStruct((n_tiles * tile_rows, d), h.dtype),
        scratch_shapes=[pltpu.SemaphoreType.DMA((tile_rows.bit_length(),)),
                        pltpu.VMEM((tile_rows, d), h.dtype)],
        compiler_params=_params("arbitrary", unchecked=True),
        name="dispatch",
    )(rows_p, rows_s, pad_start, pad_len, tail, h, h_s)


def _expert_kernel(tile_e_ref, tile_on_ref, next_e_ref, hs_ref, wg_hbm, wu_hbm, wd_hbm, y_ref,
                   wg_f, wu_f, wd_f, wg_b, wu_b, wd_b, sem_ref, slot_ref):
    i = pl.program_id(0)

    def fetch(expert, slot):
        return [pltpu.make_async_copy(src.at[expert], dst.at[slot], sem_ref.at[slot])
                for src, dst in ((wg_hbm, wg_f), (wu_hbm, wu_f), (wd_hbm, wd_f))]

    @pl.when(i == 0)
    def _():
        slot_ref[0] = 0
        for copy in fetch(tile_e_ref[0], 0):
            copy.start()

    @pl.when(jnp.logical_or(i == 0, tile_e_ref[i] != tile_e_ref[jnp.maximum(i - 1, 0)]))
    def _():
        slot = slot_ref[0]
        for copy in fetch(tile_e_ref[i], slot):
            copy.wait()
        wg_b[...] = wg_f[slot].astype(BF16)
        wu_b[...] = wu_f[slot].astype(BF16)
        wd_b[...] = wd_f[slot].astype(BF16)

        @pl.when(next_e_ref[i] >= 0)
        def _():
            for copy in fetch(next_e_ref[i], 1 - slot):
                copy.start()

        slot_ref[0] = 1 - slot

    @pl.when(tile_on_ref[i] == 1)
    def _():
        hb = hs_ref[...].astype(BF16)
        a = jnp.dot(hb, wg_b[...], preferred_element_type=F32)
        b = jnp.dot(hb, wu_b[...], preferred_element_type=F32)
        hid = a * jax.nn.sigmoid(a) * b
        y_ref[...] = jnp.dot(hid.astype(BF16), wd_b[...], preferred_element_type=F32)

    @pl.when(tile_on_ref[i] == 0)
    def _():
        y_ref[...] = jnp.zeros_like(y_ref)


def _experts(tile_e, tile_on, next_e, hs, w_gate, w_up, w_down, tm):
    n_tiles = tile_e.shape[0]
    d, f = w_gate.shape[1:]
    any_spec = pl.BlockSpec(memory_space=pl.ANY)
    grid_spec = pltpu.PrefetchScalarGridSpec(
        num_scalar_prefetch=3,
        grid=(n_tiles,),
        in_specs=[pl.BlockSpec((tm, d), lambda i, *_: (i, 0)), any_spec, any_spec, any_spec],
        out_specs=pl.BlockSpec((tm, d), lambda i, *_: (i, 0)),
        scratch_shapes=[pltpu.VMEM((2, d, f), F32), pltpu.VMEM((2, d, f), F32),
                        pltpu.VMEM((2, f, d), F32),
                        pltpu.VMEM((d, f), BF16), pltpu.VMEM((d, f), BF16),
                        pltpu.VMEM((f, d), BF16),
                        pltpu.SemaphoreType.DMA((2,)), pltpu.SMEM((1,), jnp.int32)],
    )
    return pl.pallas_call(
        _expert_kernel,
        grid_spec=grid_spec,
        out_shape=jax.ShapeDtypeStruct((n_tiles * tm, d), F32),
        compiler_params=_params("arbitrary"),
        name="experts",
    )(tile_e, tile_on, next_e, hs, w_gate, w_up, w_down)


def _combine_kernel(rows_ref, next_rows_ref, x1_ref, gate_ref, ys_hbm, y_ref, buf_ref, sem_ref):
    i = pl.program_id(0)
    n_tiles = pl.num_programs(0)
    tm = y_ref.shape[0]

    def issue(src_rows_ref, slot):
        def body(r, c):
            for k in range(TOP_K):
                _row_copy(ys_hbm, buf_ref.at[slot, k], sem_ref.at[slot],
                          src_rows_ref[0, k * tm + r], r).start()
            return c
        lax.fori_loop(0, tm, body, 0, unroll=DMA_UNROLL)

    def drain(slot):
        def body(r, c):
            for k in range(TOP_K):
                _row_copy(ys_hbm, buf_ref.at[slot, k], sem_ref.at[slot], 0, r).wait()
            return c
        lax.fori_loop(0, tm, body, 0, unroll=DMA_UNROLL)

    slot = i % 2

    @pl.when(i == 0)
    def _():
        issue(rows_ref, 0)

    drain(slot)

    @pl.when(i + 1 < n_tiles)
    def _():
        issue(next_rows_ref, 1 - slot)

    gate = gate_ref[...]
    y_ref[...] = (x1_ref[...] + gate[:, 0:1] * buf_ref[slot, 0]
                  + gate[:, 1:2] * buf_ref[slot, 1])


def _combine(rows, x1, gate, ys, tm):
    n, d = x1.shape
    last = n // tm - 1
    next_spec = pl.BlockSpec((None, 1, TOP_K * tm), lambda i: (jnp.minimum(i + 1, last), 0, 0),
                             memory_space=pltpu.SMEM)
    return pl.pallas_call(
        _combine_kernel,
        grid=(n // tm,),
        in_specs=[_tile_rows_spec(tm), next_spec,
                  pl.BlockSpec((tm, d), lambda i: (i, 0)),
                  pl.BlockSpec((tm, LANES), lambda i: (i, 0)),
                  pl.BlockSpec(memory_space=pl.ANY)],
        out_specs=pl.BlockSpec((tm, d), lambda i: (i, 0)),
        out_shape=jax.ShapeDtypeStruct((n, d), F32),
        scratch_shapes=[pltpu.VMEM((2, TOP_K, tm, d), F32), pltpu.SemaphoreType.DMA((2,))],
        compiler_params=_params("arbitrary", unchecked=True),
        name="combine",
    )(rows, rows, x1, gate, ys)


def _bucket_table(dilation):
    dist = np.arange(N_STEPS + 1, dtype=np.int64) * dilation
    max_exact = NUM_BUCKETS // 2
    df = np.maximum(dist, 1).astype(np.float32)
    large = max_exact + (np.log(df / np.float32(max_exact))
                         / np.float32(math.log(MAX_DISTANCE / max_exact))
                         * np.float32(NUM_BUCKETS - max_exact)).astype(np.int32)
    large = np.minimum(large, NUM_BUCKETS - 1)
    return np.where(dist < max_exact, dist, large).astype(np.int32)


def _sample_bias(rel_bias, t_new, win):
    buckets = np.stack([_bucket_table(dil) for _, dil in DILATED])
    by_step = jnp.transpose(rel_bias.astype(F32)[buckets], (0, 2, 1))
    nh = by_step.shape[1]
    neg = lambda *shape: jnp.full(shape, NEG_INF, F32)
    t = np.arange(t_new)
    cached, fresh = [], []
    for br, (_, dil) in enumerate(DILATED):
        rev = by_step[br, :, ::-1][:, :N_STEPS]
        if dil == 1:
            rows = [jnp.concatenate([neg(nh, win - N_STEPS + q), rev[:, :N_STEPS - q]], axis=1)
                    for q in range(t_new)]
            cached.append(jnp.stack(rows, axis=1))
        else:
            own = (t[:, None, None] == np.arange(dil)[None, None, :])
            band = jnp.where(own[None], rev[:, None, :, None], NEG_INF)
            band = band.reshape(nh, t_new, dil * N_STEPS)
            cached.append(jnp.concatenate([neg(nh, t_new, win - dil * N_STEPS), band], axis=2))
        step = t[:, None] - t[None, :]
        ok = (step >= 0) & (step % dil == 0)
        vals = by_step[br][:, np.where(ok, step // dil, 0)]
        fresh.append(jnp.where(ok[None], vals, NEG_INF))
    return jnp.stack(cached), jnp.stack(fresh)


def _tile_plan(counts, n_tiles, tm):
    tiles_e = (counts + tm - 1) // tm
    ends = jnp.cumsum(tiles_e)
    first_row = (ends - tiles_e) * tm
    tile = jnp.arange(n_tiles, dtype=jnp.int32)
    tile_e = jnp.sum((ends[None, :] <= tile[:, None]).astype(jnp.int32), axis=1)
    tile_on = (tile_e < N_EXPERTS).astype(jnp.int32)
    tile_e = jnp.minimum(tile_e, N_EXPERTS - 1)
    later = jnp.where(tile_e[None, :] > tile_e[:, None], tile_e[None, :], N_EXPERTS)
    next_e = jnp.min(later, axis=1)
    next_e = jnp.where(next_e == N_EXPERTS, -1, next_e)
    i32 = lambda a: a.astype(jnp.int32)
    return (i32(tile_e), tile_on, i32(next_e), i32(first_row), i32(first_row + counts),
            i32(tiles_e * tm - counts), i32(ends[-1:]))


TM_PROJ = 256
TM_TOKEN = 512
TM_EXPERT = 256
SAMPLE_HEADS_PER_STEP = 8
ATTN_BLOCKS_PER_STEP = 4


def kernel(x_prompt, x_sample, cache_k, cache_v, state_pool, rel_bias, ln1_w, w_in,
           q_norm_w, k_norm_w, w_pool, pool_scale, w_o, ln2_w, w_router_group,
           b_router_group, w_router_expert, b_router_expert, w_gate, w_up, w_down):
    depth = w_in.shape[0]
    assert depth == 1
    batch, seq, d_model = x_prompt.shape
    nb, t_new, _ = x_sample.shape
    win = cache_k.shape[2]
    f_exp = w_gate.shape[-1]

    w_in_b = _to_bf16(w_in[0], 256)
    w_o_b = _to_bf16(w_o[0], 256)
    wp_b = _to_bf16(w_pool[0].reshape(-1, POOL_GROUP_WIDTH), 256).reshape(w_pool.shape[1:])
    ln1 = ln1_w[0][None, :]
    ln2 = ln2_w[0][None, :]
    q_gain = jnp.tile(q_norm_w[0], N_HEADS)[None, :] * SCALE
    k_gain = jnp.tile(k_norm_w[0], N_HEADS)[None, :]
    ps = pool_scale[0][None, :]
    blk = np.arange(MXU_DIM) // HEAD_DIM
    avg = jnp.asarray((blk[:, None] == blk[None, :]) / HEAD_DIM, BF16)
    head_of_col = np.arange(ATTN_WIDTH) // HEAD_DIM
    expand_np = (np.arange(LANES)[:, None] == head_of_col[None, :])
    expand = jnp.asarray(expand_np, BF16)
    w_r = jnp.concatenate([w_router_group[0], w_router_expert[0]], axis=1)
    w_r = jnp.pad(w_r, ((0, 0), (0, LANES - w_r.shape[1]))).astype(BF16)
    b_r = jnp.pad(jnp.concatenate([b_router_group[0], b_router_expert[0]]),
                  (0, LANES - N_GROUPS - N_EXPERTS))[None, :]

    xp = x_prompt.reshape(batch * seq, d_model)
    proj = _inproj(xp, ln1, w_in_b, q_gain, k_gain, avg, TM_PROJ, batch, seq)
    k, v, u = proj[:3]
    natural = tuple(a.reshape(batch, 1, seq, ATTN_WIDTH) for a in proj[3:6])
    qkv = [natural] + [tuple(proj[6 + 3 * bi:9 + 3 * bi]) for bi in range(len(DILATED) - 1)]
    bias_all = _prompt_bias(rel_bias)
    o_list, l_list = [], []
    for branch, (qd, kd, vd) in enumerate(qkv):
        o, lse = _attn_branch(qd, kd, vd, bias_all, branch)
        o_list.append(o)
        l_list.append(lse)
    mix_p = _mix(o_list, l_list, u, expand, wp_b, ps, seq, TM_TOKEN)
    tri = jnp.asarray(np.tril(np.ones((TM_TOKEN, TM_TOKEN)), -1), BF16)
    x1_p, hp_p, gate_p, route_p, counts_p = _outproj(
        xp, mix_p, w_o_b, ln2, w_r, b_r, tri, jnp.zeros((1, LANES), F32), TM_TOKEN)

    n_s = nb * t_new
    xs = x_sample.reshape(n_s, d_model)
    k_s, v_s, u_s, q_s, _, _ = _inproj(xs, ln1, w_in_b, q_gain, k_gain, avg, n_s)
    heads_s = (nb, t_new, N_HEADS, HEAD_DIM)
    k_s5, v_s5 = k_s.reshape(heads_s), v_s.reshape(heads_s)
    to_lanes = lambda a: jnp.transpose(a, (0, 2, 3, 1))
    by_head = lambda a: jnp.transpose(a, (0, 2, 1, 3))
    bias_c, bias_n = _sample_bias(rel_bias, t_new, win)
    attn_h = _sample_attn(by_head(q_s.astype(F32).reshape(heads_s)), by_head(k_s5), by_head(v_s5),
                          to_lanes(cache_k[0]), to_lanes(cache_v[0]),
                          bias_c, bias_n, SAMPLE_HEADS_PER_STEP)
    attn_s = by_head(attn_h).reshape(n_s, ATTN_WIDTH)
    pool_s, sbuf = _sample_pool(u_s.reshape(nb, t_new, ATTN_WIDTH), state_pool[0], wp_b, ps)
    mix_s = jnp.concatenate([attn_s, pool_s.reshape(n_s, ATTN_WIDTH)], axis=1)
    x1_s, hp_s, gate_s, route_s, counts = _outproj(
        xs, mix_s, w_o_b, ln2, w_r, b_r, tri, counts_p, n_s)

    n_p = batch * seq
    n_pairs = (n_p + n_s) * TOP_K
    tm_e = TM_EXPERT
    n_tiles = -(-n_pairs // tm_e) + N_EXPERTS
    counts_i = counts[0, :N_EXPERTS].astype(jnp.int32)
    tile_e, tile_on, next_e, first_row, pad_start, pad_len, tail = _tile_plan(counts_i, n_tiles, tm_e)

    route = jnp.concatenate([route_p, route_s], axis=1)
    ids = route[:TOP_K].astype(jnp.int32)
    base = jnp.zeros_like(ids)
    for e in range(N_EXPERTS):
        base = jnp.where(ids == e, first_row[e], base)
    rows = base + route[TOP_K:2 * TOP_K].astype(jnp.int32)

    def by_tile(rows, tm):
        return jnp.transpose(rows.reshape(TOP_K, -1, tm), (1, 0, 2)).reshape(-1, 1, TOP_K * tm)

    rows_p, rows_s = by_tile(rows[:, :n_p], TM_TOKEN), by_tile(rows[:, n_p:], n_s)
    hs = _dispatch(rows_p, rows_s[0], pad_start, pad_len, tail, hp_p, hp_s, n_tiles, tm_e, TM_TOKEN)
    ys = _experts(tile_e, tile_on, next_e, hs, w_gate[0], w_up[0], w_down[0], tm_e)
    y_p = _combine(rows_p, x1_p, gate_p, ys, TM_TOKEN)
    y_s = _combine(rows_s, x1_s, gate_s, ys, n_s)

    keep = min(MAX_DISTANCE, seq)
    from_lanes = lambda a: jnp.transpose(a, (0, 3, 1, 2))[None, :, -keep:]
    return (y_p.reshape(batch, seq, d_model),
            y_s.reshape(nb, t_new, d_model),
            from_lanes(k), from_lanes(v),
            u.reshape(batch, seq, ATTN_WIDTH)[None, :, -POOL_BUF:],
            k_s5[None], v_s5[None], sbuf[None])
```
